```python
import math
import jax
import jax.numpy as jnp
from jax import lax
import numpy as np

D_MODEL = 2048
BATCH = 4
SEQ = 2048
DEPTH = 2

GRID_W = 64
CTX_LEN = 256
HEAD_DIM = 128
HY_WIDTH = D_MODEL // 4
HY_SHORT = 3
HY_FILTER_HIDDEN = 64
HY_POS_BANDS = 16
HY_POS_EMB = 1 + 2 * HY_POS_BANDS
HY_DECAY_TARGET = 1e-2
HY_FAST_DECAY = 0.3
HY_SLOW_DECAY = 1.5
MLA_HEADS = (D_MODEL // 2) // HEAD_DIM
MLA_NOPE = HEAD_DIM
MLA_ROPE = 64
MLA_V = HEAD_DIM
Q_LORA = 3 * D_MODEL // 8
KV_LORA = D_MODEL // 4
MLA_BLOCK = 128
MLA_SCALE = (MLA_NOPE + MLA_ROPE) ** -0.5
NA_HEADS = (D_MODEL // 4) // HEAD_DIM
NA_KH = 8
NA_KW = 16
NA_QB = 16
NA_KB = 2 * NA_KW
NA_SCALE = HEAD_DIM ** -0.5
FFN_HIDDEN = ((8 * D_MODEL + 3 * 256 - 1) // (3 * 256)) * 256
ROPE_THETA = 10000.0
RMS_EPS = 1e-6
MASK_VALUE = -1e30
IN_SPLITS = (3 * HY_WIDTH, Q_LORA, KV_LORA, MLA_ROPE, 3 * NA_HEADS * HEAD_DIM)
IN_COLS = sum(IN_SPLITS)

kernel_name = 'hybrid_hyena_mla_natten_dit_block'


def rmsnorm(x, g):
    xf = x.astype(jnp.float32)
    y = xf * lax.rsqrt(jnp.mean(xf * xf, axis=-1, keepdims=True) + RMS_EPS)
    return (y * g.astype(jnp.float32)).astype(x.dtype)


def modulate(x, g, shift, scale):
    return rmsnorm(x, g) * (1 + scale) + shift


def split_in(p):
    idx = [int(i) for i in np.cumsum(IN_SPLITS)[:-1]]
    return jnp.split(p, idx, axis=-1)


def axial_rope(n_tok):
    t = jnp.arange(n_tok)
    row = (t // GRID_W).astype(jnp.float32)
    col = (t % GRID_W).astype(jnp.float32)
    n_freq = MLA_ROPE // 4
    inv = ROPE_THETA ** (-jnp.arange(n_freq, dtype=jnp.float32) / n_freq)
    ang = jnp.concatenate([row[:, None] * inv, col[:, None] * inv], axis=-1)
    return jnp.cos(ang), jnp.sin(ang)


def apply_rope(x, cos, sin):
    half = x.shape[-1] // 2
    xf = x.astype(jnp.float32)
    x1, x2 = xf[..., :half], xf[..., half:]
    return jnp.concatenate([x1 * cos - x2 * sin, x2 * cos + x1 * sin], axis=-1).astype(x.dtype)


def short_conv(u, w, b):
    n = u.shape[1]
    up = jnp.pad(u, ((0, 0), (1, 1), (0, 0)))
    return up[:, :n] * w[0] + up[:, 1:n + 1] * w[1] + up[:, 2:] * w[2] + b


def hyena_filters(n, f_w1, f_b1, f_w2, f_b2, f_w3, f_freq):
    pos = jnp.arange(n, dtype=jnp.float32)
    t = jnp.linspace(0.0, 1.0, n, dtype=jnp.float32)
    bands = jnp.linspace(1e-4, HY_POS_BANDS - 1, HY_POS_BANDS, dtype=jnp.float32)
    ang = (2.0 * math.pi / n) * pos[:, None] * bands[None, :]
    z = jnp.concatenate([t[:, None], jnp.cos(ang), -jnp.sin(ang)], axis=-1)
    fr = f_freq.astype(jnp.float32)
    h = jnp.sin(fr * (z @ f_w1.astype(jnp.float32) + f_b1.astype(jnp.float32)))
    h = jnp.sin(fr * (h @ f_w2.astype(jnp.float32) + f_b2.astype(jnp.float32)))
    h = h @ f_w3.astype(jnp.float32)
    deltas = jnp.abs(jnp.linspace(math.log(HY_DECAY_TARGET) / HY_FAST_DECAY, math.log(HY_DECAY_TARGET) / HY_SLOW_DECAY, HY_WIDTH, dtype=jnp.float32))
    decay = jnp.exp(-t[:, None] * deltas[None, :])
    h = h.reshape(n, 2, HY_WIDTH) * decay[:, None, :]
    return h[:, 0], h[:, 1]


def hyena_mix(p, conv_w, conv_b, f_w1, f_b1, f_w2, f_b2, f_w3, f_freq, bias):
    n = p.shape[1]
    u = short_conv(p, conv_w, conv_b)
    v, x1, x2 = jnp.split(u, 3, axis=-1)
    h_fwd, h_bwd = hyena_filters(n, f_w1, f_b1, f_w2, f_b2, f_w3, f_freq)
    k_two = jnp.concatenate([h_fwd, jnp.zeros((1, HY_WIDTH), jnp.float32), h_bwd[:0:-1]], axis=0)
    zin = (x1 * v).astype(jnp.float32)
    y = jnp.fft.irfft(jnp.fft.rfft(zin, n=2 * n, axis=1) * jnp.fft.rfft(k_two, axis=0)[None], n=2 * n, axis=1)[:, :n]
    y = y + zin * bias.astype(jnp.float32)
    return (x2.astype(jnp.float32) * y).astype(p.dtype)


def mla_queries(cq, g_q, w_uq):
    b, n = cq.shape[0], cq.shape[1]
    q = (rmsnorm(cq, g_q) @ w_uq).reshape(b, n, MLA_HEADS, MLA_NOPE + MLA_ROPE)
    return q[..., :MLA_NOPE], q[..., MLA_NOPE:]


def mla_keys(ckv, g_kv, w_ukv):
    b, n = ckv.shape[0], ckv.shape[1]
    kv = (rmsnorm(ckv, g_kv) @ w_ukv).reshape(b, n, MLA_HEADS, MLA_NOPE + MLA_V)
    return kv[..., :MLA_NOPE], kv[..., MLA_NOPE:]


def mla_attend(qn, qr, kn, kr, v):
    s = jnp.einsum('bqhd,bkhd->bhqk', qn, kn) + jnp.einsum('bqhr,bkr->bhqk', qr, kr)
    p = jax.nn.softmax(s.astype(jnp.float32) * MLA_SCALE, axis=-1).astype(v.dtype)
    return jnp.einsum('bhqk,bkhd->bqhd', p, v)


def mla_latent(qn, qr, kn, kr, v):
    b, n = qn.shape[0], qn.shape[1]
    nb = n // MLA_BLOCK
    qn_b = jnp.moveaxis(qn.reshape(b, nb, MLA_BLOCK, MLA_HEADS, MLA_NOPE), 1, 0)
    qr_b = jnp.moveaxis(qr.reshape(b, nb, MLA_BLOCK, MLA_HEADS, MLA_ROPE), 1, 0)
    out = lax.map(lambda qq: mla_attend(qq[0], qq[1], kn, kr, v), (qn_b, qr_b))
    return jnp.moveaxis(out, 0, 1).reshape(b, n, MLA_HEADS * MLA_V)


def dense_attend(q, k, v, scale):
    s = jnp.einsum('bqhd,bkhd->bhqk', q, k).astype(jnp.float32) * scale
    p = jax.nn.softmax(s, axis=-1).astype(v.dtype)
    return jnp.einsum('bhqk,bkhd->bqhd', p, v)


def natten_latent(q, k, v, k_ctx, v_ctx, rpb):
    b, n_tok = q.shape[0], q.shape[1]
    rows = n_tok // GRID_W
    kh = min(NA_KH, rows)
    n_cb = GRID_W // NA_QB
    q_col = np.arange(GRID_W).reshape(n_cb, NA_QB)
    c_start = np.clip(q_col - NA_KW // 2, 0, GRID_W - NA_KW)
    k_col = np.clip(c_start[:, :1], 0, GRID_W - NA_KB) + np.arange(NA_KB)
    col_ok = (k_col[:, None, :] >= c_start[:, :, None]) & (k_col[:, None, :] < c_start[:, :, None] + NA_KW)
    col_idx = np.clip(k_col[:, None, :] - q_col[:, :, None] + NA_KW - 1, 0, 2 * NA_KW - 2)
    rpb_cols = rpb.astype(jnp.float32)[:, :, col_idx]
    qg = q.reshape(b, rows, GRID_W, NA_HEADS, HEAD_DIM)
    kg = k.reshape(b, rows, GRID_W, NA_HEADS, HEAD_DIM)
    vg = v.reshape(b, rows, GRID_W, NA_HEADS, HEAD_DIM)
    n_loc = kh * NA_KB

    def one_row(r):
        r0 = jnp.clip(r - kh // 2, 0, rows - kh)
        k_blk = lax.dynamic_slice_in_dim(kg, r0, kh, axis=1)[:, :, k_col]
        v_blk = lax.dynamic_slice_in_dim(vg, r0, kh, axis=1)[:, :, k_col]
        q_blk = lax.dynamic_index_in_dim(qg, r, axis=1, keepdims=False).reshape(b, n_cb, NA_QB, NA_HEADS, HEAD_DIM)
        s_loc = jnp.einsum('bnqhd,bknchd->bhnqkc', q_blk, k_blk).astype(jnp.float32) * NA_SCALE
        bias = jnp.take(rpb_cols, r0 + jnp.arange(kh) - r + NA_KH - 1, axis=1).transpose(0, 2, 3, 1, 4)
        s_loc = jnp.where(col_ok[:, :, None, :], s_loc + bias, MASK_VALUE)
        s_ctx = jnp.einsum('bnqhd,bchd->bhnqc', q_blk, k_ctx).astype(jnp.float32) * NA_SCALE
        s_all = jnp.concatenate([s_loc.reshape(s_loc.shape[:4] + (n_loc,)), s_ctx], axis=-1)
        p = jax.nn.softmax(s_all, axis=-1).astype(v.dtype)
        p_loc = p[..., :n_loc].reshape(s_loc.shape)
        o = jnp.einsum('bhnqkc,bknchd->bnqhd', p_loc, v_blk) + jnp.einsum('bhnqc,bchd->bnqhd', p[..., n_loc:], v_ctx)
        return o.reshape(b, GRID_W, NA_HEADS, HEAD_DIM)

    out = lax.map(one_row, jnp.arange(rows))
    return jnp.moveaxis(out, 0, 1).reshape(b, n_tok, NA_HEADS * HEAD_DIM)


def swiglu(h, w_gate, w_up, w_down):
    return (jax.nn.silu(h @ w_gate) * (h @ w_up)) @ w_down


def trunk_layer(x, ctx, c, c_ctx, w_ada, b_ada, g_attn_pre, g_attn_post, g_ffn_pre, g_ffn_post, w_in,
                hy_conv_w, hy_conv_b, hy_f_w1, hy_f_b1, hy_f_w2, hy_f_b2, hy_f_w3, hy_f_freq, hy_bias,
                mla_g_q, mla_w_uq, mla_g_kv, mla_w_ukv, na_rpb, w_out, w_ffn_gate, w_ffn_up, w_ffn_down, ctx_out):
    b, n = x.shape[0], x.shape[1]
    n_ctx = ctx.shape[1]
    mx = [m[:, None, :] for m in jnp.split(jax.nn.silu(c) @ w_ada + b_ada, 6, axis=-1)]
    mc = [m[None, None, :] for m in jnp.split(jax.nn.silu(c_ctx) @ w_ada + b_ada, 6, axis=-1)]

    hx, cqx, ckvx, krx, nax = split_in(modulate(x, g_attn_pre, mx[0], mx[1]) @ w_in)
    hc, cqc, ckvc, krc, nac = split_in(modulate(ctx, g_attn_pre, mc[0], mc[1]) @ w_in)

    cos, sin = axial_rope(n)
    qn_x, qr_x = mla_queries(cqx, mla_g_q, mla_w_uq)
    qr_x = apply_rope(qr_x, cos[:, None, :], sin[:, None, :])
    kn_x, v_x = mla_keys(ckvx, mla_g_kv, mla_w_ukv)
    kr_x = apply_rope(krx, cos, sin)
    kn_c, v_c = mla_keys(ckvc, mla_g_kv, mla_w_ukv)
    mla_x = mla_latent(qn_x, qr_x, jnp.concatenate([kn_x, kn_c], axis=1), jnp.concatenate([kr_x, krc], axis=1),
                       jnp.concatenate([v_x, v_c], axis=1))

    q_na, k_na, v_na = [t.reshape(b, n, NA_HEADS, HEAD_DIM) for t in jnp.split(nax, 3, axis=-1)]
    qc_na, kc_na, vc_na = [t.reshape(b, n_ctx, NA_HEADS, HEAD_DIM) for t in jnp.split(nac, 3, axis=-1)]
    na_x = natten_latent(q_na, k_na, v_na, kc_na, vc_na, na_rpb)

    hy_args = (hy_conv_w, hy_conv_b, hy_f_w1, hy_f_b1, hy_f_w2, hy_f_b2, hy_f_w3, hy_f_freq, hy_bias)
    hy_x = hyena_mix(hx, *hy_args)

    y = jnp.concatenate([hy_x, mla_x, na_x], axis=-1) @ w_out
    x_new = x + mx[2] * rmsnorm(y, g_attn_post)
    x_new = x_new + mx[5] * rmsnorm(swiglu(modulate(x_new, g_ffn_pre, mx[3], mx[4]), w_ffn_gate, w_ffn_up, w_ffn_down), g_ffn_post)

    if ctx_out:
        qn_c, qr_c = mla_queries(cqc, mla_g_q, mla_w_uq)
        mla_c = mla_attend(qn_c, qr_c, kn_c, krc, v_c).reshape(b, n_ctx, MLA_HEADS * MLA_V)
        na_c = dense_attend(qc_na, kc_na, vc_na, NA_SCALE).reshape(b, n_ctx, NA_HEADS * HEAD_DIM)
        hy_c = hyena_mix(hc, *hy_args)
        yc = jnp.concatenate([hy_c, mla_c, na_c], axis=-1) @ w_out
        ctx = ctx + mc[2] * rmsnorm(yc, g_attn_post)
        ctx = ctx + mc[5] * rmsnorm(swiglu(modulate(ctx, g_ffn_pre, mc[3], mc[4]), w_ffn_gate, w_ffn_up, w_ffn_down), g_ffn_post)
    return x_new, ctx


def setup_inputs(seed: int = 0) -> dict:
    key = jax.random.key(seed)
    ks = iter(jax.random.split(key, 32))
    D = D_MODEL

    def nrm(shape, scale):
        return scale * jax.random.normal(next(ks), shape, jnp.float32)

    def gain(shape):
        return 1.0 + nrm(shape, 0.05)

    return {
        'x': nrm((BATCH, SEQ, D), 1.0),
        'c': nrm((BATCH, D), 1.0),
        'ctx': nrm((BATCH, CTX_LEN, D), 1.0),
        'c_ctx': nrm((D,), 1.0),
        'w_ada': nrm((DEPTH, D, 6 * D), 0.5 * D ** -0.5),
        'b_ada': nrm((DEPTH, 6 * D), 0.01),
        'g_attn_pre': gain((DEPTH, D)),
        'g_attn_post': gain((DEPTH, D)),
        'g_ffn_pre': gain((DEPTH, D)),
        'g_ffn_post': gain((DEPTH, D)),
        'w_in': nrm((DEPTH, D, IN_COLS), D ** -0.5),
        'hy_conv_w': nrm((DEPTH, HY_SHORT, 3 * HY_WIDTH), HY_SHORT ** -0.5),
        'hy_conv_b': nrm((DEPTH, 3 * HY_WIDTH), 0.01),
        'hy_f_w1': nrm((DEPTH, HY_POS_EMB, HY_FILTER_HIDDEN), HY_POS_EMB ** -0.5),
        'hy_f_b1': nrm((DEPTH, HY_FILTER_HIDDEN), 0.1),
        'hy_f_w2': nrm((DEPTH, HY_FILTER_HIDDEN, HY_FILTER_HIDDEN), HY_FILTER_HIDDEN ** -0.5),
        'hy_f_b2': nrm((DEPTH, HY_FILTER_HIDDEN), 0.1),
        'hy_f_w3': nrm((DEPTH, HY_FILTER_HIDDEN, 2 * HY_WIDTH), 0.02),
        'hy_f_freq': gain((DEPTH, HY_FILTER_HIDDEN)),
        'hy_bias': nrm((DEPTH, HY_WIDTH), 0.5),
        'mla_g_q': gain((DEPTH, Q_LORA)),
        'mla_w_uq': nrm((DEPTH, Q_LORA, MLA_HEADS * (MLA_NOPE + MLA_ROPE)), Q_LORA ** -0.5),
        'mla_g_kv': gain((DEPTH, KV_LORA)),
        'mla_w_ukv': nrm((DEPTH, KV_LORA, MLA_HEADS * (MLA_NOPE + MLA_V)), KV_LORA ** -0.5),
        'na_rpb': nrm((DEPTH, NA_HEADS, 2 * NA_KH - 1, 2 * NA_KW - 1), 0.1),
        'w_out': nrm((DEPTH, D, D), D ** -0.5),
        'w_ffn_gate': nrm((DEPTH, D, FFN_HIDDEN), D ** -0.5),
        'w_ffn_up': nrm((DEPTH, D, FFN_HIDDEN), D ** -0.5),
        'w_ffn_down': nrm((DEPTH, FFN_HIDDEN, D), FFN_HIDDEN ** -0.5),
    }


def reference(x, c, ctx, c_ctx, w_ada, b_ada, g_attn_pre, g_attn_post, g_ffn_pre, g_ffn_post, w_in,
              hy_conv_w, hy_conv_b, hy_f_w1, hy_f_b1, hy_f_w2, hy_f_b2, hy_f_w3, hy_f_freq, hy_bias,
              mla_g_q, mla_w_uq, mla_g_kv, mla_w_ukv, na_rpb, w_out, w_ffn_gate, w_ffn_up, w_ffn_down):
    for l in range(DEPTH):
        x, ctx = trunk_layer(x, ctx, c, c_ctx, w_ada[l], b_ada[l], g_attn_pre[l], g_attn_post[l], g_ffn_pre[l], g_ffn_post[l],
                             w_in[l], hy_conv_w[l], hy_conv_b[l], hy_f_w1[l], hy_f_b1[l], hy_f_w2[l], hy_f_b2[l], hy_f_w3[l],
                             hy_f_freq[l], hy_bias[l], mla_g_q[l], mla_w_uq[l], mla_g_kv[l], mla_w_ukv[l], na_rpb[l],
                             w_out[l], w_ffn_gate[l], w_ffn_up[l], w_ffn_down[l], l < DEPTH - 1)
    return x
```

```python
import functools
import math

import jax
import jax.numpy as jnp
import numpy as np
from jax import lax
from jax.experimental import pallas as pl
from jax.experimental.pallas import tpu as pltpu

F32 = jnp.float32
BF16 = jnp.bfloat16

D_MODEL = 2048
BATCH = 4
SEQ = 2048
DEPTH = 2
GRID_W = 64
CTX_LEN = 256
HEAD_DIM = 128
HY_WIDTH = D_MODEL // 4
HY_FILTER_HIDDEN = 64
HY_POS_BANDS = 16
HY_DECAY_TARGET = 1e-2
HY_FAST_DECAY = 0.3
HY_SLOW_DECAY = 1.5
MLA_HEADS = (D_MODEL // 2) // HEAD_DIM
MLA_ROPE = 64
Q_LORA = 3 * D_MODEL // 8
KV_LORA = D_MODEL // 4
MLA_SCALE = (HEAD_DIM + MLA_ROPE) ** -0.5
NA_HEADS = (D_MODEL // 4) // HEAD_DIM
NA_KH = 8
NA_KW = 16
NA_SCALE = HEAD_DIM ** -0.5
FFN_HIDDEN = ((8 * D_MODEL + 3 * 256 - 1) // (3 * 256)) * 256
ROPE_THETA = 10000.0
RMS_EPS = 1e-6
MASK_VALUE = -1e30

N_LAT = BATCH * SEQ
N_CTX = BATCH * CTX_LEN
N_TOK = N_LAT + N_CTX

COL_HY = 0
COL_NA = 3 * HY_WIDTH
COL_CKV = COL_NA + 3 * NA_HEADS * HEAD_DIM
COL_KR = COL_CKV + KV_LORA
COL_CQ = COL_KR + 256
P_COLS = COL_CQ + Q_LORA
MLA_QK = 256

NA_TR = 4
NA_WR = 12
NA_TQ = NA_TR * GRID_W
NA_TK = NA_WR * GRID_W

VMEM_LIMIT = 52 * 1024 * 1024


def _cparams(sem):
    return pltpu.CompilerParams(dimension_semantics=sem, vmem_limit_bytes=VMEM_LIMIT)


def _dot(a, b):
    return jnp.dot(a, b, preferred_element_type=F32)


def _dot_nt(a, b):
    return lax.dot_general(a, b, (((1,), (1,)), ((), ())), preferred_element_type=F32)


def _dot_hi(a, b):
    return jnp.dot(a, b, preferred_element_type=F32, precision=lax.Precision.HIGHEST)


def _rms(x, g):
    ms = jnp.mean(x * x, axis=-1, keepdims=True)
    return x * lax.rsqrt(ms + RMS_EPS) * g


def _mod_row(tm):
    n_lat, per_b = N_LAT // tm, SEQ // tm
    return lambda i: jnp.where(i < n_lat, i // per_b, BATCH)


def _rope_row(tm):
    n_lat, per_b = N_LAT // tm, SEQ // tm
    return lambda i: jnp.where(i < n_lat, i % per_b, per_b + i - n_lat)


def _ada_kernel(c_ref, w_ref, b_ref, o_ref):
    a = c_ref[...]
    a = a * jax.nn.sigmoid(a)
    o_ref[0] = _dot(a.astype(BF16), w_ref[0].astype(BF16)) + b_ref[0]


def _ada(cc, w_ada, b_ada):
    tn = 1024
    n = w_ada.shape[-1]
    return pl.pallas_call(
        _ada_kernel,
        grid=(DEPTH, n // tn),
        in_specs=[
            pl.BlockSpec((8, D_MODEL), lambda l, j: (0, 0)),
            pl.BlockSpec((1, D_MODEL, tn), lambda l, j: (l, 0, j)),
            pl.BlockSpec((1, 1, tn), lambda l, j: (l, 0, j)),
        ],
        out_specs=pl.BlockSpec((1, 8, tn), lambda l, j: (l, 0, j)),
        out_shape=jax.ShapeDtypeStruct((DEPTH, 8, n), F32),
        compiler_params=_cparams(("parallel", "parallel")),
        name="ada",
    )(cc, w_ada, b_ada.reshape(DEPTH, 1, n))


def _norm_mod_to(xn_ref, x_ref, g_ref, sh_ref, sc_ref, chunk=256):
    g = g_ref[...]
    sc = 1.0 + sc_ref[0]
    sh = sh_ref[0]

    def body(r, carry):
        rows = pl.ds(pl.multiple_of(r * chunk, chunk), chunk)
        xn_ref[rows, :] = (_rms(x_ref[rows, :], g) * sc + sh).astype(BF16)
        return carry

    lax.fori_loop(0, x_ref.shape[0] // chunk, body, 0)


def _in_kernel(x_ref, g_ref, sh_ref, sc_ref, w_ref, o_ref, xn_ref):
    @pl.when(pl.program_id(1) == 0)
    def _():
        _norm_mod_to(xn_ref, x_ref, g_ref, sh_ref, sc_ref)

    o_ref[...] = _dot(xn_ref[...], w_ref[...]).astype(o_ref.dtype)


def _swiglu_kernel(x_ref, g_ref, sh_ref, sc_ref, wg_ref, wu_ref, o_ref, xn_ref):
    @pl.when(pl.program_id(1) == 0)
    def _():
        _norm_mod_to(xn_ref, x_ref, g_ref, sh_ref, sc_ref)

    xn = xn_ref[...]
    gate = _dot(xn, wg_ref[...])
    up = _dot(xn, wu_ref[...])
    o_ref[...] = (gate * jax.nn.sigmoid(gate) * up).astype(o_ref.dtype)


def _mod_specs(tm, shift_chunk, scale_chunk):
    row = _mod_row(tm)
    return [
        pl.BlockSpec((1, D_MODEL), lambda i, j: (0, 0)),
        pl.BlockSpec((1, 1, D_MODEL), lambda i, j: (row(i), 0, shift_chunk)),
        pl.BlockSpec((1, 1, D_MODEL), lambda i, j: (row(i), 0, scale_chunk)),
    ]


def _in_proj(s, g, mods, w, *, tm=1024, tn=768):
    m = s.shape[0]
    n = w.shape[1]
    return pl.pallas_call(
        _in_kernel,
        grid=(m // tm, n // tn),
        in_specs=[pl.BlockSpec((tm, D_MODEL), lambda i, j: (i, 0))]
        + _mod_specs(tm, 0, 1)
        + [pl.BlockSpec((D_MODEL, tn), lambda i, j: (0, j))],
        out_specs=pl.BlockSpec((tm, tn), lambda i, j: (i, j)),
        out_shape=jax.ShapeDtypeStruct((m, n), BF16),
        scratch_shapes=[pltpu.VMEM((tm, D_MODEL), BF16)],
        compiler_params=_cparams(("parallel", "arbitrary")),
        name="in_proj",
    )(s, g.reshape(1, D_MODEL), mods, mods, w)


def _ffn_up(s, m_rows, g, mods, wg, wu, *, tm=1024, tn=512):
    n = wg.shape[1]
    w_spec = pl.BlockSpec((D_MODEL, tn), lambda i, j: (0, j))
    return pl.pallas_call(
        _swiglu_kernel,
        grid=(m_rows // tm, n // tn),
        in_specs=[pl.BlockSpec((tm, D_MODEL), lambda i, j: (i, 0))] + _mod_specs(tm, 3, 4) + [w_spec, w_spec],
        out_specs=pl.BlockSpec((tm, tn), lambda i, j: (i, j)),
        out_shape=jax.ShapeDtypeStruct((m_rows, n), BF16),
        scratch_shapes=[pltpu.VMEM((tm, D_MODEL), BF16)],
        compiler_params=_cparams(("parallel", "arbitrary")),
        name="ffn_up",
    )(s, g.reshape(1, D_MODEL), mods, mods, wg, wu)


def _post_kernel(*refs, blocks):
    n_act = len(blocks)
    act_refs = refs[:n_act]
    w_ref, g_ref, gate_ref, res_ref, o_ref, acc_ref = refs[n_act:]
    k = pl.program_id(1)
    n_k = pl.num_programs(1)

    @pl.when(k == 0)
    def _():
        acc_ref[...] = jnp.zeros_like(acc_ref)

    start = 0
    for a_ref, nb in zip(act_refs, blocks):
        @pl.when((k >= start) & (k < start + nb))
        def _(a_ref=a_ref):
            acc_ref[...] += _dot(a_ref[...], w_ref[...])

        start += nb

    @pl.when(k == n_k - 1)
    def _():
        o_ref[...] = res_ref[...] + gate_ref[0] * _rms(acc_ref[...], g_ref[...])


def _proj_post(acts, w, g, mods, gate_chunk, res, m_rows, *, tm=512, tk=512):
    blocks = tuple(a.shape[1] // tk for a in acts)
    starts = tuple(int(v) for v in np.cumsum((0,) + blocks[:-1]))
    row = _mod_row(tm)

    def act_spec(start, nb):
        return pl.BlockSpec((tm, tk), lambda i, k: (i, jnp.clip(k - start, 0, nb - 1)))

    return pl.pallas_call(
        functools.partial(_post_kernel, blocks=blocks),
        grid=(m_rows // tm, sum(blocks)),
        in_specs=[act_spec(st, nb) for st, nb in zip(starts, blocks)]
        + [
            pl.BlockSpec((tk, D_MODEL), lambda i, k: (k, 0)),
            pl.BlockSpec((1, D_MODEL), lambda i, k: (0, 0)),
            pl.BlockSpec((1, 1, D_MODEL), lambda i, k: (row(i), 0, gate_chunk)),
            pl.BlockSpec((tm, D_MODEL), lambda i, k: (i, 0)),
        ],
        out_specs=pl.BlockSpec((tm, D_MODEL), lambda i, k: (i, 0)),
        out_shape=jax.ShapeDtypeStruct((m_rows, D_MODEL), F32),
        scratch_shapes=[pltpu.VMEM((tm, D_MODEL), F32)],
        compiler_params=_cparams(("parallel", "arbitrary")),
        name="proj_post",
    )(*acts, w, g.reshape(1, D_MODEL), mods, res)


def _rope128(r, cos_ref, sa_ref, sb_ref):
    return r * cos_ref[...] + pltpu.roll(r, 96, 1) * sa_ref[...] + pltpu.roll(r, 32, 1) * sb_ref[...]


def _q_kernel(x_ref, g_ref, w_ref, cos_ref, sa_ref, sb_ref, o_ref, xn_ref):
    @pl.when(pl.program_id(1) == 0)
    def _():
        xn_ref[...] = _rms(x_ref[...].astype(F32), g_ref[...]).astype(BF16)

    acc = _dot(xn_ref[...], w_ref[...]) * MLA_SCALE
    o_ref[:, :HEAD_DIM] = acc[:, :HEAD_DIM].astype(o_ref.dtype)
    o_ref[:, HEAD_DIM:] = _rope128(acc[:, HEAD_DIM:], cos_ref, sa_ref, sb_ref).astype(o_ref.dtype)


def _q_proj(p, m_rows, g, w, tabs, *, tm=1024):
    rope = _rope_row(tm)
    tab_spec = pl.BlockSpec((tm, HEAD_DIM), lambda i, h: (rope(i), 0))
    return pl.pallas_call(
        _q_kernel,
        grid=(m_rows // tm, MLA_HEADS),
        in_specs=[
            pl.BlockSpec((tm, Q_LORA), lambda i, h: (i, COL_CQ // Q_LORA)),
            pl.BlockSpec((1, Q_LORA), lambda i, h: (0, 0)),
            pl.BlockSpec((Q_LORA, MLA_QK), lambda i, h: (0, h)),
            tab_spec, tab_spec, tab_spec,
        ],
        out_specs=pl.BlockSpec((tm, MLA_QK), lambda i, h: (i, h)),
        out_shape=jax.ShapeDtypeStruct((m_rows, MLA_HEADS * MLA_QK), BF16),
        scratch_shapes=[pltpu.VMEM((tm, Q_LORA), BF16)],
        compiler_params=_cparams(("parallel", "arbitrary")),
        name="q_proj",
    )(p, g.reshape(1, Q_LORA), w, *tabs)


def _kv_kernel(x_ref, kr_ref, g_ref, w_ref, cos_ref, sa_ref, sb_ref, k_ref, v_ref, xn_ref, krr_ref):
    @pl.when(pl.program_id(1) == 0)
    def _():
        xn_ref[...] = _rms(x_ref[...].astype(F32), g_ref[...]).astype(BF16)
        krr_ref[...] = _rope128(kr_ref[...].astype(F32), cos_ref, sa_ref, sb_ref).astype(BF16)

    acc = _dot(xn_ref[...], w_ref[...])
    k_ref[:, :HEAD_DIM] = acc[:, :HEAD_DIM].astype(k_ref.dtype)
    k_ref[:, HEAD_DIM:] = krr_ref[...]
    v_ref[...] = acc[:, HEAD_DIM:].astype(v_ref.dtype)


def _kv_proj(p, g, w, tabs, *, tm=1024):
    m = p.shape[0]
    rope = _rope_row(tm)
    tab_spec = pl.BlockSpec((tm, HEAD_DIM), lambda i, h: (rope(i), 0))
    return pl.pallas_call(
        _kv_kernel,
        grid=(m // tm, MLA_HEADS),
        in_specs=[
            pl.BlockSpec((tm, KV_LORA), lambda i, h: (i, COL_CKV // KV_LORA)),
            pl.BlockSpec((tm, HEAD_DIM), lambda i, h: (i, COL_KR // HEAD_DIM)),
            pl.BlockSpec((1, KV_LORA), lambda i, h: (0, 0)),
            pl.BlockSpec((KV_LORA, 2 * HEAD_DIM), lambda i, h: (0, h)),
            tab_spec, tab_spec, tab_spec,
        ],
        out_specs=[
            pl.BlockSpec((tm, MLA_QK), lambda i, h: (i, h)),
            pl.BlockSpec((tm, HEAD_DIM), lambda i, h: (i, h)),
        ],
        out_shape=[
            jax.ShapeDtypeStruct((m, MLA_HEADS * MLA_QK), BF16),
            jax.ShapeDtypeStruct((m, MLA_HEADS * HEAD_DIM), BF16),
        ],
        scratch_shapes=[pltpu.VMEM((tm, KV_LORA), BF16), pltpu.VMEM((tm, HEAD_DIM), BF16)],
        compiler_params=_cparams(("parallel", "arbitrary")),
        name="kv_proj",
    )(p, p, g.reshape(1, KV_LORA), w, *tabs)


def _attn_kernel(*refs, scale, two):
    if two:
        q_ref, k1_ref, v1_ref, k2_ref, v2_ref, o_ref = refs
    else:
        q_ref, k1_ref, v1_ref, o_ref = refs
    q = q_ref[...]
    s1 = _dot_nt(q, k1_ref[...])
    if scale != 1.0:
        s1 = s1 * scale
    m = jnp.max(s1, axis=-1, keepdims=True)
    if two:
        s2 = _dot_nt(q, k2_ref[...])
        if scale != 1.0:
            s2 = s2 * scale
        m = jnp.maximum(m, jnp.max(s2, axis=-1, keepdims=True))
    p1 = jnp.exp(s1 - m)
    den = jnp.sum(p1, axis=-1, keepdims=True)
    o = _dot(p1.astype(BF16), v1_ref[...])
    if two:
        p2 = jnp.exp(s2 - m)
        den = den + jnp.sum(p2, axis=-1, keepdims=True)
        o = o + _dot(p2.astype(BF16), v2_ref[...])
    o_ref[...] = (o / den).astype(o_ref.dtype)


def _mla_latent(q, k, v, *, tq=512):
    nq = SEQ // tq
    ctx0 = N_LAT // CTX_LEN
    return pl.pallas_call(
        functools.partial(_attn_kernel, scale=1.0, two=True),
        grid=(BATCH, MLA_HEADS, nq),
        in_specs=[
            pl.BlockSpec((tq, MLA_QK), lambda b, h, i: (b * nq + i, h)),
            pl.BlockSpec((SEQ, MLA_QK), lambda b, h, i: (b, h)),
            pl.BlockSpec((SEQ, HEAD_DIM), lambda b, h, i: (b, h)),
            pl.BlockSpec((CTX_LEN, MLA_QK), lambda b, h, i: (ctx0 + b, h)),
            pl.BlockSpec((CTX_LEN, HEAD_DIM), lambda b, h, i: (ctx0 + b, h)),
        ],
        out_specs=pl.BlockSpec((tq, HEAD_DIM), lambda b, h, i: (b * nq + i, h)),
        out_shape=jax.ShapeDtypeStruct((N_LAT, MLA_HEADS * HEAD_DIM), BF16),
        compiler_params=_cparams(("parallel", "parallel", "arbitrary")),
        name="mla_latent",
    )(q, k, v, k, v)


def _ctx_attend(q, k, v, q_col0, k_col0, v_col0, heads, dqk, scale, name):
    ctx0 = N_LAT // CTX_LEN
    return pl.pallas_call(
        functools.partial(_attn_kernel, scale=scale, two=False),
        grid=(BATCH, heads),
        in_specs=[
            pl.BlockSpec((CTX_LEN, dqk), lambda b, h: (ctx0 + b, q_col0 + h)),
            pl.BlockSpec((CTX_LEN, dqk), lambda b, h: (ctx0 + b, k_col0 + h)),
            pl.BlockSpec((CTX_LEN, HEAD_DIM), lambda b, h: (ctx0 + b, v_col0 + h)),
        ],
        out_specs=pl.BlockSpec((CTX_LEN, HEAD_DIM), lambda b, h: (b, h)),
        out_shape=jax.ShapeDtypeStruct((N_CTX, heads * HEAD_DIM), BF16),
        compiler_params=_cparams(("parallel", "parallel")),
        name=name,
    )(q, k, v)


def _na_plan():
    rows = SEQ // GRID_W
    invalid = 2 * NA_KH - 1
    pairs, plan, starts = [], [], []
    for t in range(rows // NA_TR):
        kw0 = int(np.clip(NA_TR * t - NA_KH // 2, 0, rows - NA_WR))
        starts.append(kw0)
        tile = []
        for ri in range(NA_TR):
            r = NA_TR * t + ri
            r0 = int(np.clip(r - NA_KH // 2, 0, rows - NA_KH))
            assert kw0 <= r0 and r0 + NA_KH <= kw0 + NA_WR
            row = []
            for kp in range(NA_WR // 2):
                pair = []
                for kr in (kw0 + 2 * kp, kw0 + 2 * kp + 1):
                    pair.append(kr - r + NA_KH - 1 if r0 <= kr < r0 + NA_KH else invalid)
                pair = tuple(pair)
                if pair not in pairs:
                    pairs.append(pair)
                row.append(pairs.index(pair))
            tile.append(row)
        plan.append(tile)
    return starts, plan, pairs


def _na_bias_pairs(rpb, pairs):
    c = np.arange(GRID_W)
    c0 = np.clip(c - NA_KW // 2, 0, GRID_W - NA_KW)
    col_ok = (c[None, :] >= c0[:, None]) & (c[None, :] < c0[:, None] + NA_KW)
    col_idx = np.clip(c[None, :] - c[:, None] + NA_KW - 1, 0, 2 * NA_KW - 2)
    t = jnp.where(col_ok[None, None], rpb.astype(F32)[:, :, col_idx], MASK_VALUE)
    t = jnp.concatenate([t, jnp.full((NA_HEADS, 1, GRID_W, GRID_W), MASK_VALUE, F32)], axis=1)
    first = np.array([p[0] for p in pairs])
    second = np.array([p[1] for p in pairs])
    return jnp.concatenate([t[:, first], t[:, second]], axis=-1)


def _na_kernel(q_ref, k_ref, v_ref, kc_ref, vc_ref, t2_ref, o_ref, *, starts, plan):
    kc = kc_ref[...]
    vc = vc_ref[...]
    for t, (kw0, tile) in enumerate(zip(starts, plan)):
        q = q_ref[t * NA_TQ:(t + 1) * NA_TQ, :]
        kw = k_ref[kw0 * GRID_W:kw0 * GRID_W + NA_TK, :]
        vw = v_ref[kw0 * GRID_W:kw0 * GRID_W + NA_TK, :]
        bias = jnp.concatenate(
            [jnp.concatenate([t2_ref[0, idx] for idx in row], axis=1) for row in tile], axis=0)
        s = _dot_nt(q, kw) * NA_SCALE + bias
        sc = _dot_nt(q, kc) * NA_SCALE
        m = jnp.maximum(jnp.max(s, axis=-1, keepdims=True), jnp.max(sc, axis=-1, keepdims=True))
        p = jnp.exp(s - m)
        pc = jnp.exp(sc - m)
        den = jnp.sum(p, axis=-1, keepdims=True) + jnp.sum(pc, axis=-1, keepdims=True)
        o = _dot(p.astype(BF16), vw) + _dot(pc.astype(BF16), vc)
        o_ref[t * NA_TQ:(t + 1) * NA_TQ, :] = (o / den).astype(o_ref.dtype)


def _na_latent(p, t2, starts, plan):
    ctx0 = N_LAT // CTX_LEN
    cq = COL_NA // HEAD_DIM
    ck = cq + NA_HEADS
    cv = ck + NA_HEADS
    n_pairs = t2.shape[1]
    return pl.pallas_call(
        functools.partial(_na_kernel, starts=starts, plan=plan),
        grid=(NA_HEADS, BATCH),
        in_specs=[
            pl.BlockSpec((SEQ, HEAD_DIM), lambda h, b: (b, cq + h)),
            pl.BlockSpec((SEQ, HEAD_DIM), lambda h, b: (b, ck + h)),
            pl.BlockSpec((SEQ, HEAD_DIM), lambda h, b: (b, cv + h)),
            pl.BlockSpec((CTX_LEN, HEAD_DIM), lambda h, b: (ctx0 + b, ck + h)),
            pl.BlockSpec((CTX_LEN, HEAD_DIM), lambda h, b: (ctx0 + b, cv + h)),
            pl.BlockSpec((1, n_pairs, GRID_W, 2 * GRID_W), lambda h, b: (h, 0, 0, 0)),
        ],
        out_specs=pl.BlockSpec((SEQ, HEAD_DIM), lambda h, b: (b, h)),
        out_shape=jax.ShapeDtypeStruct((N_LAT, NA_HEADS * HEAD_DIM), BF16),
        compiler_params=_cparams(("parallel", "parallel")),
        name="na_latent",
    )(p, p, p, p, p, t2)


def _conv_kernel(v_ref, x1_ref, x2_ref, wv_ref, w1_ref, w2_ref, bv_ref, b1_ref, b2_ref, zin_ref, x2o_ref):
    n = v_ref.shape[0]
    row = lax.broadcasted_iota(jnp.int32, (n, 1), 0)

    def short_conv(p_ref, w_ref, b_ref):
        p = p_ref[...].astype(F32)
        prev = jnp.where(row == 0, 0.0, pltpu.roll(p, 1, 0))
        nxt = jnp.where(row == n - 1, 0.0, pltpu.roll(p, n - 1, 0))
        w = w_ref[...]
        return prev * w[0:1] + p * w[1:2] + nxt * w[2:3] + b_ref[...]

    zin_ref[...] = (short_conv(x1_ref, w1_ref, b1_ref) * short_conv(v_ref, wv_ref, bv_ref)).astype(zin_ref.dtype)
    x2o_ref[...] = short_conv(x2_ref, w2_ref, b2_ref).astype(x2o_ref.dtype)


def _hy_conv(p, conv_w, conv_b, n, row_block0, *, tc=256):
    nc = HY_WIDTH // tc

    def seg(s):
        return (pl.BlockSpec((n, tc), lambda b, j: (row_block0 + b, s * nc + j)),
                pl.BlockSpec((3, tc), lambda b, j: (0, s * nc + j)),
                pl.BlockSpec((1, tc), lambda b, j: (0, s * nc + j)))

    (pv, wv, bv), (p1, w1, b1), (p2, w2, b2) = seg(0), seg(1), seg(2)
    out_spec = pl.BlockSpec((n, tc), lambda b, j: (b, j))
    out = jax.ShapeDtypeStruct((BATCH * n, HY_WIDTH), BF16)
    return pl.pallas_call(
        _conv_kernel,
        grid=(BATCH, nc),
        in_specs=[pv, p1, p2, wv, w1, w2, bv, b1, b2],
        out_specs=[out_spec, out_spec],
        out_shape=[out, out],
        compiler_params=_cparams(("parallel", "parallel")),
        name="hy_conv",
    )(p, p, p, conv_w, conv_w, conv_w, conv_b.reshape(1, -1), conv_b.reshape(1, -1), conv_b.reshape(1, -1))


def _filt_kernel(z_ref, t_ref, dl_ref, w1_ref, b1_ref, w2_ref, b2_ref, w3_ref, fr_ref, hs_ref, ha_ref, kn_ref):
    n = z_ref.shape[0]
    fr = fr_ref[...]
    h = jnp.sin(fr * (_dot_hi(z_ref[...], w1_ref[...]) + b1_ref[...]))
    h = jnp.sin(fr * (_dot_hi(h, w2_ref[...]) + b2_ref[...]))
    h = _dot_hi(h, w3_ref[...])
    decay = jnp.exp(-t_ref[...] * dl_ref[...])
    row = lax.broadcasted_iota(jnp.int32, (n, 1), 0)
    hf = h[:, :HY_WIDTH] * decay
    hb = jnp.where(row == 0, 0.0, h[:, HY_WIDTH:] * decay)
    hs = hf + hb
    hs_ref[...] = hs.astype(hs_ref.dtype)
    ha_ref[...] = (hf - hb).astype(ha_ref.dtype)
    sign = jnp.where((row & 1) == 0, 1.0, -1.0)
    kn_ref[...] = jnp.sum(hs * sign, axis=0, keepdims=True)


def _hy_filter_taps(n, f_w1, f_b1, f_w2, f_b2, f_w3, f_freq):
    pos = jnp.arange(n, dtype=F32)
    t = jnp.linspace(0.0, 1.0, n, dtype=F32)
    bands = jnp.linspace(1e-4, HY_POS_BANDS - 1, HY_POS_BANDS, dtype=F32)
    ang = (2.0 * math.pi / n) * pos[:, None] * bands[None, :]
    z = jnp.concatenate([t[:, None], jnp.cos(ang), -jnp.sin(ang)], axis=-1)
    pad = HY_FILTER_HIDDEN - z.shape[1]
    z = jnp.pad(z, ((0, 0), (0, pad)))
    w1 = jnp.pad(f_w1.astype(F32), ((0, pad), (0, 0)))
    deltas = jnp.abs(jnp.linspace(math.log(HY_DECAY_TARGET) / HY_FAST_DECAY,
                                  math.log(HY_DECAY_TARGET) / HY_SLOW_DECAY, HY_WIDTH, dtype=F32))
    hid = HY_FILTER_HIDDEN
    return pl.pallas_call(
        _filt_kernel,
        out_shape=[
            jax.ShapeDtypeStruct((n, HY_WIDTH), BF16),
            jax.ShapeDtypeStruct((n, HY_WIDTH), BF16),
            jax.ShapeDtypeStruct((1, HY_WIDTH), F32),
        ],
        compiler_params=pltpu.CompilerParams(vmem_limit_bytes=VMEM_LIMIT),
        name="hy_filter",
    )(z, t[:, None], deltas[None, :], w1, f_b1.reshape(1, hid), f_w2, f_b2.reshape(1, hid), f_w3,
      f_freq.reshape(1, hid))


def _dft_tables(n):
    lo = int(round(math.sqrt(n)))
    while n % lo:
        lo -= 1
    hi = n // lo
    f = jnp.arange(n, dtype=jnp.int32)[:, None]
    big = 2 * n

    def ang(tt):
        return ((f * tt) % big).astype(F32) * (2.0 * math.pi / big)

    a = ang(lo * jnp.arange(hi, dtype=jnp.int32)[None, :])
    b = ang(jnp.arange(lo, dtype=jnp.int32)[None, :])
    ca, sa, cb, sb = jnp.cos(a), jnp.sin(a), jnp.cos(b), jnp.sin(b)
    cos = (ca[:, :, None] * cb[:, None, :] - sa[:, :, None] * sb[:, None, :]).reshape(n, n)
    msin = -(sa[:, :, None] * cb[:, None, :] + ca[:, :, None] * sb[:, None, :]).reshape(n, n)
    alt = jnp.where(jnp.arange(n) % 2 == 0, 1.0, -1.0).astype(F32)
    msin = jnp.where(f == 0, alt[None, :], msin)
    return cos.astype(BF16), msin.astype(BF16), msin.T.astype(BF16)


def _dft_filt_kernel(c_ref, s_ref, hs_ref, ha_ref, kr_ref, ki_ref):
    kr_ref[...] = _dot(c_ref[...], hs_ref[...])
    ki_ref[...] = _dot(s_ref[...], ha_ref[...])


def _hy_filter_spectrum(cos, msin, hs, ha, *, tf):
    n = cos.shape[0]
    tf = min(tf, n)
    tab = pl.BlockSpec((tf, n), lambda i: (i, 0))
    taps = pl.BlockSpec((n, HY_WIDTH), lambda i: (0, 0))
    out = pl.BlockSpec((tf, HY_WIDTH), lambda i: (i, 0))
    return pl.pallas_call(
        _dft_filt_kernel,
        grid=(n // tf,),
        in_specs=[tab, tab, taps, taps],
        out_specs=[out, out],
        out_shape=[jax.ShapeDtypeStruct((n, HY_WIDTH), F32)] * 2,
        compiler_params=_cparams(("parallel",)),
        name="hy_filter_dft",
    )(cos, msin, hs, ha)


def _dft_fwd_kernel(c_ref, s_ref, x_ref, kr_ref, ki_ref, kn_ref, yr_ref, yi_ref, *, tf, inv_n):
    x = x_ref[...]
    zr = _dot(c_ref[...], x)
    zi = _dot(s_ref[...], x)
    row = pl.program_id(0) * tf + lax.broadcasted_iota(jnp.int32, (tf, 1), 0)
    bin0 = row == 0
    kr = kr_ref[...]
    ki = jnp.where(bin0, 0.0, ki_ref[...])
    kr_im = jnp.where(bin0, kn_ref[...], kr)
    wt = jnp.where(bin0, inv_n, 2.0 * inv_n)
    yr_ref[0] = ((zr * kr - zi * ki) * wt).astype(yr_ref.dtype)
    yi_ref[0] = ((zr * ki + zi * kr_im) * wt).astype(yi_ref.dtype)


def _hy_dft_fwd(cos, msin, zin, kr, ki, kn, *, tf):
    n = cos.shape[0]
    tf = min(tf, n)
    tab = pl.BlockSpec((tf, n), lambda i, b: (i, 0))
    filt = pl.BlockSpec((tf, HY_WIDTH), lambda i, b: (i, 0))
    out = pl.BlockSpec((1, tf, HY_WIDTH), lambda i, b: (b, i, 0))
    return pl.pallas_call(
        functools.partial(_dft_fwd_kernel, tf=tf, inv_n=1.0 / (2 * n)),
        grid=(n // tf, BATCH),
        in_specs=[tab, tab, pl.BlockSpec((n, HY_WIDTH), lambda i, b: (b, 0)), filt, filt,
                  pl.BlockSpec((1, HY_WIDTH), lambda i, b: (0, 0))],
        out_specs=[out, out],
        out_shape=[jax.ShapeDtypeStruct((BATCH, n, HY_WIDTH), BF16)] * 2,
        compiler_params=_cparams(("parallel", "arbitrary")),
        name="hy_dft_fwd",
    )(cos, msin, zin, kr, ki, kn)


def _dft_inv_kernel(c_ref, st_ref, yr_ref, yi_ref, zin_ref, x2_ref, b_ref, o_ref):
    y = _dot(c_ref[...], yr_ref[0]) + _dot(st_ref[...], yi_ref[0])
    y = y + zin_ref[...].astype(F32) * b_ref[...]
    o_ref[...] = (x2_ref[...].astype(F32) * y).astype(o_ref.dtype)


def _hy_dft_inv(cos, msin_t, yr, yi, zin, x2, bias, *, tt):
    n = cos.shape[0]
    tt = min(tt, n)
    nt = n // tt
    tab = pl.BlockSpec((tt, n), lambda i, b: (i, 0))
    spec = pl.BlockSpec((1, n, HY_WIDTH), lambda i, b: (b, 0, 0))
    rows = pl.BlockSpec((tt, HY_WIDTH), lambda i, b: (b * nt + i, 0))
    return pl.pallas_call(
        _dft_inv_kernel,
        grid=(nt, BATCH),
        in_specs=[tab, tab, spec, spec, rows, rows, pl.BlockSpec((1, HY_WIDTH), lambda i, b: (0, 0))],
        out_specs=rows,
        out_shape=jax.ShapeDtypeStruct((BATCH * n, HY_WIDTH), BF16),
        compiler_params=_cparams(("parallel", "arbitrary")),
        name="hy_dft_inv",
    )(cos, msin_t, yr, yi, zin, x2, bias.reshape(1, HY_WIDTH))


def _hyena(p, n, row_block0, tables, conv_w, conv_b, f_w1, f_b1, f_w2, f_b2, f_w3, f_freq, bias):
    cos, msin, msin_t = tables
    zin, x2 = _hy_conv(p, conv_w, conv_b, n, row_block0)
    hs, ha, kn = _hy_filter_taps(n, f_w1, f_b1, f_w2, f_b2, f_w3, f_freq)
    kr, ki = _hy_filter_spectrum(cos, msin, hs, ha, tf=512)
    yr, yi = _hy_dft_fwd(cos, msin, zin, kr, ki, kn, tf=512)
    return _hy_dft_inv(cos, msin_t, yr, yi, zin, x2, bias, tt=512)


def _rope_tables():
    tok = jnp.arange(SEQ)
    row = (tok // GRID_W).astype(F32)
    col = (tok % GRID_W).astype(F32)
    n_freq = MLA_ROPE // 4
    inv = ROPE_THETA ** (-jnp.arange(n_freq, dtype=F32) / n_freq)
    ang = jnp.concatenate([row[:, None] * inv, col[:, None] * inv], axis=-1)
    cos, sin = jnp.cos(ang), jnp.sin(ang)
    half = MLA_ROPE // 2
    zeros = jnp.zeros((SEQ, half), F32)
    rest = HEAD_DIM - MLA_ROPE
    cos_t = jnp.concatenate([cos, cos, jnp.ones((SEQ, rest), F32)], axis=-1)
    sin_a = jnp.concatenate([-sin, zeros, jnp.zeros((SEQ, rest), F32)], axis=-1)
    sin_b = jnp.concatenate([zeros, sin, jnp.zeros((SEQ, rest), F32)], axis=-1)
    ident = jnp.ones((N_CTX, HEAD_DIM), F32)
    none = jnp.zeros((N_CTX, HEAD_DIM), F32)
    return (jnp.concatenate([cos_t, ident]), jnp.concatenate([sin_a, none]), jnp.concatenate([sin_b, none]))


def _layout_w_in(w):
    hy, cq, ckv, kr, na = jnp.split(w, [int(v) for v in np.cumsum(
        (3 * HY_WIDTH, Q_LORA, KV_LORA, MLA_ROPE))], axis=-1)
    pad = jnp.zeros((D_MODEL, COL_CQ - COL_KR - MLA_ROPE), w.dtype)
    return jnp.concatenate([hy, na, ckv, kr, pad, cq], axis=-1).astype(BF16)


def _layout_w_uq(w):
    w = w.reshape(Q_LORA, MLA_HEADS, HEAD_DIM + MLA_ROPE)
    w = jnp.pad(w, ((0, 0), (0, 0), (0, MLA_QK - HEAD_DIM - MLA_ROPE)))
    return w.reshape(Q_LORA, MLA_HEADS * MLA_QK).astype(BF16)


def kernel(x, c, ctx, c_ctx, w_ada, b_ada, g_attn_pre, g_attn_post, g_ffn_pre, g_ffn_post, w_in, hy_conv_w, hy_conv_b, hy_f_w1, hy_f_b1, hy_f_w2, hy_f_b2, hy_f_w3, hy_f_freq, hy_bias, mla_g_q, mla_w_uq, mla_g_kv, mla_w_ukv, na_rpb, w_out, w_ffn_gate, w_ffn_up, w_ffn_down):
    stream = jnp.concatenate([x.reshape(N_LAT, D_MODEL), ctx.reshape(N_CTX, D_MODEL)], axis=0)
    cc = jnp.concatenate([c, c_ctx[None, :], jnp.zeros((8 - BATCH - 1, D_MODEL), F32)], axis=0)
    mods_all = _ada(cc, w_ada, b_ada)

    rope_tabs = _rope_tables()
    dft_lat = _dft_tables(SEQ)
    dft_ctx = _dft_tables(CTX_LEN)
    na_starts, na_plan, na_pairs = _na_plan()
    ctx0 = N_LAT // CTX_LEN

    for l in range(DEPTH):
        ctx_out = l < DEPTH - 1
        m_rows = N_TOK if ctx_out else N_LAT
        mods = mods_all[l].reshape(8, 1, 6 * D_MODEL)
        p = _in_proj(stream, g_attn_pre[l], mods, _layout_w_in(w_in[l]))

        q = _q_proj(p, m_rows, mla_g_q[l], _layout_w_uq(mla_w_uq[l]), rope_tabs)
        k, v = _kv_proj(p, mla_g_kv[l], mla_w_ukv[l].astype(BF16), rope_tabs)
        mla = _mla_latent(q, k, v)

        na = _na_latent(p, _na_bias_pairs(na_rpb[l], na_pairs), na_starts, na_plan)

        hy_args = (hy_conv_w[l], hy_conv_b[l], hy_f_w1[l], hy_f_b1[l], hy_f_w2[l], hy_f_b2[l], hy_f_w3[l],
                   hy_f_freq[l], hy_bias[l])
        hy = _hyena(p, SEQ, 0, dft_lat, *hy_args)

        if ctx_out:
            mla_c = _ctx_attend(q, k, v, 0, 0, 0, MLA_HEADS, MLA_QK, 1.0, "mla_ctx")
            cq = COL_NA // HEAD_DIM
            na_c = _ctx_attend(p, p, p, cq, cq + NA_HEADS, cq + 2 * NA_HEADS, NA_HEADS, HEAD_DIM, NA_SCALE,
                               "na_ctx")
            hy_c = _hyena(p, CTX_LEN, ctx0, dft_ctx, *hy_args)
            hy = jnp.concatenate([hy, hy_c], axis=0)
            mla = jnp.concatenate([mla, mla_c], axis=0)
            na = jnp.concatenate([na, na_c], axis=0)

        stream = _proj_post([hy, mla, na], w_out[l].astype(BF16), g_attn_post[l], mods, 2, stream, m_rows)
        h = _ffn_up(stream, m_rows, g_ffn_pre[l], mods, w_ffn_gate[l].astype(BF16), w_ffn_up[l].astype(BF16))
        stream = _proj_post([h], w_ffn_down[l].astype(BF16), g_ffn_post[l], mods, 5, stream, m_rows,
                            tk=FFN_HIDDEN // 4)

    return stream[:N_LAT].reshape(BATCH, SEQ, D_MODEL)
```

```python
import functools
import math

import jax
import jax.numpy as jnp
import numpy as np
from jax import lax
from jax.experimental import pallas as pl
from jax.experimental.pallas import tpu as pltpu

F32 = jnp.float32
BF16 = jnp.bfloat16

D_MODEL = 2048
BATCH = 4
SEQ = 2048
DEPTH = 2
GRID_W = 64
CTX_LEN = 256
HEAD_DIM = 128
HY_WIDTH = D_MODEL // 4
HY_FILTER_HIDDEN = 64
HY_POS_BANDS = 16
HY_DECAY_TARGET = 1e-2
HY_FAST_DECAY = 0.3
HY_SLOW_DECAY = 1.5
MLA_HEADS = (D_MODEL // 2) // HEAD_DIM
MLA_ROPE = 64
Q_LORA = 3 * D_MODEL // 8
KV_LORA = D_MODEL // 4
MLA_SCALE = (HEAD_DIM + MLA_ROPE) ** -0.5
NA_HEADS = (D_MODEL // 4) // HEAD_DIM
NA_KH = 8
NA_KW = 16
NA_SCALE = HEAD_DIM ** -0.5
FFN_HIDDEN = ((8 * D_MODEL + 3 * 256 - 1) // (3 * 256)) * 256
ROPE_THETA = 10000.0
RMS_EPS = 1e-6
MASK_VALUE = -1e30
LOG2E = math.log2(math.e)

N_LAT = BATCH * SEQ
N_CTX = BATCH * CTX_LEN
N_TOK = N_LAT + N_CTX
CTX_BLOCK0 = N_LAT // CTX_LEN

COL_HY = 0
COL_CQ = 3 * HY_WIDTH
COL_CKV = COL_CQ + Q_LORA
COL_NA = COL_CKV + KV_LORA
COL_KR = COL_NA + 3 * NA_HEADS * HEAD_DIM
P_COLS = COL_KR + 256
IN_TN = 256
MLA_QK = 256
MLA_VW = 256

NA_TR = 4
NA_WR = 12
NA_TQ = NA_TR * GRID_W
NA_TK = NA_WR * GRID_W

POST_CW = 512
POST_NC = D_MODEL // POST_CW

VMEM_LIMIT = 52 * 1024 * 1024


def _cparams(sem):
    return pltpu.CompilerParams(dimension_semantics=sem, vmem_limit_bytes=VMEM_LIMIT)


def _dot(a, b):
    return jnp.dot(a, b, preferred_element_type=F32)


def _dot_nt(a, b):
    return lax.dot_general(a, b, (((1,), (1,)), ((), ())), preferred_element_type=F32)


def _dot_hi(a, b):
    return jnp.dot(a, b, preferred_element_type=F32, precision=lax.Precision.HIGHEST)


def _rms(x, g):
    ms = jnp.mean(x * x, axis=-1, keepdims=True)
    return x * lax.rsqrt(ms + RMS_EPS) * g


def _mod_row(tm):
    n_lat, per_b = N_LAT // tm, SEQ // tm
    return lambda i: jnp.where(i < n_lat, i // per_b, BATCH)


def _rope_row(tm):
    n_lat, per_b = N_LAT // tm, SEQ // tm
    return lambda i: jnp.where(i < n_lat, i % per_b, per_b + i - n_lat)


def _ada_kernel(c_ref, w_ref, b_ref, o_ref):
    a = c_ref[...]
    a = a * jax.nn.sigmoid(a)
    o_ref[0] = _dot(a.astype(BF16), w_ref[0].astype(BF16)) + b_ref[0]


def _ada(cc, w_ada, b_ada):
    tn = 1024
    n = w_ada.shape[-1]
    return pl.pallas_call(
        _ada_kernel,
        grid=(DEPTH, n // tn),
        in_specs=[
            pl.BlockSpec((8, D_MODEL), lambda l, j: (0, 0)),
            pl.BlockSpec((1, D_MODEL, tn), lambda l, j: (l, 0, j)),
            pl.BlockSpec((1, 1, tn), lambda l, j: (l, 0, j)),
        ],
        out_specs=pl.BlockSpec((1, 8, tn), lambda l, j: (l, 0, j)),
        out_shape=jax.ShapeDtypeStruct((DEPTH, 8, n), F32),
        compiler_params=_cparams(("parallel", "parallel")),
        name="ada",
    )(cc, w_ada, b_ada.reshape(DEPTH, 1, n))


def _norm_mod_to(xn_ref, x_ref, g_ref, sh_ref, sc_ref, chunk=256):
    g = g_ref[...]
    sc = 1.0 + sc_ref[0]
    sh = sh_ref[0]

    def body(r, carry):
        rows = pl.ds(pl.multiple_of(r * chunk, chunk), chunk)
        xn_ref[rows, :] = (_rms(x_ref[rows, :], g) * sc + sh).astype(BF16)
        return carry

    lax.fori_loop(0, x_ref.shape[0] // chunk, body, 0)


def _prenorm_kernel(x_ref, c_ref, g_ref, sh_ref, sc_ref, o_ref, *, n_lat):
    i = pl.program_id(0)

    @pl.when(i < n_lat)
    def _():
        _norm_mod_to(o_ref, x_ref, g_ref, sh_ref, sc_ref)

    @pl.when(i >= n_lat)
    def _():
        _norm_mod_to(o_ref, c_ref, g_ref, sh_ref, sc_ref)


def _prenorm(x2d, ctx2d, g, mods, *, tm=1024):
    n_lat = N_LAT // tm
    row = _mod_row(tm)
    return pl.pallas_call(
        functools.partial(_prenorm_kernel, n_lat=n_lat),
        grid=(N_TOK // tm,),
        in_specs=[
            pl.BlockSpec((tm, D_MODEL), lambda i: (jnp.minimum(i, n_lat - 1), 0)),
            pl.BlockSpec((tm, D_MODEL), lambda i: (jnp.maximum(i - n_lat, 0), 0)),
            pl.BlockSpec((1, D_MODEL), lambda i: (0, 0)),
            pl.BlockSpec((1, 1, D_MODEL), lambda i: (row(i), 0, 0)),
            pl.BlockSpec((1, 1, D_MODEL), lambda i: (row(i), 0, 1)),
        ],
        out_specs=pl.BlockSpec((tm, D_MODEL), lambda i: (i, 0)),
        out_shape=jax.ShapeDtypeStruct((N_TOK, D_MODEL), BF16),
        compiler_params=_cparams(("parallel",)),
        name="prenorm",
    )(x2d, ctx2d, g.reshape(1, D_MODEL), mods, mods)


def _in_kernel(x_ref, wa_ref, wb_ref, o_ref, *, n_a):
    j = pl.program_id(1)

    @pl.when(j < n_a)
    def _():
        o_ref[...] = _dot(x_ref[...], wa_ref[0].astype(BF16)).astype(o_ref.dtype)

    @pl.when(j >= n_a)
    def _():
        o_ref[...] = _dot(x_ref[...], wb_ref[0].astype(BF16)).astype(o_ref.dtype)


def _in_proj(xn, w_in, w_tail, l, *, tm=3072, tn=IN_TN):
    n_a = COL_NA // tn
    n_b = w_tail.shape[-1] // tn
    return pl.pallas_call(
        functools.partial(_in_kernel, n_a=n_a),
        grid=(N_TOK // tm, n_a + n_b),
        in_specs=[
            pl.BlockSpec((tm, D_MODEL), lambda i, j: (i, 0)),
            pl.BlockSpec((1, D_MODEL, tn), lambda i, j: (l, 0, jnp.minimum(j, n_a - 1))),
            pl.BlockSpec((1, D_MODEL, tn), lambda i, j: (l, 0, jnp.maximum(j - n_a, 0))),
        ],
        out_specs=pl.BlockSpec((tm, tn), lambda i, j: (i, j)),
        out_shape=jax.ShapeDtypeStruct((N_TOK, P_COLS), BF16),
        compiler_params=_cparams(("parallel", "arbitrary")),
        name="in_proj",
    )(xn, w_in, w_tail)


def _swiglu_kernel(x_ref, wg_ref, wu_ref, o_ref):
    xn = x_ref[...]
    gate = _dot(xn, wg_ref[0].astype(BF16))
    up = _dot(xn, wu_ref[0].astype(BF16))
    o_ref[...] = (gate * jax.nn.sigmoid(gate) * up).astype(o_ref.dtype)


def _ffn_up(xn, wg, wu, l, *, tn=256):
    m = xn.shape[0]
    tm = 2048 if m % 2048 == 0 else 1536
    w_spec = pl.BlockSpec((1, D_MODEL, tn), lambda i, j: (l, 0, j))
    return pl.pallas_call(
        _swiglu_kernel,
        grid=(m // tm, FFN_HIDDEN // tn),
        in_specs=[pl.BlockSpec((tm, D_MODEL), lambda i, j: (i, 0)), w_spec, w_spec],
        out_specs=pl.BlockSpec((tm, tn), lambda i, j: (i, j)),
        out_shape=jax.ShapeDtypeStruct((m, FFN_HIDDEN), BF16),
        compiler_params=_cparams(("parallel", "arbitrary")),
        name="ffn_up",
    )(xn, wg, wu)


def _post_kernel(*refs, n_act, two_src, blocks, emit_xn, n_lat):
    refs = list(refs)
    acts_lat = [refs.pop(0) for _ in range(n_act)]
    acts_ctx = [refs.pop(0) for _ in range(n_act)] if two_src else None
    w_ref, g_ref, gate_ref, res_lat = (refs.pop(0) for _ in range(4))
    res_ctx = refs.pop(0) if two_src else None
    if emit_xn:
        g2_ref, sh_ref, sc_ref = (refs.pop(0) for _ in range(3))
    o_ref = refs.pop(0)
    xn_ref = refs.pop(0) if emit_xn else None
    acc_ref, rs_ref, ss2_ref = refs

    i = pl.program_id(0)
    k = pl.program_id(1)
    nk = sum(blocks)
    is_lat = i < n_lat

    @pl.when(k == 0)
    def _():
        acc_ref[...] = jnp.zeros_like(acc_ref)

    def accumulate(a_ref):
        a = a_ref[...]
        w = w_ref[0].astype(BF16)
        for c in range(POST_NC):
            acc_ref[c] += _dot(a, w[:, c * POST_CW:(c + 1) * POST_CW])

    start = 0
    for idx, nb in enumerate(blocks):
        in_range = (k >= start) & (k < start + nb)
        if two_src:
            pl.when(in_range & is_lat)(functools.partial(accumulate, acts_lat[idx]))
            pl.when(in_range & jnp.logical_not(is_lat))(functools.partial(accumulate, acts_ctx[idx]))
        else:
            pl.when(in_range)(functools.partial(accumulate, acts_lat[idx]))
        start += nb

    @pl.when(k == nk)
    def _():
        ss = jnp.zeros((acc_ref.shape[1], 1), F32)
        for c in range(POST_NC):
            y = acc_ref[c]
            ss = ss + jnp.sum(y * y, axis=-1, keepdims=True)
        rs_ref[...] = lax.rsqrt(ss * (1.0 / D_MODEL) + RMS_EPS)
        ss2_ref[...] = jnp.zeros_like(ss2_ref)

    def residual(res_ref):
        e = k - nk
        x_new = res_ref[...] + gate_ref[0, 0] * (acc_ref[e] * rs_ref[...] * g_ref[0])
        o_ref[...] = x_new
        if emit_xn:
            acc_ref[e] = x_new
            ss2_ref[...] += jnp.sum(x_new * x_new, axis=-1, keepdims=True)

    in_res = (k >= nk) & (k < nk + POST_NC)
    if two_src:
        pl.when(in_res & is_lat)(functools.partial(residual, res_lat))
        pl.when(in_res & jnp.logical_not(is_lat))(functools.partial(residual, res_ctx))
    else:
        pl.when(in_res)(functools.partial(residual, res_lat))

    if emit_xn:
        @pl.when(k >= nk + POST_NC)
        def _():
            e = k - nk - POST_NC
            rs2 = lax.rsqrt(ss2_ref[...] * (1.0 / D_MODEL) + RMS_EPS)
            y = acc_ref[e] * rs2 * g2_ref[0]
            xn_ref[...] = (y * (1.0 + sc_ref[0, 0]) + sh_ref[0, 0]).astype(xn_ref.dtype)


def _chunked(v):
    return v.reshape(POST_NC, 1, POST_CW)


def _proj_post(acts, w, l, g, mods, gate_chunk, res, m_rows, nxt=None, *, tm=1024, tk=512):
    two_src = res[1] is not None
    n_lat = N_LAT // tm
    blocks = tuple(a.shape[1] // tk for a, _ in acts)
    starts = tuple(int(v) for v in np.cumsum((0,) + blocks[:-1]))
    nk = sum(blocks)
    emit_xn = nxt is not None
    row = _mod_row(tm)

    def lat_row(i):
        return jnp.minimum(i, n_lat - 1) if two_src else i

    def ctx_row(i):
        return jnp.maximum(i - n_lat, 0)

    def e1(k):
        return jnp.clip(k - nk, 0, POST_NC - 1)

    def e2(k):
        return jnp.clip(k - nk - POST_NC, 0, POST_NC - 1)

    def act_specs(rowf):
        return [pl.BlockSpec((tm, tk), lambda i, k, st=st, nb=nb: (rowf(i), jnp.clip(k - st, 0, nb - 1)))
                for st, nb in zip(starts, blocks)]

    def vec_spec(ef):
        return pl.BlockSpec((1, 1, POST_CW), lambda i, k: (ef(k), 0, 0))

    def mod_spec(chunk, ef):
        return pl.BlockSpec((1, 1, 1, POST_CW), lambda i, k: (row(i), chunk * POST_NC + ef(k), 0, 0))

    def mod_view(m):
        return m.reshape(8, 6 * POST_NC, 1, POST_CW)

    in_specs = act_specs(lat_row)
    args = [a for a, _ in acts]
    if two_src:
        in_specs += act_specs(ctx_row)
        args += [c for _, c in acts]
    in_specs += [
        pl.BlockSpec((1, tk, D_MODEL), lambda i, k: (l, jnp.minimum(k, nk - 1), 0)),
        vec_spec(e1),
        mod_spec(gate_chunk, e1),
        pl.BlockSpec((tm, POST_CW), lambda i, k: (lat_row(i), e1(k))),
    ]
    args += [w, _chunked(g), mod_view(mods), res[0]]
    if two_src:
        in_specs.append(pl.BlockSpec((tm, POST_CW), lambda i, k: (ctx_row(i), e1(k))))
        args.append(res[1])
    out_specs = [pl.BlockSpec((tm, POST_CW), lambda i, k: (i, e1(k)))]
    out_shape = [jax.ShapeDtypeStruct((m_rows, D_MODEL), F32)]
    if emit_xn:
        g2, mods2, sh_chunk, sc_chunk = nxt
        in_specs += [vec_spec(e2), mod_spec(sh_chunk, e2), mod_spec(sc_chunk, e2)]
        args += [_chunked(g2), mod_view(mods2), mod_view(mods2)]
        out_specs.append(pl.BlockSpec((tm, POST_CW), lambda i, k: (i, e2(k))))
        out_shape.append(jax.ShapeDtypeStruct((m_rows, D_MODEL), BF16))

    out = pl.pallas_call(
        functools.partial(_post_kernel, n_act=len(acts), two_src=two_src, blocks=blocks, emit_xn=emit_xn,
                          n_lat=n_lat),
        grid=(m_rows // tm, nk + POST_NC * (2 if emit_xn else 1)),
        in_specs=in_specs,
        out_specs=out_specs,
        out_shape=out_shape,
        scratch_shapes=[pltpu.VMEM((POST_NC, tm, POST_CW), F32), pltpu.VMEM((tm, 1), F32), pltpu.VMEM((tm, 1), F32)],
        compiler_params=_cparams(("parallel", "arbitrary")),
        name="proj_post",
    )(*args)
    return (out[0], out[1]) if emit_xn else (out[0], None)


def _rope128(r, cos_ref, sa_ref, sb_ref):
    return r * cos_ref[...] + pltpu.roll(r, 96, 1) * sa_ref[...] + pltpu.roll(r, 32, 1) * sb_ref[...]


def _q_kernel(x_ref, g_ref, w_ref, cos_ref, sa_ref, sb_ref, o_ref):
    xn = _rms(x_ref[...].astype(F32), g_ref[...]).astype(BF16)
    for h in range(MLA_HEADS):
        acc = _dot(xn, w_ref[:, h * MLA_QK:(h + 1) * MLA_QK]) * (MLA_SCALE * LOG2E)
        o_ref[:, h * MLA_QK:h * MLA_QK + HEAD_DIM] = acc[:, :HEAD_DIM].astype(o_ref.dtype)
        o_ref[:, h * MLA_QK + HEAD_DIM:(h + 1) * MLA_QK] = _rope128(
            acc[:, HEAD_DIM:], cos_ref, sa_ref, sb_ref).astype(o_ref.dtype)


def _q_proj(p, m_rows, g, w, tabs, *, tm=1024):
    rope = _rope_row(tm)
    tab_spec = pl.BlockSpec((tm, HEAD_DIM), lambda i: (rope(i), 0))
    return pl.pallas_call(
        _q_kernel,
        grid=(m_rows // tm,),
        in_specs=[
            pl.BlockSpec((tm, Q_LORA), lambda i: (i, COL_CQ // Q_LORA)),
            pl.BlockSpec((1, Q_LORA), lambda i: (0, 0)),
            pl.BlockSpec((Q_LORA, MLA_HEADS * MLA_QK), lambda i: (0, 0)),
            tab_spec, tab_spec, tab_spec,
        ],
        out_specs=pl.BlockSpec((tm, MLA_HEADS * MLA_QK), lambda i: (i, 0)),
        out_shape=jax.ShapeDtypeStruct((m_rows, MLA_HEADS * MLA_QK), BF16),
        compiler_params=_cparams(("parallel",)),
        name="q_proj",
    )(p, g.reshape(1, Q_LORA), w, *tabs)


def _kv_kernel(xa_ref, xb_ref, kr_ref, g_ref, w_ref, cos_ref, sa_ref, sb_ref, k_ref, v_ref):
    half = KV_LORA // 2
    xa = xa_ref[...].astype(F32)
    xb = xb_ref[...].astype(F32)
    ms = (jnp.sum(xa * xa, axis=-1, keepdims=True) + jnp.sum(xb * xb, axis=-1, keepdims=True)) * (1.0 / KV_LORA)
    rs = lax.rsqrt(ms + RMS_EPS)
    g = g_ref[...]
    xna = (xa * rs * g[:, :half]).astype(BF16)
    xnb = (xb * rs * g[:, half:]).astype(BF16)
    krr = _rope128(kr_ref[...].astype(F32), cos_ref, sa_ref, sb_ref).astype(k_ref.dtype)
    ones = jnp.ones((xa.shape[0], MLA_VW - HEAD_DIM), v_ref.dtype)
    for h in range(MLA_HEADS):
        w = w_ref[0, :, h * 2 * HEAD_DIM:(h + 1) * 2 * HEAD_DIM].astype(BF16)
        acc = _dot(xna, w[:half]) + _dot(xnb, w[half:])
        k_ref[:, h * MLA_QK:h * MLA_QK + HEAD_DIM] = acc[:, :HEAD_DIM].astype(k_ref.dtype)
        k_ref[:, h * MLA_QK + HEAD_DIM:(h + 1) * MLA_QK] = krr
        v_ref[:, h * MLA_VW:h * MLA_VW + HEAD_DIM] = acc[:, HEAD_DIM:].astype(v_ref.dtype)
        v_ref[:, h * MLA_VW + HEAD_DIM:(h + 1) * MLA_VW] = ones


def _kv_proj(p, g, w, l, tabs, *, tm=1024):
    m = p.shape[0]
    rope = _rope_row(tm)
    half = KV_LORA // 2
    tab_spec = pl.BlockSpec((tm, HEAD_DIM), lambda i: (rope(i), 0))
    return pl.pallas_call(
        _kv_kernel,
        grid=(m // tm,),
        in_specs=[
            pl.BlockSpec((tm, half), lambda i: (i, COL_CKV // half)),
            pl.BlockSpec((tm, half), lambda i: (i, COL_CKV // half + 1)),
            pl.BlockSpec((tm, HEAD_DIM), lambda i: (i, COL_KR // HEAD_DIM)),
            pl.BlockSpec((1, KV_LORA), lambda i: (0, 0)),
            pl.BlockSpec((1, KV_LORA, MLA_HEADS * 2 * HEAD_DIM), lambda i: (l, 0, 0)),
            tab_spec, tab_spec, tab_spec,
        ],
        out_specs=[
            pl.BlockSpec((tm, MLA_HEADS * MLA_QK), lambda i: (i, 0)),
            pl.BlockSpec((tm, MLA_HEADS * MLA_VW), lambda i: (i, 0)),
        ],
        out_shape=[
            jax.ShapeDtypeStruct((m, MLA_HEADS * MLA_QK), BF16),
            jax.ShapeDtypeStruct((m, MLA_HEADS * MLA_VW), BF16),
        ],
        compiler_params=_cparams(("parallel",)),
        name="kv_proj",
    )(p, p, p, g.reshape(1, KV_LORA), w, *tabs)


def _softmax_pv(s_list, v_list, ones_col):
    m = jnp.max(s_list[0], axis=-1, keepdims=True)
    for s in s_list[1:]:
        m = jnp.maximum(m, jnp.max(s, axis=-1, keepdims=True))
    acc = None
    den = None
    for s, v in zip(s_list, v_list):
        p = jnp.exp2(s - m)
        if not ones_col:
            d = jnp.sum(p, axis=-1, keepdims=True)
            den = d if den is None else den + d
        o = _dot(p.astype(BF16), v)
        acc = o if acc is None else acc + o
    if ones_col:
        return acc[:, :HEAD_DIM] / acc[:, HEAD_DIM:]
    return acc / den


def _attn_kernel(*refs, scale, two, ones_col, chains):
    if two:
        q_ref, k1_ref, v1_ref, k2_ref, v2_ref, o_ref = refs
    else:
        q_ref, k1_ref, v1_ref, o_ref = refs
    tq = q_ref.shape[0] // chains
    for c in range(chains):
        q = q_ref[c * tq:(c + 1) * tq, :]
        s_list = [_dot_nt(q, k1_ref[...])]
        v_list = [v1_ref[...]]
        if two:
            s_list.append(_dot_nt(q, k2_ref[...]))
            v_list.append(v2_ref[...])
        if scale != 1.0:
            s_list = [s * scale for s in s_list]
        o_ref[c * tq:(c + 1) * tq, :] = _softmax_pv(s_list, v_list, ones_col).astype(o_ref.dtype)


def _mla_latent(q, k, v, *, tq=512):
    nq = SEQ // tq
    return pl.pallas_call(
        functools.partial(_attn_kernel, scale=1.0, two=True, ones_col=True, chains=2),
        grid=(BATCH, MLA_HEADS, nq),
        in_specs=[
            pl.BlockSpec((tq, MLA_QK), lambda b, h, i: (b * nq + i, h)),
            pl.BlockSpec((SEQ, MLA_QK), lambda b, h, i: (b, h)),
            pl.BlockSpec((SEQ, MLA_VW), lambda b, h, i: (b, h)),
            pl.BlockSpec((CTX_LEN, MLA_QK), lambda b, h, i: (CTX_BLOCK0 + b, h)),
            pl.BlockSpec((CTX_LEN, MLA_VW), lambda b, h, i: (CTX_BLOCK0 + b, h)),
        ],
        out_specs=pl.BlockSpec((tq, HEAD_DIM), lambda b, h, i: (b * nq + i, h)),
        out_shape=jax.ShapeDtypeStruct((N_LAT, MLA_HEADS * HEAD_DIM), BF16),
        compiler_params=_cparams(("parallel", "parallel", "arbitrary")),
        name="mla_latent",
    )(q, k, v, k, v)


def _ctx_attend(q, k, v, q_col0, k_col0, v_col0, heads, dqk, dv, scale, ones_col, name):
    return pl.pallas_call(
        functools.partial(_attn_kernel, scale=scale, two=False, ones_col=ones_col, chains=1),
        grid=(BATCH, heads),
        in_specs=[
            pl.BlockSpec((CTX_LEN, dqk), lambda b, h: (CTX_BLOCK0 + b, q_col0 + h)),
            pl.BlockSpec((CTX_LEN, dqk), lambda b, h: (CTX_BLOCK0 + b, k_col0 + h)),
            pl.BlockSpec((CTX_LEN, dv), lambda b, h: (CTX_BLOCK0 + b, v_col0 + h)),
        ],
        out_specs=pl.BlockSpec((CTX_LEN, HEAD_DIM), lambda b, h: (b, h)),
        out_shape=jax.ShapeDtypeStruct((N_CTX, heads * HEAD_DIM), BF16),
        compiler_params=_cparams(("parallel", "parallel")),
        name=name,
    )(q, k, v)


def _na_plan():
    rows = SEQ // GRID_W
    invalid = 2 * NA_KH - 1
    pairs, plan, starts = [], [], []
    for t in range(rows // NA_TR):
        kw0 = int(np.clip(NA_TR * t - NA_KH // 2, 0, rows - NA_WR))
        starts.append(kw0)
        tile = []
        for ri in range(NA_TR):
            r = NA_TR * t + ri
            r0 = int(np.clip(r - NA_KH // 2, 0, rows - NA_KH))
            assert kw0 <= r0 and r0 + NA_KH <= kw0 + NA_WR
            row = []
            for kp in range(NA_WR // 2):
                pair = []
                for kr in (kw0 + 2 * kp, kw0 + 2 * kp + 1):
                    pair.append(kr - r + NA_KH - 1 if r0 <= kr < r0 + NA_KH else invalid)
                pair = tuple(pair)
                if pair not in pairs:
                    pairs.append(pair)
                row.append(pairs.index(pair))
            tile.append(row)
        plan.append(tile)
    return starts, plan, pairs


def _na_bias_pairs(rpb, pairs):
    c = np.arange(GRID_W)
    c0 = np.clip(c - NA_KW // 2, 0, GRID_W - NA_KW)
    col_ok = (c[None, :] >= c0[:, None]) & (c[None, :] < c0[:, None] + NA_KW)
    col_idx = np.clip(c[None, :] - c[:, None] + NA_KW - 1, 0, 2 * NA_KW - 2)
    onehot = (col_idx[None] == np.arange(2 * NA_KW - 1)[:, None, None]).astype(np.float32)
    t = jnp.einsum("lhdj,jck->lhdck", rpb.astype(F32), onehot, precision=lax.Precision.HIGHEST) * LOG2E
    t = jnp.where(col_ok, t, MASK_VALUE)
    masked = jnp.full(t.shape[:2] + (GRID_W, GRID_W), MASK_VALUE, F32)
    slabs = [t[:, :, d] for d in range(2 * NA_KH - 1)] + [masked]
    return jnp.stack([jnp.concatenate([slabs[a], slabs[b]], axis=-1) for a, b in pairs], axis=2)


def _na_kernel(q_ref, k_ref, v_ref, kc_ref, vc_ref, t2_ref, o_ref, *, starts, plan):
    kc = kc_ref[...]
    vc = vc_ref[...]
    for t, (kw0, tile) in enumerate(zip(starts, plan)):
        q = q_ref[t * NA_TQ:(t + 1) * NA_TQ, :]
        kw = k_ref[kw0 * GRID_W:kw0 * GRID_W + NA_TK, :]
        vw = v_ref[kw0 * GRID_W:kw0 * GRID_W + NA_TK, :]
        bias = jnp.concatenate(
            [jnp.concatenate([t2_ref[0, 0, idx] for idx in row], axis=1) for row in tile], axis=0)
        s = _dot_nt(q, kw) * (NA_SCALE * LOG2E) + bias
        sc = _dot_nt(q, kc) * (NA_SCALE * LOG2E)
        o_ref[t * NA_TQ:(t + 1) * NA_TQ, :] = _softmax_pv([s, sc], [vw, vc], False).astype(o_ref.dtype)


def _na_latent(p, t2, l, starts, plan):
    cq = COL_NA // HEAD_DIM
    ck = cq + NA_HEADS
    cv = ck + NA_HEADS
    n_pairs = t2.shape[2]
    return pl.pallas_call(
        functools.partial(_na_kernel, starts=starts, plan=plan),
        grid=(NA_HEADS, BATCH),
        in_specs=[
            pl.BlockSpec((SEQ, HEAD_DIM), lambda h, b: (b, cq + h)),
            pl.BlockSpec((SEQ, HEAD_DIM), lambda h, b: (b, ck + h)),
            pl.BlockSpec((SEQ, HEAD_DIM), lambda h, b: (b, cv + h)),
            pl.BlockSpec((CTX_LEN, HEAD_DIM), lambda h, b: (CTX_BLOCK0 + b, ck + h)),
            pl.BlockSpec((CTX_LEN, HEAD_DIM), lambda h, b: (CTX_BLOCK0 + b, cv + h)),
            pl.BlockSpec((1, 1, n_pairs, GRID_W, 2 * GRID_W), lambda h, b: (l, h, 0, 0, 0)),
        ],
        out_specs=pl.BlockSpec((SEQ, HEAD_DIM), lambda h, b: (b, h)),
        out_shape=jax.ShapeDtypeStruct((N_LAT, NA_HEADS * HEAD_DIM), BF16),
        compiler_params=_cparams(("parallel", "parallel")),
        name="na_latent",
    )(p, p, p, p, p, t2)


def _conv_kernel(v_ref, x1_ref, x2_ref, wv_ref, w1_ref, w2_ref, bv_ref, b1_ref, b2_ref, zin_ref, x2o_ref):
    n = v_ref.shape[0]
    row = lax.broadcasted_iota(jnp.int32, (n, 1), 0)

    def short_conv(p_ref, w_ref, b_ref):
        p = p_ref[...].astype(F32)
        prev = jnp.where(row == 0, 0.0, pltpu.roll(p, 1, 0))
        nxt = jnp.where(row == n - 1, 0.0, pltpu.roll(p, n - 1, 0))
        w = w_ref[0]
        return prev * w[0:1] + p * w[1:2] + nxt * w[2:3] + b_ref[0]

    zin_ref[...] = (short_conv(x1_ref, w1_ref, b1_ref) * short_conv(v_ref, wv_ref, bv_ref)).astype(zin_ref.dtype)
    x2o_ref[...] = short_conv(x2_ref, w2_ref, b2_ref).astype(x2o_ref.dtype)


def _hy_conv(p, conv_w, conv_b, l, n, row_block0, *, tc=256):
    nc = HY_WIDTH // tc

    def seg(s):
        return (pl.BlockSpec((n, tc), lambda b, j: (row_block0 + b, s * nc + j)),
                pl.BlockSpec((1, 3, tc), lambda b, j: (l, 0, s * nc + j)),
                pl.BlockSpec((1, 1, tc), lambda b, j: (l, 0, s * nc + j)))

    (pv, wv, bv), (p1, w1, b1), (p2, w2, b2) = seg(0), seg(1), seg(2)
    out_spec = pl.BlockSpec((n, tc), lambda b, j: (b, j))
    out = jax.ShapeDtypeStruct((BATCH * n, HY_WIDTH), BF16)
    conv_b = conv_b.reshape(DEPTH, 1, -1)
    return pl.pallas_call(
        _conv_kernel,
        grid=(BATCH, nc),
        in_specs=[pv, p1, p2, wv, w1, w2, bv, b1, b2],
        out_specs=[out_spec, out_spec],
        out_shape=[out, out],
        compiler_params=_cparams(("parallel", "parallel")),
        name="hy_conv",
    )(p, p, p, conv_w, conv_w, conv_w, conv_b, conv_b, conv_b)


def _filt_kernel(z_ref, t_ref, dl_ref, w1_ref, b1_ref, w2_ref, b2_ref, w3_ref, fr_ref, hs_ref, ha_ref, kn_ref):
    n = z_ref.shape[0]
    fr = fr_ref[...]
    h = jnp.sin(fr * (_dot_hi(z_ref[...], w1_ref[...]) + b1_ref[...]))
    h = jnp.sin(fr * (_dot_hi(h, w2_ref[...]) + b2_ref[...]))
    h = _dot_hi(h, w3_ref[...])
    decay = jnp.exp(-t_ref[...] * dl_ref[...])
    row = lax.broadcasted_iota(jnp.int32, (n, 1), 0)
    hf = h[:, :HY_WIDTH] * decay
    hb = jnp.where(row == 0, 0.0, h[:, HY_WIDTH:] * decay)
    hs = hf + hb
    hs_ref[...] = hs.astype(hs_ref.dtype)
    ha_ref[...] = (hf - hb).astype(ha_ref.dtype)
    sign = jnp.where((row & 1) == 0, 1.0, -1.0)
    kn_ref[...] = jnp.sum(hs * sign, axis=0, keepdims=True)


def _hy_filter_taps(n, f_w1, f_b1, f_w2, f_b2, f_w3, f_freq):
    pos = jnp.arange(n, dtype=F32)
    t = jnp.linspace(0.0, 1.0, n, dtype=F32)
    bands = jnp.linspace(1e-4, HY_POS_BANDS - 1, HY_POS_BANDS, dtype=F32)
    ang = (2.0 * math.pi / n) * pos[:, None] * bands[None, :]
    z = jnp.concatenate([t[:, None], jnp.cos(ang), -jnp.sin(ang)], axis=-1)
    pad = HY_FILTER_HIDDEN - z.shape[1]
    z = jnp.pad(z, ((0, 0), (0, pad)))
    w1 = jnp.pad(f_w1.astype(F32), ((0, pad), (0, 0)))
    deltas = jnp.abs(jnp.linspace(math.log(HY_DECAY_TARGET) / HY_FAST_DECAY,
                                  math.log(HY_DECAY_TARGET) / HY_SLOW_DECAY, HY_WIDTH, dtype=F32))
    hid = HY_FILTER_HIDDEN
    return pl.pallas_call(
        _filt_kernel,
        out_shape=[
            jax.ShapeDtypeStruct((n, HY_WIDTH), BF16),
            jax.ShapeDtypeStruct((n, HY_WIDTH), BF16),
            jax.ShapeDtypeStruct((1, HY_WIDTH), F32),
        ],
        compiler_params=pltpu.CompilerParams(vmem_limit_bytes=VMEM_LIMIT),
        name="hy_filter",
    )(z, t[:, None], deltas[None, :], w1, f_b1.reshape(1, hid), f_w2, f_b2.reshape(1, hid), f_w3,
      f_freq.reshape(1, hid))


def _dft_tables(n):
    lo = int(round(math.sqrt(n)))
    while n % lo:
        lo -= 1
    hi = n // lo
    f = jnp.arange(n, dtype=jnp.int32)[:, None]
    big = 2 * n

    def ang(tt):
        return ((f * tt) % big).astype(F32) * (2.0 * math.pi / big)

    a = ang(lo * jnp.arange(hi, dtype=jnp.int32)[None, :])
    b = ang(jnp.arange(lo, dtype=jnp.int32)[None, :])
    ca, sa, cb, sb = jnp.cos(a), jnp.sin(a), jnp.cos(b), jnp.sin(b)
    cos = (ca[:, :, None] * cb[:, None, :] - sa[:, :, None] * sb[:, None, :]).reshape(n, n)
    msin = -(sa[:, :, None] * cb[:, None, :] + ca[:, :, None] * sb[:, None, :]).reshape(n, n)
    alt = jnp.where(jnp.arange(n) % 2 == 0, 1.0, -1.0).astype(F32)
    msin = jnp.where(f == 0, alt[None, :], msin)
    return cos.astype(BF16), msin.astype(BF16), msin.T.astype(BF16)


def _dft_filt_kernel(c_ref, s_ref, hs_ref, ha_ref, kr_ref, ki_ref):
    kr_ref[...] = _dot(c_ref[...], hs_ref[...])
    ki_ref[...] = _dot(s_ref[...], ha_ref[...])


def _hy_filter_spectrum(cos, msin, hs, ha, *, tf):
    n = cos.shape[0]
    tf = min(tf, n)
    tab = pl.BlockSpec((tf, n), lambda i: (i, 0))
    taps = pl.BlockSpec((n, HY_WIDTH), lambda i: (0, 0))
    out = pl.BlockSpec((tf, HY_WIDTH), lambda i: (i, 0))
    return pl.pallas_call(
        _dft_filt_kernel,
        grid=(n // tf,),
        in_specs=[tab, tab, taps, taps],
        out_specs=[out, out],
        out_shape=[jax.ShapeDtypeStruct((n, HY_WIDTH), F32)] * 2,
        compiler_params=_cparams(("parallel",)),
        name="hy_filter_dft",
    )(cos, msin, hs, ha)


def _dft_fwd_kernel(c_ref, s_ref, x_ref, kr_ref, ki_ref, kn_ref, yr_ref, yi_ref, *, tf, inv_n):
    x = x_ref[...]
    zr = _dot(c_ref[...], x)
    zi = _dot(s_ref[...], x)
    row = pl.program_id(0) * tf + lax.broadcasted_iota(jnp.int32, (tf, 1), 0)
    bin0 = row == 0
    kr = kr_ref[...]
    ki = jnp.where(bin0, 0.0, ki_ref[...])
    kr_im = jnp.where(bin0, kn_ref[...], kr)
    wt = jnp.where(bin0, inv_n, 2.0 * inv_n)
    yr_ref[0] = ((zr * kr - zi * ki) * wt).astype(yr_ref.dtype)
    yi_ref[0] = ((zr * ki + zi * kr_im) * wt).astype(yi_ref.dtype)


def _hy_dft_fwd(cos, msin, zin, kr, ki, kn, *, tf):
    n = cos.shape[0]
    tf = min(tf, n)
    tab = pl.BlockSpec((tf, n), lambda i, b: (i, 0))
    filt = pl.BlockSpec((tf, HY_WIDTH), lambda i, b: (i, 0))
    out = pl.BlockSpec((1, tf, HY_WIDTH), lambda i, b: (b, i, 0))
    return pl.pallas_call(
        functools.partial(_dft_fwd_kernel, tf=tf, inv_n=1.0 / (2 * n)),
        grid=(n // tf, BATCH),
        in_specs=[tab, tab, pl.BlockSpec((n, HY_WIDTH), lambda i, b: (b, 0)), filt, filt,
                  pl.BlockSpec((1, HY_WIDTH), lambda i, b: (0, 0))],
        out_specs=[out, out],
        out_shape=[jax.ShapeDtypeStruct((BATCH, n, HY_WIDTH), BF16)] * 2,
        compiler_params=_cparams(("parallel", "arbitrary")),
        name="hy_dft_fwd",
    )(cos, msin, zin, kr, ki, kn)


def _dft_inv_kernel(c_ref, st_ref, yr_ref, yi_ref, zin_ref, x2_ref, b_ref, o_ref):
    y = _dot(c_ref[...], yr_ref[0]) + _dot(st_ref[...], yi_ref[0])
    y = y + zin_ref[...].astype(F32) * b_ref[0]
    o_ref[...] = (x2_ref[...].astype(F32) * y).astype(o_ref.dtype)


def _hy_dft_inv(cos, msin_t, yr, yi, zin, x2, bias, l, *, tt):
    n = cos.shape[0]
    tt = min(tt, n)
    nt = n // tt
    tab = pl.BlockSpec((tt, n), lambda i, b: (i, 0))
    spec = pl.BlockSpec((1, n, HY_WIDTH), lambda i, b: (b, 0, 0))
    rows = pl.BlockSpec((tt, HY_WIDTH), lambda i, b: (b * nt + i, 0))
    return pl.pallas_call(
        _dft_inv_kernel,
        grid=(nt, BATCH),
        in_specs=[tab, tab, spec, spec, rows, rows, pl.BlockSpec((1, 1, HY_WIDTH), lambda i, b: (l, 0, 0))],
        out_specs=rows,
        out_shape=jax.ShapeDtypeStruct((BATCH * n, HY_WIDTH), BF16),
        compiler_params=_cparams(("parallel", "arbitrary")),
        name="hy_dft_inv",
    )(cos, msin_t, yr, yi, zin, x2, bias.reshape(DEPTH, 1, HY_WIDTH))


def _hyena(p, l, n, row_block0, tables, conv_w, conv_b, f_w1, f_b1, f_w2, f_b2, f_w3, f_freq, bias):
    cos, msin, msin_t = tables
    zin, x2 = _hy_conv(p, conv_w, conv_b, l, n, row_block0)
    hs, ha, kn = _hy_filter_taps(n, f_w1[l], f_b1[l], f_w2[l], f_b2[l], f_w3[l], f_freq[l])
    kr, ki = _hy_filter_spectrum(cos, msin, hs, ha, tf=512)
    yr, yi = _hy_dft_fwd(cos, msin, zin, kr, ki, kn, tf=512)
    return _hy_dft_inv(cos, msin_t, yr, yi, zin, x2, bias, l, tt=512)


def _rope_tables():
    tok = jnp.arange(SEQ)
    row = (tok // GRID_W).astype(F32)
    col = (tok % GRID_W).astype(F32)
    n_freq = MLA_ROPE // 4
    inv = ROPE_THETA ** (-jnp.arange(n_freq, dtype=F32) / n_freq)
    ang = jnp.concatenate([row[:, None] * inv, col[:, None] * inv], axis=-1)
    cos, sin = jnp.cos(ang), jnp.sin(ang)
    half = MLA_ROPE // 2
    zeros = jnp.zeros((SEQ, half), F32)
    rest = HEAD_DIM - MLA_ROPE
    cos_t = jnp.concatenate([cos, cos, jnp.ones((SEQ, rest), F32)], axis=-1)
    sin_a = jnp.concatenate([-sin, zeros, jnp.zeros((SEQ, rest), F32)], axis=-1)
    sin_b = jnp.concatenate([zeros, sin, jnp.zeros((SEQ, rest), F32)], axis=-1)
    ident = jnp.ones((N_CTX, HEAD_DIM), F32)
    none = jnp.zeros((N_CTX, HEAD_DIM), F32)
    return (jnp.concatenate([cos_t, ident]), jnp.concatenate([sin_a, none]), jnp.concatenate([sin_b, none]))


def _layout_w_tail(w_in):
    na0 = 3 * HY_WIDTH + Q_LORA + KV_LORA + MLA_ROPE
    kr0 = na0 - MLA_ROPE
    pad = jnp.zeros(w_in.shape[:2] + (P_COLS - COL_KR - MLA_ROPE,), w_in.dtype)
    return jnp.concatenate([w_in[:, :, na0:], w_in[:, :, kr0:na0], pad], axis=-1)


def _layout_w_uq(w):
    w = w.reshape(Q_LORA, MLA_HEADS, HEAD_DIM + MLA_ROPE)
    w = jnp.pad(w, ((0, 0), (0, 0), (0, MLA_QK - HEAD_DIM - MLA_ROPE)))
    return w.reshape(Q_LORA, MLA_HEADS * MLA_QK).astype(BF16)


def kernel(x, c, ctx, c_ctx, w_ada, b_ada, g_attn_pre, g_attn_post, g_ffn_pre, g_ffn_post, w_in, hy_conv_w, hy_conv_b, hy_f_w1, hy_f_b1, hy_f_w2, hy_f_b2, hy_f_w3, hy_f_freq, hy_bias, mla_g_q, mla_w_uq, mla_g_kv, mla_w_ukv, na_rpb, w_out, w_ffn_gate, w_ffn_up, w_ffn_down):
    cc = jnp.concatenate([c, c_ctx[None, :], jnp.zeros((8 - BATCH - 1, D_MODEL), F32)], axis=0)
    mods_all = _ada(cc, w_ada, b_ada)
    mods = [mods_all[l].reshape(8, 1, 6 * D_MODEL) for l in range(DEPTH)]

    rope_tabs = _rope_tables()
    dft_lat = _dft_tables(SEQ)
    dft_ctx = _dft_tables(CTX_LEN)
    na_starts, na_plan, na_pairs = _na_plan()
    na_t2 = _na_bias_pairs(na_rpb, na_pairs)
    w_tail = _layout_w_tail(w_in)
    hy_w = (hy_conv_w, hy_conv_b, hy_f_w1, hy_f_b1, hy_f_w2, hy_f_b2, hy_f_w3, hy_f_freq, hy_bias)

    res = (x.reshape(N_LAT, D_MODEL), ctx.reshape(N_CTX, D_MODEL))
    xn = _prenorm(res[0], res[1], g_attn_pre[0], mods[0])

    for l in range(DEPTH):
        ctx_out = l < DEPTH - 1
        m_rows = N_TOK if ctx_out else N_LAT
        p = _in_proj(xn, w_in, w_tail, l)

        q = _q_proj(p, m_rows, mla_g_q[l], _layout_w_uq(mla_w_uq[l]), rope_tabs)
        k, v = _kv_proj(p, mla_g_kv[l], mla_w_ukv, l, rope_tabs)
        mla = [_mla_latent(q, k, v), None]
        na = [_na_latent(p, na_t2, l, na_starts, na_plan), None]
        hy = [_hyena(p, l, SEQ, 0, dft_lat, *hy_w), None]

        if ctx_out:
            mla[1] = _ctx_attend(q, k, v, 0, 0, 0, MLA_HEADS, MLA_QK, MLA_VW, 1.0, True, "mla_ctx")
            cq = COL_NA // HEAD_DIM
            na[1] = _ctx_attend(p, p, p, cq, cq + NA_HEADS, cq + 2 * NA_HEADS, NA_HEADS, HEAD_DIM, HEAD_DIM,
                                NA_SCALE * LOG2E, False, "na_ctx")
            hy[1] = _hyena(p, l, CTX_LEN, CTX_BLOCK0, dft_ctx, *hy_w)
            res_l = res
        else:
            res_l = (res[0], None)

        stream, xn = _proj_post([tuple(hy), tuple(mla), tuple(na)], w_out, l, g_attn_post[l], mods[l], 2, res_l,
                                m_rows, nxt=(g_ffn_pre[l], mods[l], 3, 4))
        h = _ffn_up(xn, w_ffn_gate, w_ffn_up, l)
        nxt = (g_attn_pre[l + 1], mods[l + 1], 0, 1) if ctx_out else None
        stream, xn = _proj_post([(h, None)], w_ffn_down, l, g_ffn_post[l], mods[l], 5, (stream, None), m_rows,
                                nxt=nxt)
        res = (stream, None)

    return stream.reshape(BATCH, SEQ, D_MODEL)
```

```python
import functools
import math

import jax
import jax.numpy as jnp
import numpy as np
from jax import lax
from jax.experimental import pallas as pl
from jax.experimental.pallas import tpu as pltpu

F32 = jnp.float32
BF16 = jnp.bfloat16

D_MODEL = 2048
BATCH = 4
SEQ = 2048
DEPTH = 2
GRID_W = 64
CTX_LEN = 256
HEAD_DIM = 128
HY_WIDTH = D_MODEL // 4
HY_FILTER_HIDDEN = 64
HY_POS_BANDS = 16
HY_DECAY_TARGET = 1e-2
HY_FAST_DECAY = 0.3
HY_SLOW_DECAY = 1.5
MLA_HEADS = (D_MODEL // 2) // HEAD_DIM
MLA_ROPE = 64
Q_LORA = 3 * D_MODEL // 8
KV_LORA = D_MODEL // 4
MLA_SCALE = (HEAD_DIM + MLA_ROPE) ** -0.5
NA_HEADS = (D_MODEL // 4) // HEAD_DIM
NA_KH = 8
NA_KW = 16
NA_SCALE = HEAD_DIM ** -0.5
FFN_HIDDEN = ((8 * D_MODEL + 3 * 256 - 1) // (3 * 256)) * 256
ROPE_THETA = 10000.0
RMS_EPS = 1e-6
MASK_VALUE = -1e30
LOG2E = math.log2(math.e)

N_LAT = BATCH * SEQ
N_CTX = BATCH * CTX_LEN
N_TOK = N_LAT + N_CTX
CTX_BLOCK0 = N_LAT // CTX_LEN

IN_TN = 256
COL_HY = 0
COL_CQ = 3 * HY_WIDTH
COL_CKV = COL_CQ + Q_LORA
COL_KR = COL_CKV + KV_LORA
COL_NA = COL_KR + IN_TN
P_COLS = COL_NA + 3 * NA_HEADS * HEAD_DIM
W_IN_NA = COL_KR + MLA_ROPE
MLA_QK = 256
MLA_VW = 256

NA_TR = 4
NA_WR = 12
NA_TQ = NA_TR * GRID_W
NA_TK = NA_WR * GRID_W

VMEM_LIMIT = 52 * 1024 * 1024


def _cparams(sem):
    return pltpu.CompilerParams(dimension_semantics=sem, vmem_limit_bytes=VMEM_LIMIT)


def _dot(a, b):
    return jnp.dot(a, b, preferred_element_type=F32)


def _dot_nt(a, b):
    return lax.dot_general(a, b, (((1,), (1,)), ((), ())), preferred_element_type=F32)


def _dot_hi(a, b):
    return jnp.dot(a, b, preferred_element_type=F32, precision=lax.Precision.HIGHEST)


def _rms(x, g):
    ms = jnp.mean(x * x, axis=-1, keepdims=True)
    return x * lax.rsqrt(ms + RMS_EPS) * g


def _mod_row(tm):
    n_lat, per_b = N_LAT // tm, SEQ // tm
    return lambda i: jnp.where(i < n_lat, i // per_b, BATCH)


def _rope_row(tm):
    n_lat, per_b = N_LAT // tm, SEQ // tm
    return lambda i: jnp.where(i < n_lat, i % per_b, per_b + i - n_lat)


def _ada_kernel(c_ref, w_ref, b_ref, o_ref):
    a = c_ref[...]
    a = a * jax.nn.sigmoid(a)
    o_ref[0] = _dot(a.astype(BF16), w_ref[0].astype(BF16)) + b_ref[0]


def _ada(cc, w_ada, b_ada):
    tn = 1024
    n = w_ada.shape[-1]
    return pl.pallas_call(
        _ada_kernel,
        grid=(DEPTH, n // tn),
        in_specs=[
            pl.BlockSpec((8, D_MODEL), lambda l, j: (0, 0)),
            pl.BlockSpec((1, D_MODEL, tn), lambda l, j: (l, 0, j)),
            pl.BlockSpec((1, 1, tn), lambda l, j: (l, 0, j)),
        ],
        out_specs=pl.BlockSpec((1, 8, tn), lambda l, j: (l, 0, j)),
        out_shape=jax.ShapeDtypeStruct((DEPTH, 8, n), F32),
        compiler_params=_cparams(("parallel", "parallel")),
        name="ada",
    )(cc, w_ada, b_ada.reshape(DEPTH, 1, n))


def _norm_mod_to(xn_ref, x_ref, g_ref, sh_ref, sc_ref, chunk=256):
    g = g_ref[...]
    sc = 1.0 + sc_ref[0]
    sh = sh_ref[0]

    def body(r, carry):
        rows = pl.ds(pl.multiple_of(r * chunk, chunk), chunk)
        xn_ref[rows, :] = (_rms(x_ref[rows, :], g) * sc + sh).astype(BF16)
        return carry

    lax.fori_loop(0, x_ref.shape[0] // chunk, body, 0)


def _prenorm_kernel(x_ref, c_ref, g_ref, sh_ref, sc_ref, o_ref, *, n_lat):
    i = pl.program_id(0)

    @pl.when(i < n_lat)
    def _():
        _norm_mod_to(o_ref, x_ref, g_ref, sh_ref, sc_ref)

    @pl.when(i >= n_lat)
    def _():
        _norm_mod_to(o_ref, c_ref, g_ref, sh_ref, sc_ref)


def _prenorm(x2d, ctx2d, g, mods, *, tm=1024):
    n_lat = N_LAT // tm
    row = _mod_row(tm)
    return pl.pallas_call(
        functools.partial(_prenorm_kernel, n_lat=n_lat),
        grid=(N_TOK // tm,),
        in_specs=[
            pl.BlockSpec((tm, D_MODEL), lambda i: (jnp.minimum(i, n_lat - 1), 0)),
            pl.BlockSpec((tm, D_MODEL), lambda i: (jnp.maximum(i - n_lat, 0), 0)),
            pl.BlockSpec((1, D_MODEL), lambda i: (0, 0)),
            pl.BlockSpec((1, 1, D_MODEL), lambda i: (row(i), 0, 0)),
            pl.BlockSpec((1, 1, D_MODEL), lambda i: (row(i), 0, 1)),
        ],
        out_specs=pl.BlockSpec((tm, D_MODEL), lambda i: (i, 0)),
        out_shape=jax.ShapeDtypeStruct((N_TOK, D_MODEL), BF16),
        compiler_params=_cparams(("parallel",)),
        name="prenorm",
    )(x2d, ctx2d, g.reshape(1, D_MODEL), mods, mods)


def _in_kernel(x_ref, w_ref, o_ref):
    o_ref[...] = _dot_nt(x_ref[...], w_ref[0].astype(BF16)).astype(o_ref.dtype)


def _in_proj(xn, w_in_t, l, *, tm=3072):
    n_head = COL_NA // IN_TN

    def w_row(j):
        per = IN_TN // MLA_ROPE
        return MLA_ROPE * jnp.where(j < n_head, j * per, W_IN_NA // MLA_ROPE + (j - n_head) * per)

    return pl.pallas_call(
        _in_kernel,
        grid=(N_TOK // tm, P_COLS // IN_TN),
        in_specs=[
            pl.BlockSpec((tm, D_MODEL), lambda i, j: (i, 0)),
            pl.BlockSpec((pl.Element(1), pl.Element(IN_TN), pl.Element(D_MODEL)), lambda i, j: (l, w_row(j), 0)),
        ],
        out_specs=pl.BlockSpec((tm, IN_TN), lambda i, j: (i, j)),
        out_shape=jax.ShapeDtypeStruct((N_TOK, P_COLS), BF16),
        compiler_params=_cparams(("parallel", "arbitrary")),
        name="in_proj",
    )(xn, w_in_t)


def _swiglu_kernel(x_ref, wg_ref, wu_ref, o_ref):
    xn = x_ref[...]
    gate = _dot(xn, wg_ref[0].astype(BF16))
    up = _dot(xn, wu_ref[0].astype(BF16))
    o_ref[...] = (gate * jax.nn.sigmoid(gate) * up).astype(o_ref.dtype)


def _ffn_up(xn, wg, wu, l, *, tn=256):
    m = xn.shape[0]
    tm = 2048 if m % 2048 == 0 else 1536
    w_spec = pl.BlockSpec((1, D_MODEL, tn), lambda i, j: (l, 0, j))
    return pl.pallas_call(
        _swiglu_kernel,
        grid=(m // tm, FFN_HIDDEN // tn),
        in_specs=[pl.BlockSpec((tm, D_MODEL), lambda i, j: (i, 0)), w_spec, w_spec],
        out_specs=pl.BlockSpec((tm, tn), lambda i, j: (i, j)),
        out_shape=jax.ShapeDtypeStruct((m, FFN_HIDDEN), BF16),
        compiler_params=_cparams(("parallel", "arbitrary")),
        name="ffn_up",
    )(xn, wg, wu)


POST_FETCH_STEP = 1
POST_ROWS = 256


def _post_kernel(*refs, n_act, two_src, blocks, emit_xn, n_lat, tm):
    refs = list(refs)
    acts_lat = [refs.pop(0) for _ in range(n_act)]
    acts_ctx = [refs.pop(0) for _ in range(n_act)] if two_src else None
    w_ref, g_ref, gate_ref, res_lat = (refs.pop(0) for _ in range(4))
    res_ctx = refs.pop(0) if two_src else None
    if emit_xn:
        g2_ref, sh_ref, sc_ref = (refs.pop(0) for _ in range(3))
    o_hbm = refs.pop(0)
    xn_ref = refs.pop(0) if emit_xn else None
    acc_ref, buf_ref, sem = refs

    i = pl.program_id(0)
    k = pl.program_id(1)
    nk = sum(blocks)
    is_lat = i < n_lat
    is_ctx = jnp.logical_not(is_lat)

    def res_copy(src, row0):
        return pltpu.make_async_copy(src.at[pl.ds(row0, tm)], buf_ref, sem.at[0])

    def out_copy(tile):
        return pltpu.make_async_copy(buf_ref, o_hbm.at[pl.ds(tile * tm, tm)], sem.at[1])

    @pl.when(k == POST_FETCH_STEP)
    def _():
        @pl.when(i > 0)
        def _():
            out_copy(i - 1).wait()

        if two_src:
            pl.when(is_lat)(lambda: res_copy(res_lat, i * tm).start())
            pl.when(is_ctx)(lambda: res_copy(res_ctx, (i - n_lat) * tm).start())
        else:
            res_copy(res_lat, i * tm).start()

    @pl.when(k == 0)
    def _():
        acc_ref[...] = jnp.zeros_like(acc_ref)

    def accumulate(a_ref):
        acc_ref[...] += _dot(a_ref[...], w_ref[0].astype(BF16))

    start = 0
    for idx, nb in enumerate(blocks):
        in_range = (k >= start) & (k < start + nb)
        if two_src:
            pl.when(in_range & is_lat)(functools.partial(accumulate, acts_lat[idx]))
            pl.when(in_range & is_ctx)(functools.partial(accumulate, acts_ctx[idx]))
        else:
            pl.when(in_range)(functools.partial(accumulate, acts_lat[idx]))
        start += nb

    @pl.when(k == nk)
    def _():
        res_copy(res_lat, 0).wait()
        g = g_ref[...]
        gate = gate_ref[0]
        if emit_xn:
            g2 = g2_ref[...]
            sc = 1.0 + sc_ref[0]
            sh = sh_ref[0]

        def body(r, carry):
            rows = pl.ds(pl.multiple_of(r * POST_ROWS, POST_ROWS), POST_ROWS)
            x_new = buf_ref[rows, :] + gate * _rms(acc_ref[rows, :], g)
            buf_ref[rows, :] = x_new
            if emit_xn:
                xn_ref[rows, :] = (_rms(x_new, g2) * sc + sh).astype(xn_ref.dtype)
            return carry

        lax.fori_loop(0, tm // POST_ROWS, body, 0)
        out_copy(i).start()

        @pl.when(i == pl.num_programs(0) - 1)
        def _():
            out_copy(i).wait()


def _proj_post(acts, w, l, g, mods, gate_chunk, res, m_rows, nxt=None, *, tm=1024, tk=512):
    two_src = res[1] is not None
    n_lat = N_LAT // tm
    blocks = tuple(a.shape[1] // tk for a, _ in acts)
    starts = tuple(int(v) for v in np.cumsum((0,) + blocks[:-1]))
    nk = sum(blocks)
    emit_xn = nxt is not None
    row = _mod_row(tm)

    def lat_row(i):
        return jnp.minimum(i, n_lat - 1) if two_src else i

    def ctx_row(i):
        return jnp.maximum(i - n_lat, 0)

    def act_specs(rowf):
        return [pl.BlockSpec((tm, tk), lambda i, k, st=st, nb=nb: (rowf(i), jnp.clip(k - st, 0, nb - 1)))
                for st, nb in zip(starts, blocks)]

    vec_spec = pl.BlockSpec((1, D_MODEL), lambda i, k: (0, 0))

    def mod_spec(chunk):
        return pl.BlockSpec((1, 1, D_MODEL), lambda i, k: (row(i), 0, chunk))

    hbm = pl.BlockSpec(memory_space=pl.ANY)
    in_specs = act_specs(lat_row)
    args = [a for a, _ in acts]
    if two_src:
        in_specs += act_specs(ctx_row)
        args += [c for _, c in acts]
    in_specs += [
        pl.BlockSpec((1, tk, D_MODEL), lambda i, k: (l, jnp.minimum(k, nk - 1), 0)),
        vec_spec,
        mod_spec(gate_chunk),
        hbm,
    ]
    args += [w, g.reshape(1, D_MODEL), mods, res[0]]
    if two_src:
        in_specs.append(hbm)
        args.append(res[1])
    out_specs = [hbm]
    out_shape = [jax.ShapeDtypeStruct((m_rows, D_MODEL), F32)]
    if emit_xn:
        g2, mods2, sh_chunk, sc_chunk = nxt
        in_specs += [vec_spec, mod_spec(sh_chunk), mod_spec(sc_chunk)]
        args += [g2.reshape(1, D_MODEL), mods2, mods2]
        out_specs.append(pl.BlockSpec((tm, D_MODEL), lambda i, k: (i, 0)))
        out_shape.append(jax.ShapeDtypeStruct((m_rows, D_MODEL), BF16))

    out = pl.pallas_call(
        functools.partial(_post_kernel, n_act=len(acts), two_src=two_src, blocks=blocks, emit_xn=emit_xn,
                          n_lat=n_lat, tm=tm),
        grid=(m_rows // tm, nk + 1),
        in_specs=in_specs,
        out_specs=out_specs,
        out_shape=out_shape,
        scratch_shapes=[pltpu.VMEM((tm, D_MODEL), F32), pltpu.VMEM((tm, D_MODEL), F32),
                        pltpu.SemaphoreType.DMA((2,))],
        compiler_params=_cparams(("arbitrary", "arbitrary")),
        name="proj_post",
    )(*args)
    return (out[0], out[1]) if emit_xn else (out[0], None)


def _rope128(r, cos_ref, sa_ref, sb_ref):
    return r * cos_ref[...] + pltpu.roll(r, 96, 1) * sa_ref[...] + pltpu.roll(r, 32, 1) * sb_ref[...]


def _q_kernel(x_ref, g_ref, w_ref, cos_ref, sa_ref, sb_ref, o_ref):
    xn = _rms(x_ref[...].astype(F32), g_ref[...]).astype(BF16)
    for h in range(MLA_HEADS):
        acc = _dot(xn, w_ref[:, h * MLA_QK:(h + 1) * MLA_QK]) * (MLA_SCALE * LOG2E)
        o_ref[:, h * MLA_QK:h * MLA_QK + HEAD_DIM] = acc[:, :HEAD_DIM].astype(o_ref.dtype)
        o_ref[:, h * MLA_QK + HEAD_DIM:(h + 1) * MLA_QK] = _rope128(
            acc[:, HEAD_DIM:], cos_ref, sa_ref, sb_ref).astype(o_ref.dtype)


def _q_proj(p, m_rows, g, w, tabs, *, tm=1024):
    rope = _rope_row(tm)
    tab_spec = pl.BlockSpec((tm, HEAD_DIM), lambda i: (rope(i), 0))
    return pl.pallas_call(
        _q_kernel,
        grid=(m_rows // tm,),
        in_specs=[
            pl.BlockSpec((tm, Q_LORA), lambda i: (i, COL_CQ // Q_LORA)),
            pl.BlockSpec((1, Q_LORA), lambda i: (0, 0)),
            pl.BlockSpec((Q_LORA, MLA_HEADS * MLA_QK), lambda i: (0, 0)),
            tab_spec, tab_spec, tab_spec,
        ],
        out_specs=pl.BlockSpec((tm, MLA_HEADS * MLA_QK), lambda i: (i, 0)),
        out_shape=jax.ShapeDtypeStruct((m_rows, MLA_HEADS * MLA_QK), BF16),
        compiler_params=_cparams(("parallel",)),
        name="q_proj",
    )(p, g.reshape(1, Q_LORA), w, *tabs)


def _kv_kernel(xa_ref, xb_ref, kr_ref, g_ref, w_ref, cos_ref, sa_ref, sb_ref, k_ref, v_ref):
    half = KV_LORA // 2
    xa = xa_ref[...].astype(F32)
    xb = xb_ref[...].astype(F32)
    ms = (jnp.sum(xa * xa, axis=-1, keepdims=True) + jnp.sum(xb * xb, axis=-1, keepdims=True)) * (1.0 / KV_LORA)
    rs = lax.rsqrt(ms + RMS_EPS)
    g = g_ref[...]
    xna = (xa * rs * g[:, :half]).astype(BF16)
    xnb = (xb * rs * g[:, half:]).astype(BF16)
    krr = _rope128(kr_ref[...].astype(F32), cos_ref, sa_ref, sb_ref).astype(k_ref.dtype)
    ones = jnp.ones((xa.shape[0], MLA_VW - HEAD_DIM), v_ref.dtype)
    for h in range(MLA_HEADS):
        w = w_ref[0, :, h * 2 * HEAD_DIM:(h + 1) * 2 * HEAD_DIM].astype(BF16)
        acc = _dot(xna, w[:half]) + _dot(xnb, w[half:])
        k_ref[:, h * MLA_QK:h * MLA_QK + HEAD_DIM] = acc[:, :HEAD_DIM].astype(k_ref.dtype)
        k_ref[:, h * MLA_QK + HEAD_DIM:(h + 1) * MLA_QK] = krr
        v_ref[:, h * MLA_VW:h * MLA_VW + HEAD_DIM] = acc[:, HEAD_DIM:].astype(v_ref.dtype)
        v_ref[:, h * MLA_VW + HEAD_DIM:(h + 1) * MLA_VW] = ones


def _kv_proj(p, g, w, l, tabs, *, tm=1024):
    m = p.shape[0]
    rope = _rope_row(tm)
    half = KV_LORA // 2
    tab_spec = pl.BlockSpec((tm, HEAD_DIM), lambda i: (rope(i), 0))
    return pl.pallas_call(
        _kv_kernel,
        grid=(m // tm,),
        in_specs=[
            pl.BlockSpec((tm, half), lambda i: (i, COL_CKV // half)),
            pl.BlockSpec((tm, half), lambda i: (i, COL_CKV // half + 1)),
            pl.BlockSpec((tm, HEAD_DIM), lambda i: (i, COL_KR // HEAD_DIM)),
            pl.BlockSpec((1, KV_LORA), lambda i: (0, 0)),
            pl.BlockSpec((1, KV_LORA, MLA_HEADS * 2 * HEAD_DIM), lambda i: (l, 0, 0)),
            tab_spec, tab_spec, tab_spec,
        ],
        out_specs=[
            pl.BlockSpec((tm, MLA_HEADS * MLA_QK), lambda i: (i, 0)),
            pl.BlockSpec((tm, MLA_HEADS * MLA_VW), lambda i: (i, 0)),
        ],
        out_shape=[
            jax.ShapeDtypeStruct((m, MLA_HEADS * MLA_QK), BF16),
            jax.ShapeDtypeStruct((m, MLA_HEADS * MLA_VW), BF16),
        ],
        compiler_params=_cparams(("parallel",)),
        name="kv_proj",
    )(p, p, p, g.reshape(1, KV_LORA), w, *tabs)


def _softmax_pv(s_list, v_list, ones_col):
    m = jnp.max(s_list[0], axis=-1, keepdims=True)
    for s in s_list[1:]:
        m = jnp.maximum(m, jnp.max(s, axis=-1, keepdims=True))
    acc = None
    den = None
    for s, v in zip(s_list, v_list):
        p = jnp.exp2(s - m)
        if not ones_col:
            d = jnp.sum(p, axis=-1, keepdims=True)
            den = d if den is None else den + d
        o = _dot(p.astype(BF16), v)
        acc = o if acc is None else acc + o
    if ones_col:
        return acc[:, :HEAD_DIM] / acc[:, HEAD_DIM:]
    return acc / den


def _attn_kernel(*refs, scale, two, ones_col, chains):
    if two:
        q_ref, k1_ref, v1_ref, k2_ref, v2_ref, o_ref = refs
    else:
        q_ref, k1_ref, v1_ref, o_ref = refs
    tq = q_ref.shape[0] // chains
    for c in range(chains):
        q = q_ref[c * tq:(c + 1) * tq, :]
        s_list = [_dot_nt(q, k1_ref[...])]
        v_list = [v1_ref[...]]
        if two:
            s_list.append(_dot_nt(q, k2_ref[...]))
            v_list.append(v2_ref[...])
        if scale != 1.0:
            s_list = [s * scale for s in s_list]
        o_ref[c * tq:(c + 1) * tq, :] = _softmax_pv(s_list, v_list, ones_col).astype(o_ref.dtype)


def _mla_latent(q, k, v, *, tq=512):
    nq = SEQ // tq
    return pl.pallas_call(
        functools.partial(_attn_kernel, scale=1.0, two=True, ones_col=True, chains=2),
        grid=(BATCH, MLA_HEADS, nq),
        in_specs=[
            pl.BlockSpec((tq, MLA_QK), lambda b, h, i: (b * nq + i, h)),
            pl.BlockSpec((SEQ, MLA_QK), lambda b, h, i: (b, h)),
            pl.BlockSpec((SEQ, MLA_VW), lambda b, h, i: (b, h)),
            pl.BlockSpec((CTX_LEN, MLA_QK), lambda b, h, i: (CTX_BLOCK0 + b, h)),
            pl.BlockSpec((CTX_LEN, MLA_VW), lambda b, h, i: (CTX_BLOCK0 + b, h)),
        ],
        out_specs=pl.BlockSpec((tq, HEAD_DIM), lambda b, h, i: (b * nq + i, h)),
        out_shape=jax.ShapeDtypeStruct((N_LAT, MLA_HEADS * HEAD_DIM), BF16),
        compiler_params=_cparams(("parallel", "parallel", "arbitrary")),
        name="mla_latent",
    )(q, k, v, k, v)


def _ctx_attend(q, k, v, q_col0, k_col0, v_col0, heads, dqk, dv, scale, ones_col, name):
    return pl.pallas_call(
        functools.partial(_attn_kernel, scale=scale, two=False, ones_col=ones_col, chains=1),
        grid=(BATCH, heads),
        in_specs=[
            pl.BlockSpec((CTX_LEN, dqk), lambda b, h: (CTX_BLOCK0 + b, q_col0 + h)),
            pl.BlockSpec((CTX_LEN, dqk), lambda b, h: (CTX_BLOCK0 + b, k_col0 + h)),
            pl.BlockSpec((CTX_LEN, dv), lambda b, h: (CTX_BLOCK0 + b, v_col0 + h)),
        ],
        out_specs=pl.BlockSpec((CTX_LEN, HEAD_DIM), lambda b, h: (b, h)),
        out_shape=jax.ShapeDtypeStruct((N_CTX, heads * HEAD_DIM), BF16),
        compiler_params=_cparams(("parallel", "parallel")),
        name=name,
    )(q, k, v)


def _na_plan():
    rows = SEQ // GRID_W
    invalid = 2 * NA_KH - 1
    pairs, plan, starts = [], [], []
    for t in range(rows // NA_TR):
        kw0 = int(np.clip(NA_TR * t - NA_KH // 2, 0, rows - NA_WR))
        starts.append(kw0)
        tile = []
        for ri in range(NA_TR):
            r = NA_TR * t + ri
            r0 = int(np.clip(r - NA_KH // 2, 0, rows - NA_KH))
            assert kw0 <= r0 and r0 + NA_KH <= kw0 + NA_WR
            row = []
            for kp in range(NA_WR // 2):
                pair = []
                for kr in (kw0 + 2 * kp, kw0 + 2 * kp + 1):
                    pair.append(kr - r + NA_KH - 1 if r0 <= kr < r0 + NA_KH else invalid)
                pair = tuple(pair)
                if pair not in pairs:
                    pairs.append(pair)
                row.append(pairs.index(pair))
            tile.append(row)
        plan.append(tile)
    return starts, plan, pairs


def _na_bias_pairs(rpb, pairs):
    c = np.arange(GRID_W)
    c0 = np.clip(c - NA_KW // 2, 0, GRID_W - NA_KW)
    col_ok = (c[None, :] >= c0[:, None]) & (c[None, :] < c0[:, None] + NA_KW)
    col_idx = np.clip(c[None, :] - c[:, None] + NA_KW - 1, 0, 2 * NA_KW - 2)
    onehot = (col_idx[None] == np.arange(2 * NA_KW - 1)[:, None, None]).astype(np.float32)
    t = jnp.einsum("lhdj,jck->lhdck", rpb.astype(F32), onehot, precision=lax.Precision.HIGHEST) * LOG2E
    t = jnp.where(col_ok, t, MASK_VALUE)
    masked = jnp.full(t.shape[:2] + (GRID_W, GRID_W), MASK_VALUE, F32)
    slabs = [t[:, :, d] for d in range(2 * NA_KH - 1)] + [masked]
    return jnp.stack([jnp.concatenate([slabs[a], slabs[b]], axis=-1) for a, b in pairs], axis=2)


def _na_kernel(q_ref, k_ref, v_ref, kc_ref, vc_ref, t2_ref, o_ref, *, starts, plan):
    kc = kc_ref[...]
    vc = vc_ref[...]
    for t, (kw0, tile) in enumerate(zip(starts, plan)):
        q = q_ref[t * NA_TQ:(t + 1) * NA_TQ, :]
        kw = k_ref[kw0 * GRID_W:kw0 * GRID_W + NA_TK, :]
        vw = v_ref[kw0 * GRID_W:kw0 * GRID_W + NA_TK, :]
        bias = jnp.concatenate(
            [jnp.concatenate([t2_ref[0, 0, idx] for idx in row], axis=1) for row in tile], axis=0)
        s = _dot_nt(q, kw) * (NA_SCALE * LOG2E) + bias
        sc = _dot_nt(q, kc) * (NA_SCALE * LOG2E)
        o_ref[t * NA_TQ:(t + 1) * NA_TQ, :] = _softmax_pv([s, sc], [vw, vc], False).astype(o_ref.dtype)


def _na_latent(p, t2, l, starts, plan):
    cq = COL_NA // HEAD_DIM
    ck = cq + NA_HEADS
    cv = ck + NA_HEADS
    n_pairs = t2.shape[2]
    return pl.pallas_call(
        functools.partial(_na_kernel, starts=starts, plan=plan),
        grid=(NA_HEADS, BATCH),
        in_specs=[
            pl.BlockSpec((SEQ, HEAD_DIM), lambda h, b: (b, cq + h)),
            pl.BlockSpec((SEQ, HEAD_DIM), lambda h, b: (b, ck + h)),
            pl.BlockSpec((SEQ, HEAD_DIM), lambda h, b: (b, cv + h)),
            pl.BlockSpec((CTX_LEN, HEAD_DIM), lambda h, b: (CTX_BLOCK0 + b, ck + h)),
            pl.BlockSpec((CTX_LEN, HEAD_DIM), lambda h, b: (CTX_BLOCK0 + b, cv + h)),
            pl.BlockSpec((1, 1, n_pairs, GRID_W, 2 * GRID_W), lambda h, b: (l, h, 0, 0, 0)),
        ],
        out_specs=pl.BlockSpec((SEQ, HEAD_DIM), lambda h, b: (b, h)),
        out_shape=jax.ShapeDtypeStruct((N_LAT, NA_HEADS * HEAD_DIM), BF16),
        compiler_params=_cparams(("parallel", "parallel")),
        name="na_latent",
    )(p, p, p, p, p, t2)


def _conv_kernel(v_ref, x1_ref, x2_ref, wv_ref, w1_ref, w2_ref, bv_ref, b1_ref, b2_ref, zin_ref, x2o_ref):
    n = v_ref.shape[0]
    row = lax.broadcasted_iota(jnp.int32, (n, 1), 0)

    def short_conv(p_ref, w_ref, b_ref):
        p = p_ref[...].astype(F32)
        prev = jnp.where(row == 0, 0.0, pltpu.roll(p, 1, 0))
        nxt = jnp.where(row == n - 1, 0.0, pltpu.roll(p, n - 1, 0))
        w = w_ref[0]
        return prev * w[0:1] + p * w[1:2] + nxt * w[2:3] + b_ref[0]

    zin_ref[...] = (short_conv(x1_ref, w1_ref, b1_ref) * short_conv(v_ref, wv_ref, bv_ref)).astype(zin_ref.dtype)
    x2o_ref[...] = short_conv(x2_ref, w2_ref, b2_ref).astype(x2o_ref.dtype)


def _hy_conv(p, conv_w, conv_b, l, n, row_block0, *, tc=256):
    nc = HY_WIDTH // tc

    def seg(s):
        return (pl.BlockSpec((n, tc), lambda b, j: (row_block0 + b, s * nc + j)),
                pl.BlockSpec((1, 3, tc), lambda b, j: (l, 0, s * nc + j)),
                pl.BlockSpec((1, 1, tc), lambda b, j: (l, 0, s * nc + j)))

    (pv, wv, bv), (p1, w1, b1), (p2, w2, b2) = seg(0), seg(1), seg(2)
    out_spec = pl.BlockSpec((n, tc), lambda b, j: (b, j))
    out = jax.ShapeDtypeStruct((BATCH * n, HY_WIDTH), BF16)
    conv_b = conv_b.reshape(DEPTH, 1, -1)
    return pl.pallas_call(
        _conv_kernel,
        grid=(BATCH, nc),
        in_specs=[pv, p1, p2, wv, w1, w2, bv, b1, b2],
        out_specs=[out_spec, out_spec],
        out_shape=[out, out],
        compiler_params=_cparams(("parallel", "parallel")),
        name="hy_conv",
    )(p, p, p, conv_w, conv_w, conv_w, conv_b, conv_b, conv_b)


def _filt_kernel(z_ref, t_ref, dl_ref, w1_ref, b1_ref, w2_ref, b2_ref, w3_ref, fr_ref, hs_ref, ha_ref, kn_ref):
    n = z_ref.shape[0]
    fr = fr_ref[...]
    h = jnp.sin(fr * (_dot_hi(z_ref[...], w1_ref[...]) + b1_ref[...]))
    h = jnp.sin(fr * (_dot_hi(h, w2_ref[...]) + b2_ref[...]))
    h = _dot_hi(h, w3_ref[...])
    decay = jnp.exp(-t_ref[...] * dl_ref[...])
    row = lax.broadcasted_iota(jnp.int32, (n, 1), 0)
    hf = h[:, :HY_WIDTH] * decay
    hb = jnp.where(row == 0, 0.0, h[:, HY_WIDTH:] * decay)
    hs = hf + hb
    hs_ref[...] = hs.astype(hs_ref.dtype)
    ha_ref[...] = (hf - hb).astype(ha_ref.dtype)
    sign = jnp.where((row & 1) == 0, 1.0, -1.0)
    kn_ref[...] = jnp.sum(hs * sign, axis=0, keepdims=True)


def _hy_filter_taps(n, f_w1, f_b1, f_w2, f_b2, f_w3, f_freq):
    pos = jnp.arange(n, dtype=F32)
    t = jnp.linspace(0.0, 1.0, n, dtype=F32)
    bands = jnp.linspace(1e-4, HY_POS_BANDS - 1, HY_POS_BANDS, dtype=F32)
    ang = (2.0 * math.pi / n) * pos[:, None] * bands[None, :]
    z = jnp.concatenate([t[:, None], jnp.cos(ang), -jnp.sin(ang)], axis=-1)
    pad = HY_FILTER_HIDDEN - z.shape[1]
    z = jnp.pad(z, ((0, 0), (0, pad)))
    w1 = jnp.pad(f_w1.astype(F32), ((0, pad), (0, 0)))
    deltas = jnp.abs(jnp.linspace(math.log(HY_DECAY_TARGET) / HY_FAST_DECAY,
                                  math.log(HY_DECAY_TARGET) / HY_SLOW_DECAY, HY_WIDTH, dtype=F32))
    hid = HY_FILTER_HIDDEN
    return pl.pallas_call(
        _filt_kernel,
        out_shape=[
            jax.ShapeDtypeStruct((n, HY_WIDTH), BF16),
            jax.ShapeDtypeStruct((n, HY_WIDTH), BF16),
            jax.ShapeDtypeStruct((1, HY_WIDTH), F32),
        ],
        compiler_params=pltpu.CompilerParams(vmem_limit_bytes=VMEM_LIMIT),
        name="hy_filter",
    )(z, t[:, None], deltas[None, :], w1, f_b1.reshape(1, hid), f_w2, f_b2.reshape(1, hid), f_w3,
      f_freq.reshape(1, hid))


def _dft_tables(n):
    lo = int(round(math.sqrt(n)))
    while n % lo:
        lo -= 1
    hi = n // lo
    f = jnp.arange(n, dtype=jnp.int32)[:, None]
    big = 2 * n

    def ang(tt):
        return ((f * tt) % big).astype(F32) * (2.0 * math.pi / big)

    a = ang(lo * jnp.arange(hi, dtype=jnp.int32)[None, :])
    b = ang(jnp.arange(lo, dtype=jnp.int32)[None, :])
    ca, sa, cb, sb = jnp.cos(a), jnp.sin(a), jnp.cos(b), jnp.sin(b)
    cos = (ca[:, :, None] * cb[:, None, :] - sa[:, :, None] * sb[:, None, :]).reshape(n, n)
    msin = -(sa[:, :, None] * cb[:, None, :] + ca[:, :, None] * sb[:, None, :]).reshape(n, n)
    alt = jnp.where(jnp.arange(n) % 2 == 0, 1.0, -1.0).astype(F32)
    msin = jnp.where(f == 0, alt[None, :], msin)
    return cos.astype(BF16), msin.astype(BF16), msin.T.astype(BF16)


def _dft_filt_kernel(c_ref, s_ref, hs_ref, ha_ref, kr_ref, ki_ref):
    kr_ref[...] = _dot(c_ref[...], hs_ref[...])
    ki_ref[...] = _dot(s_ref[...], ha_ref[...])


def _hy_filter_spectrum(cos, msin, hs, ha, *, tf):
    n = cos.shape[0]
    tf = min(tf, n)
    tab = pl.BlockSpec((tf, n), lambda i: (i, 0))
    taps = pl.BlockSpec((n, HY_WIDTH), lambda i: (0, 0))
    out = pl.BlockSpec((tf, HY_WIDTH), lambda i: (i, 0))
    return pl.pallas_call(
        _dft_filt_kernel,
        grid=(n // tf,),
        in_specs=[tab, tab, taps, taps],
        out_specs=[out, out],
        out_shape=[jax.ShapeDtypeStruct((n, HY_WIDTH), F32)] * 2,
        compiler_params=_cparams(("parallel",)),
        name="hy_filter_dft",
    )(cos, msin, hs, ha)


def _dft_fwd_kernel(c_ref, s_ref, x_ref, kr_ref, ki_ref, kn_ref, yr_ref, yi_ref, *, tf, inv_n):
    x = x_ref[...]
    zr = _dot(c_ref[...], x)
    zi = _dot(s_ref[...], x)
    row = pl.program_id(0) * tf + lax.broadcasted_iota(jnp.int32, (tf, 1), 0)
    bin0 = row == 0
    kr = kr_ref[...]
    ki = jnp.where(bin0, 0.0, ki_ref[...])
    kr_im = jnp.where(bin0, kn_ref[...], kr)
    wt = jnp.where(bin0, inv_n, 2.0 * inv_n)
    yr_ref[0] = ((zr * kr - zi * ki) * wt).astype(yr_ref.dtype)
    yi_ref[0] = ((zr * ki + zi * kr_im) * wt).astype(yi_ref.dtype)


def _hy_dft_fwd(cos, msin, zin, kr, ki, kn, *, tf):
    n = cos.shape[0]
    tf = min(tf, n)
    tab = pl.BlockSpec((tf, n), lambda i, b: (i, 0))
    filt = pl.BlockSpec((tf, HY_WIDTH), lambda i, b: (i, 0))
    out = pl.BlockSpec((1, tf, HY_WIDTH), lambda i, b: (b, i, 0))
    return pl.pallas_call(
        functools.partial(_dft_fwd_kernel, tf=tf, inv_n=1.0 / (2 * n)),
        grid=(n // tf, BATCH),
        in_specs=[tab, tab, pl.BlockSpec((n, HY_WIDTH), lambda i, b: (b, 0)), filt, filt,
                  pl.BlockSpec((1, HY_WIDTH), lambda i, b: (0, 0))],
        out_specs=[out, out],
        out_shape=[jax.ShapeDtypeStruct((BATCH, n, HY_WIDTH), BF16)] * 2,
        compiler_params=_cparams(("parallel", "arbitrary")),
        name="hy_dft_fwd",
    )(cos, msin, zin, kr, ki, kn)


def _dft_inv_kernel(c_ref, st_ref, yr_ref, yi_ref, zin_ref, x2_ref, b_ref, o_ref):
    y = _dot(c_ref[...], yr_ref[0]) + _dot(st_ref[...], yi_ref[0])
    y = y + zin_ref[...].astype(F32) * b_ref[0]
    o_ref[...] = (x2_ref[...].astype(F32) * y).astype(o_ref.dtype)


def _hy_dft_inv(cos, msin_t, yr, yi, zin, x2, bias, l, *, tt):
    n = cos.shape[0]
    tt = min(tt, n)
    nt = n // tt
    tab = pl.BlockSpec((tt, n), lambda i, b: (i, 0))
    spec = pl.BlockSpec((1, n, HY_WIDTH), lambda i, b: (b, 0, 0))
    rows = pl.BlockSpec((tt, HY_WIDTH), lambda i, b: (b * nt + i, 0))
    return pl.pallas_call(
        _dft_inv_kernel,
        grid=(nt, BATCH),
        in_specs=[tab, tab, spec, spec, rows, rows, pl.BlockSpec((1, 1, HY_WIDTH), lambda i, b: (l, 0, 0))],
        out_specs=rows,
        out_shape=jax.ShapeDtypeStruct((BATCH * n, HY_WIDTH), BF16),
        compiler_params=_cparams(("parallel", "arbitrary")),
        name="hy_dft_inv",
    )(cos, msin_t, yr, yi, zin, x2, bias.reshape(DEPTH, 1, HY_WIDTH))


def _hyena(p, l, n, row_block0, tables, conv_w, conv_b, f_w1, f_b1, f_w2, f_b2, f_w3, f_freq, bias):
    cos, msin, msin_t = tables
    zin, x2 = _hy_conv(p, conv_w, conv_b, l, n, row_block0)
    hs, ha, kn = _hy_filter_taps(n, f_w1[l], f_b1[l], f_w2[l], f_b2[l], f_w3[l], f_freq[l])
    kr, ki = _hy_filter_spectrum(cos, msin, hs, ha, tf=512)
    yr, yi = _hy_dft_fwd(cos, msin, zin, kr, ki, kn, tf=512)
    return _hy_dft_inv(cos, msin_t, yr, yi, zin, x2, bias, l, tt=512)


def _rope_tables():
    tok = jnp.arange(SEQ)
    row = (tok // GRID_W).astype(F32)
    col = (tok % GRID_W).astype(F32)
    n_freq = MLA_ROPE // 4
    inv = ROPE_THETA ** (-jnp.arange(n_freq, dtype=F32) / n_freq)
    ang = jnp.concatenate([row[:, None] * inv, col[:, None] * inv], axis=-1)
    cos, sin = jnp.cos(ang), jnp.sin(ang)
    half = MLA_ROPE // 2
    zeros = jnp.zeros((SEQ, half), F32)
    rest = HEAD_DIM - MLA_ROPE
    cos_t = jnp.concatenate([cos, cos, jnp.ones((SEQ, rest), F32)], axis=-1)
    sin_a = jnp.concatenate([-sin, zeros, jnp.zeros((SEQ, rest), F32)], axis=-1)
    sin_b = jnp.concatenate([zeros, sin, jnp.zeros((SEQ, rest), F32)], axis=-1)
    ident = jnp.ones((N_CTX, HEAD_DIM), F32)
    none = jnp.zeros((N_CTX, HEAD_DIM), F32)
    return (jnp.concatenate([cos_t, ident]), jnp.concatenate([sin_a, none]), jnp.concatenate([sin_b, none]))


def _layout_w_uq(w):
    w = w.reshape(Q_LORA, MLA_HEADS, HEAD_DIM + MLA_ROPE)
    w = jnp.pad(w, ((0, 0), (0, 0), (0, MLA_QK - HEAD_DIM - MLA_ROPE)))
    return w.reshape(Q_LORA, MLA_HEADS * MLA_QK).astype(BF16)


def kernel(x, c, ctx, c_ctx, w_ada, b_ada, g_attn_pre, g_attn_post, g_ffn_pre, g_ffn_post, w_in, hy_conv_w, hy_conv_b, hy_f_w1, hy_f_b1, hy_f_w2, hy_f_b2, hy_f_w3, hy_f_freq, hy_bias, mla_g_q, mla_w_uq, mla_g_kv, mla_w_ukv, na_rpb, w_out, w_ffn_gate, w_ffn_up, w_ffn_down):
    cc = jnp.concatenate([c, c_ctx[None, :], jnp.zeros((8 - BATCH - 1, D_MODEL), F32)], axis=0)
    mods_all = _ada(cc, w_ada, b_ada)
    mods = [mods_all[l].reshape(8, 1, 6 * D_MODEL) for l in range(DEPTH)]

    rope_tabs = _rope_tables()
    dft_lat = _dft_tables(SEQ)
    dft_ctx = _dft_tables(CTX_LEN)
    na_starts, na_plan, na_pairs = _na_plan()
    na_t2 = _na_bias_pairs(na_rpb, na_pairs)
    w_in_t = jnp.swapaxes(w_in, 1, 2)
    hy_w = (hy_conv_w, hy_conv_b, hy_f_w1, hy_f_b1, hy_f_w2, hy_f_b2, hy_f_w3, hy_f_freq, hy_bias)

    res = (x.reshape(N_LAT, D_MODEL), ctx.reshape(N_CTX, D_MODEL))
    xn = _prenorm(res[0], res[1], g_attn_pre[0], mods[0])

    for l in range(DEPTH):
        ctx_out = l < DEPTH - 1
        m_rows = N_TOK if ctx_out else N_LAT
        p = _in_proj(xn, w_in_t, l)

        q = _q_proj(p, m_rows, mla_g_q[l], _layout_w_uq(mla_w_uq[l]), rope_tabs)
        k, v = _kv_proj(p, mla_g_kv[l], mla_w_ukv, l, rope_tabs)
        mla = [_mla_latent(q, k, v), None]
        na = [_na_latent(p, na_t2, l, na_starts, na_plan), None]
        hy = [_hyena(p, l, SEQ, 0, dft_lat, *hy_w), None]

        if ctx_out:
            mla[1] = _ctx_attend(q, k, v, 0, 0, 0, MLA_HEADS, MLA_QK, MLA_VW, 1.0, True, "mla_ctx")
            cq = COL_NA // HEAD_DIM
            na[1] = _ctx_attend(p, p, p, cq, cq + NA_HEADS, cq + 2 * NA_HEADS, NA_HEADS, HEAD_DIM, HEAD_DIM,
                                NA_SCALE * LOG2E, False, "na_ctx")
            hy[1] = _hyena(p, l, CTX_LEN, CTX_BLOCK0, dft_ctx, *hy_w)
            res_l = res
        else:
            res_l = (res[0], None)

        stream, xn = _proj_post([tuple(hy), tuple(mla), tuple(na)], w_out, l, g_attn_post[l], mods[l], 2, res_l,
                                m_rows, nxt=(g_ffn_pre[l], mods[l], 3, 4))
        h = _ffn_up(xn, w_ffn_gate, w_ffn_up, l)
        nxt = (g_attn_pre[l + 1], mods[l + 1], 0, 1) if ctx_out else None
        stream, xn = _proj_post([(h, None)], w_ffn_down, l, g_ffn_post[l], mods[l], 5, (stream, None), m_rows,
                                nxt=nxt)
        res = (stream, None)

    return stream.reshape(BATCH, SEQ, D_MODEL)
```

```python
import functools
import math

import jax
import jax.numpy as jnp
import numpy as np
from jax import lax
from jax.experimental import pallas as pl
from jax.experimental.pallas import tpu as pltpu

F32 = jnp.float32
BF16 = jnp.bfloat16

D_MODEL = 2048
BATCH = 4
SEQ = 2048
DEPTH = 2
GRID_W = 64
CTX_LEN = 256
HEAD_DIM = 128
HY_WIDTH = D_MODEL // 4
HY_FILTER_HIDDEN = 64
HY_POS_BANDS = 16
HY_DECAY_TARGET = 1e-2
HY_FAST_DECAY = 0.3
HY_SLOW_DECAY = 1.5
MLA_HEADS = (D_MODEL // 2) // HEAD_DIM
MLA_ROPE = 64
Q_LORA = 3 * D_MODEL // 8
KV_LORA = D_MODEL // 4
MLA_SCALE = (HEAD_DIM + MLA_ROPE) ** -0.5
NA_HEADS = (D_MODEL // 4) // HEAD_DIM
NA_KH = 8
NA_KW = 16
NA_SCALE = HEAD_DIM ** -0.5
FFN_HIDDEN = ((8 * D_MODEL + 3 * 256 - 1) // (3 * 256)) * 256
ROPE_THETA = 10000.0
RMS_EPS = 1e-6
MASK_VALUE = -1e30
LOG2E = math.log2(math.e)

N_LAT = BATCH * SEQ
N_CTX = BATCH * CTX_LEN
N_TOK = N_LAT + N_CTX
CTX_BLOCK0 = N_LAT // CTX_LEN

IN_TN = 256
COL_HY = 0
COL_CQ = 3 * HY_WIDTH
COL_CKV = COL_CQ + Q_LORA
COL_KR = COL_CKV + KV_LORA
COL_NA = COL_KR + IN_TN
P_COLS = COL_NA + 3 * NA_HEADS * HEAD_DIM
W_IN_NA = COL_KR + MLA_ROPE
MLA_QK = 256
MLA_VW = 256

NA_TR = 4
NA_WR = 12
NA_TQ = NA_TR * GRID_W
NA_TK = NA_WR * GRID_W

VMEM_LIMIT = 52 * 1024 * 1024


def _cparams(sem):
    return pltpu.CompilerParams(dimension_semantics=sem, vmem_limit_bytes=VMEM_LIMIT)


def _dot(a, b):
    return jnp.dot(a, b, preferred_element_type=F32)


def _dot_nt(a, b):
    return lax.dot_general(a, b, (((1,), (1,)), ((), ())), preferred_element_type=F32)


def _dot_hi(a, b):
    return jnp.dot(a, b, preferred_element_type=F32, precision=lax.Precision.HIGHEST)


def _rms(x, g):
    ms = jnp.mean(x * x, axis=-1, keepdims=True)
    return x * lax.rsqrt(ms + RMS_EPS) * g


def _mod_row(tm):
    n_lat, per_b = N_LAT // tm, SEQ // tm
    return lambda i: jnp.where(i < n_lat, i // per_b, BATCH)


def _rope_row(tm):
    n_lat, per_b = N_LAT // tm, SEQ // tm
    return lambda i: jnp.where(i < n_lat, i % per_b, per_b + i - n_lat)


def _ada_kernel(c_ref, w_ref, b_ref, o_ref):
    a = c_ref[...]
    a = a * jax.nn.sigmoid(a)
    o_ref[0] = _dot(a.astype(BF16), w_ref[0].astype(BF16)) + b_ref[0]


def _ada(cc, w_ada, b_ada):
    tn = 1024
    n = w_ada.shape[-1]
    return pl.pallas_call(
        _ada_kernel,
        grid=(DEPTH, n // tn),
        in_specs=[
            pl.BlockSpec((8, D_MODEL), lambda l, j: (0, 0)),
            pl.BlockSpec((1, D_MODEL, tn), lambda l, j: (l, 0, j)),
            pl.BlockSpec((1, 1, tn), lambda l, j: (l, 0, j)),
        ],
        out_specs=pl.BlockSpec((1, 8, tn), lambda l, j: (l, 0, j)),
        out_shape=jax.ShapeDtypeStruct((DEPTH, 8, n), F32),
        compiler_params=_cparams(("parallel", "parallel")),
        name="ada",
    )(cc, w_ada, b_ada.reshape(DEPTH, 1, n))


def _norm_mod_to(xn_ref, x_ref, g_ref, sh_ref, sc_ref, chunk=256):
    g = g_ref[...]
    sc = 1.0 + sc_ref[0]
    sh = sh_ref[0]

    def body(r, carry):
        rows = pl.ds(pl.multiple_of(r * chunk, chunk), chunk)
        xn_ref[rows, :] = (_rms(x_ref[rows, :], g) * sc + sh).astype(BF16)
        return carry

    lax.fori_loop(0, x_ref.shape[0] // chunk, body, 0)


def _prenorm_kernel(x_ref, c_ref, g_ref, sh_ref, sc_ref, o_ref, *, n_lat):
    i = pl.program_id(0)

    @pl.when(i < n_lat)
    def _():
        _norm_mod_to(o_ref, x_ref, g_ref, sh_ref, sc_ref)

    @pl.when(i >= n_lat)
    def _():
        _norm_mod_to(o_ref, c_ref, g_ref, sh_ref, sc_ref)


def _prenorm(x2d, ctx2d, g, mods, *, tm=1024):
    n_lat = N_LAT // tm
    row = _mod_row(tm)
    return pl.pallas_call(
        functools.partial(_prenorm_kernel, n_lat=n_lat),
        grid=(N_TOK // tm,),
        in_specs=[
            pl.BlockSpec((tm, D_MODEL), lambda i: (jnp.minimum(i, n_lat - 1), 0)),
            pl.BlockSpec((tm, D_MODEL), lambda i: (jnp.maximum(i - n_lat, 0), 0)),
            pl.BlockSpec((1, D_MODEL), lambda i: (0, 0)),
            pl.BlockSpec((1, 1, D_MODEL), lambda i: (row(i), 0, 0)),
            pl.BlockSpec((1, 1, D_MODEL), lambda i: (row(i), 0, 1)),
        ],
        out_specs=pl.BlockSpec((tm, D_MODEL), lambda i: (i, 0)),
        out_shape=jax.ShapeDtypeStruct((N_TOK, D_MODEL), BF16),
        compiler_params=_cparams(("parallel",)),
        name="prenorm",
    )(x2d, ctx2d, g.reshape(1, D_MODEL), mods, mods)


def _in_kernel(x_ref, w_ref, o_ref):
    o_ref[...] = _dot_nt(x_ref[...], w_ref[0].astype(BF16)).astype(o_ref.dtype)


def _in_proj(xn, w_in_t, l, *, tm=3072):
    n_head = COL_NA // IN_TN

    def w_row(j):
        per = IN_TN // MLA_ROPE
        return MLA_ROPE * jnp.where(j < n_head, j * per, W_IN_NA // MLA_ROPE + (j - n_head) * per)

    return pl.pallas_call(
        _in_kernel,
        grid=(N_TOK // tm, P_COLS // IN_TN),
        in_specs=[
            pl.BlockSpec((tm, D_MODEL), lambda i, j: (i, 0)),
            pl.BlockSpec((pl.Element(1), pl.Element(IN_TN), pl.Element(D_MODEL)), lambda i, j: (l, w_row(j), 0)),
        ],
        out_specs=pl.BlockSpec((tm, IN_TN), lambda i, j: (i, j)),
        out_shape=jax.ShapeDtypeStruct((N_TOK, P_COLS), BF16),
        compiler_params=_cparams(("parallel", "arbitrary")),
        name="in_proj",
    )(xn, w_in_t)


def _swiglu_kernel(x_ref, wg_ref, wu_ref, o_ref):
    xn = x_ref[...]
    gate = _dot(xn, wg_ref[0].astype(BF16))
    up = _dot(xn, wu_ref[0].astype(BF16))
    o_ref[...] = (gate * jax.nn.sigmoid(gate) * up).astype(o_ref.dtype)


def _ffn_up(xn, wg, wu, l, *, tn=256):
    m = xn.shape[0]
    tm = 2048 if m % 2048 == 0 else 1536
    w_spec = pl.BlockSpec((1, D_MODEL, tn), lambda i, j: (l, 0, j))
    return pl.pallas_call(
        _swiglu_kernel,
        grid=(m // tm, FFN_HIDDEN // tn),
        in_specs=[pl.BlockSpec((tm, D_MODEL), lambda i, j: (i, 0)), w_spec, w_spec],
        out_specs=pl.BlockSpec((tm, tn), lambda i, j: (i, j)),
        out_shape=jax.ShapeDtypeStruct((m, FFN_HIDDEN), BF16),
        compiler_params=_cparams(("parallel", "arbitrary")),
        name="ffn_up",
    )(xn, wg, wu)


POST_TM = 256
POST_SUB = 128
POST_WCH = 256
POST_SLOTS = 3


def _post_kernel(*refs, widths, two_src, emit_xn, n_lat, l):
    refs = list(refs)
    n_act = len(widths)
    acts_lat = [refs.pop(0) for _ in range(n_act)]
    acts_ctx = [refs.pop(0) for _ in range(n_act)] if two_src else None
    w_hbm, g_ref, gate_ref, res_lat = (refs.pop(0) for _ in range(4))
    res_ctx = refs.pop(0) if two_src else None
    if emit_xn:
        g2_ref, sh_ref, sc_ref = (refs.pop(0) for _ in range(3))
    o_hbm = refs.pop(0)
    xn_ref = refs.pop(0) if emit_xn else None
    w_scr, stage, buf, sem_w, sem_res, sem_out = refs

    tm = POST_TM
    i = pl.program_id(0)
    n_i = pl.num_programs(0)
    n_chunks = sum(widths) // POST_WCH
    is_lat = i < n_lat
    is_ctx = jnp.logical_not(is_lat)

    def w_copy(c, slot):
        return pltpu.make_async_copy(w_hbm.at[l, pl.ds(c * POST_WCH, POST_WCH)], stage.at[slot], sem_w.at[slot])

    def res_copy(src, row0, slot):
        return pltpu.make_async_copy(src.at[pl.ds(row0, tm)], buf.at[slot], sem_res.at[slot])

    def res_start(tile, slot):
        if two_src:
            pl.when(tile < n_lat)(lambda: res_copy(res_lat, tile * tm, slot).start())
            pl.when(tile >= n_lat)(lambda: res_copy(res_ctx, (tile - n_lat) * tm, slot).start())
        else:
            res_copy(res_lat, tile * tm, slot).start()

    def out_copy(tile, slot):
        return pltpu.make_async_copy(buf.at[slot], o_hbm.at[pl.ds(tile * tm, tm)], sem_out.at[slot])

    @pl.when(i == 0)
    def _():
        res_start(0, 0)
        w_copy(0, 0).start()

        def body(c, carry):
            s = c % 2
            w_copy(c, s).wait()

            @pl.when(c + 1 < n_chunks)
            def _():
                w_copy(c + 1, 1 - s).start()

            w_scr[pl.ds(pl.multiple_of(c * POST_WCH, POST_WCH), POST_WCH), :] = stage[s].astype(BF16)
            return carry

        lax.fori_loop(0, n_chunks, body, 0)

    slot = i % POST_SLOTS
    nslot = (i + 1) % POST_SLOTS

    @pl.when(i + 1 < n_i)
    def _():
        @pl.when(i >= POST_SLOTS - 1)
        def _():
            out_copy(i + 1 - POST_SLOTS, nslot).wait()

        res_start(i + 1, nslot)

    res_copy(res_lat, 0, slot).wait()

    def compute(act_refs):
        g = g_ref[...]
        gate = gate_ref[0]
        if emit_xn:
            g2 = g2_ref[...]
            sc = 1.0 + sc_ref[0]
            sh = sh_ref[0]
        for r in range(tm // POST_SUB):
            rows = pl.ds(r * POST_SUB, POST_SUB)
            y = None
            k0 = 0
            for a_ref, kw in zip(act_refs, widths):
                part = _dot(a_ref[rows, :], w_scr[k0:k0 + kw, :])
                y = part if y is None else y + part
                k0 += kw
            x_new = buf[slot, rows, :] + gate * _rms(y, g)
            buf[slot, rows, :] = x_new
            if emit_xn:
                xn_ref[rows, :] = (_rms(x_new, g2) * sc + sh).astype(xn_ref.dtype)

    if two_src:
        pl.when(is_lat)(functools.partial(compute, acts_lat))
        pl.when(is_ctx)(functools.partial(compute, acts_ctx))
    else:
        compute(acts_lat)

    out_copy(i, slot).start()

    @pl.when(i == n_i - 1)
    def _():
        for back in range(POST_SLOTS):
            out_copy(i - back, (i - back) % POST_SLOTS).wait()


def _proj_post(acts, w, l, g, mods, gate_chunk, res, m_rows, nxt=None):
    two_src = res[1] is not None
    tm = POST_TM
    n_lat = N_LAT // tm
    widths = tuple(a.shape[1] for a, _ in acts)
    k_total = sum(widths)
    emit_xn = nxt is not None
    row = _mod_row(tm)

    def lat_row(i):
        return jnp.minimum(i, n_lat - 1) if two_src else i

    def ctx_row(i):
        return jnp.maximum(i - n_lat, 0)

    def act_specs(rowf):
        return [pl.BlockSpec((tm, kw), lambda i: (rowf(i), 0)) for kw in widths]

    vec_spec = pl.BlockSpec((1, D_MODEL), lambda i: (0, 0))

    def mod_spec(chunk):
        return pl.BlockSpec((1, 1, D_MODEL), lambda i: (row(i), 0, chunk))

    hbm = pl.BlockSpec(memory_space=pl.ANY)
    in_specs = act_specs(lat_row)
    args = [a for a, _ in acts]
    if two_src:
        in_specs += act_specs(ctx_row)
        args += [c for _, c in acts]
    in_specs += [hbm, vec_spec, mod_spec(gate_chunk), hbm]
    args += [w, g.reshape(1, D_MODEL), mods, res[0]]
    if two_src:
        in_specs.append(hbm)
        args.append(res[1])
    out_specs = [hbm]
    out_shape = [jax.ShapeDtypeStruct((m_rows, D_MODEL), F32)]
    if emit_xn:
        g2, mods2, sh_chunk, sc_chunk = nxt
        in_specs += [vec_spec, mod_spec(sh_chunk), mod_spec(sc_chunk)]
        args += [g2.reshape(1, D_MODEL), mods2, mods2]
        out_specs.append(pl.BlockSpec((tm, D_MODEL), lambda i: (i, 0)))
        out_shape.append(jax.ShapeDtypeStruct((m_rows, D_MODEL), BF16))

    out = pl.pallas_call(
        functools.partial(_post_kernel, widths=widths, two_src=two_src, emit_xn=emit_xn, n_lat=n_lat, l=l),
        grid=(m_rows // tm,),
        in_specs=in_specs,
        out_specs=out_specs,
        out_shape=out_shape,
        scratch_shapes=[
            pltpu.VMEM((k_total, D_MODEL), BF16),
            pltpu.VMEM((2, POST_WCH, D_MODEL), F32),
            pltpu.VMEM((POST_SLOTS, tm, D_MODEL), F32),
            pltpu.SemaphoreType.DMA((2,)),
            pltpu.SemaphoreType.DMA((POST_SLOTS,)),
            pltpu.SemaphoreType.DMA((POST_SLOTS,)),
        ],
        compiler_params=_cparams(("arbitrary",)),
        name="proj_post",
    )(*args)
    return (out[0], out[1]) if emit_xn else (out[0], None)


def _rope128(r, cos_ref, sa_ref, sb_ref):
    return r * cos_ref[...] + pltpu.roll(r, 96, 1) * sa_ref[...] + pltpu.roll(r, 32, 1) * sb_ref[...]


def _q_kernel(x_ref, g_ref, w_ref, cos_ref, sa_ref, sb_ref, o_ref):
    xn = _rms(x_ref[...].astype(F32), g_ref[...]).astype(BF16)
    for h in range(MLA_HEADS):
        acc = _dot(xn, w_ref[:, h * MLA_QK:(h + 1) * MLA_QK]) * (MLA_SCALE * LOG2E)
        o_ref[:, h * MLA_QK:h * MLA_QK + HEAD_DIM] = acc[:, :HEAD_DIM].astype(o_ref.dtype)
        o_ref[:, h * MLA_QK + HEAD_DIM:(h + 1) * MLA_QK] = _rope128(
            acc[:, HEAD_DIM:], cos_ref, sa_ref, sb_ref).astype(o_ref.dtype)


def _q_proj(p, m_rows, g, w, tabs, *, tm=1024):
    rope = _rope_row(tm)
    tab_spec = pl.BlockSpec((tm, HEAD_DIM), lambda i: (rope(i), 0))
    return pl.pallas_call(
        _q_kernel,
        grid=(m_rows // tm,),
        in_specs=[
            pl.BlockSpec((tm, Q_LORA), lambda i: (i, COL_CQ // Q_LORA)),
            pl.BlockSpec((1, Q_LORA), lambda i: (0, 0)),
            pl.BlockSpec((Q_LORA, MLA_HEADS * MLA_QK), lambda i: (0, 0)),
            tab_spec, tab_spec, tab_spec,
        ],
        out_specs=pl.BlockSpec((tm, MLA_HEADS * MLA_QK), lambda i: (i, 0)),
        out_shape=jax.ShapeDtypeStruct((m_rows, MLA_HEADS * MLA_QK), BF16),
        compiler_params=_cparams(("parallel",)),
        name="q_proj",
    )(p, g.reshape(1, Q_LORA), w, *tabs)


def _kv_kernel(xa_ref, xb_ref, kr_ref, g_ref, w_ref, cos_ref, sa_ref, sb_ref, k_ref, v_ref):
    half = KV_LORA // 2
    xa = xa_ref[...].astype(F32)
    xb = xb_ref[...].astype(F32)
    ms = (jnp.sum(xa * xa, axis=-1, keepdims=True) + jnp.sum(xb * xb, axis=-1, keepdims=True)) * (1.0 / KV_LORA)
    rs = lax.rsqrt(ms + RMS_EPS)
    g = g_ref[...]
    xna = (xa * rs * g[:, :half]).astype(BF16)
    xnb = (xb * rs * g[:, half:]).astype(BF16)
    krr = _rope128(kr_ref[...].astype(F32), cos_ref, sa_ref, sb_ref).astype(k_ref.dtype)
    ones = jnp.ones((xa.shape[0], MLA_VW - HEAD_DIM), v_ref.dtype)
    for h in range(MLA_HEADS):
        w = w_ref[0, :, h * 2 * HEAD_DIM:(h + 1) * 2 * HEAD_DIM].astype(BF16)
        acc = _dot(xna, w[:half]) + _dot(xnb, w[half:])
        k_ref[:, h * MLA_QK:h * MLA_QK + HEAD_DIM] = acc[:, :HEAD_DIM].astype(k_ref.dtype)
        k_ref[:, h * MLA_QK + HEAD_DIM:(h + 1) * MLA_QK] = krr
        v_ref[:, h * MLA_VW:h * MLA_VW + HEAD_DIM] = acc[:, HEAD_DIM:].astype(v_ref.dtype)
        v_ref[:, h * MLA_VW + HEAD_DIM:(h + 1) * MLA_VW] = ones


def _kv_proj(p, g, w, l, tabs, *, tm=1024):
    m = p.shape[0]
    rope = _rope_row(tm)
    half = KV_LORA // 2
    tab_spec = pl.BlockSpec((tm, HEAD_DIM), lambda i: (rope(i), 0))
    return pl.pallas_call(
        _kv_kernel,
        grid=(m // tm,),
        in_specs=[
            pl.BlockSpec((tm, half), lambda i: (i, COL_CKV // half)),
            pl.BlockSpec((tm, half), lambda i: (i, COL_CKV // half + 1)),
            pl.BlockSpec((tm, HEAD_DIM), lambda i: (i, COL_KR // HEAD_DIM)),
            pl.BlockSpec((1, KV_LORA), lambda i: (0, 0)),
            pl.BlockSpec((1, KV_LORA, MLA_HEADS * 2 * HEAD_DIM), lambda i: (l, 0, 0)),
            tab_spec, tab_spec, tab_spec,
        ],
        out_specs=[
            pl.BlockSpec((tm, MLA_HEADS * MLA_QK), lambda i: (i, 0)),
            pl.BlockSpec((tm, MLA_HEADS * MLA_VW), lambda i: (i, 0)),
        ],
        out_shape=[
            jax.ShapeDtypeStruct((m, MLA_HEADS * MLA_QK), BF16),
            jax.ShapeDtypeStruct((m, MLA_HEADS * MLA_VW), BF16),
        ],
        compiler_params=_cparams(("parallel",)),
        name="kv_proj",
    )(p, p, p, g.reshape(1, KV_LORA), w, *tabs)


def _softmax_pv(s_list, v_list, ones_col):
    m = jnp.max(s_list[0], axis=-1, keepdims=True)
    for s in s_list[1:]:
        m = jnp.maximum(m, jnp.max(s, axis=-1, keepdims=True))
    acc = None
    den = None
    for s, v in zip(s_list, v_list):
        p = jnp.exp2(s - m)
        if not ones_col:
            d = jnp.sum(p, axis=-1, keepdims=True)
            den = d if den is None else den + d
        o = _dot(p.astype(BF16), v)
        acc = o if acc is None else acc + o
    if ones_col:
        return acc[:, :HEAD_DIM] / acc[:, HEAD_DIM:]
    return acc / den


def _attn_kernel(*refs, scale, two, ones_col, chains):
    if two:
        q_ref, k1_ref, v1_ref, k2_ref, v2_ref, o_ref = refs
    else:
        q_ref, k1_ref, v1_ref, o_ref = refs
    tq = q_ref.shape[0] // chains
    for c in range(chains):
        q = q_ref[c * tq:(c + 1) * tq, :]
        s_list = [_dot_nt(q, k1_ref[...])]
        v_list = [v1_ref[...]]
        if two:
            s_list.append(_dot_nt(q, k2_ref[...]))
            v_list.append(v2_ref[...])
        if scale != 1.0:
            s_list = [s * scale for s in s_list]
        o_ref[c * tq:(c + 1) * tq, :] = _softmax_pv(s_list, v_list, ones_col).astype(o_ref.dtype)


def _mla_latent(q, k, v, *, tq=2048):
    nq = SEQ // tq
    return pl.pallas_call(
        functools.partial(_attn_kernel, scale=1.0, two=True, ones_col=True, chains=8),
        grid=(BATCH, MLA_HEADS, nq),
        in_specs=[
            pl.BlockSpec((tq, MLA_QK), lambda b, h, i: (b * nq + i, h)),
            pl.BlockSpec((SEQ, MLA_QK), lambda b, h, i: (b, h)),
            pl.BlockSpec((SEQ, MLA_VW), lambda b, h, i: (b, h)),
            pl.BlockSpec((CTX_LEN, MLA_QK), lambda b, h, i: (CTX_BLOCK0 + b, h)),
            pl.BlockSpec((CTX_LEN, MLA_VW), lambda b, h, i: (CTX_BLOCK0 + b, h)),
        ],
        out_specs=pl.BlockSpec((tq, HEAD_DIM), lambda b, h, i: (b * nq + i, h)),
        out_shape=jax.ShapeDtypeStruct((N_LAT, MLA_HEADS * HEAD_DIM), BF16),
        compiler_params=_cparams(("parallel", "parallel", "arbitrary")),
        name="mla_latent",
    )(q, k, v, k, v)


def _ctx_attend(q, k, v, q_col0, k_col0, v_col0, heads, dqk, dv, scale, ones_col, name):
    return pl.pallas_call(
        functools.partial(_attn_kernel, scale=scale, two=False, ones_col=ones_col, chains=1),
        grid=(BATCH, heads),
        in_specs=[
            pl.BlockSpec((CTX_LEN, dqk), lambda b, h: (CTX_BLOCK0 + b, q_col0 + h)),
            pl.BlockSpec((CTX_LEN, dqk), lambda b, h: (CTX_BLOCK0 + b, k_col0 + h)),
            pl.BlockSpec((CTX_LEN, dv), lambda b, h: (CTX_BLOCK0 + b, v_col0 + h)),
        ],
        out_specs=pl.BlockSpec((CTX_LEN, HEAD_DIM), lambda b, h: (b, h)),
        out_shape=jax.ShapeDtypeStruct((N_CTX, heads * HEAD_DIM), BF16),
        compiler_params=_cparams(("parallel", "parallel")),
        name=name,
    )(q, k, v)


def _na_plan():
    rows = SEQ // GRID_W
    invalid = 2 * NA_KH - 1
    pairs, plan, starts = [], [], []
    for t in range(rows // NA_TR):
        kw0 = int(np.clip(NA_TR * t - NA_KH // 2, 0, rows - NA_WR))
        starts.append(kw0)
        tile = []
        for ri in range(NA_TR):
            r = NA_TR * t + ri
            r0 = int(np.clip(r - NA_KH // 2, 0, rows - NA_KH))
            assert kw0 <= r0 and r0 + NA_KH <= kw0 + NA_WR
            row = []
            for kp in range(NA_WR // 2):
                pair = []
                for kr in (kw0 + 2 * kp, kw0 + 2 * kp + 1):
                    pair.append(kr - r + NA_KH - 1 if r0 <= kr < r0 + NA_KH else invalid)
                pair = tuple(pair)
                if pair not in pairs:
                    pairs.append(pair)
                row.append(pairs.index(pair))
            tile.append(row)
        plan.append(tile)
    return starts, plan, pairs


def _na_bias_pairs(rpb, pairs):
    c = np.arange(GRID_W)
    c0 = np.clip(c - NA_KW // 2, 0, GRID_W - NA_KW)
    col_ok = (c[None, :] >= c0[:, None]) & (c[None, :] < c0[:, None] + NA_KW)
    col_idx = np.clip(c[None, :] - c[:, None] + NA_KW - 1, 0, 2 * NA_KW - 2)
    onehot = (col_idx[None] == np.arange(2 * NA_KW - 1)[:, None, None]).astype(np.float32)
    t = jnp.einsum("lhdj,jck->lhdck", rpb.astype(F32), onehot, precision=lax.Precision.HIGHEST) * LOG2E
    t = jnp.where(col_ok, t, MASK_VALUE)
    masked = jnp.full(t.shape[:2] + (GRID_W, GRID_W), MASK_VALUE, F32)
    slabs = [t[:, :, d] for d in range(2 * NA_KH - 1)] + [masked]
    return jnp.stack([jnp.concatenate([slabs[a], slabs[b]], axis=-1) for a, b in pairs], axis=2)


def _na_kernel(q_ref, k_ref, v_ref, kc_ref, vc_ref, t2_ref, o_ref, *, starts, plan):
    kc = kc_ref[...]
    vc = vc_ref[...]
    for t, (kw0, tile) in enumerate(zip(starts, plan)):
        q = q_ref[t * NA_TQ:(t + 1) * NA_TQ, :]
        kw = k_ref[kw0 * GRID_W:kw0 * GRID_W + NA_TK, :]
        vw = v_ref[kw0 * GRID_W:kw0 * GRID_W + NA_TK, :]
        bias = jnp.concatenate(
            [jnp.concatenate([t2_ref[0, 0, idx] for idx in row], axis=1) for row in tile], axis=0)
        s = _dot_nt(q, kw) * (NA_SCALE * LOG2E) + bias
        sc = _dot_nt(q, kc) * (NA_SCALE * LOG2E)
        o_ref[t * NA_TQ:(t + 1) * NA_TQ, :] = _softmax_pv([s, sc], [vw, vc], False).astype(o_ref.dtype)


def _na_latent(p, t2, l, starts, plan):
    cq = COL_NA // HEAD_DIM
    ck = cq + NA_HEADS
    cv = ck + NA_HEADS
    n_pairs = t2.shape[2]
    return pl.pallas_call(
        functools.partial(_na_kernel, starts=starts, plan=plan),
        grid=(NA_HEADS, BATCH),
        in_specs=[
            pl.BlockSpec((SEQ, HEAD_DIM), lambda h, b: (b, cq + h)),
            pl.BlockSpec((SEQ, HEAD_DIM), lambda h, b: (b, ck + h)),
            pl.BlockSpec((SEQ, HEAD_DIM), lambda h, b: (b, cv + h)),
            pl.BlockSpec((CTX_LEN, HEAD_DIM), lambda h, b: (CTX_BLOCK0 + b, ck + h)),
            pl.BlockSpec((CTX_LEN, HEAD_DIM), lambda h, b: (CTX_BLOCK0 + b, cv + h)),
            pl.BlockSpec((1, 1, n_pairs, GRID_W, 2 * GRID_W), lambda h, b: (l, h, 0, 0, 0)),
        ],
        out_specs=pl.BlockSpec((SEQ, HEAD_DIM), lambda h, b: (b, h)),
        out_shape=jax.ShapeDtypeStruct((N_LAT, NA_HEADS * HEAD_DIM), BF16),
        compiler_params=_cparams(("parallel", "parallel")),
        name="na_latent",
    )(p, p, p, p, p, t2)


def _conv_kernel(v_ref, x1_ref, x2_ref, wv_ref, w1_ref, w2_ref, bv_ref, b1_ref, b2_ref, zin_ref, x2o_ref):
    n = v_ref.shape[0]
    row = lax.broadcasted_iota(jnp.int32, (n, 1), 0)

    def short_conv(p_ref, w_ref, b_ref):
        p = p_ref[...].astype(F32)
        prev = jnp.where(row == 0, 0.0, pltpu.roll(p, 1, 0))
        nxt = jnp.where(row == n - 1, 0.0, pltpu.roll(p, n - 1, 0))
        w = w_ref[0]
        return prev * w[0:1] + p * w[1:2] + nxt * w[2:3] + b_ref[0]

    zin_ref[...] = (short_conv(x1_ref, w1_ref, b1_ref) * short_conv(v_ref, wv_ref, bv_ref)).astype(zin_ref.dtype)
    x2o_ref[...] = short_conv(x2_ref, w2_ref, b2_ref).astype(x2o_ref.dtype)


def _hy_conv(p, conv_w, conv_b, l, n, row_block0, *, tc=256):
    nc = HY_WIDTH // tc

    def seg(s):
        return (pl.BlockSpec((n, tc), lambda b, j: (row_block0 + b, s * nc + j)),
                pl.BlockSpec((1, 3, tc), lambda b, j: (l, 0, s * nc + j)),
                pl.BlockSpec((1, 1, tc), lambda b, j: (l, 0, s * nc + j)))

    (pv, wv, bv), (p1, w1, b1), (p2, w2, b2) = seg(0), seg(1), seg(2)
    out_spec = pl.BlockSpec((n, tc), lambda b, j: (b, j))
    out = jax.ShapeDtypeStruct((BATCH * n, HY_WIDTH), BF16)
    conv_b = conv_b.reshape(DEPTH, 1, -1)
    return pl.pallas_call(
        _conv_kernel,
        grid=(BATCH, nc),
        in_specs=[pv, p1, p2, wv, w1, w2, bv, b1, b2],
        out_specs=[out_spec, out_spec],
        out_shape=[out, out],
        compiler_params=_cparams(("parallel", "parallel")),
        name="hy_conv",
    )(p, p, p, conv_w, conv_w, conv_w, conv_b, conv_b, conv_b)


def _filt_kernel(z_ref, t_ref, dl_ref, w1_ref, b1_ref, w2_ref, b2_ref, w3_ref, fr_ref, hs_ref, ha_ref, kn_ref):
    n = z_ref.shape[0]
    fr = fr_ref[...]
    h = jnp.sin(fr * (_dot_hi(z_ref[...], w1_ref[...]) + b1_ref[...]))
    h = jnp.sin(fr * (_dot_hi(h, w2_ref[...]) + b2_ref[...]))
    h = _dot_hi(h, w3_ref[...])
    decay = jnp.exp(-t_ref[...] * dl_ref[...])
    row = lax.broadcasted_iota(jnp.int32, (n, 1), 0)
    hf = h[:, :HY_WIDTH] * decay
    hb = jnp.where(row == 0, 0.0, h[:, HY_WIDTH:] * decay)
    hs = hf + hb
    hs_ref[...] = hs.astype(hs_ref.dtype)
    ha_ref[...] = (hf - hb).astype(ha_ref.dtype)
    sign = jnp.where((row & 1) == 0, 1.0, -1.0)
    kn_ref[...] = jnp.sum(hs * sign, axis=0, keepdims=True)


def _hy_filter_taps(n, f_w1, f_b1, f_w2, f_b2, f_w3, f_freq):
    pos = jnp.arange(n, dtype=F32)
    t = jnp.linspace(0.0, 1.0, n, dtype=F32)
    bands = jnp.linspace(1e-4, HY_POS_BANDS - 1, HY_POS_BANDS, dtype=F32)
    ang = (2.0 * math.pi / n) * pos[:, None] * bands[None, :]
    z = jnp.concatenate([t[:, None], jnp.cos(ang), -jnp.sin(ang)], axis=-1)
    pad = HY_FILTER_HIDDEN - z.shape[1]
    z = jnp.pad(z, ((0, 0), (0, pad)))
    w1 = jnp.pad(f_w1.astype(F32), ((0, pad), (0, 0)))
    deltas = jnp.abs(jnp.linspace(math.log(HY_DECAY_TARGET) / HY_FAST_DECAY,
                                  math.log(HY_DECAY_TARGET) / HY_SLOW_DECAY, HY_WIDTH, dtype=F32))
    hid = HY_FILTER_HIDDEN
    return pl.pallas_call(
        _filt_kernel,
        out_shape=[
            jax.ShapeDtypeStruct((n, HY_WIDTH), BF16),
            jax.ShapeDtypeStruct((n, HY_WIDTH), BF16),
            jax.ShapeDtypeStruct((1, HY_WIDTH), F32),
        ],
        compiler_params=pltpu.CompilerParams(vmem_limit_bytes=VMEM_LIMIT),
        name="hy_filter",
    )(z, t[:, None], deltas[None, :], w1, f_b1.reshape(1, hid), f_w2, f_b2.reshape(1, hid), f_w3,
      f_freq.reshape(1, hid))


def _dft_tables(n):
    lo = int(round(math.sqrt(n)))
    while n % lo:
        lo -= 1
    hi = n // lo
    f = jnp.arange(n, dtype=jnp.int32)[:, None]
    big = 2 * n

    def ang(tt):
        return ((f * tt) % big).astype(F32) * (2.0 * math.pi / big)

    a = ang(lo * jnp.arange(hi, dtype=jnp.int32)[None, :])
    b = ang(jnp.arange(lo, dtype=jnp.int32)[None, :])
    ca, sa, cb, sb = jnp.cos(a), jnp.sin(a), jnp.cos(b), jnp.sin(b)
    cos = (ca[:, :, None] * cb[:, None, :] - sa[:, :, None] * sb[:, None, :]).reshape(n, n)
    msin = -(sa[:, :, None] * cb[:, None, :] + ca[:, :, None] * sb[:, None, :]).reshape(n, n)
    alt = jnp.where(jnp.arange(n) % 2 == 0, 1.0, -1.0).astype(F32)
    msin = jnp.where(f == 0, alt[None, :], msin)
    return cos.astype(BF16), msin.astype(BF16), msin.T.astype(BF16)


def _dft_filt_kernel(c_ref, s_ref, hs_ref, ha_ref, kr_ref, ki_ref):
    kr_ref[...] = _dot(c_ref[...], hs_ref[...])
    ki_ref[...] = _dot(s_ref[...], ha_ref[...])


def _hy_filter_spectrum(cos, msin, hs, ha, *, tf):
    n = cos.shape[0]
    tf = min(tf, n)
    tab = pl.BlockSpec((tf, n), lambda i: (i, 0))
    taps = pl.BlockSpec((n, HY_WIDTH), lambda i: (0, 0))
    out = pl.BlockSpec((tf, HY_WIDTH), lambda i: (i, 0))
    return pl.pallas_call(
        _dft_filt_kernel,
        grid=(n // tf,),
        in_specs=[tab, tab, taps, taps],
        out_specs=[out, out],
        out_shape=[jax.ShapeDtypeStruct((n, HY_WIDTH), F32)] * 2,
        compiler_params=_cparams(("parallel",)),
        name="hy_filter_dft",
    )(cos, msin, hs, ha)


def _dft_fwd_kernel(c_ref, s_ref, x_ref, kr_ref, ki_ref, kn_ref, yr_ref, yi_ref, *, tf, inv_n):
    x = x_ref[...]
    zr = _dot(c_ref[...], x)
    zi = _dot(s_ref[...], x)
    row = pl.program_id(0) * tf + lax.broadcasted_iota(jnp.int32, (tf, 1), 0)
    bin0 = row == 0
    kr = kr_ref[...]
    ki = jnp.where(bin0, 0.0, ki_ref[...])
    kr_im = jnp.where(bin0, kn_ref[...], kr)
    wt = jnp.where(bin0, inv_n, 2.0 * inv_n)
    yr_ref[0] = ((zr * kr - zi * ki) * wt).astype(yr_ref.dtype)
    yi_ref[0] = ((zr * ki + zi * kr_im) * wt).astype(yi_ref.dtype)


def _hy_dft_fwd(cos, msin, zin, kr, ki, kn, *, tf):
    n = cos.shape[0]
    tf = min(tf, n)
    tab = pl.BlockSpec((tf, n), lambda i, b: (i, 0))
    filt = pl.BlockSpec((tf, HY_WIDTH), lambda i, b: (i, 0))
    out = pl.BlockSpec((1, tf, HY_WIDTH), lambda i, b: (b, i, 0))
    return pl.pallas_call(
        functools.partial(_dft_fwd_kernel, tf=tf, inv_n=1.0 / (2 * n)),
        grid=(n // tf, BATCH),
        in_specs=[tab, tab, pl.BlockSpec((n, HY_WIDTH), lambda i, b: (b, 0)), filt, filt,
                  pl.BlockSpec((1, HY_WIDTH), lambda i, b: (0, 0))],
        out_specs=[out, out],
        out_shape=[jax.ShapeDtypeStruct((BATCH, n, HY_WIDTH), BF16)] * 2,
        compiler_params=_cparams(("parallel", "arbitrary")),
        name="hy_dft_fwd",
    )(cos, msin, zin, kr, ki, kn)


def _dft_inv_kernel(c_ref, st_ref, yr_ref, yi_ref, zin_ref, x2_ref, b_ref, o_ref):
    y = _dot(c_ref[...], yr_ref[0]) + _dot(st_ref[...], yi_ref[0])
    y = y + zin_ref[...].astype(F32) * b_ref[0]
    o_ref[...] = (x2_ref[...].astype(F32) * y).astype(o_ref.dtype)


def _hy_dft_inv(cos, msin_t, yr, yi, zin, x2, bias, l, *, tt):
    n = cos.shape[0]
    tt = min(tt, n)
    nt = n // tt
    tab = pl.BlockSpec((tt, n), lambda i, b: (i, 0))
    spec = pl.BlockSpec((1, n, HY_WIDTH), lambda i, b: (b, 0, 0))
    rows = pl.BlockSpec((tt, HY_WIDTH), lambda i, b: (b * nt + i, 0))
    return pl.pallas_call(
        _dft_inv_kernel,
        grid=(nt, BATCH),
        in_specs=[tab, tab, spec, spec, rows, rows, pl.BlockSpec((1, 1, HY_WIDTH), lambda i, b: (l, 0, 0))],
        out_specs=rows,
        out_shape=jax.ShapeDtypeStruct((BATCH * n, HY_WIDTH), BF16),
        compiler_params=_cparams(("parallel", "arbitrary")),
        name="hy_dft_inv",
    )(cos, msin_t, yr, yi, zin, x2, bias.reshape(DEPTH, 1, HY_WIDTH))


def _hyena(p, l, n, row_block0, tables, conv_w, conv_b, f_w1, f_b1, f_w2, f_b2, f_w3, f_freq, bias):
    cos, msin, msin_t = tables
    zin, x2 = _hy_conv(p, conv_w, conv_b, l, n, row_block0)
    hs, ha, kn = _hy_filter_taps(n, f_w1[l], f_b1[l], f_w2[l], f_b2[l], f_w3[l], f_freq[l])
    kr, ki = _hy_filter_spectrum(cos, msin, hs, ha, tf=512)
    yr, yi = _hy_dft_fwd(cos, msin, zin, kr, ki, kn, tf=512)
    return _hy_dft_inv(cos, msin_t, yr, yi, zin, x2, bias, l, tt=512)


def _rope_tables():
    tok = jnp.arange(SEQ)
    row = (tok // GRID_W).astype(F32)
    col = (tok % GRID_W).astype(F32)
    n_freq = MLA_ROPE // 4
    inv = ROPE_THETA ** (-jnp.arange(n_freq, dtype=F32) / n_freq)
    ang = jnp.concatenate([row[:, None] * inv, col[:, None] * inv], axis=-1)
    cos, sin = jnp.cos(ang), jnp.sin(ang)
    half = MLA_ROPE // 2
    zeros = jnp.zeros((SEQ, half), F32)
    rest = HEAD_DIM - MLA_ROPE
    cos_t = jnp.concatenate([cos, cos, jnp.ones((SEQ, rest), F32)], axis=-1)
    sin_a = jnp.concatenate([-sin, zeros, jnp.zeros((SEQ, rest), F32)], axis=-1)
    sin_b = jnp.concatenate([zeros, sin, jnp.zeros((SEQ, rest), F32)], axis=-1)
    ident = jnp.ones((N_CTX, HEAD_DIM), F32)
    none = jnp.zeros((N_CTX, HEAD_DIM), F32)
    return (jnp.concatenate([cos_t, ident]), jnp.concatenate([sin_a, none]), jnp.concatenate([sin_b, none]))


def _layout_w_uq(w):
    w = w.reshape(Q_LORA, MLA_HEADS, HEAD_DIM + MLA_ROPE)
    w = jnp.pad(w, ((0, 0), (0, 0), (0, MLA_QK - HEAD_DIM - MLA_ROPE)))
    return w.reshape(Q_LORA, MLA_HEADS * MLA_QK).astype(BF16)


def kernel(x, c, ctx, c_ctx, w_ada, b_ada, g_attn_pre, g_attn_post, g_ffn_pre, g_ffn_post, w_in, hy_conv_w, hy_conv_b, hy_f_w1, hy_f_b1, hy_f_w2, hy_f_b2, hy_f_w3, hy_f_freq, hy_bias, mla_g_q, mla_w_uq, mla_g_kv, mla_w_ukv, na_rpb, w_out, w_ffn_gate, w_ffn_up, w_ffn_down):
    cc = jnp.concatenate([c, c_ctx[None, :], jnp.zeros((8 - BATCH - 1, D_MODEL), F32)], axis=0)
    mods_all = _ada(cc, w_ada, b_ada)
    mods = [mods_all[l].reshape(8, 1, 6 * D_MODEL) for l in range(DEPTH)]

    rope_tabs = _rope_tables()
    dft_lat = _dft_tables(SEQ)
    dft_ctx = _dft_tables(CTX_LEN)
    na_starts, na_plan, na_pairs = _na_plan()
    na_t2 = _na_bias_pairs(na_rpb, na_pairs)
    w_in_t = jnp.swapaxes(w_in, 1, 2)
    hy_w = (hy_conv_w, hy_conv_b, hy_f_w1, hy_f_b1, hy_f_w2, hy_f_b2, hy_f_w3, hy_f_freq, hy_bias)

    res = (x.reshape(N_LAT, D_MODEL), ctx.reshape(N_CTX, D_MODEL))
    xn = _prenorm(res[0], res[1], g_attn_pre[0], mods[0])

    for l in range(DEPTH):
        ctx_out = l < DEPTH - 1
        m_rows = N_TOK if ctx_out else N_LAT
        p = _in_proj(xn, w_in_t, l)

        q = _q_proj(p, m_rows, mla_g_q[l], _layout_w_uq(mla_w_uq[l]), rope_tabs)
        k, v = _kv_proj(p, mla_g_kv[l], mla_w_ukv, l, rope_tabs)
        mla = [_mla_latent(q, k, v), None]
        na = [_na_latent(p, na_t2, l, na_starts, na_plan), None]
        hy = [_hyena(p, l, SEQ, 0, dft_lat, *hy_w), None]

        if ctx_out:
            mla[1] = _ctx_attend(q, k, v, 0, 0, 0, MLA_HEADS, MLA_QK, MLA_VW, 1.0, True, "mla_ctx")
            cq = COL_NA // HEAD_DIM
            na[1] = _ctx_attend(p, p, p, cq, cq + NA_HEADS, cq + 2 * NA_HEADS, NA_HEADS, HEAD_DIM, HEAD_DIM,
                                NA_SCALE * LOG2E, False, "na_ctx")
            hy[1] = _hyena(p, l, CTX_LEN, CTX_BLOCK0, dft_ctx, *hy_w)
            res_l = res
        else:
            res_l = (res[0], None)

        stream, xn = _proj_post([tuple(hy), tuple(mla), tuple(na)], w_out, l, g_attn_post[l], mods[l], 2, res_l,
                                m_rows, nxt=(g_ffn_pre[l], mods[l], 3, 4))
        h = _ffn_up(xn, w_ffn_gate, w_ffn_up, l)
        nxt = (g_attn_pre[l + 1], mods[l + 1], 0, 1) if ctx_out else None
        stream, xn = _proj_post([(h, None)], w_ffn_down, l, g_ffn_post[l], mods[l], 5, (stream, None), m_rows,
                                nxt=nxt)
        res = (stream, None)

    return stream.reshape(BATCH, SEQ, D_MODEL)
```

```python
import functools
import math

import jax
import jax.numpy as jnp
import numpy as np
from jax import lax
from jax.experimental import pallas as pl
from jax.experimental.pallas import tpu as pltpu

F32 = jnp.float32
BF16 = jnp.bfloat16

D_MODEL = 2048
BATCH = 4
SEQ = 2048
DEPTH = 2
GRID_W = 64
CTX_LEN = 256
HEAD_DIM = 128
HY_WIDTH = D_MODEL // 4
HY_FILTER_HIDDEN = 64
HY_POS_BANDS = 16
HY_DECAY_TARGET = 1e-2
HY_FAST_DECAY = 0.3
HY_SLOW_DECAY = 1.5
MLA_HEADS = (D_MODEL // 2) // HEAD_DIM
MLA_ROPE = 64
Q_LORA = 3 * D_MODEL // 8
KV_LORA = D_MODEL // 4
MLA_SCALE = (HEAD_DIM + MLA_ROPE) ** -0.5
NA_HEADS = (D_MODEL // 4) // HEAD_DIM
NA_KH = 8
NA_KW = 16
NA_SCALE = HEAD_DIM ** -0.5
FFN_HIDDEN = ((8 * D_MODEL + 3 * 256 - 1) // (3 * 256)) * 256
ROPE_THETA = 10000.0
RMS_EPS = 1e-6
MASK_VALUE = -1e30
LOG2E = math.log2(math.e)

N_LAT = BATCH * SEQ
N_CTX = BATCH * CTX_LEN
N_TOK = N_LAT + N_CTX
CTX_BLOCK0 = N_LAT // CTX_LEN

IN_TN = 512
COL_HY = 0
COL_CQ = 3 * HY_WIDTH
COL_CKV = COL_CQ + Q_LORA
COL_KR = COL_CKV + KV_LORA
W_IN_NA = COL_KR + MLA_ROPE
COL_NA = -(-W_IN_NA // IN_TN) * IN_TN
P_COLS = COL_NA + 3 * NA_HEADS * HEAD_DIM
MLA_QK = 256
MLA_VW = 256

NA_TR = 4
NA_WR = 12
NA_TQ = NA_TR * GRID_W
NA_TK = NA_WR * GRID_W

VMEM_LIMIT = 52 * 1024 * 1024


def _cparams(sem):
    return pltpu.CompilerParams(dimension_semantics=sem, vmem_limit_bytes=VMEM_LIMIT)


def _dot(a, b):
    return jnp.dot(a, b, preferred_element_type=F32)


def _dot_nt(a, b):
    return lax.dot_general(a, b, (((1,), (1,)), ((), ())), preferred_element_type=F32)


def _dot_hi(a, b):
    return jnp.dot(a, b, preferred_element_type=F32, precision=lax.Precision.HIGHEST)


def _rms(x, g):
    ms = jnp.mean(x * x, axis=-1, keepdims=True)
    return x * lax.rsqrt(ms + RMS_EPS) * g


def _mod_row(tm):
    n_lat, per_b = N_LAT // tm, SEQ // tm
    return lambda i: jnp.where(i < n_lat, i // per_b, BATCH)


def _rope_row(tm):
    n_lat, per_b = N_LAT // tm, SEQ // tm
    return lambda i: jnp.where(i < n_lat, i % per_b, per_b + i - n_lat)


def _ada_kernel(c_ref, w_ref, b_ref, o_ref):
    a = c_ref[...]
    a = a * jax.nn.sigmoid(a)
    o_ref[0] = _dot(a.astype(BF16), w_ref[0].astype(BF16)) + b_ref[0]


def _ada(cc, w_ada, b_ada):
    tn = 1024
    n = w_ada.shape[-1]
    return pl.pallas_call(
        _ada_kernel,
        grid=(DEPTH, n // tn),
        in_specs=[
            pl.BlockSpec((8, D_MODEL), lambda l, j: (0, 0)),
            pl.BlockSpec((1, D_MODEL, tn), lambda l, j: (l, 0, j)),
            pl.BlockSpec((1, 1, tn), lambda l, j: (l, 0, j)),
        ],
        out_specs=pl.BlockSpec((1, 8, tn), lambda l, j: (l, 0, j)),
        out_shape=jax.ShapeDtypeStruct((DEPTH, 8, n), F32),
        compiler_params=_cparams(("parallel", "parallel")),
        name="ada",
    )(cc, w_ada, b_ada.reshape(DEPTH, 1, n))


def _norm_mod_to(xn_ref, x_ref, g_ref, sh_ref, sc_ref, chunk=256):
    g = g_ref[...]
    sc = 1.0 + sc_ref[0]
    sh = sh_ref[0]

    def body(r, carry):
        rows = pl.ds(pl.multiple_of(r * chunk, chunk), chunk)
        xn_ref[rows, :] = (_rms(x_ref[rows, :], g) * sc + sh).astype(BF16)
        return carry

    lax.fori_loop(0, x_ref.shape[0] // chunk, body, 0)


def _prenorm_kernel(x_ref, c_ref, g_ref, sh_ref, sc_ref, o_ref, *, n_lat):
    i = pl.program_id(0)

    @pl.when(i < n_lat)
    def _():
        _norm_mod_to(o_ref, x_ref, g_ref, sh_ref, sc_ref)

    @pl.when(i >= n_lat)
    def _():
        _norm_mod_to(o_ref, c_ref, g_ref, sh_ref, sc_ref)


def _prenorm(x2d, ctx2d, g, mods, *, tm=1024):
    n_lat = N_LAT // tm
    row = _mod_row(tm)
    return pl.pallas_call(
        functools.partial(_prenorm_kernel, n_lat=n_lat),
        grid=(N_TOK // tm,),
        in_specs=[
            pl.BlockSpec((tm, D_MODEL), lambda i: (jnp.minimum(i, n_lat - 1), 0)),
            pl.BlockSpec((tm, D_MODEL), lambda i: (jnp.maximum(i - n_lat, 0), 0)),
            pl.BlockSpec((1, D_MODEL), lambda i: (0, 0)),
            pl.BlockSpec((1, 1, D_MODEL), lambda i: (row(i), 0, 0)),
            pl.BlockSpec((1, 1, D_MODEL), lambda i: (row(i), 0, 1)),
        ],
        out_specs=pl.BlockSpec((tm, D_MODEL), lambda i: (i, 0)),
        out_shape=jax.ShapeDtypeStruct((N_TOK, D_MODEL), BF16),
        compiler_params=_cparams(("parallel",)),
        name="prenorm",
    )(x2d, ctx2d, g.reshape(1, D_MODEL), mods, mods)


def _in_kernel(x_ref, w_ref, o_ref):
    o_ref[...] = _dot_nt(x_ref[...], w_ref[0].astype(BF16)).astype(o_ref.dtype)


def _in_proj(xn, w_in_t, l, *, tm=3072):
    n_head = COL_NA // IN_TN

    def w_row(j):
        per = IN_TN // MLA_ROPE
        return MLA_ROPE * jnp.where(j < n_head, j * per, W_IN_NA // MLA_ROPE + (j - n_head) * per)

    return pl.pallas_call(
        _in_kernel,
        grid=(N_TOK // tm, P_COLS // IN_TN),
        in_specs=[
            pl.BlockSpec((tm, D_MODEL), lambda i, j: (i, 0)),
            pl.BlockSpec((pl.Element(1), pl.Element(IN_TN), pl.Element(D_MODEL)), lambda i, j: (l, w_row(j), 0)),
        ],
        out_specs=pl.BlockSpec((tm, IN_TN), lambda i, j: (i, j)),
        out_shape=jax.ShapeDtypeStruct((N_TOK, P_COLS), BF16),
        compiler_params=_cparams(("parallel", "arbitrary")),
        name="in_proj",
    )(xn, w_in_t)


def _swiglu_kernel(x_ref, wg_ref, wu_ref, o_ref):
    xn = x_ref[...]
    gate = _dot(xn, wg_ref[0].astype(BF16))
    up = _dot(xn, wu_ref[0].astype(BF16))
    o_ref[...] = (gate * jax.nn.sigmoid(gate) * up).astype(o_ref.dtype)


def _ffn_up(xn, wg, wu, l, *, tn=512):
    m = xn.shape[0]
    tm = 2048 if m % 2048 == 0 else 1536
    w_spec = pl.BlockSpec((1, D_MODEL, tn), lambda i, j: (l, 0, j))
    return pl.pallas_call(
        _swiglu_kernel,
        grid=(m // tm, FFN_HIDDEN // tn),
        in_specs=[pl.BlockSpec((tm, D_MODEL), lambda i, j: (i, 0)), w_spec, w_spec],
        out_specs=pl.BlockSpec((tm, tn), lambda i, j: (i, j)),
        out_shape=jax.ShapeDtypeStruct((m, FFN_HIDDEN), BF16),
        compiler_params=_cparams(("parallel", "arbitrary")),
        name="ffn_up",
    )(xn, wg, wu)


POST_TM = 256
POST_SUB = 128
POST_WCH = 256
POST_SLOTS = 3


def _post_kernel(*refs, widths, two_src, emit_xn, n_lat, l):
    refs = list(refs)
    n_act = len(widths)
    acts_lat = [refs.pop(0) for _ in range(n_act)]
    acts_ctx = [refs.pop(0) for _ in range(n_act)] if two_src else None
    w_hbm, g_ref, gate_ref, res_lat = (refs.pop(0) for _ in range(4))
    res_ctx = refs.pop(0) if two_src else None
    if emit_xn:
        g2_ref, sh_ref, sc_ref = (refs.pop(0) for _ in range(3))
    o_hbm = refs.pop(0)
    xn_ref = refs.pop(0) if emit_xn else None
    w_scr, stage, buf, sem_w, sem_res, sem_out = refs

    tm = POST_TM
    i = pl.program_id(0)
    n_i = pl.num_programs(0)
    n_chunks = sum(widths) // POST_WCH
    is_lat = i < n_lat
    is_ctx = jnp.logical_not(is_lat)

    def w_copy(c, slot):
        return pltpu.make_async_copy(w_hbm.at[l, pl.ds(c * POST_WCH, POST_WCH)], stage.at[slot], sem_w.at[slot])

    def res_copy(src, row0, slot):
        return pltpu.make_async_copy(src.at[pl.ds(row0, tm)], buf.at[slot], sem_res.at[slot])

    def res_start(tile, slot):
        if two_src:
            pl.when(tile < n_lat)(lambda: res_copy(res_lat, tile * tm, slot).start())
            pl.when(tile >= n_lat)(lambda: res_copy(res_ctx, (tile - n_lat) * tm, slot).start())
        else:
            res_copy(res_lat, tile * tm, slot).start()

    def out_copy(tile, slot):
        return pltpu.make_async_copy(buf.at[slot], o_hbm.at[pl.ds(tile * tm, tm)], sem_out.at[slot])

    @pl.when(i == 0)
    def _():
        res_start(0, 0)
        w_copy(0, 0).start()

        def body(c, carry):
            s = c % 2
            w_copy(c, s).wait()

            @pl.when(c + 1 < n_chunks)
            def _():
                w_copy(c + 1, 1 - s).start()

            w_scr[pl.ds(pl.multiple_of(c * POST_WCH, POST_WCH), POST_WCH), :] = stage[s].astype(BF16)
            return carry

        lax.fori_loop(0, n_chunks, body, 0)

    slot = i % POST_SLOTS
    nslot = (i + 1) % POST_SLOTS

    @pl.when(i + 1 < n_i)
    def _():
        @pl.when(i >= POST_SLOTS - 1)
        def _():
            out_copy(i + 1 - POST_SLOTS, nslot).wait()

        res_start(i + 1, nslot)

    res_copy(res_lat, 0, slot).wait()

    def compute(act_refs):
        g = g_ref[...]
        gate = gate_ref[0]
        if emit_xn:
            g2 = g2_ref[...]
            sc = 1.0 + sc_ref[0]
            sh = sh_ref[0]
        for r in range(tm // POST_SUB):
            rows = pl.ds(r * POST_SUB, POST_SUB)
            y = None
            k0 = 0
            for a_ref, kw in zip(act_refs, widths):
                part = _dot(a_ref[rows, :], w_scr[k0:k0 + kw, :])
                y = part if y is None else y + part
                k0 += kw
            x_new = buf[slot, rows, :] + gate * _rms(y, g)
            buf[slot, rows, :] = x_new
            if emit_xn:
                xn_ref[rows, :] = (_rms(x_new, g2) * sc + sh).astype(xn_ref.dtype)

    if two_src:
        pl.when(is_lat)(functools.partial(compute, acts_lat))
        pl.when(is_ctx)(functools.partial(compute, acts_ctx))
    else:
        compute(acts_lat)

    out_copy(i, slot).start()

    @pl.when(i == n_i - 1)
    def _():
        for back in range(POST_SLOTS):
            out_copy(i - back, (i - back) % POST_SLOTS).wait()


def _proj_post(acts, w, l, g, mods, gate_chunk, res, m_rows, nxt=None):
    two_src = res[1] is not None
    tm = POST_TM
    n_lat = N_LAT // tm
    widths = tuple(a.shape[1] for a, _ in acts)
    k_total = sum(widths)
    emit_xn = nxt is not None
    row = _mod_row(tm)

    def lat_row(i):
        return jnp.minimum(i, n_lat - 1) if two_src else i

    def ctx_row(i):
        return jnp.maximum(i - n_lat, 0)

    def act_specs(rowf):
        return [pl.BlockSpec((tm, kw), lambda i: (rowf(i), 0)) for kw in widths]

    vec_spec = pl.BlockSpec((1, D_MODEL), lambda i: (0, 0))

    def mod_spec(chunk):
        return pl.BlockSpec((1, 1, D_MODEL), lambda i: (row(i), 0, chunk))

    hbm = pl.BlockSpec(memory_space=pl.ANY)
    in_specs = act_specs(lat_row)
    args = [a for a, _ in acts]
    if two_src:
        in_specs += act_specs(ctx_row)
        args += [c for _, c in acts]
    in_specs += [hbm, vec_spec, mod_spec(gate_chunk), hbm]
    args += [w, g.reshape(1, D_MODEL), mods, res[0]]
    if two_src:
        in_specs.append(hbm)
        args.append(res[1])
    out_specs = [hbm]
    out_shape = [jax.ShapeDtypeStruct((m_rows, D_MODEL), F32)]
    if emit_xn:
        g2, mods2, sh_chunk, sc_chunk = nxt
        in_specs += [vec_spec, mod_spec(sh_chunk), mod_spec(sc_chunk)]
        args += [g2.reshape(1, D_MODEL), mods2, mods2]
        out_specs.append(pl.BlockSpec((tm, D_MODEL), lambda i: (i, 0)))
        out_shape.append(jax.ShapeDtypeStruct((m_rows, D_MODEL), BF16))

    out = pl.pallas_call(
        functools.partial(_post_kernel, widths=widths, two_src=two_src, emit_xn=emit_xn, n_lat=n_lat, l=l),
        grid=(m_rows // tm,),
        in_specs=in_specs,
        out_specs=out_specs,
        out_shape=out_shape,
        scratch_shapes=[
            pltpu.VMEM((k_total, D_MODEL), BF16),
            pltpu.VMEM((2, POST_WCH, D_MODEL), F32),
            pltpu.VMEM((POST_SLOTS, tm, D_MODEL), F32),
            pltpu.SemaphoreType.DMA((2,)),
            pltpu.SemaphoreType.DMA((POST_SLOTS,)),
            pltpu.SemaphoreType.DMA((POST_SLOTS,)),
        ],
        compiler_params=_cparams(("arbitrary",)),
        name="proj_post",
    )(*args)
    return (out[0], out[1]) if emit_xn else (out[0], None)


def _rope128(r, cos_ref, sa_ref, sb_ref):
    return r * cos_ref[...] + pltpu.roll(r, 96, 1) * sa_ref[...] + pltpu.roll(r, 32, 1) * sb_ref[...]


def _q_kernel(x_ref, g_ref, w_ref, cos_ref, sa_ref, sb_ref, o_ref):
    xn = _rms(x_ref[...].astype(F32), g_ref[...]).astype(BF16)
    for h in range(MLA_HEADS):
        acc = _dot(xn, w_ref[:, h * MLA_QK:(h + 1) * MLA_QK]) * (MLA_SCALE * LOG2E)
        o_ref[:, h * MLA_QK:h * MLA_QK + HEAD_DIM] = acc[:, :HEAD_DIM].astype(o_ref.dtype)
        o_ref[:, h * MLA_QK + HEAD_DIM:(h + 1) * MLA_QK] = _rope128(
            acc[:, HEAD_DIM:], cos_ref, sa_ref, sb_ref).astype(o_ref.dtype)


def _q_proj(p, m_rows, g, w, tabs, *, tm=1024):
    rope = _rope_row(tm)
    tab_spec = pl.BlockSpec((tm, HEAD_DIM), lambda i: (rope(i), 0))
    return pl.pallas_call(
        _q_kernel,
        grid=(m_rows // tm,),
        in_specs=[
            pl.BlockSpec((tm, Q_LORA), lambda i: (i, COL_CQ // Q_LORA)),
            pl.BlockSpec((1, Q_LORA), lambda i: (0, 0)),
            pl.BlockSpec((Q_LORA, MLA_HEADS * MLA_QK), lambda i: (0, 0)),
            tab_spec, tab_spec, tab_spec,
        ],
        out_specs=pl.BlockSpec((tm, MLA_HEADS * MLA_QK), lambda i: (i, 0)),
        out_shape=jax.ShapeDtypeStruct((m_rows, MLA_HEADS * MLA_QK), BF16),
        compiler_params=_cparams(("parallel",)),
        name="q_proj",
    )(p, g.reshape(1, Q_LORA), w, *tabs)


def _kv_kernel(xa_ref, xb_ref, kr_ref, g_ref, w_ref, cos_ref, sa_ref, sb_ref, k_ref, v_ref):
    half = KV_LORA // 2
    xa = xa_ref[...].astype(F32)
    xb = xb_ref[...].astype(F32)
    ms = (jnp.sum(xa * xa, axis=-1, keepdims=True) + jnp.sum(xb * xb, axis=-1, keepdims=True)) * (1.0 / KV_LORA)
    rs = lax.rsqrt(ms + RMS_EPS)
    g = g_ref[...]
    xna = (xa * rs * g[:, :half]).astype(BF16)
    xnb = (xb * rs * g[:, half:]).astype(BF16)
    krr = _rope128(kr_ref[...].astype(F32), cos_ref, sa_ref, sb_ref).astype(k_ref.dtype)
    ones = jnp.ones((xa.shape[0], MLA_VW - HEAD_DIM), v_ref.dtype)
    for h in range(MLA_HEADS):
        w = w_ref[0, :, h * 2 * HEAD_DIM:(h + 1) * 2 * HEAD_DIM].astype(BF16)
        acc = _dot(xna, w[:half]) + _dot(xnb, w[half:])
        k_ref[:, h * MLA_QK:h * MLA_QK + HEAD_DIM] = acc[:, :HEAD_DIM].astype(k_ref.dtype)
        k_ref[:, h * MLA_QK + HEAD_DIM:(h + 1) * MLA_QK] = krr
        v_ref[:, h * MLA_VW:h * MLA_VW + HEAD_DIM] = acc[:, HEAD_DIM:].astype(v_ref.dtype)
        v_ref[:, h * MLA_VW + HEAD_DIM:(h + 1) * MLA_VW] = ones


def _kv_proj(p, g, w, l, tabs, *, tm=1024):
    m = p.shape[0]
    rope = _rope_row(tm)
    half = KV_LORA // 2
    tab_spec = pl.BlockSpec((tm, HEAD_DIM), lambda i: (rope(i), 0))
    return pl.pallas_call(
        _kv_kernel,
        grid=(m // tm,),
        in_specs=[
            pl.BlockSpec((tm, half), lambda i: (i, COL_CKV // half)),
            pl.BlockSpec((tm, half), lambda i: (i, COL_CKV // half + 1)),
            pl.BlockSpec((tm, HEAD_DIM), lambda i: (i, COL_KR // HEAD_DIM)),
            pl.BlockSpec((1, KV_LORA), lambda i: (0, 0)),
            pl.BlockSpec((1, KV_LORA, MLA_HEADS * 2 * HEAD_DIM), lambda i: (l, 0, 0)),
            tab_spec, tab_spec, tab_spec,
        ],
        out_specs=[
            pl.BlockSpec((tm, MLA_HEADS * MLA_QK), lambda i: (i, 0)),
            pl.BlockSpec((tm, MLA_HEADS * MLA_VW), lambda i: (i, 0)),
        ],
        out_shape=[
            jax.ShapeDtypeStruct((m, MLA_HEADS * MLA_QK), BF16),
            jax.ShapeDtypeStruct((m, MLA_HEADS * MLA_VW), BF16),
        ],
        compiler_params=_cparams(("parallel",)),
        name="kv_proj",
    )(p, p, p, g.reshape(1, KV_LORA), w, *tabs)


def _softmax_pv(s_list, v_list, ones_col):
    m = jnp.max(s_list[0], axis=-1, keepdims=True)
    for s in s_list[1:]:
        m = jnp.maximum(m, jnp.max(s, axis=-1, keepdims=True))
    acc = None
    den = None
    for s, v in zip(s_list, v_list):
        p = jnp.exp2(s - m)
        if not ones_col:
            d = jnp.sum(p, axis=-1, keepdims=True)
            den = d if den is None else den + d
        o = _dot(p.astype(BF16), v)
        acc = o if acc is None else acc + o
    if ones_col:
        return acc[:, :HEAD_DIM] / acc[:, HEAD_DIM:]
    return acc / den


def _attn_kernel(*refs, scale, two, ones_col, chains):
    if two:
        q_ref, k1_ref, v1_ref, k2_ref, v2_ref, o_ref = refs
    else:
        q_ref, k1_ref, v1_ref, o_ref = refs
    tq = q_ref.shape[0] // chains
    for c in range(chains):
        q = q_ref[c * tq:(c + 1) * tq, :]
        s_list = [_dot_nt(q, k1_ref[...])]
        v_list = [v1_ref[...]]
        if two:
            s_list.append(_dot_nt(q, k2_ref[...]))
            v_list.append(v2_ref[...])
        if scale != 1.0:
            s_list = [s * scale for s in s_list]
        o_ref[c * tq:(c + 1) * tq, :] = _softmax_pv(s_list, v_list, ones_col).astype(o_ref.dtype)


def _mla_latent(q, k, v, *, tq=2048):
    nq = SEQ // tq
    return pl.pallas_call(
        functools.partial(_attn_kernel, scale=1.0, two=True, ones_col=True, chains=8),
        grid=(BATCH, MLA_HEADS, nq),
        in_specs=[
            pl.BlockSpec((tq, MLA_QK), lambda b, h, i: (b * nq + i, h)),
            pl.BlockSpec((SEQ, MLA_QK), lambda b, h, i: (b, h)),
            pl.BlockSpec((SEQ, MLA_VW), lambda b, h, i: (b, h)),
            pl.BlockSpec((CTX_LEN, MLA_QK), lambda b, h, i: (CTX_BLOCK0 + b, h)),
            pl.BlockSpec((CTX_LEN, MLA_VW), lambda b, h, i: (CTX_BLOCK0 + b, h)),
        ],
        out_specs=pl.BlockSpec((tq, HEAD_DIM), lambda b, h, i: (b * nq + i, h)),
        out_shape=jax.ShapeDtypeStruct((N_LAT, MLA_HEADS * HEAD_DIM), BF16),
        compiler_params=_cparams(("parallel", "parallel", "arbitrary")),
        name="mla_latent",
    )(q, k, v, k, v)


def _ctx_attend(q, k, v, q_col0, k_col0, v_col0, heads, dqk, dv, scale, ones_col, name):
    return pl.pallas_call(
        functools.partial(_attn_kernel, scale=scale, two=False, ones_col=ones_col, chains=1),
        grid=(BATCH, heads),
        in_specs=[
            pl.BlockSpec((CTX_LEN, dqk), lambda b, h: (CTX_BLOCK0 + b, q_col0 + h)),
            pl.BlockSpec((CTX_LEN, dqk), lambda b, h: (CTX_BLOCK0 + b, k_col0 + h)),
            pl.BlockSpec((CTX_LEN, dv), lambda b, h: (CTX_BLOCK0 + b, v_col0 + h)),
        ],
        out_specs=pl.BlockSpec((CTX_LEN, HEAD_DIM), lambda b, h: (b, h)),
        out_shape=jax.ShapeDtypeStruct((N_CTX, heads * HEAD_DIM), BF16),
        compiler_params=_cparams(("parallel", "parallel")),
        name=name,
    )(q, k, v)


def _na_plan():
    rows = SEQ // GRID_W
    invalid = 2 * NA_KH - 1
    pairs, plan, starts = [], [], []
    for t in range(rows // NA_TR):
        kw0 = int(np.clip(NA_TR * t - NA_KH // 2, 0, rows - NA_WR))
        starts.append(kw0)
        tile = []
        for ri in range(NA_TR):
            r = NA_TR * t + ri
            r0 = int(np.clip(r - NA_KH // 2, 0, rows - NA_KH))
            assert kw0 <= r0 and r0 + NA_KH <= kw0 + NA_WR
            row = []
            for kp in range(NA_WR // 2):
                pair = []
                for kr in (kw0 + 2 * kp, kw0 + 2 * kp + 1):
                    pair.append(kr - r + NA_KH - 1 if r0 <= kr < r0 + NA_KH else invalid)
                pair = tuple(pair)
                if pair not in pairs:
                    pairs.append(pair)
                row.append(pairs.index(pair))
            tile.append(row)
        plan.append(tile)
    return starts, plan, pairs


def _na_bias_pairs(rpb, pairs):
    c = np.arange(GRID_W)
    c0 = np.clip(c - NA_KW // 2, 0, GRID_W - NA_KW)
    col_ok = (c[None, :] >= c0[:, None]) & (c[None, :] < c0[:, None] + NA_KW)
    col_idx = np.clip(c[None, :] - c[:, None] + NA_KW - 1, 0, 2 * NA_KW - 2)
    onehot = (col_idx[None] == np.arange(2 * NA_KW - 1)[:, None, None]).astype(np.float32)
    t = jnp.einsum("lhdj,jck->lhdck", rpb.astype(F32), onehot, precision=lax.Precision.HIGHEST) * LOG2E
    t = jnp.where(col_ok, t, MASK_VALUE)
    masked = jnp.full(t.shape[:2] + (GRID_W, GRID_W), MASK_VALUE, F32)
    slabs = [t[:, :, d] for d in range(2 * NA_KH - 1)] + [masked]
    return jnp.stack([jnp.concatenate([slabs[a], slabs[b]], axis=-1) for a, b in pairs], axis=2)


def _na_kernel(q_ref, k_ref, v_ref, kc_ref, vc_ref, t2_ref, o_ref, *, starts, plan):
    kc = kc_ref[...]
    vc = vc_ref[...]
    for t, (kw0, tile) in enumerate(zip(starts, plan)):
        q = q_ref[t * NA_TQ:(t + 1) * NA_TQ, :]
        kw = k_ref[kw0 * GRID_W:kw0 * GRID_W + NA_TK, :]
        vw = v_ref[kw0 * GRID_W:kw0 * GRID_W + NA_TK, :]
        bias = jnp.concatenate(
            [jnp.concatenate([t2_ref[0, 0, idx] for idx in row], axis=1) for row in tile], axis=0)
        s = _dot_nt(q, kw) * (NA_SCALE * LOG2E) + bias
        sc = _dot_nt(q, kc) * (NA_SCALE * LOG2E)
        o_ref[t * NA_TQ:(t + 1) * NA_TQ, :] = _softmax_pv([s, sc], [vw, vc], False).astype(o_ref.dtype)


def _na_latent(p, t2, l, starts, plan):
    cq = COL_NA // HEAD_DIM
    ck = cq + NA_HEADS
    cv = ck + NA_HEADS
    n_pairs = t2.shape[2]
    return pl.pallas_call(
        functools.partial(_na_kernel, starts=starts, plan=plan),
        grid=(NA_HEADS, BATCH),
        in_specs=[
            pl.BlockSpec((SEQ, HEAD_DIM), lambda h, b: (b, cq + h)),
            pl.BlockSpec((SEQ, HEAD_DIM), lambda h, b: (b, ck + h)),
            pl.BlockSpec((SEQ, HEAD_DIM), lambda h, b: (b, cv + h)),
            pl.BlockSpec((CTX_LEN, HEAD_DIM), lambda h, b: (CTX_BLOCK0 + b, ck + h)),
            pl.BlockSpec((CTX_LEN, HEAD_DIM), lambda h, b: (CTX_BLOCK0 + b, cv + h)),
            pl.BlockSpec((1, 1, n_pairs, GRID_W, 2 * GRID_W), lambda h, b: (l, h, 0, 0, 0)),
        ],
        out_specs=pl.BlockSpec((SEQ, HEAD_DIM), lambda h, b: (b, h)),
        out_shape=jax.ShapeDtypeStruct((N_LAT, NA_HEADS * HEAD_DIM), BF16),
        compiler_params=_cparams(("parallel", "parallel")),
        name="na_latent",
    )(p, p, p, p, p, t2)


def _conv_kernel(v_ref, x1_ref, x2_ref, wv_ref, w1_ref, w2_ref, bv_ref, b1_ref, b2_ref, zin_ref, x2o_ref):
    n = v_ref.shape[0]
    row = lax.broadcasted_iota(jnp.int32, (n, 1), 0)

    def short_conv(p_ref, w_ref, b_ref):
        p = p_ref[...].astype(F32)
        prev = jnp.where(row == 0, 0.0, pltpu.roll(p, 1, 0))
        nxt = jnp.where(row == n - 1, 0.0, pltpu.roll(p, n - 1, 0))
        w = w_ref[0]
        return prev * w[0:1] + p * w[1:2] + nxt * w[2:3] + b_ref[0]

    zin_ref[...] = (short_conv(x1_ref, w1_ref, b1_ref) * short_conv(v_ref, wv_ref, bv_ref)).astype(zin_ref.dtype)
    x2o_ref[...] = short_conv(x2_ref, w2_ref, b2_ref).astype(x2o_ref.dtype)


def _hy_conv(p, conv_w, conv_b, l, n, row_block0, *, tc=256):
    nc = HY_WIDTH // tc

    def seg(s):
        return (pl.BlockSpec((n, tc), lambda b, j: (row_block0 + b, s * nc + j)),
                pl.BlockSpec((1, 3, tc), lambda b, j: (l, 0, s * nc + j)),
                pl.BlockSpec((1, 1, tc), lambda b, j: (l, 0, s * nc + j)))

    (pv, wv, bv), (p1, w1, b1), (p2, w2, b2) = seg(0), seg(1), seg(2)
    out_spec = pl.BlockSpec((n, tc), lambda b, j: (0, b * nc + j))
    out = jax.ShapeDtypeStruct((n, BATCH * HY_WIDTH), BF16)
    conv_b = conv_b.reshape(DEPTH, 1, -1)
    return pl.pallas_call(
        _conv_kernel,
        grid=(BATCH, nc),
        in_specs=[pv, p1, p2, wv, w1, w2, bv, b1, b2],
        out_specs=[out_spec, out_spec],
        out_shape=[out, out],
        compiler_params=_cparams(("parallel", "parallel")),
        name="hy_conv",
    )(p, p, p, conv_w, conv_w, conv_w, conv_b, conv_b, conv_b)


def _filt_kernel(z_ref, t_ref, dl_ref, w1_ref, b1_ref, w2_ref, b2_ref, w3_ref, fr_ref, hs_ref, ha_ref, kn_ref):
    n = z_ref.shape[0]
    fr = fr_ref[...]
    h = jnp.sin(fr * (_dot_hi(z_ref[...], w1_ref[...]) + b1_ref[...]))
    h = jnp.sin(fr * (_dot_hi(h, w2_ref[...]) + b2_ref[...]))
    h = _dot_hi(h, w3_ref[...])
    decay = jnp.exp(-t_ref[...] * dl_ref[...])
    row = lax.broadcasted_iota(jnp.int32, (n, 1), 0)
    hf = h[:, :HY_WIDTH] * decay
    hb = jnp.where(row == 0, 0.0, h[:, HY_WIDTH:] * decay)
    hs = hf + hb
    hs_ref[...] = hs.astype(hs_ref.dtype)
    ha_ref[...] = (hf - hb).astype(ha_ref.dtype)
    sign = jnp.where((row & 1) == 0, 1.0, -1.0)
    kn_ref[...] = jnp.sum(hs * sign, axis=0, keepdims=True)


def _hy_filter_taps(n, f_w1, f_b1, f_w2, f_b2, f_w3, f_freq):
    pos = jnp.arange(n, dtype=F32)
    t = jnp.linspace(0.0, 1.0, n, dtype=F32)
    bands = jnp.linspace(1e-4, HY_POS_BANDS - 1, HY_POS_BANDS, dtype=F32)
    ang = (2.0 * math.pi / n) * pos[:, None] * bands[None, :]
    z = jnp.concatenate([t[:, None], jnp.cos(ang), -jnp.sin(ang)], axis=-1)
    pad = HY_FILTER_HIDDEN - z.shape[1]
    z = jnp.pad(z, ((0, 0), (0, pad)))
    w1 = jnp.pad(f_w1.astype(F32), ((0, pad), (0, 0)))
    deltas = jnp.abs(jnp.linspace(math.log(HY_DECAY_TARGET) / HY_FAST_DECAY,
                                  math.log(HY_DECAY_TARGET) / HY_SLOW_DECAY, HY_WIDTH, dtype=F32))
    hid = HY_FILTER_HIDDEN
    return pl.pallas_call(
        _filt_kernel,
        out_shape=[
            jax.ShapeDtypeStruct((n, HY_WIDTH), BF16),
            jax.ShapeDtypeStruct((n, HY_WIDTH), BF16),
            jax.ShapeDtypeStruct((1, HY_WIDTH), F32),
        ],
        compiler_params=pltpu.CompilerParams(vmem_limit_bytes=VMEM_LIMIT),
        name="hy_filter",
    )(z, t[:, None], deltas[None, :], w1, f_b1.reshape(1, hid), f_w2, f_b2.reshape(1, hid), f_w3,
      f_freq.reshape(1, hid))


def _dft_tables(n):
    lo = 16 if n < 1024 else 32
    hi = n // lo
    t = jnp.arange(n, dtype=jnp.int32)[None, :]
    big = 2 * n

    def ang(ff):
        return ((ff * t) % big).astype(F32) * (2.0 * math.pi / big)

    a = ang(lo * jnp.arange(hi, dtype=jnp.int32)[:, None])
    b = ang(jnp.arange(lo, dtype=jnp.int32)[:, None])
    ca, sa, cb, sb = jnp.cos(a), jnp.sin(a), jnp.cos(b), jnp.sin(b)
    cos = (ca[:, None, :] * cb[None, :, :] - sa[:, None, :] * sb[None, :, :]).reshape(n, n)
    base = -(sa[:, None, :] * cb[None, :, :] + ca[:, None, :] * sb[None, :, :]).reshape(n, n)
    idx = jnp.arange(n)
    alt = jnp.where(idx % 2 == 0, 1.0, -1.0).astype(F32)
    msin = jnp.where(idx[:, None] == 0, alt[None, :], base)
    msin_t = jnp.where(idx[None, :] == 0, alt[:, None], base)
    return cos.astype(BF16), msin.astype(BF16), msin_t.astype(BF16)


def _dft_filt_kernel(c_ref, s_ref, hs_ref, ha_ref, kr_ref, ki_ref):
    kr_ref[...] = _dot(c_ref[...], hs_ref[...])
    ki_ref[...] = _dot(s_ref[...], ha_ref[...])


def _hy_filter_spectrum(cos, msin, hs, ha, *, tf):
    n = cos.shape[0]
    tf = min(tf, n)
    tab = pl.BlockSpec((tf, n), lambda i: (i, 0))
    taps = pl.BlockSpec((n, HY_WIDTH), lambda i: (0, 0))
    out = pl.BlockSpec((tf, HY_WIDTH), lambda i: (i, 0))
    return pl.pallas_call(
        _dft_filt_kernel,
        grid=(n // tf,),
        in_specs=[tab, tab, taps, taps],
        out_specs=[out, out],
        out_shape=[jax.ShapeDtypeStruct((n, HY_WIDTH), F32)] * 2,
        compiler_params=_cparams(("parallel",)),
        name="hy_filter_dft",
    )(cos, msin, hs, ha)


def _batch_cols(b):
    return slice(b * HY_WIDTH, (b + 1) * HY_WIDTH)


def _dft_fwd_kernel(c_ref, s_ref, x_ref, kr_ref, ki_ref, kn_ref, yr_ref, yi_ref, *, tf, inv_n):
    x = x_ref[...]
    zr = _dot(c_ref[...], x)
    zi = _dot(s_ref[...], x)
    row = pl.program_id(0) * tf + lax.broadcasted_iota(jnp.int32, (tf, 1), 0)
    bin0 = row == 0
    wt = jnp.where(bin0, inv_n, 2.0 * inv_n)
    kr = kr_ref[...] * wt
    ki = jnp.where(bin0, 0.0, ki_ref[...] * wt)
    kr_im = jnp.where(bin0, kn_ref[...] * wt, kr)
    for b in range(BATCH):
        cols = _batch_cols(b)
        yr_ref[:, cols] = (zr[:, cols] * kr - zi[:, cols] * ki).astype(yr_ref.dtype)
        yi_ref[:, cols] = (zr[:, cols] * ki + zi[:, cols] * kr_im).astype(yi_ref.dtype)


def _hy_dft_fwd(cos, msin, zin, kr, ki, kn, *, tf):
    n = cos.shape[0]
    tf = min(tf, n)
    tab = pl.BlockSpec((tf, n), lambda i: (i, 0))
    filt = pl.BlockSpec((tf, HY_WIDTH), lambda i: (i, 0))
    out = pl.BlockSpec((tf, BATCH * HY_WIDTH), lambda i: (i, 0))
    whole = pl.BlockSpec((n, BATCH * HY_WIDTH), lambda i: (0, 0), pipeline_mode=pl.Buffered(1))
    return pl.pallas_call(
        functools.partial(_dft_fwd_kernel, tf=tf, inv_n=1.0 / (2 * n)),
        grid=(n // tf,),
        in_specs=[tab, tab, whole, filt, filt, pl.BlockSpec((1, HY_WIDTH), lambda i: (0, 0))],
        out_specs=[out, out],
        out_shape=[jax.ShapeDtypeStruct((n, BATCH * HY_WIDTH), BF16)] * 2,
        compiler_params=_cparams(("parallel",)),
        name="hy_dft_fwd",
    )(cos, msin, zin, kr, ki, kn)


def _dft_inv_kernel(c_ref, st_ref, yr_ref, yi_ref, zin_ref, x2_ref, b_ref, o_ref):
    y = _dot(c_ref[...], yr_ref[...]) + _dot(st_ref[...], yi_ref[...])
    bias = b_ref[0]
    for b in range(BATCH):
        cols = _batch_cols(b)
        yb = y[:, cols] + zin_ref[:, cols].astype(F32) * bias
        o_ref[b] = (x2_ref[:, cols].astype(F32) * yb).astype(o_ref.dtype)


def _hy_dft_inv(cos, msin_t, yr, yi, zin, x2, bias, l, *, tt):
    n = cos.shape[0]
    tt = min(tt, n)
    tab = pl.BlockSpec((tt, n), lambda i: (i, 0))
    whole = pl.BlockSpec((n, BATCH * HY_WIDTH), lambda i: (0, 0), pipeline_mode=pl.Buffered(1))
    rows = pl.BlockSpec((tt, BATCH * HY_WIDTH), lambda i: (i, 0))
    out = pl.pallas_call(
        _dft_inv_kernel,
        grid=(n // tt,),
        in_specs=[tab, tab, whole, whole, rows, rows, pl.BlockSpec((1, 1, HY_WIDTH), lambda i: (l, 0, 0))],
        out_specs=pl.BlockSpec((BATCH, tt, HY_WIDTH), lambda i: (0, i, 0)),
        out_shape=jax.ShapeDtypeStruct((BATCH, n, HY_WIDTH), BF16),
        compiler_params=_cparams(("parallel",)),
        name="hy_dft_inv",
    )(cos, msin_t, yr, yi, zin, x2, bias.reshape(DEPTH, 1, HY_WIDTH))
    return out.reshape(BATCH * n, HY_WIDTH)


def _hyena(p, l, n, row_block0, tables, conv_w, conv_b, f_w1, f_b1, f_w2, f_b2, f_w3, f_freq, bias):
    cos, msin, msin_t = tables
    zin, x2 = _hy_conv(p, conv_w, conv_b, l, n, row_block0)
    hs, ha, kn = _hy_filter_taps(n, f_w1[l], f_b1[l], f_w2[l], f_b2[l], f_w3[l], f_freq[l])
    kr, ki = _hy_filter_spectrum(cos, msin, hs, ha, tf=512)
    yr, yi = _hy_dft_fwd(cos, msin, zin, kr, ki, kn, tf=512)
    return _hy_dft_inv(cos, msin_t, yr, yi, zin, x2, bias, l, tt=512)


def _rope_tables():
    tok = jnp.arange(SEQ)
    row = (tok // GRID_W).astype(F32)
    col = (tok % GRID_W).astype(F32)
    n_freq = MLA_ROPE // 4
    inv = ROPE_THETA ** (-jnp.arange(n_freq, dtype=F32) / n_freq)
    ang = jnp.concatenate([row[:, None] * inv, col[:, None] * inv], axis=-1)
    cos, sin = jnp.cos(ang), jnp.sin(ang)
    half = MLA_ROPE // 2
    zeros = jnp.zeros((SEQ, half), F32)
    rest = HEAD_DIM - MLA_ROPE
    cos_t = jnp.concatenate([cos, cos, jnp.ones((SEQ, rest), F32)], axis=-1)
    sin_a = jnp.concatenate([-sin, zeros, jnp.zeros((SEQ, rest), F32)], axis=-1)
    sin_b = jnp.concatenate([zeros, sin, jnp.zeros((SEQ, rest), F32)], axis=-1)
    ident = jnp.ones((N_CTX, HEAD_DIM), F32)
    none = jnp.zeros((N_CTX, HEAD_DIM), F32)
    return (jnp.concatenate([cos_t, ident]), jnp.concatenate([sin_a, none]), jnp.concatenate([sin_b, none]))


def _layout_w_uq(w):
    w = w.reshape(Q_LORA, MLA_HEADS, HEAD_DIM + MLA_ROPE)
    w = jnp.pad(w, ((0, 0), (0, 0), (0, MLA_QK - HEAD_DIM - MLA_ROPE)))
    return w.reshape(Q_LORA, MLA_HEADS * MLA_QK).astype(BF16)


def kernel(x, c, ctx, c_ctx, w_ada, b_ada, g_attn_pre, g_attn_post, g_ffn_pre, g_ffn_post, w_in, hy_conv_w, hy_conv_b, hy_f_w1, hy_f_b1, hy_f_w2, hy_f_b2, hy_f_w3, hy_f_freq, hy_bias, mla_g_q, mla_w_uq, mla_g_kv, mla_w_ukv, na_rpb, w_out, w_ffn_gate, w_ffn_up, w_ffn_down):
    cc = jnp.concatenate([c, c_ctx[None, :], jnp.zeros((8 - BATCH - 1, D_MODEL), F32)], axis=0)
    mods_all = _ada(cc, w_ada, b_ada)
    mods = [mods_all[l].reshape(8, 1, 6 * D_MODEL) for l in range(DEPTH)]

    rope_tabs = _rope_tables()
    dft_lat = _dft_tables(SEQ)
    dft_ctx = _dft_tables(CTX_LEN)
    na_starts, na_plan, na_pairs = _na_plan()
    na_t2 = _na_bias_pairs(na_rpb, na_pairs)
    w_in_t = jnp.swapaxes(w_in, 1, 2)
    hy_w = (hy_conv_w, hy_conv_b, hy_f_w1, hy_f_b1, hy_f_w2, hy_f_b2, hy_f_w3, hy_f_freq, hy_bias)

    res = (x.reshape(N_LAT, D_MODEL), ctx.reshape(N_CTX, D_MODEL))
    xn = _prenorm(res[0], res[1], g_attn_pre[0], mods[0])

    for l in range(DEPTH):
        ctx_out = l < DEPTH - 1
        m_rows = N_TOK if ctx_out else N_LAT
        p = _in_proj(xn, w_in_t, l)

        q = _q_proj(p, m_rows, mla_g_q[l], _layout_w_uq(mla_w_uq[l]), rope_tabs)
        k, v = _kv_proj(p, mla_g_kv[l], mla_w_ukv, l, rope_tabs)
        mla = [_mla_latent(q, k, v), None]
        na = [_na_latent(p, na_t2, l, na_starts, na_plan), None]
        hy = [_hyena(p, l, SEQ, 0, dft_lat, *hy_w), None]

        if ctx_out:
            mla[1] = _ctx_attend(q, k, v, 0, 0, 0, MLA_HEADS, MLA_QK, MLA_VW, 1.0, True, "mla_ctx")
            cq = COL_NA // HEAD_DIM
            na[1] = _ctx_attend(p, p, p, cq, cq + NA_HEADS, cq + 2 * NA_HEADS, NA_HEADS, HEAD_DIM, HEAD_DIM,
                                NA_SCALE * LOG2E, False, "na_ctx")
            hy[1] = _hyena(p, l, CTX_LEN, CTX_BLOCK0, dft_ctx, *hy_w)
            res_l = res
        else:
            res_l = (res[0], None)

        stream, xn = _proj_post([tuple(hy), tuple(mla), tuple(na)], w_out, l, g_attn_post[l], mods[l], 2, res_l,
                                m_rows, nxt=(g_ffn_pre[l], mods[l], 3, 4))
        h = _ffn_up(xn, w_ffn_gate, w_ffn_up, l)
        nxt = (g_attn_pre[l + 1], mods[l + 1], 0, 1) if ctx_out else None
        stream, xn = _proj_post([(h, None)], w_ffn_down, l, g_ffn_post[l], mods[l], 5, (stream, None), m_rows,
                                nxt=nxt)
        res = (stream, None)

    return stream.reshape(BATCH, SEQ, D_MODEL)
```

```python
import functools
import math

import jax
import jax.numpy as jnp
import numpy as np
from jax import lax
from jax.experimental import pallas as pl
from jax.experimental.pallas import tpu as pltpu

F32 = jnp.float32
BF16 = jnp.bfloat16

D_MODEL = 2048
BATCH = 4
SEQ = 2048
DEPTH = 2
GRID_W = 64
CTX_LEN = 256
HEAD_DIM = 128
HY_WIDTH = D_MODEL // 4
HY_FILTER_HIDDEN = 64
HY_POS_BANDS = 16
HY_DECAY_TARGET = 1e-2
HY_FAST_DECAY = 0.3
HY_SLOW_DECAY = 1.5
MLA_HEADS = (D_MODEL // 2) // HEAD_DIM
MLA_ROPE = 64
Q_LORA = 3 * D_MODEL // 8
KV_LORA = D_MODEL // 4
MLA_SCALE = (HEAD_DIM + MLA_ROPE) ** -0.5
NA_HEADS = (D_MODEL // 4) // HEAD_DIM
NA_KH = 8
NA_KW = 16
NA_SCALE = HEAD_DIM ** -0.5
FFN_HIDDEN = ((8 * D_MODEL + 3 * 256 - 1) // (3 * 256)) * 256
ROPE_THETA = 10000.0
RMS_EPS = 1e-6
MASK_VALUE = -1e30
LOG2E = math.log2(math.e)

N_LAT = BATCH * SEQ
N_CTX = BATCH * CTX_LEN
N_TOK = N_LAT + N_CTX
CTX_BLOCK0 = N_LAT // CTX_LEN

IN_TN = 512
COL_HY = 0
COL_CQ = 3 * HY_WIDTH
COL_CKV = COL_CQ + Q_LORA
COL_KR = COL_CKV + KV_LORA
W_IN_NA = COL_KR + MLA_ROPE
COL_NA = -(-W_IN_NA // IN_TN) * IN_TN
P_COLS = COL_NA + 3 * NA_HEADS * HEAD_DIM
MLA_QK = 256
MLA_VW = 256

NA_TR = 4
NA_WR = 12
NA_TQ = NA_TR * GRID_W
NA_TK = NA_WR * GRID_W

VMEM_LIMIT = 52 * 1024 * 1024


def _cparams(sem):
    return pltpu.CompilerParams(dimension_semantics=sem, vmem_limit_bytes=VMEM_LIMIT)


def _dot(a, b):
    return jnp.dot(a, b, preferred_element_type=F32)


def _dot_nt(a, b):
    return lax.dot_general(a, b, (((1,), (1,)), ((), ())), preferred_element_type=F32)


def _dot_hi(a, b):
    return jnp.dot(a, b, preferred_element_type=F32, precision=lax.Precision.HIGHEST)


def _rms(x, g):
    ms = jnp.mean(x * x, axis=-1, keepdims=True)
    return x * lax.rsqrt(ms + RMS_EPS) * g


def _mod_row(tm):
    n_lat, per_b = N_LAT // tm, SEQ // tm
    return lambda i: jnp.where(i < n_lat, i // per_b, BATCH)


def _rope_row(tm):
    n_lat, per_b = N_LAT // tm, SEQ // tm
    return lambda i: jnp.where(i < n_lat, i % per_b, per_b + i - n_lat)


def _ada_kernel(c_ref, w_ref, b_ref, o_ref):
    a = c_ref[...]
    a = a * jax.nn.sigmoid(a)
    o_ref[0] = _dot(a.astype(BF16), w_ref[0].astype(BF16)) + b_ref[0]


def _ada(cc, w_ada, b_ada):
    tn = 1024
    n = w_ada.shape[-1]
    return pl.pallas_call(
        _ada_kernel,
        grid=(DEPTH, n // tn),
        in_specs=[
            pl.BlockSpec((8, D_MODEL), lambda l, j: (0, 0)),
            pl.BlockSpec((1, D_MODEL, tn), lambda l, j: (l, 0, j)),
            pl.BlockSpec((1, 1, tn), lambda l, j: (l, 0, j)),
        ],
        out_specs=pl.BlockSpec((1, 8, tn), lambda l, j: (l, 0, j)),
        out_shape=jax.ShapeDtypeStruct((DEPTH, 8, n), F32),
        compiler_params=_cparams(("parallel", "parallel")),
        name="ada",
    )(cc, w_ada, b_ada.reshape(DEPTH, 1, n))


def _norm_mod_to(xn_ref, x_ref, g_ref, sh_ref, sc_ref, chunk=256):
    g = g_ref[...]
    sc = 1.0 + sc_ref[0]
    sh = sh_ref[0]

    def body(r, carry):
        rows = pl.ds(pl.multiple_of(r * chunk, chunk), chunk)
        xn_ref[rows, :] = (_rms(x_ref[rows, :], g) * sc + sh).astype(BF16)
        return carry

    lax.fori_loop(0, x_ref.shape[0] // chunk, body, 0)


def _prenorm_kernel(x_ref, c_ref, g_ref, sh_ref, sc_ref, o_ref, *, n_lat):
    i = pl.program_id(0)

    @pl.when(i < n_lat)
    def _():
        _norm_mod_to(o_ref, x_ref, g_ref, sh_ref, sc_ref)

    @pl.when(i >= n_lat)
    def _():
        _norm_mod_to(o_ref, c_ref, g_ref, sh_ref, sc_ref)


def _prenorm(x2d, ctx2d, g, mods, *, tm=1024):
    n_lat = N_LAT // tm
    row = _mod_row(tm)
    return pl.pallas_call(
        functools.partial(_prenorm_kernel, n_lat=n_lat),
        grid=(N_TOK // tm,),
        in_specs=[
            pl.BlockSpec((tm, D_MODEL), lambda i: (jnp.minimum(i, n_lat - 1), 0)),
            pl.BlockSpec((tm, D_MODEL), lambda i: (jnp.maximum(i - n_lat, 0), 0)),
            pl.BlockSpec((1, D_MODEL), lambda i: (0, 0)),
            pl.BlockSpec((1, 1, D_MODEL), lambda i: (row(i), 0, 0)),
            pl.BlockSpec((1, 1, D_MODEL), lambda i: (row(i), 0, 1)),
        ],
        out_specs=pl.BlockSpec((tm, D_MODEL), lambda i: (i, 0)),
        out_shape=jax.ShapeDtypeStruct((N_TOK, D_MODEL), BF16),
        compiler_params=_cparams(("parallel",)),
        name="prenorm",
    )(x2d, ctx2d, g.reshape(1, D_MODEL), mods, mods)


def _in_kernel(x_ref, w_ref, o_ref):
    o_ref[...] = _dot_nt(x_ref[...], w_ref[0].astype(BF16)).astype(o_ref.dtype)


def _in_proj(xn, w_in_t, l, *, tm=3072):
    n_head = COL_NA // IN_TN

    def w_row(j):
        per = IN_TN // MLA_ROPE
        return MLA_ROPE * jnp.where(j < n_head, j * per, W_IN_NA // MLA_ROPE + (j - n_head) * per)

    return pl.pallas_call(
        _in_kernel,
        grid=(N_TOK // tm, P_COLS // IN_TN),
        in_specs=[
            pl.BlockSpec((tm, D_MODEL), lambda i, j: (i, 0)),
            pl.BlockSpec((pl.Element(1), pl.Element(IN_TN), pl.Element(D_MODEL)), lambda i, j: (l, w_row(j), 0)),
        ],
        out_specs=pl.BlockSpec((tm, IN_TN), lambda i, j: (i, j)),
        out_shape=jax.ShapeDtypeStruct((N_TOK, P_COLS), BF16),
        compiler_params=_cparams(("parallel", "arbitrary")),
        name="in_proj",
    )(xn, w_in_t)


def _swiglu_kernel(x_ref, wg_ref, wu_ref, o_ref):
    xn = x_ref[...]
    gate = _dot(xn, wg_ref[0].astype(BF16))
    up = _dot(xn, wu_ref[0].astype(BF16))
    o_ref[...] = (gate * jax.nn.sigmoid(gate) * up).astype(o_ref.dtype)


def _ffn_up(xn, wg, wu, l, *, tn=512):
    m = xn.shape[0]
    tm = 2048 if m % 2048 == 0 else 1536
    w_spec = pl.BlockSpec((1, D_MODEL, tn), lambda i, j: (l, 0, j))
    return pl.pallas_call(
        _swiglu_kernel,
        grid=(m // tm, FFN_HIDDEN // tn),
        in_specs=[pl.BlockSpec((tm, D_MODEL), lambda i, j: (i, 0)), w_spec, w_spec],
        out_specs=pl.BlockSpec((tm, tn), lambda i, j: (i, j)),
        out_shape=jax.ShapeDtypeStruct((m, FFN_HIDDEN), BF16),
        compiler_params=_cparams(("parallel", "arbitrary")),
        name="ffn_up",
    )(xn, wg, wu)


POST_TM = 256
POST_SUB = 128
POST_WCH = 256
POST_WSLOTS = 4
POST_SLOTS = 3


def _post_kernel(*refs, widths, two_src, emit_xn, n_lat, l):
    refs = list(refs)
    n_act = len(widths)
    acts_lat = [refs.pop(0) for _ in range(n_act)]
    acts_ctx = [refs.pop(0) for _ in range(n_act)] if two_src else None
    w_hbm, g_ref, gate_ref, res_lat = (refs.pop(0) for _ in range(4))
    res_ctx = refs.pop(0) if two_src else None
    if emit_xn:
        g2_ref, sh_ref, sc_ref = (refs.pop(0) for _ in range(3))
    o_hbm = refs.pop(0)
    xn_ref = refs.pop(0) if emit_xn else None
    w_scr, stage, buf, sem_w, sem_res, sem_out = refs

    tm = POST_TM
    i = pl.program_id(0)
    n_i = pl.num_programs(0)
    n_chunks = sum(widths) // POST_WCH
    is_lat = i < n_lat
    is_ctx = jnp.logical_not(is_lat)

    def w_copy(c, slot):
        return pltpu.make_async_copy(w_hbm.at[l, pl.ds(c * POST_WCH, POST_WCH)], stage.at[slot], sem_w.at[slot])

    def res_copy(src, row0, slot):
        return pltpu.make_async_copy(src.at[pl.ds(row0, tm)], buf.at[slot], sem_res.at[slot])

    def res_start(tile, slot):
        if two_src:
            pl.when(tile < n_lat)(lambda: res_copy(res_lat, tile * tm, slot).start())
            pl.when(tile >= n_lat)(lambda: res_copy(res_ctx, (tile - n_lat) * tm, slot).start())
        else:
            res_copy(res_lat, tile * tm, slot).start()

    def out_copy(tile, slot):
        return pltpu.make_async_copy(buf.at[slot], o_hbm.at[pl.ds(tile * tm, tm)], sem_out.at[slot])

    @pl.when(i == 0)
    def _():
        res_start(0, 0)
        for c in range(POST_WSLOTS - 1):
            w_copy(c, c).start()

        def body(c, carry):
            s = c % POST_WSLOTS
            w_copy(c, s).wait()
            ahead = c + POST_WSLOTS - 1

            @pl.when(ahead < n_chunks)
            def _():
                w_copy(ahead, ahead % POST_WSLOTS).start()

            w_scr[pl.ds(pl.multiple_of(c * POST_WCH, POST_WCH), POST_WCH), :] = stage[s].astype(BF16)
            return carry

        lax.fori_loop(0, n_chunks, body, 0)

    slot = i % POST_SLOTS
    nslot = (i + 1) % POST_SLOTS

    @pl.when(i + 1 < n_i)
    def _():
        @pl.when(i >= POST_SLOTS - 1)
        def _():
            out_copy(i + 1 - POST_SLOTS, nslot).wait()

        res_start(i + 1, nslot)

    res_copy(res_lat, 0, slot).wait()

    def compute(act_refs):
        g = g_ref[...]
        gate = gate_ref[0]
        if emit_xn:
            g2 = g2_ref[...]
            sc = 1.0 + sc_ref[0]
            sh = sh_ref[0]
        for r in range(tm // POST_SUB):
            rows = pl.ds(r * POST_SUB, POST_SUB)
            y = None
            k0 = 0
            for a_ref, kw in zip(act_refs, widths):
                part = _dot(a_ref[rows, :], w_scr[k0:k0 + kw, :])
                y = part if y is None else y + part
                k0 += kw
            x_new = buf[slot, rows, :] + gate * _rms(y, g)
            buf[slot, rows, :] = x_new
            if emit_xn:
                xn_ref[rows, :] = (_rms(x_new, g2) * sc + sh).astype(xn_ref.dtype)

    if two_src:
        pl.when(is_lat)(functools.partial(compute, acts_lat))
        pl.when(is_ctx)(functools.partial(compute, acts_ctx))
    else:
        compute(acts_lat)

    out_copy(i, slot).start()

    @pl.when(i == n_i - 1)
    def _():
        for back in range(POST_SLOTS):
            out_copy(i - back, (i - back) % POST_SLOTS).wait()


def _proj_post(acts, w, l, g, mods, gate_chunk, res, m_rows, nxt=None):
    two_src = res[1] is not None
    tm = POST_TM
    n_lat = N_LAT // tm
    widths = tuple(a.shape[1] for a, _ in acts)
    k_total = sum(widths)
    emit_xn = nxt is not None
    row = _mod_row(tm)

    def lat_row(i):
        return jnp.minimum(i, n_lat - 1) if two_src else i

    def ctx_row(i):
        return jnp.maximum(i - n_lat, 0)

    def act_specs(rowf):
        return [pl.BlockSpec((tm, kw), lambda i: (rowf(i), 0)) for kw in widths]

    vec_spec = pl.BlockSpec((1, D_MODEL), lambda i: (0, 0))

    def mod_spec(chunk):
        return pl.BlockSpec((1, 1, D_MODEL), lambda i: (row(i), 0, chunk))

    hbm = pl.BlockSpec(memory_space=pl.ANY)
    in_specs = act_specs(lat_row)
    args = [a for a, _ in acts]
    if two_src:
        in_specs += act_specs(ctx_row)
        args += [c for _, c in acts]
    in_specs += [hbm, vec_spec, mod_spec(gate_chunk), hbm]
    args += [w, g.reshape(1, D_MODEL), mods, res[0]]
    if two_src:
        in_specs.append(hbm)
        args.append(res[1])
    out_specs = [hbm]
    out_shape = [jax.ShapeDtypeStruct((m_rows, D_MODEL), F32)]
    if emit_xn:
        g2, mods2, sh_chunk, sc_chunk = nxt
        in_specs += [vec_spec, mod_spec(sh_chunk), mod_spec(sc_chunk)]
        args += [g2.reshape(1, D_MODEL), mods2, mods2]
        out_specs.append(pl.BlockSpec((tm, D_MODEL), lambda i: (i, 0)))
        out_shape.append(jax.ShapeDtypeStruct((m_rows, D_MODEL), BF16))

    out = pl.pallas_call(
        functools.partial(_post_kernel, widths=widths, two_src=two_src, emit_xn=emit_xn, n_lat=n_lat, l=l),
        grid=(m_rows // tm,),
        in_specs=in_specs,
        out_specs=out_specs,
        out_shape=out_shape,
        scratch_shapes=[
            pltpu.VMEM((k_total, D_MODEL), BF16),
            pltpu.VMEM((POST_WSLOTS, POST_WCH, D_MODEL), F32),
            pltpu.VMEM((POST_SLOTS, tm, D_MODEL), F32),
            pltpu.SemaphoreType.DMA((POST_WSLOTS,)),
            pltpu.SemaphoreType.DMA((POST_SLOTS,)),
            pltpu.SemaphoreType.DMA((POST_SLOTS,)),
        ],
        compiler_params=_cparams(("arbitrary",)),
        name="proj_post",
    )(*args)
    return (out[0], out[1]) if emit_xn else (out[0], None)


def _rope128(r, cos_ref, sa_ref, sb_ref):
    return r * cos_ref[...] + pltpu.roll(r, 96, 1) * sa_ref[...] + pltpu.roll(r, 32, 1) * sb_ref[...]


def _q_kernel(x_ref, g_ref, w_ref, cos_ref, sa_ref, sb_ref, o_ref):
    xn = _rms(x_ref[...].astype(F32), g_ref[...] * (MLA_SCALE * LOG2E)).astype(BF16)
    for h in range(MLA_HEADS):
        acc = _dot(xn, w_ref[:, h * MLA_QK:(h + 1) * MLA_QK])
        o_ref[:, h * MLA_QK:h * MLA_QK + HEAD_DIM] = acc[:, :HEAD_DIM].astype(o_ref.dtype)
        o_ref[:, h * MLA_QK + HEAD_DIM:(h + 1) * MLA_QK] = _rope128(
            acc[:, HEAD_DIM:], cos_ref, sa_ref, sb_ref).astype(o_ref.dtype)


def _q_proj(p, m_rows, g, w, tabs, *, tm=1024):
    rope = _rope_row(tm)
    tab_spec = pl.BlockSpec((tm, HEAD_DIM), lambda i: (rope(i), 0))
    return pl.pallas_call(
        _q_kernel,
        grid=(m_rows // tm,),
        in_specs=[
            pl.BlockSpec((tm, Q_LORA), lambda i: (i, COL_CQ // Q_LORA)),
            pl.BlockSpec((1, Q_LORA), lambda i: (0, 0)),
            pl.BlockSpec((Q_LORA, MLA_HEADS * MLA_QK), lambda i: (0, 0)),
            tab_spec, tab_spec, tab_spec,
        ],
        out_specs=pl.BlockSpec((tm, MLA_HEADS * MLA_QK), lambda i: (i, 0)),
        out_shape=jax.ShapeDtypeStruct((m_rows, MLA_HEADS * MLA_QK), BF16),
        compiler_params=_cparams(("parallel",)),
        name="q_proj",
    )(p, g.reshape(1, Q_LORA), w, *tabs)


def _kv_kernel(xa_ref, xb_ref, kr_ref, g_ref, w_ref, cos_ref, sa_ref, sb_ref, k_ref, v_ref):
    half = KV_LORA // 2
    xa = xa_ref[...].astype(F32)
    xb = xb_ref[...].astype(F32)
    ms = (jnp.sum(xa * xa, axis=-1, keepdims=True) + jnp.sum(xb * xb, axis=-1, keepdims=True)) * (1.0 / KV_LORA)
    rs = lax.rsqrt(ms + RMS_EPS)
    g = g_ref[...]
    xna = (xa * rs * g[:, :half]).astype(BF16)
    xnb = (xb * rs * g[:, half:]).astype(BF16)
    krr = _rope128(kr_ref[...].astype(F32), cos_ref, sa_ref, sb_ref).astype(k_ref.dtype)
    ones = jnp.ones((xa.shape[0], MLA_VW - HEAD_DIM), v_ref.dtype)
    for h in range(MLA_HEADS):
        w = w_ref[0, :, h * 2 * HEAD_DIM:(h + 1) * 2 * HEAD_DIM].astype(BF16)
        acc = _dot(xna, w[:half]) + _dot(xnb, w[half:])
        k_ref[:, h * MLA_QK:h * MLA_QK + HEAD_DIM] = acc[:, :HEAD_DIM].astype(k_ref.dtype)
        k_ref[:, h * MLA_QK + HEAD_DIM:(h + 1) * MLA_QK] = krr
        v_ref[:, h * MLA_VW:h * MLA_VW + HEAD_DIM] = acc[:, HEAD_DIM:].astype(v_ref.dtype)
        v_ref[:, h * MLA_VW + HEAD_DIM:(h + 1) * MLA_VW] = ones


def _kv_proj(p, g, w, l, tabs, *, tm=1024):
    m = p.shape[0]
    rope = _rope_row(tm)
    half = KV_LORA // 2
    tab_spec = pl.BlockSpec((tm, HEAD_DIM), lambda i: (rope(i), 0))
    return pl.pallas_call(
        _kv_kernel,
        grid=(m // tm,),
        in_specs=[
            pl.BlockSpec((tm, half), lambda i: (i, COL_CKV // half)),
            pl.BlockSpec((tm, half), lambda i: (i, COL_CKV // half + 1)),
            pl.BlockSpec((tm, HEAD_DIM), lambda i: (i, COL_KR // HEAD_DIM)),
            pl.BlockSpec((1, KV_LORA), lambda i: (0, 0)),
            pl.BlockSpec((1, KV_LORA, MLA_HEADS * 2 * HEAD_DIM), lambda i: (l, 0, 0)),
            tab_spec, tab_spec, tab_spec,
        ],
        out_specs=[
            pl.BlockSpec((tm, MLA_HEADS * MLA_QK), lambda i: (i, 0)),
            pl.BlockSpec((tm, MLA_HEADS * MLA_VW), lambda i: (i, 0)),
        ],
        out_shape=[
            jax.ShapeDtypeStruct((m, MLA_HEADS * MLA_QK), BF16),
            jax.ShapeDtypeStruct((m, MLA_HEADS * MLA_VW), BF16),
        ],
        compiler_params=_cparams(("parallel",)),
        name="kv_proj",
    )(p, p, p, g.reshape(1, KV_LORA), w, *tabs)


def _softmax_pv(s_list, v_list, ones_col):
    m = jnp.max(s_list[0], axis=-1, keepdims=True)
    for s in s_list[1:]:
        m = jnp.maximum(m, jnp.max(s, axis=-1, keepdims=True))
    acc = None
    den = None
    for s, v in zip(s_list, v_list):
        p = jnp.exp2(s - m)
        if not ones_col:
            d = jnp.sum(p, axis=-1, keepdims=True)
            den = d if den is None else den + d
        o = _dot(p.astype(BF16), v)
        acc = o if acc is None else acc + o
    if ones_col:
        return acc[:, :HEAD_DIM] / acc[:, HEAD_DIM:]
    return acc / den


def _attn_kernel(*refs, scale, two, ones_col, chains, heads):
    if two:
        q_ref, k1_ref, v1_ref, k2_ref, v2_ref, o_ref = refs
    else:
        q_ref, k1_ref, v1_ref, o_ref = refs
    tq = q_ref.shape[0] // chains
    dqk = q_ref.shape[1] // heads
    dv = v1_ref.shape[1] // heads
    for h in range(heads):
        qk_cols = slice(h * dqk, (h + 1) * dqk)
        v_cols = slice(h * dv, (h + 1) * dv)
        for c in range(chains):
            rows = slice(c * tq, (c + 1) * tq)
            q = q_ref[rows, qk_cols]
            s_list = [_dot_nt(q, k1_ref[:, qk_cols])]
            v_list = [v1_ref[:, v_cols]]
            if two:
                s_list.append(_dot_nt(q, k2_ref[:, qk_cols]))
                v_list.append(v2_ref[:, v_cols])
            if scale != 1.0:
                s_list = [s * scale for s in s_list]
            o_ref[rows, h * HEAD_DIM:(h + 1) * HEAD_DIM] = _softmax_pv(s_list, v_list, ones_col).astype(o_ref.dtype)


def _mla_latent(q, k, v, *, tq=2048):
    nq = SEQ // tq
    return pl.pallas_call(
        functools.partial(_attn_kernel, scale=1.0, two=True, ones_col=True, chains=8, heads=1),
        grid=(BATCH, MLA_HEADS, nq),
        in_specs=[
            pl.BlockSpec((tq, MLA_QK), lambda b, h, i: (b * nq + i, h)),
            pl.BlockSpec((SEQ, MLA_QK), lambda b, h, i: (b, h)),
            pl.BlockSpec((SEQ, MLA_VW), lambda b, h, i: (b, h)),
            pl.BlockSpec((CTX_LEN, MLA_QK), lambda b, h, i: (CTX_BLOCK0 + b, h)),
            pl.BlockSpec((CTX_LEN, MLA_VW), lambda b, h, i: (CTX_BLOCK0 + b, h)),
        ],
        out_specs=pl.BlockSpec((tq, HEAD_DIM), lambda b, h, i: (b * nq + i, h)),
        out_shape=jax.ShapeDtypeStruct((N_LAT, MLA_HEADS * HEAD_DIM), BF16),
        compiler_params=_cparams(("parallel", "parallel", "arbitrary")),
        name="mla_latent",
    )(q, k, v, k, v)


def _ctx_attend(q, k, v, q_col0, k_col0, v_col0, heads, dqk, dv, scale, ones_col, name):
    return pl.pallas_call(
        functools.partial(_attn_kernel, scale=scale, two=False, ones_col=ones_col, chains=1, heads=heads),
        grid=(BATCH,),
        in_specs=[
            pl.BlockSpec((CTX_LEN, heads * dqk), lambda b: (CTX_BLOCK0 + b, q_col0)),
            pl.BlockSpec((CTX_LEN, heads * dqk), lambda b: (CTX_BLOCK0 + b, k_col0)),
            pl.BlockSpec((CTX_LEN, heads * dv), lambda b: (CTX_BLOCK0 + b, v_col0)),
        ],
        out_specs=pl.BlockSpec((CTX_LEN, heads * HEAD_DIM), lambda b: (b, 0)),
        out_shape=jax.ShapeDtypeStruct((N_CTX, heads * HEAD_DIM), BF16),
        compiler_params=_cparams(("parallel",)),
        name=name,
    )(q, k, v)


def _na_plan():
    rows = SEQ // GRID_W
    invalid = 2 * NA_KH - 1
    pairs, plan, starts = [], [], []
    for t in range(rows // NA_TR):
        kw0 = int(np.clip(NA_TR * t - NA_KH // 2, 0, rows - NA_WR))
        starts.append(kw0)
        tile = []
        for ri in range(NA_TR):
            r = NA_TR * t + ri
            r0 = int(np.clip(r - NA_KH // 2, 0, rows - NA_KH))
            assert kw0 <= r0 and r0 + NA_KH <= kw0 + NA_WR
            row = []
            for kp in range(NA_WR // 2):
                pair = []
                for kr in (kw0 + 2 * kp, kw0 + 2 * kp + 1):
                    pair.append(kr - r + NA_KH - 1 if r0 <= kr < r0 + NA_KH else invalid)
                pair = tuple(pair)
                if pair not in pairs:
                    pairs.append(pair)
                row.append(pairs.index(pair))
            tile.append(row)
        plan.append(tile)
    return starts, plan, pairs


def _na_bias_pairs(rpb, pairs):
    c = np.arange(GRID_W)
    c0 = np.clip(c - NA_KW // 2, 0, GRID_W - NA_KW)
    col_ok = (c[None, :] >= c0[:, None]) & (c[None, :] < c0[:, None] + NA_KW)
    col_idx = np.clip(c[None, :] - c[:, None] + NA_KW - 1, 0, 2 * NA_KW - 2)
    onehot = (col_idx[None] == np.arange(2 * NA_KW - 1)[:, None, None]).astype(np.float32)
    t = jnp.einsum("lhdj,jck->lhdck", rpb.astype(F32), onehot, precision=lax.Precision.HIGHEST) * LOG2E
    t = jnp.where(col_ok, t, MASK_VALUE)
    masked = jnp.full(t.shape[:2] + (GRID_W, GRID_W), MASK_VALUE, F32)
    slabs = [t[:, :, d] for d in range(2 * NA_KH - 1)] + [masked]
    return jnp.stack([jnp.concatenate([slabs[a], slabs[b]], axis=-1) for a, b in pairs], axis=2)


def _na_kernel(q_ref, k_ref, v_ref, kc_ref, vc_ref, t2_ref, o_ref, vo_ref, vco_ref, *, starts, plan):
    ones = jnp.ones((SEQ, HEAD_DIM), BF16)
    vo_ref[:, :HEAD_DIM] = v_ref[...]
    vo_ref[:, HEAD_DIM:] = ones
    vco_ref[:, :HEAD_DIM] = vc_ref[...]
    vco_ref[:, HEAD_DIM:] = ones[:CTX_LEN]
    kc = kc_ref[...]
    vc = vco_ref[...]
    for t, (kw0, tile) in enumerate(zip(starts, plan)):
        q = q_ref[t * NA_TQ:(t + 1) * NA_TQ, :]
        kw = k_ref[kw0 * GRID_W:kw0 * GRID_W + NA_TK, :]
        vw = vo_ref[kw0 * GRID_W:kw0 * GRID_W + NA_TK, :]
        bias = jnp.concatenate(
            [jnp.concatenate([t2_ref[0, 0, idx] for idx in row], axis=1) for row in tile], axis=0)
        s = _dot_nt(q, kw) * (NA_SCALE * LOG2E) + bias
        sc = _dot_nt(q, kc) * (NA_SCALE * LOG2E)
        o_ref[t * NA_TQ:(t + 1) * NA_TQ, :] = _softmax_pv([s, sc], [vw, vc], True).astype(o_ref.dtype)


def _na_latent(p, t2, l, starts, plan):
    cq = COL_NA // HEAD_DIM
    ck = cq + NA_HEADS
    cv = ck + NA_HEADS
    n_pairs = t2.shape[2]
    return pl.pallas_call(
        functools.partial(_na_kernel, starts=starts, plan=plan),
        grid=(NA_HEADS, BATCH),
        in_specs=[
            pl.BlockSpec((SEQ, HEAD_DIM), lambda h, b: (b, cq + h)),
            pl.BlockSpec((SEQ, HEAD_DIM), lambda h, b: (b, ck + h)),
            pl.BlockSpec((SEQ, HEAD_DIM), lambda h, b: (b, cv + h)),
            pl.BlockSpec((CTX_LEN, HEAD_DIM), lambda h, b: (CTX_BLOCK0 + b, ck + h)),
            pl.BlockSpec((CTX_LEN, HEAD_DIM), lambda h, b: (CTX_BLOCK0 + b, cv + h)),
            pl.BlockSpec((1, 1, n_pairs, GRID_W, 2 * GRID_W), lambda h, b: (l, h, 0, 0, 0)),
        ],
        out_specs=pl.BlockSpec((SEQ, HEAD_DIM), lambda h, b: (b, h)),
        out_shape=jax.ShapeDtypeStruct((N_LAT, NA_HEADS * HEAD_DIM), BF16),
        scratch_shapes=[pltpu.VMEM((SEQ, 2 * HEAD_DIM), BF16), pltpu.VMEM((CTX_LEN, 2 * HEAD_DIM), BF16)],
        compiler_params=_cparams(("parallel", "parallel")),
        name="na_latent",
    )(p, p, p, p, p, t2)


def _conv_kernel(v_ref, x1_ref, x2_ref, wv_ref, w1_ref, w2_ref, bv_ref, b1_ref, b2_ref, zin_ref, x2o_ref):
    n = v_ref.shape[0]
    row = lax.broadcasted_iota(jnp.int32, (n, 1), 0)

    def short_conv(p_ref, w_ref, b_ref):
        p = p_ref[...].astype(F32)
        prev = jnp.where(row == 0, 0.0, pltpu.roll(p, 1, 0))
        nxt = jnp.where(row == n - 1, 0.0, pltpu.roll(p, n - 1, 0))
        w = w_ref[0]
        return prev * w[0:1] + p * w[1:2] + nxt * w[2:3] + b_ref[0]

    zin_ref[...] = (short_conv(x1_ref, w1_ref, b1_ref) * short_conv(v_ref, wv_ref, bv_ref)).astype(zin_ref.dtype)
    x2o_ref[...] = short_conv(x2_ref, w2_ref, b2_ref).astype(x2o_ref.dtype)


def _hy_conv(p, conv_w, conv_b, l, n, row_block0, *, tc=256):
    nc = HY_WIDTH // tc

    def seg(s):
        return (pl.BlockSpec((n, tc), lambda b, j: (row_block0 + b, s * nc + j)),
                pl.BlockSpec((1, 3, tc), lambda b, j: (l, 0, s * nc + j)),
                pl.BlockSpec((1, 1, tc), lambda b, j: (l, 0, s * nc + j)))

    (pv, wv, bv), (p1, w1, b1), (p2, w2, b2) = seg(0), seg(1), seg(2)
    out_spec = pl.BlockSpec((n, tc), lambda b, j: (0, b * nc + j))
    out = jax.ShapeDtypeStruct((n, BATCH * HY_WIDTH), BF16)
    conv_b = conv_b.reshape(DEPTH, 1, -1)
    return pl.pallas_call(
        _conv_kernel,
        grid=(BATCH, nc),
        in_specs=[pv, p1, p2, wv, w1, w2, bv, b1, b2],
        out_specs=[out_spec, out_spec],
        out_shape=[out, out],
        compiler_params=_cparams(("parallel", "parallel")),
        name="hy_conv",
    )(p, p, p, conv_w, conv_w, conv_w, conv_b, conv_b, conv_b)


def _filt_kernel(z_ref, t_ref, dl_ref, w1_ref, b1_ref, w2_ref, b2_ref, w3_ref, fr_ref, hs_ref, ha_ref, kn_ref):
    n = z_ref.shape[0]
    fr = fr_ref[...]
    h = jnp.sin(fr * (_dot_hi(z_ref[...], w1_ref[...]) + b1_ref[...]))
    h = jnp.sin(fr * (_dot_hi(h, w2_ref[...]) + b2_ref[...]))
    h = _dot_hi(h, w3_ref[...])
    decay = jnp.exp(-t_ref[...] * dl_ref[...])
    row = lax.broadcasted_iota(jnp.int32, (n, 1), 0)
    hf = h[:, :HY_WIDTH] * decay
    hb = jnp.where(row == 0, 0.0, h[:, HY_WIDTH:] * decay)
    hs = hf + hb
    hs_ref[...] = hs.astype(hs_ref.dtype)
    ha_ref[...] = (hf - hb).astype(ha_ref.dtype)
    sign = jnp.where((row & 1) == 0, 1.0, -1.0)
    kn_ref[...] = jnp.sum(hs * sign, axis=0, keepdims=True)


def _hy_filter_taps(n, f_w1, f_b1, f_w2, f_b2, f_w3, f_freq):
    pos = jnp.arange(n, dtype=F32)
    t = jnp.linspace(0.0, 1.0, n, dtype=F32)
    bands = jnp.linspace(1e-4, HY_POS_BANDS - 1, HY_POS_BANDS, dtype=F32)
    ang = (2.0 * math.pi / n) * pos[:, None] * bands[None, :]
    z = jnp.concatenate([t[:, None], jnp.cos(ang), -jnp.sin(ang)], axis=-1)
    pad = HY_FILTER_HIDDEN - z.shape[1]
    z = jnp.pad(z, ((0, 0), (0, pad)))
    w1 = jnp.pad(f_w1.astype(F32), ((0, pad), (0, 0)))
    deltas = jnp.abs(jnp.linspace(math.log(HY_DECAY_TARGET) / HY_FAST_DECAY,
                                  math.log(HY_DECAY_TARGET) / HY_SLOW_DECAY, HY_WIDTH, dtype=F32))
    hid = HY_FILTER_HIDDEN
    return pl.pallas_call(
        _filt_kernel,
        out_shape=[
            jax.ShapeDtypeStruct((n, HY_WIDTH), BF16),
            jax.ShapeDtypeStruct((n, HY_WIDTH), BF16),
            jax.ShapeDtypeStruct((1, HY_WIDTH), F32),
        ],
        compiler_params=pltpu.CompilerParams(vmem_limit_bytes=VMEM_LIMIT),
        name="hy_filter",
    )(z, t[:, None], deltas[None, :], w1, f_b1.reshape(1, hid), f_w2, f_b2.reshape(1, hid), f_w3,
      f_freq.reshape(1, hid))


def _dft_tables(n):
    lo = 16 if n < 1024 else 32
    hi = n // lo
    t = jnp.arange(n, dtype=jnp.int32)[None, :]
    big = 2 * n

    def ang(ff):
        return ((ff * t) % big).astype(F32) * (2.0 * math.pi / big)

    a = ang(lo * jnp.arange(hi, dtype=jnp.int32)[:, None])
    b = ang(jnp.arange(lo, dtype=jnp.int32)[:, None])
    ca, sa, cb, sb = jnp.cos(a), jnp.sin(a), jnp.cos(b), jnp.sin(b)
    cos = (ca[:, None, :] * cb[None, :, :] - sa[:, None, :] * sb[None, :, :]).reshape(n, n)
    base = -(sa[:, None, :] * cb[None, :, :] + ca[:, None, :] * sb[None, :, :]).reshape(n, n)
    idx = jnp.arange(n)
    alt = jnp.where(idx % 2 == 0, 1.0, -1.0).astype(F32)
    msin = jnp.where(idx[:, None] == 0, alt[None, :], base)
    msin_t = jnp.where(idx[None, :] == 0, alt[:, None], base)
    return cos.astype(BF16), msin.astype(BF16), msin_t.astype(BF16)


def _dft_filt_kernel(c_ref, s_ref, hs_ref, ha_ref, kr_ref, ki_ref):
    kr_ref[...] = _dot(c_ref[...], hs_ref[...])
    ki_ref[...] = _dot(s_ref[...], ha_ref[...])


def _hy_filter_spectrum(cos, msin, hs, ha, *, tf):
    n = cos.shape[0]
    tf = min(tf, n)
    tab = pl.BlockSpec((tf, n), lambda i: (i, 0))
    taps = pl.BlockSpec((n, HY_WIDTH), lambda i: (0, 0))
    out = pl.BlockSpec((tf, HY_WIDTH), lambda i: (i, 0))
    return pl.pallas_call(
        _dft_filt_kernel,
        grid=(n // tf,),
        in_specs=[tab, tab, taps, taps],
        out_specs=[out, out],
        out_shape=[jax.ShapeDtypeStruct((n, HY_WIDTH), F32)] * 2,
        compiler_params=_cparams(("parallel",)),
        name="hy_filter_dft",
    )(cos, msin, hs, ha)


def _batch_cols(b):
    return slice(b * HY_WIDTH, (b + 1) * HY_WIDTH)


def _dft_fwd_kernel(c_ref, s_ref, x_ref, kr_ref, ki_ref, kn_ref, yr_ref, yi_ref, *, tf, inv_n):
    x = x_ref[...]
    zr = _dot(c_ref[...], x)
    zi = _dot(s_ref[...], x)
    row = pl.program_id(0) * tf + lax.broadcasted_iota(jnp.int32, (tf, 1), 0)
    bin0 = row == 0
    wt = jnp.where(bin0, inv_n, 2.0 * inv_n)
    kr = kr_ref[...] * wt
    ki = jnp.where(bin0, 0.0, ki_ref[...] * wt)
    kr_im = jnp.where(bin0, kn_ref[...] * wt, kr)
    for b in range(BATCH):
        cols = _batch_cols(b)
        yr_ref[:, cols] = (zr[:, cols] * kr - zi[:, cols] * ki).astype(yr_ref.dtype)
        yi_ref[:, cols] = (zr[:, cols] * ki + zi[:, cols] * kr_im).astype(yi_ref.dtype)


def _hy_dft_fwd(cos, msin, zin, kr, ki, kn, *, tf):
    n = cos.shape[0]
    tf = min(tf, n)
    tab = pl.BlockSpec((tf, n), lambda i: (i, 0))
    filt = pl.BlockSpec((tf, HY_WIDTH), lambda i: (i, 0))
    out = pl.BlockSpec((tf, BATCH * HY_WIDTH), lambda i: (i, 0))
    whole = pl.BlockSpec((n, BATCH * HY_WIDTH), lambda i: (0, 0), pipeline_mode=pl.Buffered(1))
    return pl.pallas_call(
        functools.partial(_dft_fwd_kernel, tf=tf, inv_n=1.0 / (2 * n)),
        grid=(n // tf,),
        in_specs=[tab, tab, whole, filt, filt, pl.BlockSpec((1, HY_WIDTH), lambda i: (0, 0))],
        out_specs=[out, out],
        out_shape=[jax.ShapeDtypeStruct((n, BATCH * HY_WIDTH), BF16)] * 2,
        compiler_params=_cparams(("parallel",)),
        name="hy_dft_fwd",
    )(cos, msin, zin, kr, ki, kn)


def _dft_inv_kernel(c_ref, st_ref, yr_ref, yi_ref, zin_ref, x2_ref, b_ref, o_ref):
    y = _dot(c_ref[...], yr_ref[...]) + _dot(st_ref[...], yi_ref[...])
    bias = b_ref[0]
    for b in range(BATCH):
        cols = _batch_cols(b)
        yb = y[:, cols] + zin_ref[:, cols].astype(F32) * bias
        o_ref[b] = (x2_ref[:, cols].astype(F32) * yb).astype(o_ref.dtype)


def _hy_dft_inv(cos, msin_t, yr, yi, zin, x2, bias, l, *, tt):
    n = cos.shape[0]
    tt = min(tt, n)
    tab = pl.BlockSpec((tt, n), lambda i: (i, 0))
    whole = pl.BlockSpec((n, BATCH * HY_WIDTH), lambda i: (0, 0), pipeline_mode=pl.Buffered(1))
    rows = pl.BlockSpec((tt, BATCH * HY_WIDTH), lambda i: (i, 0))
    out = pl.pallas_call(
        _dft_inv_kernel,
        grid=(n // tt,),
        in_specs=[tab, tab, whole, whole, rows, rows, pl.BlockSpec((1, 1, HY_WIDTH), lambda i: (l, 0, 0))],
        out_specs=pl.BlockSpec((BATCH, tt, HY_WIDTH), lambda i: (0, i, 0)),
        out_shape=jax.ShapeDtypeStruct((BATCH, n, HY_WIDTH), BF16),
        compiler_params=_cparams(("parallel",)),
        name="hy_dft_inv",
    )(cos, msin_t, yr, yi, zin, x2, bias.reshape(DEPTH, 1, HY_WIDTH))
    return out.reshape(BATCH * n, HY_WIDTH)


def _hyena(p, l, n, row_block0, tables, conv_w, conv_b, f_w1, f_b1, f_w2, f_b2, f_w3, f_freq, bias):
    cos, msin, msin_t = tables
    zin, x2 = _hy_conv(p, conv_w, conv_b, l, n, row_block0)
    hs, ha, kn = _hy_filter_taps(n, f_w1[l], f_b1[l], f_w2[l], f_b2[l], f_w3[l], f_freq[l])
    kr, ki = _hy_filter_spectrum(cos, msin, hs, ha, tf=512)
    yr, yi = _hy_dft_fwd(cos, msin, zin, kr, ki, kn, tf=512)
    return _hy_dft_inv(cos, msin_t, yr, yi, zin, x2, bias, l, tt=512)


def _rope_tables():
    tok = jnp.arange(SEQ)
    row = (tok // GRID_W).astype(F32)
    col = (tok % GRID_W).astype(F32)
    n_freq = MLA_ROPE // 4
    inv = ROPE_THETA ** (-jnp.arange(n_freq, dtype=F32) / n_freq)
    ang = jnp.concatenate([row[:, None] * inv, col[:, None] * inv], axis=-1)
    cos, sin = jnp.cos(ang), jnp.sin(ang)
    half = MLA_ROPE // 2
    zeros = jnp.zeros((SEQ, half), F32)
    rest = HEAD_DIM - MLA_ROPE
    cos_t = jnp.concatenate([cos, cos, jnp.ones((SEQ, rest), F32)], axis=-1)
    sin_a = jnp.concatenate([-sin, zeros, jnp.zeros((SEQ, rest), F32)], axis=-1)
    sin_b = jnp.concatenate([zeros, sin, jnp.zeros((SEQ, rest), F32)], axis=-1)
    ident = jnp.ones((N_CTX, HEAD_DIM), F32)
    none = jnp.zeros((N_CTX, HEAD_DIM), F32)
    return (jnp.concatenate([cos_t, ident]), jnp.concatenate([sin_a, none]), jnp.concatenate([sin_b, none]))


def _layout_w_uq(w):
    w = w.reshape(Q_LORA, MLA_HEADS, HEAD_DIM + MLA_ROPE)
    w = jnp.pad(w, ((0, 0), (0, 0), (0, MLA_QK - HEAD_DIM - MLA_ROPE)))
    return w.reshape(Q_LORA, MLA_HEADS * MLA_QK).astype(BF16)


def kernel(x, c, ctx, c_ctx, w_ada, b_ada, g_attn_pre, g_attn_post, g_ffn_pre, g_ffn_post, w_in, hy_conv_w, hy_conv_b, hy_f_w1, hy_f_b1, hy_f_w2, hy_f_b2, hy_f_w3, hy_f_freq, hy_bias, mla_g_q, mla_w_uq, mla_g_kv, mla_w_ukv, na_rpb, w_out, w_ffn_gate, w_ffn_up, w_ffn_down):
    cc = jnp.concatenate([c, c_ctx[None, :], jnp.zeros((8 - BATCH - 1, D_MODEL), F32)], axis=0)
    mods_all = _ada(cc, w_ada, b_ada)
    mods = [mods_all[l].reshape(8, 1, 6 * D_MODEL) for l in range(DEPTH)]

    rope_tabs = _rope_tables()
    dft_lat = _dft_tables(SEQ)
    dft_ctx = _dft_tables(CTX_LEN)
    na_starts, na_plan, na_pairs = _na_plan()
    na_t2 = _na_bias_pairs(na_rpb, na_pairs)
    w_in_t = jnp.swapaxes(w_in, 1, 2)
    hy_w = (hy_conv_w, hy_conv_b, hy_f_w1, hy_f_b1, hy_f_w2, hy_f_b2, hy_f_w3, hy_f_freq, hy_bias)

    res = (x.reshape(N_LAT, D_MODEL), ctx.reshape(N_CTX, D_MODEL))
    xn = _prenorm(res[0], res[1], g_attn_pre[0], mods[0])

    for l in range(DEPTH):
        ctx_out = l < DEPTH - 1
        m_rows = N_TOK if ctx_out else N_LAT
        p = _in_proj(xn, w_in_t, l)

        q = _q_proj(p, m_rows, mla_g_q[l], _layout_w_uq(mla_w_uq[l]), rope_tabs)
        k, v = _kv_proj(p, mla_g_kv[l], mla_w_ukv, l, rope_tabs)
        mla = [_mla_latent(q, k, v), None]
        na = [_na_latent(p, na_t2, l, na_starts, na_plan), None]
        hy = [_hyena(p, l, SEQ, 0, dft_lat, *hy_w), None]

        if ctx_out:
            mla[1] = _ctx_attend(q, k, v, 0, 0, 0, MLA_HEADS, MLA_QK, MLA_VW, 1.0, True, "mla_ctx")
            cq = COL_NA // (NA_HEADS * HEAD_DIM)
            na[1] = _ctx_attend(p, p, p, cq, cq + 1, cq + 2, NA_HEADS, HEAD_DIM, HEAD_DIM, NA_SCALE * LOG2E, False,
                                "na_ctx")
            hy[1] = _hyena(p, l, CTX_LEN, CTX_BLOCK0, dft_ctx, *hy_w)
            res_l = res
        else:
            res_l = (res[0], None)

        stream, xn = _proj_post([tuple(hy), tuple(mla), tuple(na)], w_out, l, g_attn_post[l], mods[l], 2, res_l,
                                m_rows, nxt=(g_ffn_pre[l], mods[l], 3, 4))
        h = _ffn_up(xn, w_ffn_gate, w_ffn_up, l)
        nxt = (g_attn_pre[l + 1], mods[l + 1], 0, 1) if ctx_out else None
        stream, xn = _proj_post([(h, None)], w_ffn_down, l, g_ffn_post[l], mods[l], 5, (stream, None), m_rows,
                                nxt=nxt)
        res = (stream, None)

    return stream.reshape(BATCH, SEQ, D_MODEL)
```

```python
import functools
import math

import jax
import jax.numpy as jnp
import numpy as np
from jax import lax
from jax.experimental import pallas as pl
from jax.experimental.pallas import tpu as pltpu

F32 = jnp.float32
BF16 = jnp.bfloat16

D_MODEL = 2048
BATCH = 4
SEQ = 2048
DEPTH = 2
GRID_W = 64
CTX_LEN = 256
HEAD_DIM = 128
HY_WIDTH = D_MODEL // 4
HY_FILTER_HIDDEN = 64
HY_POS_BANDS = 16
HY_DECAY_TARGET = 1e-2
HY_FAST_DECAY = 0.3
HY_SLOW_DECAY = 1.5
MLA_HEADS = (D_MODEL // 2) // HEAD_DIM
MLA_ROPE = 64
Q_LORA = 3 * D_MODEL // 8
KV_LORA = D_MODEL // 4
MLA_SCALE = (HEAD_DIM + MLA_ROPE) ** -0.5
NA_HEADS = (D_MODEL // 4) // HEAD_DIM
NA_KH = 8
NA_KW = 16
NA_SCALE = HEAD_DIM ** -0.5
FFN_HIDDEN = ((8 * D_MODEL + 3 * 256 - 1) // (3 * 256)) * 256
ROPE_THETA = 10000.0
RMS_EPS = 1e-6
MASK_VALUE = -1e30
LOG2E = math.log2(math.e)

N_LAT = BATCH * SEQ
N_CTX = BATCH * CTX_LEN
N_TOK = N_LAT + N_CTX
CTX_BLOCK0 = N_LAT // CTX_LEN

IN_TN = 512
COL_HY = 0
COL_CQ = 3 * HY_WIDTH
COL_CKV = COL_CQ + Q_LORA
COL_KR = COL_CKV + KV_LORA
W_IN_NA = COL_KR + MLA_ROPE
COL_NA = -(-W_IN_NA // IN_TN) * IN_TN
P_COLS = COL_NA + 3 * NA_HEADS * HEAD_DIM
MLA_QK = 256
MLA_VW = 256

NA_TR = 4
NA_WR = 12
NA_TQ = NA_TR * GRID_W
NA_TK = NA_WR * GRID_W

VMEM_LIMIT = 52 * 1024 * 1024


def _cparams(sem):
    return pltpu.CompilerParams(dimension_semantics=sem, vmem_limit_bytes=VMEM_LIMIT)


def _dot(a, b):
    return jnp.dot(a, b, preferred_element_type=F32)


def _dot_nt(a, b):
    return lax.dot_general(a, b, (((1,), (1,)), ((), ())), preferred_element_type=F32)


def _dot_hi(a, b):
    return jnp.dot(a, b, preferred_element_type=F32, precision=lax.Precision.HIGHEST)


def _rms(x, g):
    ms = jnp.mean(x * x, axis=-1, keepdims=True)
    return x * lax.rsqrt(ms + RMS_EPS) * g


def _mod_row(tm):
    n_lat, per_b = N_LAT // tm, SEQ // tm
    return lambda i: jnp.where(i < n_lat, i // per_b, BATCH)


def _rope_row(tm):
    n_lat, per_b = N_LAT // tm, SEQ // tm
    return lambda i: jnp.where(i < n_lat, i % per_b, per_b + i - n_lat)


def _ada_kernel(c_ref, w_ref, b_ref, o_ref):
    a = c_ref[...]
    a = a * jax.nn.sigmoid(a)
    o_ref[0] = _dot(a.astype(BF16), w_ref[0].astype(BF16)) + b_ref[0]


def _ada(cc, w_ada, b_ada):
    tn = 1024
    n = w_ada.shape[-1]
    return pl.pallas_call(
        _ada_kernel,
        grid=(DEPTH, n // tn),
        in_specs=[
            pl.BlockSpec((8, D_MODEL), lambda l, j: (0, 0)),
            pl.BlockSpec((1, D_MODEL, tn), lambda l, j: (l, 0, j)),
            pl.BlockSpec((1, 1, tn), lambda l, j: (l, 0, j)),
        ],
        out_specs=pl.BlockSpec((1, 8, tn), lambda l, j: (l, 0, j)),
        out_shape=jax.ShapeDtypeStruct((DEPTH, 8, n), F32),
        compiler_params=_cparams(("parallel", "parallel")),
        name="ada",
    )(cc, w_ada, b_ada.reshape(DEPTH, 1, n))


def _norm_mod_to(xn_ref, x_ref, g_ref, sh_ref, sc_ref, chunk=256):
    g = g_ref[...]
    sc = 1.0 + sc_ref[0]
    sh = sh_ref[0]

    def body(r, carry):
        rows = pl.ds(pl.multiple_of(r * chunk, chunk), chunk)
        xn_ref[rows, :] = (_rms(x_ref[rows, :], g) * sc + sh).astype(BF16)
        return carry

    lax.fori_loop(0, x_ref.shape[0] // chunk, body, 0)


def _prenorm_kernel(x_ref, c_ref, g_ref, sh_ref, sc_ref, o_ref, *, n_lat):
    i = pl.program_id(0)

    @pl.when(i < n_lat)
    def _():
        _norm_mod_to(o_ref, x_ref, g_ref, sh_ref, sc_ref)

    @pl.when(i >= n_lat)
    def _():
        _norm_mod_to(o_ref, c_ref, g_ref, sh_ref, sc_ref)


def _prenorm(x2d, ctx2d, g, mods, *, tm=1024):
    n_lat = N_LAT // tm
    row = _mod_row(tm)
    return pl.pallas_call(
        functools.partial(_prenorm_kernel, n_lat=n_lat),
        grid=(N_TOK // tm,),
        in_specs=[
            pl.BlockSpec((tm, D_MODEL), lambda i: (jnp.minimum(i, n_lat - 1), 0)),
            pl.BlockSpec((tm, D_MODEL), lambda i: (jnp.maximum(i - n_lat, 0), 0)),
            pl.BlockSpec((1, D_MODEL), lambda i: (0, 0)),
            pl.BlockSpec((1, 1, D_MODEL), lambda i: (row(i), 0, 0)),
            pl.BlockSpec((1, 1, D_MODEL), lambda i: (row(i), 0, 1)),
        ],
        out_specs=pl.BlockSpec((tm, D_MODEL), lambda i: (i, 0)),
        out_shape=jax.ShapeDtypeStruct((N_TOK, D_MODEL), BF16),
        compiler_params=_cparams(("parallel",)),
        name="prenorm",
    )(x2d, ctx2d, g.reshape(1, D_MODEL), mods, mods)


def _in_kernel(x_ref, w_ref, o_ref):
    o_ref[...] = _dot_nt(x_ref[...], w_ref[0].astype(BF16)).astype(o_ref.dtype)


def _in_proj(xn, w_in_t, l, *, tm=3072):
    n_head = COL_NA // IN_TN

    def w_row(j):
        per = IN_TN // MLA_ROPE
        return MLA_ROPE * jnp.where(j < n_head, j * per, W_IN_NA // MLA_ROPE + (j - n_head) * per)

    return pl.pallas_call(
        _in_kernel,
        grid=(N_TOK // tm, P_COLS // IN_TN),
        in_specs=[
            pl.BlockSpec((tm, D_MODEL), lambda i, j: (i, 0)),
            pl.BlockSpec((pl.Element(1), pl.Element(IN_TN), pl.Element(D_MODEL)), lambda i, j: (l, w_row(j), 0)),
        ],
        out_specs=pl.BlockSpec((tm, IN_TN), lambda i, j: (i, j)),
        out_shape=jax.ShapeDtypeStruct((N_TOK, P_COLS), BF16),
        compiler_params=_cparams(("parallel", "arbitrary")),
        name="in_proj",
    )(xn, w_in_t)


FFN_CW = 256


def _swiglu_kernel(x_ref, wg_ref, wu_ref, o_ref):
    xn = x_ref[...]
    for c in range(o_ref.shape[1] // FFN_CW):
        cols = slice(c * FFN_CW, (c + 1) * FFN_CW)
        gate = _dot(xn, wg_ref[0, :, cols].astype(BF16))
        up = _dot(xn, wu_ref[0, :, cols].astype(BF16))
        o_ref[:, cols] = (gate * jax.nn.sigmoid(gate) * up).astype(o_ref.dtype)


def _ffn_up(xn, wg, wu, l, *, tn=512):
    m = xn.shape[0]
    tm = 2048 if m % 2048 == 0 else 1536
    w_spec = pl.BlockSpec((1, D_MODEL, tn), lambda i, j: (l, 0, j))
    return pl.pallas_call(
        _swiglu_kernel,
        grid=(m // tm, FFN_HIDDEN // tn),
        in_specs=[pl.BlockSpec((tm, D_MODEL), lambda i, j: (i, 0)), w_spec, w_spec],
        out_specs=pl.BlockSpec((tm, tn), lambda i, j: (i, j)),
        out_shape=jax.ShapeDtypeStruct((m, FFN_HIDDEN), BF16),
        compiler_params=_cparams(("parallel", "arbitrary")),
        name="ffn_up",
    )(xn, wg, wu)


POST_TM_OUT = 512
POST_TM_DOWN = 256
POST_SUB = 128
POST_WCH = 256
POST_WSLOTS = 4
POST_SLOTS = 3


def _post_kernel(*refs, widths, two_src, emit_xn, n_lat, l, tm):
    refs = list(refs)
    n_act = len(widths)
    acts_lat = [refs.pop(0) for _ in range(n_act)]
    acts_ctx = [refs.pop(0) for _ in range(n_act)] if two_src else None
    w_hbm, g_ref, gate_ref, res_lat = (refs.pop(0) for _ in range(4))
    res_ctx = refs.pop(0) if two_src else None
    if emit_xn:
        g2_ref, sh_ref, sc_ref = (refs.pop(0) for _ in range(3))
    o_hbm = refs.pop(0)
    xn_ref = refs.pop(0) if emit_xn else None
    w_scr, stage, buf, sem_w, sem_res, sem_out = refs

    i = pl.program_id(0)
    n_i = pl.num_programs(0)
    n_chunks = sum(widths) // POST_WCH
    is_lat = i < n_lat
    is_ctx = jnp.logical_not(is_lat)

    def w_copy(c, slot):
        return pltpu.make_async_copy(w_hbm.at[l, pl.ds(c * POST_WCH, POST_WCH)], stage.at[slot], sem_w.at[slot])

    def res_copy(src, row0, slot):
        return pltpu.make_async_copy(src.at[pl.ds(row0, tm)], buf.at[slot], sem_res.at[slot])

    def res_start(tile, slot):
        if two_src:
            pl.when(tile < n_lat)(lambda: res_copy(res_lat, tile * tm, slot).start())
            pl.when(tile >= n_lat)(lambda: res_copy(res_ctx, (tile - n_lat) * tm, slot).start())
        else:
            res_copy(res_lat, tile * tm, slot).start()

    def out_copy(tile, slot):
        return pltpu.make_async_copy(buf.at[slot], o_hbm.at[pl.ds(tile * tm, tm)], sem_out.at[slot])

    @pl.when(i == 0)
    def _():
        res_start(0, 0)
        for c in range(POST_WSLOTS - 1):
            w_copy(c, c).start()

        def body(c, carry):
            s = c % POST_WSLOTS
            w_copy(c, s).wait()
            ahead = c + POST_WSLOTS - 1

            @pl.when(ahead < n_chunks)
            def _():
                w_copy(ahead, ahead % POST_WSLOTS).start()

            w_scr[pl.ds(pl.multiple_of(c * POST_WCH, POST_WCH), POST_WCH), :] = stage[s].astype(BF16)
            return carry

        lax.fori_loop(0, n_chunks, body, 0)

    slot = i % POST_SLOTS
    nslot = (i + 1) % POST_SLOTS

    @pl.when(i + 1 < n_i)
    def _():
        @pl.when(i >= POST_SLOTS - 1)
        def _():
            out_copy(i + 1 - POST_SLOTS, nslot).wait()

        res_start(i + 1, nslot)

    res_copy(res_lat, 0, slot).wait()

    def compute(act_refs):
        g = g_ref[...]
        gate = gate_ref[0]
        if emit_xn:
            g2 = g2_ref[...]
            sc = 1.0 + sc_ref[0]
            sh = sh_ref[0]
        for r in range(tm // POST_SUB):
            rows = pl.ds(r * POST_SUB, POST_SUB)
            y = None
            k0 = 0
            for a_ref, kw in zip(act_refs, widths):
                part = _dot(a_ref[rows, :], w_scr[k0:k0 + kw, :])
                y = part if y is None else y + part
                k0 += kw
            x_new = buf[slot, rows, :] + gate * _rms(y, g)
            buf[slot, rows, :] = x_new
            if emit_xn:
                xn_ref[rows, :] = (_rms(x_new, g2) * sc + sh).astype(xn_ref.dtype)

    if two_src:
        pl.when(is_lat)(functools.partial(compute, acts_lat))
        pl.when(is_ctx)(functools.partial(compute, acts_ctx))
    else:
        compute(acts_lat)

    out_copy(i, slot).start()

    @pl.when(i == n_i - 1)
    def _():
        for back in range(POST_SLOTS):
            out_copy(i - back, (i - back) % POST_SLOTS).wait()


def _proj_post(acts, w, l, g, mods, gate_chunk, res, m_rows, nxt=None, *, tm):
    two_src = res[1] is not None
    n_lat = N_LAT // tm
    widths = tuple(a.shape[1] for a, _ in acts)
    k_total = sum(widths)
    emit_xn = nxt is not None
    row = _mod_row(tm)

    def lat_row(i):
        return jnp.minimum(i, n_lat - 1) if two_src else i

    def ctx_row(i):
        return jnp.maximum(i - n_lat, 0)

    def act_specs(rowf):
        return [pl.BlockSpec((tm, kw), lambda i: (rowf(i), 0)) for kw in widths]

    vec_spec = pl.BlockSpec((1, D_MODEL), lambda i: (0, 0))

    def mod_spec(chunk):
        return pl.BlockSpec((1, 1, D_MODEL), lambda i: (row(i), 0, chunk))

    hbm = pl.BlockSpec(memory_space=pl.ANY)
    in_specs = act_specs(lat_row)
    args = [a for a, _ in acts]
    if two_src:
        in_specs += act_specs(ctx_row)
        args += [c for _, c in acts]
    in_specs += [hbm, vec_spec, mod_spec(gate_chunk), hbm]
    args += [w, g.reshape(1, D_MODEL), mods, res[0]]
    if two_src:
        in_specs.append(hbm)
        args.append(res[1])
    out_specs = [hbm]
    out_shape = [jax.ShapeDtypeStruct((m_rows, D_MODEL), F32)]
    if emit_xn:
        g2, mods2, sh_chunk, sc_chunk = nxt
        in_specs += [vec_spec, mod_spec(sh_chunk), mod_spec(sc_chunk)]
        args += [g2.reshape(1, D_MODEL), mods2, mods2]
        out_specs.append(pl.BlockSpec((tm, D_MODEL), lambda i: (i, 0)))
        out_shape.append(jax.ShapeDtypeStruct((m_rows, D_MODEL), BF16))

    out = pl.pallas_call(
        functools.partial(_post_kernel, widths=widths, two_src=two_src, emit_xn=emit_xn, n_lat=n_lat, l=l, tm=tm),
        grid=(m_rows // tm,),
        in_specs=in_specs,
        out_specs=out_specs,
        out_shape=out_shape,
        scratch_shapes=[
            pltpu.VMEM((k_total, D_MODEL), BF16),
            pltpu.VMEM((POST_WSLOTS, POST_WCH, D_MODEL), F32),
            pltpu.VMEM((POST_SLOTS, tm, D_MODEL), F32),
            pltpu.SemaphoreType.DMA((POST_WSLOTS,)),
            pltpu.SemaphoreType.DMA((POST_SLOTS,)),
            pltpu.SemaphoreType.DMA((POST_SLOTS,)),
        ],
        compiler_params=_cparams(("arbitrary",)),
        name="proj_post",
    )(*args)
    return (out[0], out[1]) if emit_xn else (out[0], None)


def _rope128(r, cos_ref, sa_ref, sb_ref):
    return r * cos_ref[...] + pltpu.roll(r, 96, 1) * sa_ref[...] + pltpu.roll(r, 32, 1) * sb_ref[...]


def _q_kernel(x_ref, g_ref, w_ref, cos_ref, sa_ref, sb_ref, o_ref):
    xn = _rms(x_ref[...].astype(F32), g_ref[...] * (MLA_SCALE * LOG2E)).astype(BF16)
    for h in range(MLA_HEADS):
        acc = _dot(xn, w_ref[:, h * MLA_QK:(h + 1) * MLA_QK])
        o_ref[:, h * MLA_QK:h * MLA_QK + HEAD_DIM] = acc[:, :HEAD_DIM].astype(o_ref.dtype)
        o_ref[:, h * MLA_QK + HEAD_DIM:(h + 1) * MLA_QK] = _rope128(
            acc[:, HEAD_DIM:], cos_ref, sa_ref, sb_ref).astype(o_ref.dtype)


def _q_proj(p, m_rows, g, w, tabs, *, tm=1024):
    rope = _rope_row(tm)
    tab_spec = pl.BlockSpec((tm, HEAD_DIM), lambda i: (rope(i), 0))
    return pl.pallas_call(
        _q_kernel,
        grid=(m_rows // tm,),
        in_specs=[
            pl.BlockSpec((tm, Q_LORA), lambda i: (i, COL_CQ // Q_LORA)),
            pl.BlockSpec((1, Q_LORA), lambda i: (0, 0)),
            pl.BlockSpec((Q_LORA, MLA_HEADS * MLA_QK), lambda i: (0, 0)),
            tab_spec, tab_spec, tab_spec,
        ],
        out_specs=pl.BlockSpec((tm, MLA_HEADS * MLA_QK), lambda i: (i, 0)),
        out_shape=jax.ShapeDtypeStruct((m_rows, MLA_HEADS * MLA_QK), BF16),
        compiler_params=_cparams(("parallel",)),
        name="q_proj",
    )(p, g.reshape(1, Q_LORA), w, *tabs)


def _kv_kernel(xa_ref, xb_ref, kr_ref, g_ref, w_ref, cos_ref, sa_ref, sb_ref, k_ref, v_ref):
    half = KV_LORA // 2
    xa = xa_ref[...].astype(F32)
    xb = xb_ref[...].astype(F32)
    ms = (jnp.sum(xa * xa, axis=-1, keepdims=True) + jnp.sum(xb * xb, axis=-1, keepdims=True)) * (1.0 / KV_LORA)
    rs = lax.rsqrt(ms + RMS_EPS)
    g = g_ref[...]
    xna = (xa * rs * g[:, :half]).astype(BF16)
    xnb = (xb * rs * g[:, half:]).astype(BF16)
    krr = _rope128(kr_ref[...].astype(F32), cos_ref, sa_ref, sb_ref).astype(k_ref.dtype)
    ones = jnp.ones((xa.shape[0], MLA_VW - HEAD_DIM), v_ref.dtype)
    for h in range(MLA_HEADS):
        w = w_ref[0, :, h * 2 * HEAD_DIM:(h + 1) * 2 * HEAD_DIM].astype(BF16)
        acc = _dot(xna, w[:half]) + _dot(xnb, w[half:])
        k_ref[:, h * MLA_QK:h * MLA_QK + HEAD_DIM] = acc[:, :HEAD_DIM].astype(k_ref.dtype)
        k_ref[:, h * MLA_QK + HEAD_DIM:(h + 1) * MLA_QK] = krr
        v_ref[:, h * MLA_VW:h * MLA_VW + HEAD_DIM] = acc[:, HEAD_DIM:].astype(v_ref.dtype)
        v_ref[:, h * MLA_VW + HEAD_DIM:(h + 1) * MLA_VW] = ones


def _kv_proj(p, g, w, l, tabs, *, tm=1024):
    m = p.shape[0]
    rope = _rope_row(tm)
    half = KV_LORA // 2
    tab_spec = pl.BlockSpec((tm, HEAD_DIM), lambda i: (rope(i), 0))
    return pl.pallas_call(
        _kv_kernel,
        grid=(m // tm,),
        in_specs=[
            pl.BlockSpec((tm, half), lambda i: (i, COL_CKV // half)),
            pl.BlockSpec((tm, half), lambda i: (i, COL_CKV // half + 1)),
            pl.BlockSpec((tm, HEAD_DIM), lambda i: (i, COL_KR // HEAD_DIM)),
            pl.BlockSpec((1, KV_LORA), lambda i: (0, 0)),
            pl.BlockSpec((1, KV_LORA, MLA_HEADS * 2 * HEAD_DIM), lambda i: (l, 0, 0)),
            tab_spec, tab_spec, tab_spec,
        ],
        out_specs=[
            pl.BlockSpec((tm, MLA_HEADS * MLA_QK), lambda i: (i, 0)),
            pl.BlockSpec((tm, MLA_HEADS * MLA_VW), lambda i: (i, 0)),
        ],
        out_shape=[
            jax.ShapeDtypeStruct((m, MLA_HEADS * MLA_QK), BF16),
            jax.ShapeDtypeStruct((m, MLA_HEADS * MLA_VW), BF16),
        ],
        compiler_params=_cparams(("parallel",)),
        name="kv_proj",
    )(p, p, p, g.reshape(1, KV_LORA), w, *tabs)


def _softmax_pv(s_list, v_list, ones_col):
    m = jnp.max(s_list[0], axis=-1, keepdims=True)
    for s in s_list[1:]:
        m = jnp.maximum(m, jnp.max(s, axis=-1, keepdims=True))
    acc = None
    den = None
    for s, v in zip(s_list, v_list):
        p = jnp.exp2(s - m)
        if not ones_col:
            d = jnp.sum(p, axis=-1, keepdims=True)
            den = d if den is None else den + d
        o = _dot(p.astype(BF16), v)
        acc = o if acc is None else acc + o
    if ones_col:
        return acc[:, :HEAD_DIM] / acc[:, HEAD_DIM:]
    return acc / den


def _attn_kernel(*refs, scale, two, ones_col, chains, heads):
    if two:
        q_ref, k1_ref, v1_ref, k2_ref, v2_ref, o_ref = refs
    else:
        q_ref, k1_ref, v1_ref, o_ref = refs
    tq = q_ref.shape[0] // chains
    dqk = q_ref.shape[1] // heads
    dv = v1_ref.shape[1] // heads
    for h in range(heads):
        qk_cols = slice(h * dqk, (h + 1) * dqk)
        v_cols = slice(h * dv, (h + 1) * dv)
        for c in range(chains):
            rows = slice(c * tq, (c + 1) * tq)
            q = q_ref[rows, qk_cols]
            s_list = [_dot_nt(q, k1_ref[:, qk_cols])]
            v_list = [v1_ref[:, v_cols]]
            if two:
                s_list.append(_dot_nt(q, k2_ref[:, qk_cols]))
                v_list.append(v2_ref[:, v_cols])
            if scale != 1.0:
                s_list = [s * scale for s in s_list]
            o_ref[rows, h * HEAD_DIM:(h + 1) * HEAD_DIM] = _softmax_pv(s_list, v_list, ones_col).astype(o_ref.dtype)


def _mla_latent(q, k, v, *, hs=2, chains=4):
    return pl.pallas_call(
        functools.partial(_attn_kernel, scale=1.0, two=True, ones_col=True, chains=chains, heads=hs),
        grid=(BATCH, MLA_HEADS // hs),
        in_specs=[
            pl.BlockSpec((SEQ, hs * MLA_QK), lambda b, h: (b, h)),
            pl.BlockSpec((SEQ, hs * MLA_QK), lambda b, h: (b, h)),
            pl.BlockSpec((SEQ, hs * MLA_VW), lambda b, h: (b, h)),
            pl.BlockSpec((CTX_LEN, hs * MLA_QK), lambda b, h: (CTX_BLOCK0 + b, h)),
            pl.BlockSpec((CTX_LEN, hs * MLA_VW), lambda b, h: (CTX_BLOCK0 + b, h)),
        ],
        out_specs=pl.BlockSpec((SEQ, hs * HEAD_DIM), lambda b, h: (b, h)),
        out_shape=jax.ShapeDtypeStruct((N_LAT, MLA_HEADS * HEAD_DIM), BF16),
        compiler_params=_cparams(("parallel", "parallel")),
        name="mla_latent",
    )(q, k, v, k, v)


def _ctx_attend(q, k, v, q_col0, k_col0, v_col0, heads, dqk, dv, scale, ones_col, name):
    return pl.pallas_call(
        functools.partial(_attn_kernel, scale=scale, two=False, ones_col=ones_col, chains=1, heads=heads),
        grid=(BATCH,),
        in_specs=[
            pl.BlockSpec((CTX_LEN, heads * dqk), lambda b: (CTX_BLOCK0 + b, q_col0)),
            pl.BlockSpec((CTX_LEN, heads * dqk), lambda b: (CTX_BLOCK0 + b, k_col0)),
            pl.BlockSpec((CTX_LEN, heads * dv), lambda b: (CTX_BLOCK0 + b, v_col0)),
        ],
        out_specs=pl.BlockSpec((CTX_LEN, heads * HEAD_DIM), lambda b: (b, 0)),
        out_shape=jax.ShapeDtypeStruct((N_CTX, heads * HEAD_DIM), BF16),
        compiler_params=_cparams(("parallel",)),
        name=name,
    )(q, k, v)


def _na_plan():
    rows = SEQ // GRID_W
    invalid = 2 * NA_KH - 1
    pairs, plan, starts = [], [], []
    for t in range(rows // NA_TR):
        kw0 = int(np.clip(NA_TR * t - NA_KH // 2, 0, rows - NA_WR))
        starts.append(kw0)
        tile = []
        for ri in range(NA_TR):
            r = NA_TR * t + ri
            r0 = int(np.clip(r - NA_KH // 2, 0, rows - NA_KH))
            assert kw0 <= r0 and r0 + NA_KH <= kw0 + NA_WR
            row = []
            for kp in range(NA_WR // 2):
                pair = []
                for kr in (kw0 + 2 * kp, kw0 + 2 * kp + 1):
                    pair.append(kr - r + NA_KH - 1 if r0 <= kr < r0 + NA_KH else invalid)
                pair = tuple(pair)
                if pair not in pairs:
                    pairs.append(pair)
                row.append(pairs.index(pair))
            tile.append(row)
        plan.append(tile)
    return starts, plan, pairs


def _na_bias_pairs(rpb, pairs):
    c = np.arange(GRID_W)
    c0 = np.clip(c - NA_KW // 2, 0, GRID_W - NA_KW)
    col_ok = (c[None, :] >= c0[:, None]) & (c[None, :] < c0[:, None] + NA_KW)
    col_idx = np.clip(c[None, :] - c[:, None] + NA_KW - 1, 0, 2 * NA_KW - 2)
    onehot = (col_idx[None] == np.arange(2 * NA_KW - 1)[:, None, None]).astype(np.float32)
    t = jnp.einsum("lhdj,jck->lhdck", rpb.astype(F32), onehot, precision=lax.Precision.HIGHEST) * LOG2E
    t = jnp.where(col_ok, t, MASK_VALUE)
    masked = jnp.full(t.shape[:2] + (GRID_W, GRID_W), MASK_VALUE, F32)
    slabs = [t[:, :, d] for d in range(2 * NA_KH - 1)] + [masked]
    return jnp.stack([jnp.concatenate([slabs[a], slabs[b]], axis=-1) for a, b in pairs], axis=2)


def _na_kernel(q_ref, k_ref, v_ref, kc_ref, vc_ref, t2_ref, o_ref, vo_ref, vco_ref, *, starts, plan):
    ones = jnp.ones((SEQ, HEAD_DIM), BF16)
    vo_ref[:, :HEAD_DIM] = v_ref[...]
    vo_ref[:, HEAD_DIM:] = ones
    vco_ref[:, :HEAD_DIM] = vc_ref[...]
    vco_ref[:, HEAD_DIM:] = ones[:CTX_LEN]
    kc = kc_ref[...]
    vc = vco_ref[...]
    for t, (kw0, tile) in enumerate(zip(starts, plan)):
        q = q_ref[t * NA_TQ:(t + 1) * NA_TQ, :]
        kw = k_ref[kw0 * GRID_W:kw0 * GRID_W + NA_TK, :]
        vw = vo_ref[kw0 * GRID_W:kw0 * GRID_W + NA_TK, :]
        bias = jnp.concatenate(
            [jnp.concatenate([t2_ref[0, 0, idx] for idx in row], axis=1) for row in tile], axis=0)
        s = _dot_nt(q, kw) * (NA_SCALE * LOG2E) + bias
        sc = _dot_nt(q, kc) * (NA_SCALE * LOG2E)
        o_ref[t * NA_TQ:(t + 1) * NA_TQ, :] = _softmax_pv([s, sc], [vw, vc], True).astype(o_ref.dtype)


def _na_latent(p, t2, l, starts, plan):
    cq = COL_NA // HEAD_DIM
    ck = cq + NA_HEADS
    cv = ck + NA_HEADS
    n_pairs = t2.shape[2]
    return pl.pallas_call(
        functools.partial(_na_kernel, starts=starts, plan=plan),
        grid=(NA_HEADS, BATCH),
        in_specs=[
            pl.BlockSpec((SEQ, HEAD_DIM), lambda h, b: (b, cq + h)),
            pl.BlockSpec((SEQ, HEAD_DIM), lambda h, b: (b, ck + h)),
            pl.BlockSpec((SEQ, HEAD_DIM), lambda h, b: (b, cv + h)),
            pl.BlockSpec((CTX_LEN, HEAD_DIM), lambda h, b: (CTX_BLOCK0 + b, ck + h)),
            pl.BlockSpec((CTX_LEN, HEAD_DIM), lambda h, b: (CTX_BLOCK0 + b, cv + h)),
            pl.BlockSpec((1, 1, n_pairs, GRID_W, 2 * GRID_W), lambda h, b: (l, h, 0, 0, 0)),
        ],
        out_specs=pl.BlockSpec((SEQ, HEAD_DIM), lambda h, b: (b, h)),
        out_shape=jax.ShapeDtypeStruct((N_LAT, NA_HEADS * HEAD_DIM), BF16),
        scratch_shapes=[pltpu.VMEM((SEQ, 2 * HEAD_DIM), BF16), pltpu.VMEM((CTX_LEN, 2 * HEAD_DIM), BF16)],
        compiler_params=_cparams(("parallel", "parallel")),
        name="na_latent",
    )(p, p, p, p, p, t2)


def _conv_kernel(v_ref, x1_ref, x2_ref, wv_ref, w1_ref, w2_ref, bv_ref, b1_ref, b2_ref, zin_ref, x2o_ref):
    n = v_ref.shape[0]
    row = lax.broadcasted_iota(jnp.int32, (n, 1), 0)

    def short_conv(p_ref, w_ref, b_ref):
        p = p_ref[...].astype(F32)
        prev = jnp.where(row == 0, 0.0, pltpu.roll(p, 1, 0))
        nxt = jnp.where(row == n - 1, 0.0, pltpu.roll(p, n - 1, 0))
        w = w_ref[0]
        return prev * w[0:1] + p * w[1:2] + nxt * w[2:3] + b_ref[0]

    zin_ref[...] = (short_conv(x1_ref, w1_ref, b1_ref) * short_conv(v_ref, wv_ref, bv_ref)).astype(zin_ref.dtype)
    x2o_ref[...] = short_conv(x2_ref, w2_ref, b2_ref).astype(x2o_ref.dtype)


def _hy_conv(p, conv_w, conv_b, l, n, row_block0, *, tc=256):
    nc = HY_WIDTH // tc

    def seg(s):
        return (pl.BlockSpec((n, tc), lambda b, j: (row_block0 + b, s * nc + j)),
                pl.BlockSpec((1, 3, tc), lambda b, j: (l, 0, s * nc + j)),
                pl.BlockSpec((1, 1, tc), lambda b, j: (l, 0, s * nc + j)))

    (pv, wv, bv), (p1, w1, b1), (p2, w2, b2) = seg(0), seg(1), seg(2)
    out_spec = pl.BlockSpec((n, tc), lambda b, j: (0, b * nc + j))
    out = jax.ShapeDtypeStruct((n, BATCH * HY_WIDTH), BF16)
    conv_b = conv_b.reshape(DEPTH, 1, -1)
    return pl.pallas_call(
        _conv_kernel,
        grid=(BATCH, nc),
        in_specs=[pv, p1, p2, wv, w1, w2, bv, b1, b2],
        out_specs=[out_spec, out_spec],
        out_shape=[out, out],
        compiler_params=_cparams(("parallel", "parallel")),
        name="hy_conv",
    )(p, p, p, conv_w, conv_w, conv_w, conv_b, conv_b, conv_b)


def _filt_kernel(z_ref, t_ref, dl_ref, w1_ref, b1_ref, w2_ref, b2_ref, w3_ref, fr_ref, hs_ref, ha_ref, kn_ref):
    n = z_ref.shape[0]
    fr = fr_ref[...]
    h = jnp.sin(fr * (_dot_hi(z_ref[...], w1_ref[...]) + b1_ref[...]))
    h = jnp.sin(fr * (_dot_hi(h, w2_ref[...]) + b2_ref[...]))
    h = _dot_hi(h, w3_ref[...])
    decay = jnp.exp(-t_ref[...] * dl_ref[...])
    row = lax.broadcasted_iota(jnp.int32, (n, 1), 0)
    hf = h[:, :HY_WIDTH] * decay
    hb = jnp.where(row == 0, 0.0, h[:, HY_WIDTH:] * decay)
    hs = hf + hb
    hs_ref[...] = hs.astype(hs_ref.dtype)
    ha_ref[...] = (hf - hb).astype(ha_ref.dtype)
    sign = jnp.where((row & 1) == 0, 1.0, -1.0)
    kn_ref[...] = jnp.sum(hs * sign, axis=0, keepdims=True)


def _hy_filter_taps(n, f_w1, f_b1, f_w2, f_b2, f_w3, f_freq):
    pos = jnp.arange(n, dtype=F32)
    t = jnp.linspace(0.0, 1.0, n, dtype=F32)
    bands = jnp.linspace(1e-4, HY_POS_BANDS - 1, HY_POS_BANDS, dtype=F32)
    ang = (2.0 * math.pi / n) * pos[:, None] * bands[None, :]
    z = jnp.concatenate([t[:, None], jnp.cos(ang), -jnp.sin(ang)], axis=-1)
    pad = HY_FILTER_HIDDEN - z.shape[1]
    z = jnp.pad(z, ((0, 0), (0, pad)))
    w1 = jnp.pad(f_w1.astype(F32), ((0, pad), (0, 0)))
    deltas = jnp.abs(jnp.linspace(math.log(HY_DECAY_TARGET) / HY_FAST_DECAY,
                                  math.log(HY_DECAY_TARGET) / HY_SLOW_DECAY, HY_WIDTH, dtype=F32))
    hid = HY_FILTER_HIDDEN
    return pl.pallas_call(
        _filt_kernel,
        out_shape=[
            jax.ShapeDtypeStruct((n, HY_WIDTH), BF16),
            jax.ShapeDtypeStruct((n, HY_WIDTH), BF16),
            jax.ShapeDtypeStruct((1, HY_WIDTH), F32),
        ],
        compiler_params=pltpu.CompilerParams(vmem_limit_bytes=VMEM_LIMIT),
        name="hy_filter",
    )(z, t[:, None], deltas[None, :], w1, f_b1.reshape(1, hid), f_w2, f_b2.reshape(1, hid), f_w3,
      f_freq.reshape(1, hid))


def _dft_tables(n):
    lo = 16 if n < 1024 else 32
    hi = n // lo
    t = jnp.arange(n, dtype=jnp.int32)[None, :]
    big = 2 * n

    def ang(ff):
        return ((ff * t) % big).astype(F32) * (2.0 * math.pi / big)

    a = ang(lo * jnp.arange(hi, dtype=jnp.int32)[:, None])
    b = ang(jnp.arange(lo, dtype=jnp.int32)[:, None])
    ca, sa, cb, sb = jnp.cos(a), jnp.sin(a), jnp.cos(b), jnp.sin(b)
    cos = (ca[:, None, :] * cb[None, :, :] - sa[:, None, :] * sb[None, :, :]).reshape(n, n)
    base = -(sa[:, None, :] * cb[None, :, :] + ca[:, None, :] * sb[None, :, :]).reshape(n, n)
    idx = jnp.arange(n)
    alt = jnp.where(idx % 2 == 0, 1.0, -1.0).astype(F32)
    msin = jnp.where(idx[:, None] == 0, alt[None, :], base)
    msin_t = jnp.where(idx[None, :] == 0, alt[:, None], base)
    return cos.astype(BF16), msin.astype(BF16), msin_t.astype(BF16)


def _dft_filt_kernel(c_ref, s_ref, hs_ref, ha_ref, kr_ref, ki_ref):
    kr_ref[...] = _dot(c_ref[...], hs_ref[...])
    ki_ref[...] = _dot(s_ref[...], ha_ref[...])


def _hy_filter_spectrum(cos, msin, hs, ha, *, tf):
    n = cos.shape[0]
    tf = min(tf, n)
    tab = pl.BlockSpec((tf, n), lambda i: (i, 0))
    taps = pl.BlockSpec((n, HY_WIDTH), lambda i: (0, 0))
    out = pl.BlockSpec((tf, HY_WIDTH), lambda i: (i, 0))
    return pl.pallas_call(
        _dft_filt_kernel,
        grid=(n // tf,),
        in_specs=[tab, tab, taps, taps],
        out_specs=[out, out],
        out_shape=[jax.ShapeDtypeStruct((n, HY_WIDTH), F32)] * 2,
        compiler_params=_cparams(("parallel",)),
        name="hy_filter_dft",
    )(cos, msin, hs, ha)


def _batch_cols(b):
    return slice(b * HY_WIDTH, (b + 1) * HY_WIDTH)


def _dft_fwd_kernel(c_ref, s_ref, x_ref, kr_ref, ki_ref, kn_ref, yr_ref, yi_ref, *, tf, inv_n):
    x = x_ref[...]
    zr = _dot(c_ref[...], x)
    zi = _dot(s_ref[...], x)
    row = pl.program_id(0) * tf + lax.broadcasted_iota(jnp.int32, (tf, 1), 0)
    bin0 = row == 0
    wt = jnp.where(bin0, inv_n, 2.0 * inv_n)
    kr = kr_ref[...] * wt
    ki = jnp.where(bin0, 0.0, ki_ref[...] * wt)
    kr_im = jnp.where(bin0, kn_ref[...] * wt, kr)
    for b in range(BATCH):
        cols = _batch_cols(b)
        yr_ref[:, cols] = (zr[:, cols] * kr - zi[:, cols] * ki).astype(yr_ref.dtype)
        yi_ref[:, cols] = (zr[:, cols] * ki + zi[:, cols] * kr_im).astype(yi_ref.dtype)


def _hy_dft_fwd(cos, msin, zin, kr, ki, kn, *, tf):
    n = cos.shape[0]
    tf = min(tf, n)
    tab = pl.BlockSpec((tf, n), lambda i: (i, 0))
    filt = pl.BlockSpec((tf, HY_WIDTH), lambda i: (i, 0))
    out = pl.BlockSpec((tf, BATCH * HY_WIDTH), lambda i: (i, 0))
    whole = pl.BlockSpec((n, BATCH * HY_WIDTH), lambda i: (0, 0), pipeline_mode=pl.Buffered(1))
    return pl.pallas_call(
        functools.partial(_dft_fwd_kernel, tf=tf, inv_n=1.0 / (2 * n)),
        grid=(n // tf,),
        in_specs=[tab, tab, whole, filt, filt, pl.BlockSpec((1, HY_WIDTH), lambda i: (0, 0))],
        out_specs=[out, out],
        out_shape=[jax.ShapeDtypeStruct((n, BATCH * HY_WIDTH), BF16)] * 2,
        compiler_params=_cparams(("parallel",)),
        name="hy_dft_fwd",
    )(cos, msin, zin, kr, ki, kn)


def _dft_inv_kernel(c_ref, st_ref, yr_ref, yi_ref, zin_ref, x2_ref, b_ref, o_ref):
    y = _dot(c_ref[...], yr_ref[...]) + _dot(st_ref[...], yi_ref[...])
    bias = b_ref[0]
    for b in range(BATCH):
        cols = _batch_cols(b)
        yb = y[:, cols] + zin_ref[:, cols].astype(F32) * bias
        o_ref[b] = (x2_ref[:, cols].astype(F32) * yb).astype(o_ref.dtype)


def _hy_dft_inv(cos, msin_t, yr, yi, zin, x2, bias, l, *, tt):
    n = cos.shape[0]
    tt = min(tt, n)
    tab = pl.BlockSpec((tt, n), lambda i: (i, 0))
    whole = pl.BlockSpec((n, BATCH * HY_WIDTH), lambda i: (0, 0), pipeline_mode=pl.Buffered(1))
    rows = pl.BlockSpec((tt, BATCH * HY_WIDTH), lambda i: (i, 0))
    out = pl.pallas_call(
        _dft_inv_kernel,
        grid=(n // tt,),
        in_specs=[tab, tab, whole, whole, rows, rows, pl.BlockSpec((1, 1, HY_WIDTH), lambda i: (l, 0, 0))],
        out_specs=pl.BlockSpec((BATCH, tt, HY_WIDTH), lambda i: (0, i, 0)),
        out_shape=jax.ShapeDtypeStruct((BATCH, n, HY_WIDTH), BF16),
        compiler_params=_cparams(("parallel",)),
        name="hy_dft_inv",
    )(cos, msin_t, yr, yi, zin, x2, bias.reshape(DEPTH, 1, HY_WIDTH))
    return out.reshape(BATCH * n, HY_WIDTH)


def _hyena(p, l, n, row_block0, tables, conv_w, conv_b, f_w1, f_b1, f_w2, f_b2, f_w3, f_freq, bias):
    cos, msin, msin_t = tables
    zin, x2 = _hy_conv(p, conv_w, conv_b, l, n, row_block0)
    hs, ha, kn = _hy_filter_taps(n, f_w1[l], f_b1[l], f_w2[l], f_b2[l], f_w3[l], f_freq[l])
    kr, ki = _hy_filter_spectrum(cos, msin, hs, ha, tf=512)
    yr, yi = _hy_dft_fwd(cos, msin, zin, kr, ki, kn, tf=512)
    return _hy_dft_inv(cos, msin_t, yr, yi, zin, x2, bias, l, tt=512)


def _rope_tables():
    tok = jnp.arange(SEQ)
    row = (tok // GRID_W).astype(F32)
    col = (tok % GRID_W).astype(F32)
    n_freq = MLA_ROPE // 4
    inv = ROPE_THETA ** (-jnp.arange(n_freq, dtype=F32) / n_freq)
    ang = jnp.concatenate([row[:, None] * inv, col[:, None] * inv], axis=-1)
    cos, sin = jnp.cos(ang), jnp.sin(ang)
    half = MLA_ROPE // 2
    zeros = jnp.zeros((SEQ, half), F32)
    rest = HEAD_DIM - MLA_ROPE
    cos_t = jnp.concatenate([cos, cos, jnp.ones((SEQ, rest), F32)], axis=-1)
    sin_a = jnp.concatenate([-sin, zeros, jnp.zeros((SEQ, rest), F32)], axis=-1)
    sin_b = jnp.concatenate([zeros, sin, jnp.zeros((SEQ, rest), F32)], axis=-1)
    ident = jnp.ones((N_CTX, HEAD_DIM), F32)
    none = jnp.zeros((N_CTX, HEAD_DIM), F32)
    return (jnp.concatenate([cos_t, ident]), jnp.concatenate([sin_a, none]), jnp.concatenate([sin_b, none]))


def _layout_w_uq(w):
    w = w.reshape(Q_LORA, MLA_HEADS, HEAD_DIM + MLA_ROPE)
    w = jnp.pad(w, ((0, 0), (0, 0), (0, MLA_QK - HEAD_DIM - MLA_ROPE)))
    return w.reshape(Q_LORA, MLA_HEADS * MLA_QK).astype(BF16)


def kernel(x, c, ctx, c_ctx, w_ada, b_ada, g_attn_pre, g_attn_post, g_ffn_pre, g_ffn_post, w_in, hy_conv_w, hy_conv_b, hy_f_w1, hy_f_b1, hy_f_w2, hy_f_b2, hy_f_w3, hy_f_freq, hy_bias, mla_g_q, mla_w_uq, mla_g_kv, mla_w_ukv, na_rpb, w_out, w_ffn_gate, w_ffn_up, w_ffn_down):
    cc = jnp.concatenate([c, c_ctx[None, :], jnp.zeros((8 - BATCH - 1, D_MODEL), F32)], axis=0)
    mods_all = _ada(cc, w_ada, b_ada)
    mods = [mods_all[l].reshape(8, 1, 6 * D_MODEL) for l in range(DEPTH)]

    rope_tabs = _rope_tables()
    dft_lat = _dft_tables(SEQ)
    dft_ctx = _dft_tables(CTX_LEN)
    na_starts, na_plan, na_pairs = _na_plan()
    na_t2 = _na_bias_pairs(na_rpb, na_pairs)
    w_in_t = jnp.swapaxes(w_in, 1, 2)
    hy_w = (hy_conv_w, hy_conv_b, hy_f_w1, hy_f_b1, hy_f_w2, hy_f_b2, hy_f_w3, hy_f_freq, hy_bias)

    res = (x.reshape(N_LAT, D_MODEL), ctx.reshape(N_CTX, D_MODEL))
    xn = _prenorm(res[0], res[1], g_attn_pre[0], mods[0])

    for l in range(DEPTH):
        ctx_out = l < DEPTH - 1
        m_rows = N_TOK if ctx_out else N_LAT
        p = _in_proj(xn, w_in_t, l)

        q = _q_proj(p, m_rows, mla_g_q[l], _layout_w_uq(mla_w_uq[l]), rope_tabs)
        k, v = _kv_proj(p, mla_g_kv[l], mla_w_ukv, l, rope_tabs)
        mla = [_mla_latent(q, k, v), None]
        na = [_na_latent(p, na_t2, l, na_starts, na_plan), None]
        hy = [_hyena(p, l, SEQ, 0, dft_lat, *hy_w), None]

        if ctx_out:
            mla[1] = _ctx_attend(q, k, v, 0, 0, 0, MLA_HEADS, MLA_QK, MLA_VW, 1.0, True, "mla_ctx")
            cq = COL_NA // (NA_HEADS * HEAD_DIM)
            na[1] = _ctx_attend(p, p, p, cq, cq + 1, cq + 2, NA_HEADS, HEAD_DIM, HEAD_DIM, NA_SCALE * LOG2E, False,
                                "na_ctx")
            hy[1] = _hyena(p, l, CTX_LEN, CTX_BLOCK0, dft_ctx, *hy_w)
            res_l = res
        else:
            res_l = (res[0], None)

        stream, xn = _proj_post([tuple(hy), tuple(mla), tuple(na)], w_out, l, g_attn_post[l], mods[l], 2, res_l,
                                m_rows, nxt=(g_ffn_pre[l], mods[l], 3, 4), tm=POST_TM_OUT)
        h = _ffn_up(xn, w_ffn_gate, w_ffn_up, l)
        nxt = (g_attn_pre[l + 1], mods[l + 1], 0, 1) if ctx_out else None
        stream, xn = _proj_post([(h, None)], w_ffn_down, l, g_ffn_post[l], mods[l], 5, (stream, None), m_rows,
                                nxt=nxt, tm=POST_TM_DOWN)
        res = (stream, None)

    return stream.reshape(BATCH, SEQ, D_MODEL)
```

```python
import functools
import math

import jax
import jax.numpy as jnp
import numpy as np
from jax import lax
from jax.experimental import pallas as pl
from jax.experimental.pallas import tpu as pltpu

F32 = jnp.float32
BF16 = jnp.bfloat16

D_MODEL = 2048
BATCH = 4
SEQ = 2048
DEPTH = 2
GRID_W = 64
CTX_LEN = 256
HEAD_DIM = 128
HY_WIDTH = D_MODEL // 4
HY_FILTER_HIDDEN = 64
HY_POS_BANDS = 16
HY_DECAY_TARGET = 1e-2
HY_FAST_DECAY = 0.3
HY_SLOW_DECAY = 1.5
MLA_HEADS = (D_MODEL // 2) // HEAD_DIM
MLA_ROPE = 64
Q_LORA = 3 * D_MODEL // 8
KV_LORA = D_MODEL // 4
MLA_SCALE = (HEAD_DIM + MLA_ROPE) ** -0.5
NA_HEADS = (D_MODEL // 4) // HEAD_DIM
NA_KH = 8
NA_KW = 16
NA_SCALE = HEAD_DIM ** -0.5
FFN_HIDDEN = ((8 * D_MODEL + 3 * 256 - 1) // (3 * 256)) * 256
ROPE_THETA = 10000.0
RMS_EPS = 1e-6
MASK_VALUE = -1e30
LOG2E = math.log2(math.e)

N_LAT = BATCH * SEQ
N_CTX = BATCH * CTX_LEN
N_TOK = N_LAT + N_CTX
CTX_BLOCK0 = N_LAT // CTX_LEN

IN_TN = 512
COL_HY = 0
COL_CQ = 3 * HY_WIDTH
COL_CKV = COL_CQ + Q_LORA
COL_KR = COL_CKV + KV_LORA
W_IN_NA = COL_KR + MLA_ROPE
COL_NA = -(-W_IN_NA // IN_TN) * IN_TN
P_COLS = COL_NA + 3 * NA_HEADS * HEAD_DIM
MLA_QK = 256
MLA_VW = 256

NA_TR = 4
NA_WR = 12
NA_TQ = NA_TR * GRID_W
NA_TK = NA_WR * GRID_W

VMEM_LIMIT = 52 * 1024 * 1024


def _cparams(sem):
    return pltpu.CompilerParams(dimension_semantics=sem, vmem_limit_bytes=VMEM_LIMIT)


def _dot(a, b):
    return jnp.dot(a, b, preferred_element_type=F32)


def _dot_nt(a, b):
    return lax.dot_general(a, b, (((1,), (1,)), ((), ())), preferred_element_type=F32)


def _dot_hi(a, b):
    return jnp.dot(a, b, preferred_element_type=F32, precision=lax.Precision.HIGHEST)


def _rms(x, g):
    ms = jnp.mean(x * x, axis=-1, keepdims=True)
    return x * lax.rsqrt(ms + RMS_EPS) * g


def _mod_row(tm):
    n_lat, per_b = N_LAT // tm, SEQ // tm
    return lambda i: jnp.where(i < n_lat, i // per_b, BATCH)


def _rope_row(tm):
    n_lat, per_b = N_LAT // tm, SEQ // tm
    return lambda i: jnp.where(i < n_lat, i % per_b, per_b + i - n_lat)


def _ada_kernel(c_ref, w_ref, b_ref, o_ref):
    a = c_ref[...]
    a = a * jax.nn.sigmoid(a)
    o_ref[0] = _dot(a.astype(BF16), w_ref[0].astype(BF16)) + b_ref[0]


def _ada(cc, w_ada, b_ada):
    tn = 1024
    n = w_ada.shape[-1]
    return pl.pallas_call(
        _ada_kernel,
        grid=(DEPTH, n // tn),
        in_specs=[
            pl.BlockSpec((8, D_MODEL), lambda l, j: (0, 0)),
            pl.BlockSpec((1, D_MODEL, tn), lambda l, j: (l, 0, j)),
            pl.BlockSpec((1, 1, tn), lambda l, j: (l, 0, j)),
        ],
        out_specs=pl.BlockSpec((1, 8, tn), lambda l, j: (l, 0, j)),
        out_shape=jax.ShapeDtypeStruct((DEPTH, 8, n), F32),
        compiler_params=_cparams(("parallel", "parallel")),
        name="ada",
    )(cc, w_ada, b_ada.reshape(DEPTH, 1, n))


def _norm_mod_to(xn_ref, x_ref, g_ref, sh_ref, sc_ref, chunk=256):
    g = g_ref[...]
    sc = 1.0 + sc_ref[0]
    sh = sh_ref[0]

    def body(r, carry):
        rows = pl.ds(pl.multiple_of(r * chunk, chunk), chunk)
        xn_ref[rows, :] = (_rms(x_ref[rows, :], g) * sc + sh).astype(BF16)
        return carry

    lax.fori_loop(0, x_ref.shape[0] // chunk, body, 0)


def _prenorm_kernel(x_ref, c_ref, g_ref, sh_ref, sc_ref, o_ref, *, n_lat):
    i = pl.program_id(0)

    @pl.when(i < n_lat)
    def _():
        _norm_mod_to(o_ref, x_ref, g_ref, sh_ref, sc_ref)

    @pl.when(i >= n_lat)
    def _():
        _norm_mod_to(o_ref, c_ref, g_ref, sh_ref, sc_ref)


def _prenorm(x2d, ctx2d, g, mods, *, tm=1024):
    n_lat = N_LAT // tm
    row = _mod_row(tm)
    return pl.pallas_call(
        functools.partial(_prenorm_kernel, n_lat=n_lat),
        grid=(N_TOK // tm,),
        in_specs=[
            pl.BlockSpec((tm, D_MODEL), lambda i: (jnp.minimum(i, n_lat - 1), 0)),
            pl.BlockSpec((tm, D_MODEL), lambda i: (jnp.maximum(i - n_lat, 0), 0)),
            pl.BlockSpec((1, D_MODEL), lambda i: (0, 0)),
            pl.BlockSpec((1, 1, D_MODEL), lambda i: (row(i), 0, 0)),
            pl.BlockSpec((1, 1, D_MODEL), lambda i: (row(i), 0, 1)),
        ],
        out_specs=pl.BlockSpec((tm, D_MODEL), lambda i: (i, 0)),
        out_shape=jax.ShapeDtypeStruct((N_TOK, D_MODEL), BF16),
        compiler_params=_cparams(("parallel",)),
        name="prenorm",
    )(x2d, ctx2d, g.reshape(1, D_MODEL), mods, mods)


def _in_kernel(x_ref, w_ref, o_ref):
    o_ref[...] = _dot_nt(x_ref[...], w_ref[0].astype(BF16)).astype(o_ref.dtype)


def _in_proj(xn, w_in_t, l, *, tm=3072):
    n_head = COL_NA // IN_TN

    def w_row(j):
        per = IN_TN // MLA_ROPE
        return MLA_ROPE * jnp.where(j < n_head, j * per, W_IN_NA // MLA_ROPE + (j - n_head) * per)

    return pl.pallas_call(
        _in_kernel,
        grid=(N_TOK // tm, P_COLS // IN_TN),
        in_specs=[
            pl.BlockSpec((tm, D_MODEL), lambda i, j: (i, 0)),
            pl.BlockSpec((pl.Element(1), pl.Element(IN_TN), pl.Element(D_MODEL)), lambda i, j: (l, w_row(j), 0)),
        ],
        out_specs=pl.BlockSpec((tm, IN_TN), lambda i, j: (i, j)),
        out_shape=jax.ShapeDtypeStruct((N_TOK, P_COLS), BF16),
        compiler_params=_cparams(("parallel", "arbitrary")),
        name="in_proj",
    )(xn, w_in_t)


FFN_CW = 256


def _swiglu_kernel(x_ref, wg_ref, wu_ref, o_ref):
    xn = x_ref[...]
    for c in range(o_ref.shape[1] // FFN_CW):
        cols = slice(c * FFN_CW, (c + 1) * FFN_CW)
        gate = _dot(xn, wg_ref[0, :, cols].astype(BF16))
        up = _dot(xn, wu_ref[0, :, cols].astype(BF16))
        o_ref[:, cols] = (gate * jax.nn.sigmoid(gate) * up).astype(o_ref.dtype)


def _ffn_up(xn, wg, wu, l, *, tn=512):
    m = xn.shape[0]
    tm = 2048 if m % 2048 == 0 else 1536
    w_spec = pl.BlockSpec((1, D_MODEL, tn), lambda i, j: (l, 0, j))
    return pl.pallas_call(
        _swiglu_kernel,
        grid=(m // tm, FFN_HIDDEN // tn),
        in_specs=[pl.BlockSpec((tm, D_MODEL), lambda i, j: (i, 0)), w_spec, w_spec],
        out_specs=pl.BlockSpec((tm, tn), lambda i, j: (i, j)),
        out_shape=jax.ShapeDtypeStruct((m, FFN_HIDDEN), BF16),
        compiler_params=_cparams(("parallel", "arbitrary")),
        name="ffn_up",
    )(xn, wg, wu)


POST_TM_OUT = 512
POST_TM_DOWN = 256
POST_SUB = 128
POST_WCH = 256
POST_WSLOTS = 4
POST_SLOTS = 3


def _post_kernel(*refs, widths, two_src, emit_xn, n_lat, l, tm):
    refs = list(refs)
    n_act = len(widths)
    acts_lat = [refs.pop(0) for _ in range(n_act)]
    acts_ctx = [refs.pop(0) for _ in range(n_act)] if two_src else None
    w_hbm, g_ref, gate_ref, res_lat = (refs.pop(0) for _ in range(4))
    res_ctx = refs.pop(0) if two_src else None
    if emit_xn:
        g2_ref, sh_ref, sc_ref = (refs.pop(0) for _ in range(3))
    o_hbm = refs.pop(0)
    xn_ref = refs.pop(0) if emit_xn else None
    w_scr, stage, buf, sem_w, sem_res, sem_out = refs

    i = pl.program_id(0)
    n_i = pl.num_programs(0)
    n_chunks = sum(widths) // POST_WCH
    is_lat = i < n_lat
    is_ctx = jnp.logical_not(is_lat)

    def w_copy(c, slot):
        return pltpu.make_async_copy(w_hbm.at[l, pl.ds(c * POST_WCH, POST_WCH)], stage.at[slot], sem_w.at[slot])

    def res_copy(src, row0, slot):
        return pltpu.make_async_copy(src.at[pl.ds(row0, tm)], buf.at[slot], sem_res.at[slot])

    def res_start(tile, slot):
        if two_src:
            pl.when(tile < n_lat)(lambda: res_copy(res_lat, tile * tm, slot).start())
            pl.when(tile >= n_lat)(lambda: res_copy(res_ctx, (tile - n_lat) * tm, slot).start())
        else:
            res_copy(res_lat, tile * tm, slot).start()

    def out_copy(tile, slot):
        return pltpu.make_async_copy(buf.at[slot], o_hbm.at[pl.ds(tile * tm, tm)], sem_out.at[slot])

    @pl.when(i == 0)
    def _():
        res_start(0, 0)
        for c in range(POST_WSLOTS - 1):
            w_copy(c, c).start()

        def body(c, carry):
            s = c % POST_WSLOTS
            w_copy(c, s).wait()
            ahead = c + POST_WSLOTS - 1

            @pl.when(ahead < n_chunks)
            def _():
                w_copy(ahead, ahead % POST_WSLOTS).start()

            w_scr[pl.ds(pl.multiple_of(c * POST_WCH, POST_WCH), POST_WCH), :] = stage[s].astype(BF16)
            return carry

        lax.fori_loop(0, n_chunks, body, 0)

    slot = i % POST_SLOTS
    nslot = (i + 1) % POST_SLOTS

    @pl.when(i + 1 < n_i)
    def _():
        @pl.when(i >= POST_SLOTS - 1)
        def _():
            out_copy(i + 1 - POST_SLOTS, nslot).wait()

        res_start(i + 1, nslot)

    res_copy(res_lat, 0, slot).wait()

    def compute(act_refs):
        g = g_ref[...]
        gate = gate_ref[0]
        if emit_xn:
            g2 = g2_ref[...]
            sc = 1.0 + sc_ref[0]
            sh = sh_ref[0]
        for r in range(tm // POST_SUB):
            rows = pl.ds(r * POST_SUB, POST_SUB)
            y = None
            k0 = 0
            for a_ref, kw in zip(act_refs, widths):
                part = _dot(a_ref[rows, :], w_scr[k0:k0 + kw, :])
                y = part if y is None else y + part
                k0 += kw
            x_new = buf[slot, rows, :] + gate * _rms(y, g)
            buf[slot, rows, :] = x_new
            if emit_xn:
                xn_ref[rows, :] = (_rms(x_new, g2) * sc + sh).astype(xn_ref.dtype)

    if two_src:
        pl.when(is_lat)(functools.partial(compute, acts_lat))
        pl.when(is_ctx)(functools.partial(compute, acts_ctx))
    else:
        compute(acts_lat)

    out_copy(i, slot).start()

    @pl.when(i == n_i - 1)
    def _():
        for back in range(POST_SLOTS):
            out_copy(i - back, (i - back) % POST_SLOTS).wait()


def _proj_post(acts, w, l, g, mods, gate_chunk, res, m_rows, nxt=None, *, tm):
    two_src = res[1] is not None
    n_lat = N_LAT // tm
    widths = tuple(a.shape[1] for a, _ in acts)
    k_total = sum(widths)
    emit_xn = nxt is not None
    row = _mod_row(tm)

    def lat_row(i):
        return jnp.minimum(i, n_lat - 1) if two_src else i

    def ctx_row(i):
        return jnp.maximum(i - n_lat, 0)

    def act_specs(rowf):
        return [pl.BlockSpec((tm, kw), lambda i: (rowf(i), 0)) for kw in widths]

    vec_spec = pl.BlockSpec((1, D_MODEL), lambda i: (0, 0))

    def mod_spec(chunk):
        return pl.BlockSpec((1, 1, D_MODEL), lambda i: (row(i), 0, chunk))

    hbm = pl.BlockSpec(memory_space=pl.ANY)
    in_specs = act_specs(lat_row)
    args = [a for a, _ in acts]
    if two_src:
        in_specs += act_specs(ctx_row)
        args += [c for _, c in acts]
    in_specs += [hbm, vec_spec, mod_spec(gate_chunk), hbm]
    args += [w, g.reshape(1, D_MODEL), mods, res[0]]
    if two_src:
        in_specs.append(hbm)
        args.append(res[1])
    out_specs = [hbm]
    out_shape = [jax.ShapeDtypeStruct((m_rows, D_MODEL), F32)]
    if emit_xn:
        g2, mods2, sh_chunk, sc_chunk = nxt
        in_specs += [vec_spec, mod_spec(sh_chunk), mod_spec(sc_chunk)]
        args += [g2.reshape(1, D_MODEL), mods2, mods2]
        out_specs.append(pl.BlockSpec((tm, D_MODEL), lambda i: (i, 0)))
        out_shape.append(jax.ShapeDtypeStruct((m_rows, D_MODEL), BF16))

    out = pl.pallas_call(
        functools.partial(_post_kernel, widths=widths, two_src=two_src, emit_xn=emit_xn, n_lat=n_lat, l=l, tm=tm),
        grid=(m_rows // tm,),
        in_specs=in_specs,
        out_specs=out_specs,
        out_shape=out_shape,
        scratch_shapes=[
            pltpu.VMEM((k_total, D_MODEL), BF16),
            pltpu.VMEM((POST_WSLOTS, POST_WCH, D_MODEL), F32),
            pltpu.VMEM((POST_SLOTS, tm, D_MODEL), F32),
            pltpu.SemaphoreType.DMA((POST_WSLOTS,)),
            pltpu.SemaphoreType.DMA((POST_SLOTS,)),
            pltpu.SemaphoreType.DMA((POST_SLOTS,)),
        ],
        compiler_params=_cparams(("arbitrary",)),
        name="proj_post",
    )(*args)
    return (out[0], out[1]) if emit_xn else (out[0], None)


def _rope128(r, cos_ref, sa_ref, sb_ref):
    return r * cos_ref[...] + pltpu.roll(r, 96, 1) * sa_ref[...] + pltpu.roll(r, 32, 1) * sb_ref[...]


def _q_kernel(x_ref, g_ref, w_ref, cos_ref, sa_ref, sb_ref, o_ref):
    xn = _rms(x_ref[...].astype(F32), g_ref[...] * (MLA_SCALE * LOG2E)).astype(BF16)
    for h in range(MLA_HEADS):
        acc = _dot(xn, w_ref[:, h * MLA_QK:(h + 1) * MLA_QK])
        o_ref[:, h * MLA_QK:h * MLA_QK + HEAD_DIM] = acc[:, :HEAD_DIM].astype(o_ref.dtype)
        o_ref[:, h * MLA_QK + HEAD_DIM:(h + 1) * MLA_QK] = _rope128(
            acc[:, HEAD_DIM:], cos_ref, sa_ref, sb_ref).astype(o_ref.dtype)


def _q_proj(p, m_rows, g, w, tabs, *, tm=1024):
    rope = _rope_row(tm)
    tab_spec = pl.BlockSpec((tm, HEAD_DIM), lambda i: (rope(i), 0))
    return pl.pallas_call(
        _q_kernel,
        grid=(m_rows // tm,),
        in_specs=[
            pl.BlockSpec((tm, Q_LORA), lambda i: (i, COL_CQ // Q_LORA)),
            pl.BlockSpec((1, Q_LORA), lambda i: (0, 0)),
            pl.BlockSpec((Q_LORA, MLA_HEADS * MLA_QK), lambda i: (0, 0)),
            tab_spec, tab_spec, tab_spec,
        ],
        out_specs=pl.BlockSpec((tm, MLA_HEADS * MLA_QK), lambda i: (i, 0)),
        out_shape=jax.ShapeDtypeStruct((m_rows, MLA_HEADS * MLA_QK), BF16),
        compiler_params=_cparams(("parallel",)),
        name="q_proj",
    )(p, g.reshape(1, Q_LORA), w, *tabs)


def _kv_kernel(xa_ref, xb_ref, kr_ref, g_ref, w_ref, cos_ref, sa_ref, sb_ref, k_ref, v_ref, krr_ref):
    half = KV_LORA // 2
    xa = xa_ref[...].astype(F32)
    xb = xb_ref[...].astype(F32)
    ms = (jnp.sum(xa * xa, axis=-1, keepdims=True) + jnp.sum(xb * xb, axis=-1, keepdims=True)) * (1.0 / KV_LORA)
    rs = lax.rsqrt(ms + RMS_EPS)
    g = g_ref[...]
    xna = (xa * rs * g[:, :half]).astype(BF16)
    xnb = (xb * rs * g[:, half:]).astype(BF16)
    krr_ref[...] = _rope128(kr_ref[...].astype(F32), cos_ref, sa_ref, sb_ref).astype(krr_ref.dtype)
    for h in range(MLA_HEADS):
        cols = slice(h * HEAD_DIM, (h + 1) * HEAD_DIM)
        w = w_ref[0, :, h * 2 * HEAD_DIM:(h + 1) * 2 * HEAD_DIM].astype(BF16)
        acc = _dot(xna, w[:half]) + _dot(xnb, w[half:])
        k_ref[:, cols] = acc[:, :HEAD_DIM].astype(k_ref.dtype)
        v_ref[:, cols] = acc[:, HEAD_DIM:].astype(v_ref.dtype)


def _kv_proj(p, g, w, l, tabs, *, tm=1024):
    m = p.shape[0]
    rope = _rope_row(tm)
    half = KV_LORA // 2
    tab_spec = pl.BlockSpec((tm, HEAD_DIM), lambda i: (rope(i), 0))
    return pl.pallas_call(
        _kv_kernel,
        grid=(m // tm,),
        in_specs=[
            pl.BlockSpec((tm, half), lambda i: (i, COL_CKV // half)),
            pl.BlockSpec((tm, half), lambda i: (i, COL_CKV // half + 1)),
            pl.BlockSpec((tm, HEAD_DIM), lambda i: (i, COL_KR // HEAD_DIM)),
            pl.BlockSpec((1, KV_LORA), lambda i: (0, 0)),
            pl.BlockSpec((1, KV_LORA, MLA_HEADS * 2 * HEAD_DIM), lambda i: (l, 0, 0)),
            tab_spec, tab_spec, tab_spec,
        ],
        out_specs=[
            pl.BlockSpec((tm, MLA_HEADS * HEAD_DIM), lambda i: (i, 0)),
            pl.BlockSpec((tm, MLA_HEADS * HEAD_DIM), lambda i: (i, 0)),
            pl.BlockSpec((tm, HEAD_DIM), lambda i: (i, 0)),
        ],
        out_shape=[
            jax.ShapeDtypeStruct((m, MLA_HEADS * HEAD_DIM), BF16),
            jax.ShapeDtypeStruct((m, MLA_HEADS * HEAD_DIM), BF16),
            jax.ShapeDtypeStruct((m, HEAD_DIM), BF16),
        ],
        compiler_params=_cparams(("parallel",)),
        name="kv_proj",
    )(p, p, p, g.reshape(1, KV_LORA), w, *tabs)


def _softmax_pv(s_list, v_list, ones_col):
    m = jnp.max(s_list[0], axis=-1, keepdims=True)
    for s in s_list[1:]:
        m = jnp.maximum(m, jnp.max(s, axis=-1, keepdims=True))
    acc = None
    den = None
    for s, v in zip(s_list, v_list):
        p = jnp.exp2(s - m)
        if not ones_col:
            d = jnp.sum(p, axis=-1, keepdims=True)
            den = d if den is None else den + d
        o = _dot(p.astype(BF16), v)
        acc = o if acc is None else acc + o
    if ones_col:
        return acc[:, :HEAD_DIM] / acc[:, HEAD_DIM:]
    return acc / den


def _mla_kernel(*refs, hs, chains, n_lat):
    if n_lat:
        q_ref, kn_ref, kr_ref, v_ref, knc_ref, krc_ref, vc_ref, o_ref, k_scr, v_scr = refs
    else:
        q_ref, knc_ref, krc_ref, vc_ref, o_ref, k_scr, v_scr = refs
    n_ctx = knc_ref.shape[0]
    ones = jnp.ones((n_lat + n_ctx, MLA_VW - HEAD_DIM), BF16)
    for j in range(hs):
        src = slice(j * HEAD_DIM, (j + 1) * HEAD_DIM)
        nope = slice(j * MLA_QK, j * MLA_QK + HEAD_DIM)
        rot = slice(j * MLA_QK + HEAD_DIM, (j + 1) * MLA_QK)
        val = slice(j * MLA_VW, j * MLA_VW + HEAD_DIM)
        if n_lat:
            k_scr[:n_lat, nope] = kn_ref[:, src]
            k_scr[:n_lat, rot] = kr_ref[...]
            v_scr[:n_lat, val] = v_ref[:, src]
        k_scr[n_lat:, nope] = knc_ref[:, src]
        k_scr[n_lat:, rot] = krc_ref[...]
        v_scr[n_lat:, val] = vc_ref[:, src]
        v_scr[:, j * MLA_VW + HEAD_DIM:(j + 1) * MLA_VW] = ones
    tq = q_ref.shape[0] // chains
    for j in range(hs):
        qk_cols = slice(j * MLA_QK, (j + 1) * MLA_QK)
        v_cols = slice(j * MLA_VW, (j + 1) * MLA_VW)
        for c in range(chains):
            rows = slice(c * tq, (c + 1) * tq)
            s = _dot_nt(q_ref[rows, qk_cols], k_scr[:, qk_cols])
            o_ref[rows, j * HEAD_DIM:(j + 1) * HEAD_DIM] = _softmax_pv([s], [v_scr[:, v_cols]], True).astype(
                o_ref.dtype)


def _mla_scratch(n_keys, hs):
    return [pltpu.VMEM((n_keys, hs * MLA_QK), BF16), pltpu.VMEM((n_keys, hs * MLA_VW), BF16)]


def _mla_latent(q, kn, v, kr, *, hs=2, chains=4):
    lat = pl.BlockSpec((SEQ, hs * HEAD_DIM), lambda b, h: (b, h))
    ctx = pl.BlockSpec((CTX_LEN, hs * HEAD_DIM), lambda b, h: (CTX_BLOCK0 + b, h))
    return pl.pallas_call(
        functools.partial(_mla_kernel, hs=hs, chains=chains, n_lat=SEQ),
        grid=(BATCH, MLA_HEADS // hs),
        in_specs=[
            pl.BlockSpec((SEQ, hs * MLA_QK), lambda b, h: (b, h)),
            lat,
            pl.BlockSpec((SEQ, HEAD_DIM), lambda b, h: (b, 0)),
            lat,
            ctx,
            pl.BlockSpec((CTX_LEN, HEAD_DIM), lambda b, h: (CTX_BLOCK0 + b, 0)),
            ctx,
        ],
        out_specs=pl.BlockSpec((SEQ, hs * HEAD_DIM), lambda b, h: (b, h)),
        out_shape=jax.ShapeDtypeStruct((N_LAT, MLA_HEADS * HEAD_DIM), BF16),
        scratch_shapes=_mla_scratch(SEQ + CTX_LEN, hs),
        compiler_params=_cparams(("parallel", "parallel")),
        name="mla_latent",
    )(q, kn, kr, v, kn, kr, v)


def _mla_ctx(q, kn, v, kr):
    wide = pl.BlockSpec((CTX_LEN, MLA_HEADS * HEAD_DIM), lambda b: (CTX_BLOCK0 + b, 0))
    return pl.pallas_call(
        functools.partial(_mla_kernel, hs=MLA_HEADS, chains=1, n_lat=0),
        grid=(BATCH,),
        in_specs=[
            pl.BlockSpec((CTX_LEN, MLA_HEADS * MLA_QK), lambda b: (CTX_BLOCK0 + b, 0)),
            wide,
            pl.BlockSpec((CTX_LEN, HEAD_DIM), lambda b: (CTX_BLOCK0 + b, 0)),
            wide,
        ],
        out_specs=pl.BlockSpec((CTX_LEN, MLA_HEADS * HEAD_DIM), lambda b: (b, 0)),
        out_shape=jax.ShapeDtypeStruct((N_CTX, MLA_HEADS * HEAD_DIM), BF16),
        scratch_shapes=_mla_scratch(CTX_LEN, MLA_HEADS),
        compiler_params=_cparams(("parallel",)),
        name="mla_ctx",
    )(q, kn, kr, v)


def _na_ctx_kernel(q_ref, k_ref, v_ref, o_ref):
    for h in range(NA_HEADS):
        cols = slice(h * HEAD_DIM, (h + 1) * HEAD_DIM)
        s = _dot_nt(q_ref[:, cols], k_ref[:, cols]) * (NA_SCALE * LOG2E)
        o_ref[:, cols] = _softmax_pv([s], [v_ref[:, cols]], False).astype(o_ref.dtype)


def _na_ctx(p):
    width = NA_HEADS * HEAD_DIM
    col0 = COL_NA // width

    def spec(c):
        return pl.BlockSpec((CTX_LEN, width), lambda b: (CTX_BLOCK0 + b, col0 + c))

    return pl.pallas_call(
        _na_ctx_kernel,
        grid=(BATCH,),
        in_specs=[spec(0), spec(1), spec(2)],
        out_specs=pl.BlockSpec((CTX_LEN, width), lambda b: (b, 0)),
        out_shape=jax.ShapeDtypeStruct((N_CTX, width), BF16),
        compiler_params=_cparams(("parallel",)),
        name="na_ctx",
    )(p, p, p)


def _na_plan():
    rows = SEQ // GRID_W
    invalid = 2 * NA_KH - 1
    pairs, plan, starts = [], [], []
    for t in range(rows // NA_TR):
        kw0 = int(np.clip(NA_TR * t - NA_KH // 2, 0, rows - NA_WR))
        starts.append(kw0)
        tile = []
        for ri in range(NA_TR):
            r = NA_TR * t + ri
            r0 = int(np.clip(r - NA_KH // 2, 0, rows - NA_KH))
            assert kw0 <= r0 and r0 + NA_KH <= kw0 + NA_WR
            row = []
            for kp in range(NA_WR // 2):
                pair = []
                for kr in (kw0 + 2 * kp, kw0 + 2 * kp + 1):
                    pair.append(kr - r + NA_KH - 1 if r0 <= kr < r0 + NA_KH else invalid)
                pair = tuple(pair)
                if pair not in pairs:
                    pairs.append(pair)
                row.append(pairs.index(pair))
            tile.append(row)
        plan.append(tile)
    return starts, plan, pairs


def _na_bias_pairs(rpb, pairs):
    c = np.arange(GRID_W)
    c0 = np.clip(c - NA_KW // 2, 0, GRID_W - NA_KW)
    col_ok = (c[None, :] >= c0[:, None]) & (c[None, :] < c0[:, None] + NA_KW)
    col_idx = np.clip(c[None, :] - c[:, None] + NA_KW - 1, 0, 2 * NA_KW - 2)
    onehot = (col_idx[None] == np.arange(2 * NA_KW - 1)[:, None, None]).astype(np.float32)
    t = jnp.einsum("lhdj,jck->lhdck", rpb.astype(F32), onehot, precision=lax.Precision.HIGHEST) * LOG2E
    t = jnp.where(col_ok, t, MASK_VALUE)
    masked = jnp.full(t.shape[:2] + (GRID_W, GRID_W), MASK_VALUE, F32)
    slabs = [t[:, :, d] for d in range(2 * NA_KH - 1)] + [masked]
    return jnp.stack([jnp.concatenate([slabs[a], slabs[b]], axis=-1) for a, b in pairs], axis=2)


def _na_kernel(q_ref, k_ref, v_ref, kc_ref, vc_ref, t2_ref, o_ref, vo_ref, vco_ref, *, starts, plan):
    ones = jnp.ones((SEQ, HEAD_DIM), BF16)
    vo_ref[:, :HEAD_DIM] = v_ref[...]
    vo_ref[:, HEAD_DIM:] = ones
    vco_ref[:, :HEAD_DIM] = vc_ref[...]
    vco_ref[:, HEAD_DIM:] = ones[:CTX_LEN]
    kc = kc_ref[...]
    vc = vco_ref[...]
    for t, (kw0, tile) in enumerate(zip(starts, plan)):
        q = q_ref[t * NA_TQ:(t + 1) * NA_TQ, :]
        kw = k_ref[kw0 * GRID_W:kw0 * GRID_W + NA_TK, :]
        vw = vo_ref[kw0 * GRID_W:kw0 * GRID_W + NA_TK, :]
        bias = jnp.concatenate(
            [jnp.concatenate([t2_ref[0, 0, idx] for idx in row], axis=1) for row in tile], axis=0)
        s = _dot_nt(q, kw) * (NA_SCALE * LOG2E) + bias
        sc = _dot_nt(q, kc) * (NA_SCALE * LOG2E)
        o_ref[t * NA_TQ:(t + 1) * NA_TQ, :] = _softmax_pv([s, sc], [vw, vc], True).astype(o_ref.dtype)


def _na_latent(p, t2, l, starts, plan):
    cq = COL_NA // HEAD_DIM
    ck = cq + NA_HEADS
    cv = ck + NA_HEADS
    n_pairs = t2.shape[2]
    return pl.pallas_call(
        functools.partial(_na_kernel, starts=starts, plan=plan),
        grid=(NA_HEADS, BATCH),
        in_specs=[
            pl.BlockSpec((SEQ, HEAD_DIM), lambda h, b: (b, cq + h)),
            pl.BlockSpec((SEQ, HEAD_DIM), lambda h, b: (b, ck + h)),
            pl.BlockSpec((SEQ, HEAD_DIM), lambda h, b: (b, cv + h)),
            pl.BlockSpec((CTX_LEN, HEAD_DIM), lambda h, b: (CTX_BLOCK0 + b, ck + h)),
            pl.BlockSpec((CTX_LEN, HEAD_DIM), lambda h, b: (CTX_BLOCK0 + b, cv + h)),
            pl.BlockSpec((1, 1, n_pairs, GRID_W, 2 * GRID_W), lambda h, b: (l, h, 0, 0, 0)),
        ],
        out_specs=pl.BlockSpec((SEQ, HEAD_DIM), lambda h, b: (b, h)),
        out_shape=jax.ShapeDtypeStruct((N_LAT, NA_HEADS * HEAD_DIM), BF16),
        scratch_shapes=[pltpu.VMEM((SEQ, 2 * HEAD_DIM), BF16), pltpu.VMEM((CTX_LEN, 2 * HEAD_DIM), BF16)],
        compiler_params=_cparams(("parallel", "parallel")),
        name="na_latent",
    )(p, p, p, p, p, t2)


def _conv_kernel(v_ref, x1_ref, x2_ref, wv_ref, w1_ref, w2_ref, bv_ref, b1_ref, b2_ref, zin_ref, x2o_ref):
    n = v_ref.shape[0]
    row = lax.broadcasted_iota(jnp.int32, (n, 1), 0)

    def short_conv(p_ref, w_ref, b_ref):
        p = p_ref[...].astype(F32)
        prev = jnp.where(row == 0, 0.0, pltpu.roll(p, 1, 0))
        nxt = jnp.where(row == n - 1, 0.0, pltpu.roll(p, n - 1, 0))
        w = w_ref[0]
        return prev * w[0:1] + p * w[1:2] + nxt * w[2:3] + b_ref[0]

    zin_ref[...] = (short_conv(x1_ref, w1_ref, b1_ref) * short_conv(v_ref, wv_ref, bv_ref)).astype(zin_ref.dtype)
    x2o_ref[...] = short_conv(x2_ref, w2_ref, b2_ref).astype(x2o_ref.dtype)


def _hy_conv(p, conv_w, conv_b, l, n, row_block0, *, tc=256):
    nc = HY_WIDTH // tc

    def seg(s):
        return (pl.BlockSpec((n, tc), lambda b, j: (row_block0 + b, s * nc + j)),
                pl.BlockSpec((1, 3, tc), lambda b, j: (l, 0, s * nc + j)),
                pl.BlockSpec((1, 1, tc), lambda b, j: (l, 0, s * nc + j)))

    (pv, wv, bv), (p1, w1, b1), (p2, w2, b2) = seg(0), seg(1), seg(2)
    out_spec = pl.BlockSpec((n, tc), lambda b, j: (0, b * nc + j))
    out = jax.ShapeDtypeStruct((n, BATCH * HY_WIDTH), BF16)
    conv_b = conv_b.reshape(DEPTH, 1, -1)
    return pl.pallas_call(
        _conv_kernel,
        grid=(BATCH, nc),
        in_specs=[pv, p1, p2, wv, w1, w2, bv, b1, b2],
        out_specs=[out_spec, out_spec],
        out_shape=[out, out],
        compiler_params=_cparams(("parallel", "parallel")),
        name="hy_conv",
    )(p, p, p, conv_w, conv_w, conv_w, conv_b, conv_b, conv_b)


def _filt_kernel(z_ref, t_ref, dl_ref, w1_ref, b1_ref, w2_ref, b2_ref, w3_ref, fr_ref, hs_ref, ha_ref, kn_ref):
    n = z_ref.shape[0]
    fr = fr_ref[...]
    h = jnp.sin(fr * (_dot_hi(z_ref[...], w1_ref[...]) + b1_ref[...]))
    h = jnp.sin(fr * (_dot_hi(h, w2_ref[...]) + b2_ref[...]))
    h = _dot_hi(h, w3_ref[...])
    decay = jnp.exp(-t_ref[...] * dl_ref[...])
    row = lax.broadcasted_iota(jnp.int32, (n, 1), 0)
    hf = h[:, :HY_WIDTH] * decay
    hb = jnp.where(row == 0, 0.0, h[:, HY_WIDTH:] * decay)
    hs = hf + hb
    hs_ref[...] = hs.astype(hs_ref.dtype)
    ha_ref[...] = (hf - hb).astype(ha_ref.dtype)
    sign = jnp.where((row & 1) == 0, 1.0, -1.0)
    kn_ref[...] = jnp.sum(hs * sign, axis=0, keepdims=True)


def _hy_filter_taps(n, f_w1, f_b1, f_w2, f_b2, f_w3, f_freq):
    pos = jnp.arange(n, dtype=F32)
    t = jnp.linspace(0.0, 1.0, n, dtype=F32)
    bands = jnp.linspace(1e-4, HY_POS_BANDS - 1, HY_POS_BANDS, dtype=F32)
    ang = (2.0 * math.pi / n) * pos[:, None] * bands[None, :]
    z = jnp.concatenate([t[:, None], jnp.cos(ang), -jnp.sin(ang)], axis=-1)
    pad = HY_FILTER_HIDDEN - z.shape[1]
    z = jnp.pad(z, ((0, 0), (0, pad)))
    w1 = jnp.pad(f_w1.astype(F32), ((0, pad), (0, 0)))
    deltas = jnp.abs(jnp.linspace(math.log(HY_DECAY_TARGET) / HY_FAST_DECAY,
                                  math.log(HY_DECAY_TARGET) / HY_SLOW_DECAY, HY_WIDTH, dtype=F32))
    hid = HY_FILTER_HIDDEN
    return pl.pallas_call(
        _filt_kernel,
        out_shape=[
            jax.ShapeDtypeStruct((n, HY_WIDTH), BF16),
            jax.ShapeDtypeStruct((n, HY_WIDTH), BF16),
            jax.ShapeDtypeStruct((1, HY_WIDTH), F32),
        ],
        compiler_params=pltpu.CompilerParams(vmem_limit_bytes=VMEM_LIMIT),
        name="hy_filter",
    )(z, t[:, None], deltas[None, :], w1, f_b1.reshape(1, hid), f_w2, f_b2.reshape(1, hid), f_w3,
      f_freq.reshape(1, hid))


def _dft_tables(n):
    lo = 16 if n < 1024 else 32
    hi = n // lo
    t = jnp.arange(n, dtype=jnp.int32)[None, :]
    big = 2 * n

    def ang(ff):
        return ((ff * t) % big).astype(F32) * (2.0 * math.pi / big)

    a = ang(lo * jnp.arange(hi, dtype=jnp.int32)[:, None])
    b = ang(jnp.arange(lo, dtype=jnp.int32)[:, None])
    ca, sa, cb, sb = jnp.cos(a), jnp.sin(a), jnp.cos(b), jnp.sin(b)
    cos = (ca[:, None, :] * cb[None, :, :] - sa[:, None, :] * sb[None, :, :]).reshape(n, n)
    base = -(sa[:, None, :] * cb[None, :, :] + ca[:, None, :] * sb[None, :, :]).reshape(n, n)
    idx = jnp.arange(n)
    alt = jnp.where(idx % 2 == 0, 1.0, -1.0).astype(F32)
    msin = jnp.where(idx[:, None] == 0, alt[None, :], base)
    msin_t = jnp.where(idx[None, :] == 0, alt[:, None], base)
    return cos.astype(BF16), msin.astype(BF16), msin_t.astype(BF16)


def _dft_filt_kernel(c_ref, s_ref, hs_ref, ha_ref, kr_ref, ki_ref):
    kr_ref[...] = _dot(c_ref[...], hs_ref[...])
    ki_ref[...] = _dot(s_ref[...], ha_ref[...])


def _hy_filter_spectrum(cos, msin, hs, ha, *, tf):
    n = cos.shape[0]
    tf = min(tf, n)
    tab = pl.BlockSpec((tf, n), lambda i: (i, 0))
    taps = pl.BlockSpec((n, HY_WIDTH), lambda i: (0, 0))
    out = pl.BlockSpec((tf, HY_WIDTH), lambda i: (i, 0))
    return pl.pallas_call(
        _dft_filt_kernel,
        grid=(n // tf,),
        in_specs=[tab, tab, taps, taps],
        out_specs=[out, out],
        out_shape=[jax.ShapeDtypeStruct((n, HY_WIDTH), F32)] * 2,
        compiler_params=_cparams(("parallel",)),
        name="hy_filter_dft",
    )(cos, msin, hs, ha)


def _batch_cols(b):
    return slice(b * HY_WIDTH, (b + 1) * HY_WIDTH)


def _dft_fwd_kernel(c_ref, s_ref, x_ref, kr_ref, ki_ref, kn_ref, yr_ref, yi_ref, *, tf, inv_n):
    x = x_ref[...]
    zr = _dot(c_ref[...], x)
    zi = _dot(s_ref[...], x)
    row = pl.program_id(0) * tf + lax.broadcasted_iota(jnp.int32, (tf, 1), 0)
    bin0 = row == 0
    wt = jnp.where(bin0, inv_n, 2.0 * inv_n)
    kr = kr_ref[...] * wt
    ki = jnp.where(bin0, 0.0, ki_ref[...] * wt)
    kr_im = jnp.where(bin0, kn_ref[...] * wt, kr)
    for b in range(BATCH):
        cols = _batch_cols(b)
        yr_ref[:, cols] = (zr[:, cols] * kr - zi[:, cols] * ki).astype(yr_ref.dtype)
        yi_ref[:, cols] = (zr[:, cols] * ki + zi[:, cols] * kr_im).astype(yi_ref.dtype)


def _hy_dft_fwd(cos, msin, zin, kr, ki, kn, *, tf):
    n = cos.shape[0]
    tf = min(tf, n)
    tab = pl.BlockSpec((tf, n), lambda i: (i, 0))
    filt = pl.BlockSpec((tf, HY_WIDTH), lambda i: (i, 0))
    out = pl.BlockSpec((tf, BATCH * HY_WIDTH), lambda i: (i, 0))
    whole = pl.BlockSpec((n, BATCH * HY_WIDTH), lambda i: (0, 0), pipeline_mode=pl.Buffered(1))
    return pl.pallas_call(
        functools.partial(_dft_fwd_kernel, tf=tf, inv_n=1.0 / (2 * n)),
        grid=(n // tf,),
        in_specs=[tab, tab, whole, filt, filt, pl.BlockSpec((1, HY_WIDTH), lambda i: (0, 0))],
        out_specs=[out, out],
        out_shape=[jax.ShapeDtypeStruct((n, BATCH * HY_WIDTH), BF16)] * 2,
        compiler_params=_cparams(("parallel",)),
        name="hy_dft_fwd",
    )(cos, msin, zin, kr, ki, kn)


def _dft_inv_kernel(c_ref, st_ref, yr_ref, yi_ref, zin_ref, x2_ref, b_ref, o_ref):
    y = _dot(c_ref[...], yr_ref[...]) + _dot(st_ref[...], yi_ref[...])
    bias = b_ref[0]
    for b in range(BATCH):
        cols = _batch_cols(b)
        yb = y[:, cols] + zin_ref[:, cols].astype(F32) * bias
        o_ref[b] = (x2_ref[:, cols].astype(F32) * yb).astype(o_ref.dtype)


def _hy_dft_inv(cos, msin_t, yr, yi, zin, x2, bias, l, *, tt):
    n = cos.shape[0]
    tt = min(tt, n)
    tab = pl.BlockSpec((tt, n), lambda i: (i, 0))
    whole = pl.BlockSpec((n, BATCH * HY_WIDTH), lambda i: (0, 0), pipeline_mode=pl.Buffered(1))
    rows = pl.BlockSpec((tt, BATCH * HY_WIDTH), lambda i: (i, 0))
    out = pl.pallas_call(
        _dft_inv_kernel,
        grid=(n // tt,),
        in_specs=[tab, tab, whole, whole, rows, rows, pl.BlockSpec((1, 1, HY_WIDTH), lambda i: (l, 0, 0))],
        out_specs=pl.BlockSpec((BATCH, tt, HY_WIDTH), lambda i: (0, i, 0)),
        out_shape=jax.ShapeDtypeStruct((BATCH, n, HY_WIDTH), BF16),
        compiler_params=_cparams(("parallel",)),
        name="hy_dft_inv",
    )(cos, msin_t, yr, yi, zin, x2, bias.reshape(DEPTH, 1, HY_WIDTH))
    return out.reshape(BATCH * n, HY_WIDTH)


def _hyena(p, l, n, row_block0, tables, conv_w, conv_b, f_w1, f_b1, f_w2, f_b2, f_w3, f_freq, bias):
    cos, msin, msin_t = tables
    zin, x2 = _hy_conv(p, conv_w, conv_b, l, n, row_block0)
    hs, ha, kn = _hy_filter_taps(n, f_w1[l], f_b1[l], f_w2[l], f_b2[l], f_w3[l], f_freq[l])
    kr, ki = _hy_filter_spectrum(cos, msin, hs, ha, tf=512)
    yr, yi = _hy_dft_fwd(cos, msin, zin, kr, ki, kn, tf=512)
    return _hy_dft_inv(cos, msin_t, yr, yi, zin, x2, bias, l, tt=512)


def _rope_tables():
    tok = jnp.arange(SEQ)
    row = (tok // GRID_W).astype(F32)
    col = (tok % GRID_W).astype(F32)
    n_freq = MLA_ROPE // 4
    inv = ROPE_THETA ** (-jnp.arange(n_freq, dtype=F32) / n_freq)
    ang = jnp.concatenate([row[:, None] * inv, col[:, None] * inv], axis=-1)
    cos, sin = jnp.cos(ang), jnp.sin(ang)
    half = MLA_ROPE // 2
    zeros = jnp.zeros((SEQ, half), F32)
    rest = HEAD_DIM - MLA_ROPE
    cos_t = jnp.concatenate([cos, cos, jnp.ones((SEQ, rest), F32)], axis=-1)
    sin_a = jnp.concatenate([-sin, zeros, jnp.zeros((SEQ, rest), F32)], axis=-1)
    sin_b = jnp.concatenate([zeros, sin, jnp.zeros((SEQ, rest), F32)], axis=-1)
    ident = jnp.ones((N_CTX, HEAD_DIM), F32)
    none = jnp.zeros((N_CTX, HEAD_DIM), F32)
    return (jnp.concatenate([cos_t, ident]), jnp.concatenate([sin_a, none]), jnp.concatenate([sin_b, none]))


def _layout_w_uq(w):
    w = w.reshape(Q_LORA, MLA_HEADS, HEAD_DIM + MLA_ROPE)
    w = jnp.pad(w, ((0, 0), (0, 0), (0, MLA_QK - HEAD_DIM - MLA_ROPE)))
    return w.reshape(Q_LORA, MLA_HEADS * MLA_QK).astype(BF16)


def kernel(x, c, ctx, c_ctx, w_ada, b_ada, g_attn_pre, g_attn_post, g_ffn_pre, g_ffn_post, w_in, hy_conv_w, hy_conv_b, hy_f_w1, hy_f_b1, hy_f_w2, hy_f_b2, hy_f_w3, hy_f_freq, hy_bias, mla_g_q, mla_w_uq, mla_g_kv, mla_w_ukv, na_rpb, w_out, w_ffn_gate, w_ffn_up, w_ffn_down):
    cc = jnp.concatenate([c, c_ctx[None, :], jnp.zeros((8 - BATCH - 1, D_MODEL), F32)], axis=0)
    mods_all = _ada(cc, w_ada, b_ada)
    mods = [mods_all[l].reshape(8, 1, 6 * D_MODEL) for l in range(DEPTH)]

    rope_tabs = _rope_tables()
    dft_lat = _dft_tables(SEQ)
    dft_ctx = _dft_tables(CTX_LEN)
    na_starts, na_plan, na_pairs = _na_plan()
    na_t2 = _na_bias_pairs(na_rpb, na_pairs)
    w_in_t = jnp.swapaxes(w_in, 1, 2)
    hy_w = (hy_conv_w, hy_conv_b, hy_f_w1, hy_f_b1, hy_f_w2, hy_f_b2, hy_f_w3, hy_f_freq, hy_bias)

    res = (x.reshape(N_LAT, D_MODEL), ctx.reshape(N_CTX, D_MODEL))
    xn = _prenorm(res[0], res[1], g_attn_pre[0], mods[0])

    for l in range(DEPTH):
        ctx_out = l < DEPTH - 1
        m_rows = N_TOK if ctx_out else N_LAT
        p = _in_proj(xn, w_in_t, l)

        q = _q_proj(p, m_rows, mla_g_q[l], _layout_w_uq(mla_w_uq[l]), rope_tabs)
        kn, v, kr = _kv_proj(p, mla_g_kv[l], mla_w_ukv, l, rope_tabs)
        mla = [_mla_latent(q, kn, v, kr), None]
        na = [_na_latent(p, na_t2, l, na_starts, na_plan), None]
        hy = [_hyena(p, l, SEQ, 0, dft_lat, *hy_w), None]

        if ctx_out:
            mla[1] = _mla_ctx(q, kn, v, kr)
            na[1] = _na_ctx(p)
            hy[1] = _hyena(p, l, CTX_LEN, CTX_BLOCK0, dft_ctx, *hy_w)
            res_l = res
        else:
            res_l = (res[0], None)

        stream, xn = _proj_post([tuple(hy), tuple(mla), tuple(na)], w_out, l, g_attn_post[l], mods[l], 2, res_l,
                                m_rows, nxt=(g_ffn_pre[l], mods[l], 3, 4), tm=POST_TM_OUT)
        h = _ffn_up(xn, w_ffn_gate, w_ffn_up, l)
        nxt = (g_attn_pre[l + 1], mods[l + 1], 0, 1) if ctx_out else None
        stream, xn = _proj_post([(h, None)], w_ffn_down, l, g_ffn_post[l], mods[l], 5, (stream, None), m_rows,
                                nxt=nxt, tm=POST_TM_DOWN)
        res = (stream, None)

    return stream.reshape(BATCH, SEQ, D_MODEL)
```

```python
import functools
import math

import jax
import jax.numpy as jnp
import numpy as np
from jax import lax
from jax.experimental import pallas as pl
from jax.experimental.pallas import tpu as pltpu

F32 = jnp.float32
BF16 = jnp.bfloat16

D_MODEL = 2048
BATCH = 4
SEQ = 2048
DEPTH = 2
GRID_W = 64
CTX_LEN = 256
HEAD_DIM = 128
HY_WIDTH = D_MODEL // 4
HY_FILTER_HIDDEN = 64
HY_POS_BANDS = 16
HY_DECAY_TARGET = 1e-2
HY_FAST_DECAY = 0.3
HY_SLOW_DECAY = 1.5
MLA_HEADS = (D_MODEL // 2) // HEAD_DIM
MLA_ROPE = 64
Q_LORA = 3 * D_MODEL // 8
KV_LORA = D_MODEL // 4
MLA_SCALE = (HEAD_DIM + MLA_ROPE) ** -0.5
NA_HEADS = (D_MODEL // 4) // HEAD_DIM
NA_KH = 8
NA_KW = 16
NA_SCALE = HEAD_DIM ** -0.5
FFN_HIDDEN = ((8 * D_MODEL + 3 * 256 - 1) // (3 * 256)) * 256
ROPE_THETA = 10000.0
RMS_EPS = 1e-6
MASK_VALUE = -1e30
LOG2E = math.log2(math.e)

N_LAT = BATCH * SEQ
N_CTX = BATCH * CTX_LEN
N_TOK = N_LAT + N_CTX
CTX_BLOCK0 = N_LAT // CTX_LEN

IN_TN = 512
COL_HY = 0
COL_CQ = 3 * HY_WIDTH
COL_CKV = COL_CQ + Q_LORA
COL_KR = COL_CKV + KV_LORA
W_IN_NA = COL_KR + MLA_ROPE
COL_NA = -(-W_IN_NA // IN_TN) * IN_TN
P_COLS = COL_NA + 3 * NA_HEADS * HEAD_DIM
MLA_QK = 256
MLA_VW = 256

NA_TR = 4
NA_WR = 12
NA_TQ = NA_TR * GRID_W
NA_TK = NA_WR * GRID_W

VMEM_LIMIT = 52 * 1024 * 1024


def _cparams(sem):
    return pltpu.CompilerParams(dimension_semantics=sem, vmem_limit_bytes=VMEM_LIMIT)


def _dot(a, b):
    return jnp.dot(a, b, preferred_element_type=F32)


def _dot_nt(a, b):
    return lax.dot_general(a, b, (((1,), (1,)), ((), ())), preferred_element_type=F32)


def _dot_hi(a, b):
    return jnp.dot(a, b, preferred_element_type=F32, precision=lax.Precision.HIGHEST)


def _rms(x, g):
    ms = jnp.mean(x * x, axis=-1, keepdims=True)
    return x * lax.rsqrt(ms + RMS_EPS) * g


def _mod_row(tm):
    n_lat, per_b = N_LAT // tm, SEQ // tm
    return lambda i: jnp.where(i < n_lat, i // per_b, BATCH)


def _rope_row(tm):
    n_lat, per_b = N_LAT // tm, SEQ // tm
    return lambda i: jnp.where(i < n_lat, i % per_b, per_b + i - n_lat)


def _ada_kernel(c_ref, w_ref, b_ref, o_ref):
    a = c_ref[...]
    a = a * jax.nn.sigmoid(a)
    o_ref[0] = _dot(a.astype(BF16), w_ref[0].astype(BF16)) + b_ref[0]


def _ada(cc, w_ada, b_ada):
    tn = 1024
    n = w_ada.shape[-1]
    return pl.pallas_call(
        _ada_kernel,
        grid=(DEPTH, n // tn),
        in_specs=[
            pl.BlockSpec((8, D_MODEL), lambda l, j: (0, 0)),
            pl.BlockSpec((1, D_MODEL, tn), lambda l, j: (l, 0, j)),
            pl.BlockSpec((1, 1, tn), lambda l, j: (l, 0, j)),
        ],
        out_specs=pl.BlockSpec((1, 8, tn), lambda l, j: (l, 0, j)),
        out_shape=jax.ShapeDtypeStruct((DEPTH, 8, n), F32),
        compiler_params=_cparams(("parallel", "parallel")),
        name="ada",
    )(cc, w_ada, b_ada.reshape(DEPTH, 1, n))


def _norm_mod_to(xn_ref, x_ref, g_ref, sh_ref, sc_ref, chunk=256):
    g = g_ref[...]
    sc = 1.0 + sc_ref[0]
    sh = sh_ref[0]

    def body(r, carry):
        rows = pl.ds(pl.multiple_of(r * chunk, chunk), chunk)
        xn_ref[rows, :] = (_rms(x_ref[rows, :], g) * sc + sh).astype(BF16)
        return carry

    lax.fori_loop(0, x_ref.shape[0] // chunk, body, 0)


def _prenorm_kernel(x_ref, c_ref, g_ref, sh_ref, sc_ref, o_ref, *, n_lat):
    i = pl.program_id(0)

    @pl.when(i < n_lat)
    def _():
        _norm_mod_to(o_ref, x_ref, g_ref, sh_ref, sc_ref)

    @pl.when(i >= n_lat)
    def _():
        _norm_mod_to(o_ref, c_ref, g_ref, sh_ref, sc_ref)


def _prenorm(x2d, ctx2d, g, mods, *, tm=1024):
    n_lat = N_LAT // tm
    row = _mod_row(tm)
    return pl.pallas_call(
        functools.partial(_prenorm_kernel, n_lat=n_lat),
        grid=(N_TOK // tm,),
        in_specs=[
            pl.BlockSpec((tm, D_MODEL), lambda i: (jnp.minimum(i, n_lat - 1), 0)),
            pl.BlockSpec((tm, D_MODEL), lambda i: (jnp.maximum(i - n_lat, 0), 0)),
            pl.BlockSpec((1, D_MODEL), lambda i: (0, 0)),
            pl.BlockSpec((1, 1, D_MODEL), lambda i: (row(i), 0, 0)),
            pl.BlockSpec((1, 1, D_MODEL), lambda i: (row(i), 0, 1)),
        ],
        out_specs=pl.BlockSpec((tm, D_MODEL), lambda i: (i, 0)),
        out_shape=jax.ShapeDtypeStruct((N_TOK, D_MODEL), BF16),
        compiler_params=_cparams(("parallel",)),
        name="prenorm",
    )(x2d, ctx2d, g.reshape(1, D_MODEL), mods, mods)


def _in_kernel(x_ref, w_ref, o_ref):
    o_ref[...] = _dot_nt(x_ref[...], w_ref[0].astype(BF16)).astype(o_ref.dtype)


def _in_proj(xn, w_in_t, l, *, tm=3072):
    n_head = COL_NA // IN_TN

    def w_row(j):
        per = IN_TN // MLA_ROPE
        return MLA_ROPE * jnp.where(j < n_head, j * per, W_IN_NA // MLA_ROPE + (j - n_head) * per)

    return pl.pallas_call(
        _in_kernel,
        grid=(N_TOK // tm, P_COLS // IN_TN),
        in_specs=[
            pl.BlockSpec((tm, D_MODEL), lambda i, j: (i, 0)),
            pl.BlockSpec((pl.Element(1), pl.Element(IN_TN), pl.Element(D_MODEL)), lambda i, j: (l, w_row(j), 0)),
        ],
        out_specs=pl.BlockSpec((tm, IN_TN), lambda i, j: (i, j)),
        out_shape=jax.ShapeDtypeStruct((N_TOK, P_COLS), BF16),
        compiler_params=_cparams(("parallel", "arbitrary")),
        name="in_proj",
    )(xn, w_in_t)


FFN_CW = 256


def _swiglu_kernel(x_ref, wg_ref, wu_ref, o_ref):
    xn = x_ref[...]
    for c in range(o_ref.shape[1] // FFN_CW):
        cols = slice(c * FFN_CW, (c + 1) * FFN_CW)
        gate = _dot(xn, wg_ref[0, :, cols].astype(BF16))
        up = _dot(xn, wu_ref[0, :, cols].astype(BF16))
        o_ref[:, cols] = (gate * jax.nn.sigmoid(gate) * up).astype(o_ref.dtype)


def _ffn_up(xn, wg, wu, l, *, tn=512):
    m = xn.shape[0]
    tm = 2048 if m % 2048 == 0 else 1536
    w_spec = pl.BlockSpec((1, D_MODEL, tn), lambda i, j: (l, 0, j))
    return pl.pallas_call(
        _swiglu_kernel,
        grid=(m // tm, FFN_HIDDEN // tn),
        in_specs=[pl.BlockSpec((tm, D_MODEL), lambda i, j: (i, 0)), w_spec, w_spec],
        out_specs=pl.BlockSpec((tm, tn), lambda i, j: (i, j)),
        out_shape=jax.ShapeDtypeStruct((m, FFN_HIDDEN), BF16),
        compiler_params=_cparams(("parallel", "arbitrary")),
        name="ffn_up",
    )(xn, wg, wu)


POST_TM_OUT = 512
POST_TM_DOWN = 256
POST_SUB = 128
POST_WCH = 256
POST_WSLOTS = 4
POST_SLOTS = 3


def _post_kernel(*refs, widths, two_src, emit_xn, n_lat, l, tm):
    refs = list(refs)
    n_act = len(widths)
    acts_lat = [refs.pop(0) for _ in range(n_act)]
    acts_ctx = [refs.pop(0) for _ in range(n_act)] if two_src else None
    w_hbm, g_ref, gate_ref, res_lat = (refs.pop(0) for _ in range(4))
    res_ctx = refs.pop(0) if two_src else None
    if emit_xn:
        g2_ref, sh_ref, sc_ref = (refs.pop(0) for _ in range(3))
    o_hbm = refs.pop(0)
    xn_ref = refs.pop(0) if emit_xn else None
    w_scr, stage, buf, sem_w, sem_res, sem_out = refs

    i = pl.program_id(0)
    n_i = pl.num_programs(0)
    n_chunks = sum(widths) // POST_WCH
    is_lat = i < n_lat
    is_ctx = jnp.logical_not(is_lat)

    def w_copy(c, slot):
        return pltpu.make_async_copy(w_hbm.at[l, pl.ds(c * POST_WCH, POST_WCH)], stage.at[slot], sem_w.at[slot])

    def res_copy(src, row0, slot):
        return pltpu.make_async_copy(src.at[pl.ds(row0, tm)], buf.at[slot], sem_res.at[slot])

    def res_start(tile, slot):
        if two_src:
            pl.when(tile < n_lat)(lambda: res_copy(res_lat, tile * tm, slot).start())
            pl.when(tile >= n_lat)(lambda: res_copy(res_ctx, (tile - n_lat) * tm, slot).start())
        else:
            res_copy(res_lat, tile * tm, slot).start()

    def out_copy(tile, slot):
        return pltpu.make_async_copy(buf.at[slot], o_hbm.at[pl.ds(tile * tm, tm)], sem_out.at[slot])

    @pl.when(i == 0)
    def _():
        res_start(0, 0)
        for c in range(POST_WSLOTS - 1):
            w_copy(c, c).start()

        def body(c, carry):
            s = c % POST_WSLOTS
            w_copy(c, s).wait()
            ahead = c + POST_WSLOTS - 1

            @pl.when(ahead < n_chunks)
            def _():
                w_copy(ahead, ahead % POST_WSLOTS).start()

            w_scr[pl.ds(pl.multiple_of(c * POST_WCH, POST_WCH), POST_WCH), :] = stage[s].astype(BF16)
            return carry

        lax.fori_loop(0, n_chunks, body, 0)

    slot = i % POST_SLOTS
    nslot = (i + 1) % POST_SLOTS

    @pl.when(i + 1 < n_i)
    def _():
        @pl.when(i >= POST_SLOTS - 1)
        def _():
            out_copy(i + 1 - POST_SLOTS, nslot).wait()

        res_start(i + 1, nslot)

    res_copy(res_lat, 0, slot).wait()

    def compute(act_refs):
        gg = gate_ref[0] * g_ref[...]
        if emit_xn:
            g2s = g2_ref[...] * (1.0 + sc_ref[0])
            sh = sh_ref[0]
        for r in range(tm // POST_SUB):
            rows = pl.ds(r * POST_SUB, POST_SUB)
            y = None
            k0 = 0
            for a_ref, kw in zip(act_refs, widths):
                part = _dot(a_ref[rows, :], w_scr[k0:k0 + kw, :])
                y = part if y is None else y + part
                k0 += kw
            x_new = buf[slot, rows, :] + _rms(y, gg)
            buf[slot, rows, :] = x_new
            if emit_xn:
                xn_ref[rows, :] = (_rms(x_new, g2s) + sh).astype(xn_ref.dtype)

    if two_src:
        pl.when(is_lat)(functools.partial(compute, acts_lat))
        pl.when(is_ctx)(functools.partial(compute, acts_ctx))
    else:
        compute(acts_lat)

    out_copy(i, slot).start()

    @pl.when(i == n_i - 1)
    def _():
        for back in range(POST_SLOTS):
            out_copy(i - back, (i - back) % POST_SLOTS).wait()


def _proj_post(acts, w, l, g, mods, gate_chunk, res, m_rows, nxt=None, *, tm):
    two_src = res[1] is not None
    n_lat = N_LAT // tm
    widths = tuple(a.shape[1] for a, _ in acts)
    k_total = sum(widths)
    emit_xn = nxt is not None
    row = _mod_row(tm)

    def lat_row(i):
        return jnp.minimum(i, n_lat - 1) if two_src else i

    def ctx_row(i):
        return jnp.maximum(i - n_lat, 0)

    def act_specs(rowf):
        return [pl.BlockSpec((tm, kw), lambda i: (rowf(i), 0)) for kw in widths]

    vec_spec = pl.BlockSpec((1, D_MODEL), lambda i: (0, 0))

    def mod_spec(chunk):
        return pl.BlockSpec((1, 1, D_MODEL), lambda i: (row(i), 0, chunk))

    hbm = pl.BlockSpec(memory_space=pl.ANY)
    in_specs = act_specs(lat_row)
    args = [a for a, _ in acts]
    if two_src:
        in_specs += act_specs(ctx_row)
        args += [c for _, c in acts]
    in_specs += [hbm, vec_spec, mod_spec(gate_chunk), hbm]
    args += [w, g.reshape(1, D_MODEL), mods, res[0]]
    if two_src:
        in_specs.append(hbm)
        args.append(res[1])
    out_specs = [hbm]
    out_shape = [jax.ShapeDtypeStruct((m_rows, D_MODEL), F32)]
    if emit_xn:
        g2, mods2, sh_chunk, sc_chunk = nxt
        in_specs += [vec_spec, mod_spec(sh_chunk), mod_spec(sc_chunk)]
        args += [g2.reshape(1, D_MODEL), mods2, mods2]
        out_specs.append(pl.BlockSpec((tm, D_MODEL), lambda i: (i, 0)))
        out_shape.append(jax.ShapeDtypeStruct((m_rows, D_MODEL), BF16))

    out = pl.pallas_call(
        functools.partial(_post_kernel, widths=widths, two_src=two_src, emit_xn=emit_xn, n_lat=n_lat, l=l, tm=tm),
        grid=(m_rows // tm,),
        in_specs=in_specs,
        out_specs=out_specs,
        out_shape=out_shape,
        scratch_shapes=[
            pltpu.VMEM((k_total, D_MODEL), BF16),
            pltpu.VMEM((POST_WSLOTS, POST_WCH, D_MODEL), F32),
            pltpu.VMEM((POST_SLOTS, tm, D_MODEL), F32),
            pltpu.SemaphoreType.DMA((POST_WSLOTS,)),
            pltpu.SemaphoreType.DMA((POST_SLOTS,)),
            pltpu.SemaphoreType.DMA((POST_SLOTS,)),
        ],
        compiler_params=_cparams(("arbitrary",)),
        name="proj_post",
    )(*args)
    return (out[0], out[1]) if emit_xn else (out[0], None)


def _rope128(r, cos_ref, sa_ref, sb_ref):
    return r * cos_ref[...] + pltpu.roll(r, 96, 1) * sa_ref[...] + pltpu.roll(r, 32, 1) * sb_ref[...]


def _q_kernel(x_ref, g_ref, w_ref, cos_ref, sa_ref, sb_ref, o_ref):
    xn = _rms(x_ref[...].astype(F32), g_ref[...] * (MLA_SCALE * LOG2E)).astype(BF16)
    for h in range(MLA_HEADS):
        acc = _dot(xn, w_ref[:, h * MLA_QK:(h + 1) * MLA_QK])
        o_ref[:, h * MLA_QK:h * MLA_QK + HEAD_DIM] = acc[:, :HEAD_DIM].astype(o_ref.dtype)
        o_ref[:, h * MLA_QK + HEAD_DIM:(h + 1) * MLA_QK] = _rope128(
            acc[:, HEAD_DIM:], cos_ref, sa_ref, sb_ref).astype(o_ref.dtype)


def _q_proj(p, m_rows, g, w, tabs, *, tm=1024):
    rope = _rope_row(tm)
    tab_spec = pl.BlockSpec((tm, HEAD_DIM), lambda i: (rope(i), 0))
    return pl.pallas_call(
        _q_kernel,
        grid=(m_rows // tm,),
        in_specs=[
            pl.BlockSpec((tm, Q_LORA), lambda i: (i, COL_CQ // Q_LORA)),
            pl.BlockSpec((1, Q_LORA), lambda i: (0, 0)),
            pl.BlockSpec((Q_LORA, MLA_HEADS * MLA_QK), lambda i: (0, 0)),
            tab_spec, tab_spec, tab_spec,
        ],
        out_specs=pl.BlockSpec((tm, MLA_HEADS * MLA_QK), lambda i: (i, 0)),
        out_shape=jax.ShapeDtypeStruct((m_rows, MLA_HEADS * MLA_QK), BF16),
        compiler_params=_cparams(("parallel",)),
        name="q_proj",
    )(p, g.reshape(1, Q_LORA), w, *tabs)


def _kv_kernel(xa_ref, xb_ref, kr_ref, g_ref, w_ref, cos_ref, sa_ref, sb_ref, k_ref, v_ref, krr_ref):
    half = KV_LORA // 2
    xa = xa_ref[...].astype(F32)
    xb = xb_ref[...].astype(F32)
    ms = (jnp.sum(xa * xa, axis=-1, keepdims=True) + jnp.sum(xb * xb, axis=-1, keepdims=True)) * (1.0 / KV_LORA)
    rs = lax.rsqrt(ms + RMS_EPS)
    g = g_ref[...]
    xna = (xa * rs * g[:, :half]).astype(BF16)
    xnb = (xb * rs * g[:, half:]).astype(BF16)
    lane = lax.broadcasted_iota(jnp.int32, (1, HEAD_DIM), 1)
    krr = _rope128(kr_ref[...].astype(F32), cos_ref, sa_ref, sb_ref)
    krr_ref[...] = jnp.where(lane < MLA_ROPE, krr, 0.0).astype(krr_ref.dtype)
    for h in range(MLA_HEADS):
        cols = slice(h * HEAD_DIM, (h + 1) * HEAD_DIM)
        w = w_ref[0, :, h * 2 * HEAD_DIM:(h + 1) * 2 * HEAD_DIM].astype(BF16)
        acc = _dot(xna, w[:half]) + _dot(xnb, w[half:])
        k_ref[:, cols] = acc[:, :HEAD_DIM].astype(k_ref.dtype)
        v_ref[:, cols] = acc[:, HEAD_DIM:].astype(v_ref.dtype)


def _kv_proj(p, g, w, l, tabs, *, tm=1024):
    m = p.shape[0]
    rope = _rope_row(tm)
    half = KV_LORA // 2
    tab_spec = pl.BlockSpec((tm, HEAD_DIM), lambda i: (rope(i), 0))
    return pl.pallas_call(
        _kv_kernel,
        grid=(m // tm,),
        in_specs=[
            pl.BlockSpec((tm, half), lambda i: (i, COL_CKV // half)),
            pl.BlockSpec((tm, half), lambda i: (i, COL_CKV // half + 1)),
            pl.BlockSpec((tm, HEAD_DIM), lambda i: (i, COL_KR // HEAD_DIM)),
            pl.BlockSpec((1, KV_LORA), lambda i: (0, 0)),
            pl.BlockSpec((1, KV_LORA, MLA_HEADS * 2 * HEAD_DIM), lambda i: (l, 0, 0)),
            tab_spec, tab_spec, tab_spec,
        ],
        out_specs=[
            pl.BlockSpec((tm, MLA_HEADS * HEAD_DIM), lambda i: (i, 0)),
            pl.BlockSpec((tm, MLA_HEADS * HEAD_DIM), lambda i: (i, 0)),
            pl.BlockSpec((tm, HEAD_DIM), lambda i: (i, 0)),
        ],
        out_shape=[
            jax.ShapeDtypeStruct((m, MLA_HEADS * HEAD_DIM), BF16),
            jax.ShapeDtypeStruct((m, MLA_HEADS * HEAD_DIM), BF16),
            jax.ShapeDtypeStruct((m, HEAD_DIM), BF16),
        ],
        compiler_params=_cparams(("parallel",)),
        name="kv_proj",
    )(p, p, p, g.reshape(1, KV_LORA), w, *tabs)


def _softmax_pv(s_list, v_list, ones_col):
    m = jnp.max(s_list[0], axis=-1, keepdims=True)
    for s in s_list[1:]:
        m = jnp.maximum(m, jnp.max(s, axis=-1, keepdims=True))
    acc = None
    den = None
    for s, v in zip(s_list, v_list):
        p = jnp.exp2(s - m)
        if not ones_col:
            d = jnp.sum(p, axis=-1, keepdims=True)
            den = d if den is None else den + d
        o = _dot(p.astype(BF16), v)
        acc = o if acc is None else acc + o
    if ones_col:
        return acc[:, :HEAD_DIM] / acc[:, HEAD_DIM:]
    return acc / den


def _mla_kernel(*refs, hs, chains, n_lat):
    if n_lat:
        q_ref, kn_ref, kr_ref, v_ref, knc_ref, krc_ref, vc_ref, o_ref, k_scr, v_scr = refs
    else:
        q_ref, knc_ref, krc_ref, vc_ref, o_ref, k_scr, v_scr = refs
    n_ctx = knc_ref.shape[0]
    ones = jnp.ones((n_lat + n_ctx, MLA_VW - HEAD_DIM), BF16)
    for j in range(hs):
        src = slice(j * HEAD_DIM, (j + 1) * HEAD_DIM)
        nope = slice(j * MLA_QK, j * MLA_QK + HEAD_DIM)
        rot = slice(j * MLA_QK + HEAD_DIM, (j + 1) * MLA_QK)
        val = slice(j * MLA_VW, j * MLA_VW + HEAD_DIM)
        if n_lat:
            k_scr[:n_lat, nope] = kn_ref[:, src]
            k_scr[:n_lat, rot] = kr_ref[...]
            v_scr[:n_lat, val] = v_ref[:, src]
        k_scr[n_lat:, nope] = knc_ref[:, src]
        k_scr[n_lat:, rot] = krc_ref[...]
        v_scr[n_lat:, val] = vc_ref[:, src]
        v_scr[:, j * MLA_VW + HEAD_DIM:(j + 1) * MLA_VW] = ones
    tq = q_ref.shape[0] // chains
    for j in range(hs):
        qk_cols = slice(j * MLA_QK, (j + 1) * MLA_QK)
        v_cols = slice(j * MLA_VW, (j + 1) * MLA_VW)
        for c in range(chains):
            rows = slice(c * tq, (c + 1) * tq)
            s = _dot_nt(q_ref[rows, qk_cols], k_scr[:, qk_cols])
            o_ref[rows, j * HEAD_DIM:(j + 1) * HEAD_DIM] = _softmax_pv([s], [v_scr[:, v_cols]], True).astype(
                o_ref.dtype)


def _mla_scratch(n_keys, hs):
    return [pltpu.VMEM((n_keys, hs * MLA_QK), BF16), pltpu.VMEM((n_keys, hs * MLA_VW), BF16)]


def _mla_latent(q, kn, v, kr, *, hs=2, chains=4):
    lat = pl.BlockSpec((SEQ, hs * HEAD_DIM), lambda b, h: (b, h))
    ctx = pl.BlockSpec((CTX_LEN, hs * HEAD_DIM), lambda b, h: (CTX_BLOCK0 + b, h))
    return pl.pallas_call(
        functools.partial(_mla_kernel, hs=hs, chains=chains, n_lat=SEQ),
        grid=(BATCH, MLA_HEADS // hs),
        in_specs=[
            pl.BlockSpec((SEQ, hs * MLA_QK), lambda b, h: (b, h)),
            lat,
            pl.BlockSpec((SEQ, HEAD_DIM), lambda b, h: (b, 0)),
            lat,
            ctx,
            pl.BlockSpec((CTX_LEN, HEAD_DIM), lambda b, h: (CTX_BLOCK0 + b, 0)),
            ctx,
        ],
        out_specs=pl.BlockSpec((SEQ, hs * HEAD_DIM), lambda b, h: (b, h)),
        out_shape=jax.ShapeDtypeStruct((N_LAT, MLA_HEADS * HEAD_DIM), BF16),
        scratch_shapes=_mla_scratch(SEQ + CTX_LEN, hs),
        compiler_params=_cparams(("parallel", "parallel")),
        name="mla_latent",
    )(q, kn, kr, v, kn, kr, v)


def _mla_ctx(q, kn, v, kr):
    wide = pl.BlockSpec((CTX_LEN, MLA_HEADS * HEAD_DIM), lambda b: (CTX_BLOCK0 + b, 0))
    return pl.pallas_call(
        functools.partial(_mla_kernel, hs=MLA_HEADS, chains=1, n_lat=0),
        grid=(BATCH,),
        in_specs=[
            pl.BlockSpec((CTX_LEN, MLA_HEADS * MLA_QK), lambda b: (CTX_BLOCK0 + b, 0)),
            wide,
            pl.BlockSpec((CTX_LEN, HEAD_DIM), lambda b: (CTX_BLOCK0 + b, 0)),
            wide,
        ],
        out_specs=pl.BlockSpec((CTX_LEN, MLA_HEADS * HEAD_DIM), lambda b: (b, 0)),
        out_shape=jax.ShapeDtypeStruct((N_CTX, MLA_HEADS * HEAD_DIM), BF16),
        scratch_shapes=_mla_scratch(CTX_LEN, MLA_HEADS),
        compiler_params=_cparams(("parallel",)),
        name="mla_ctx",
    )(q, kn, kr, v)


def _na_ctx_kernel(q_ref, k_ref, v_ref, o_ref):
    for h in range(NA_HEADS):
        cols = slice(h * HEAD_DIM, (h + 1) * HEAD_DIM)
        s = _dot_nt(q_ref[:, cols], k_ref[:, cols]) * (NA_SCALE * LOG2E)
        o_ref[:, cols] = _softmax_pv([s], [v_ref[:, cols]], False).astype(o_ref.dtype)


def _na_ctx(p):
    width = NA_HEADS * HEAD_DIM
    col0 = COL_NA // width

    def spec(c):
        return pl.BlockSpec((CTX_LEN, width), lambda b: (CTX_BLOCK0 + b, col0 + c))

    return pl.pallas_call(
        _na_ctx_kernel,
        grid=(BATCH,),
        in_specs=[spec(0), spec(1), spec(2)],
        out_specs=pl.BlockSpec((CTX_LEN, width), lambda b: (b, 0)),
        out_shape=jax.ShapeDtypeStruct((N_CTX, width), BF16),
        compiler_params=_cparams(("parallel",)),
        name="na_ctx",
    )(p, p, p)


def _na_plan():
    rows = SEQ // GRID_W
    invalid = 2 * NA_KH - 1
    pairs, plan, starts = [], [], []
    for t in range(rows // NA_TR):
        kw0 = int(np.clip(NA_TR * t - NA_KH // 2, 0, rows - NA_WR))
        starts.append(kw0)
        tile = []
        for ri in range(NA_TR):
            r = NA_TR * t + ri
            r0 = int(np.clip(r - NA_KH // 2, 0, rows - NA_KH))
            assert kw0 <= r0 and r0 + NA_KH <= kw0 + NA_WR
            row = []
            for kp in range(NA_WR // 2):
                pair = []
                for kr in (kw0 + 2 * kp, kw0 + 2 * kp + 1):
                    pair.append(kr - r + NA_KH - 1 if r0 <= kr < r0 + NA_KH else invalid)
                pair = tuple(pair)
                if pair not in pairs:
                    pairs.append(pair)
                row.append(pairs.index(pair))
            tile.append(row)
        plan.append(tile)
    return starts, plan, pairs


def _na_bias_pairs(rpb, pairs):
    c = np.arange(GRID_W)
    c0 = np.clip(c - NA_KW // 2, 0, GRID_W - NA_KW)
    col_ok = (c[None, :] >= c0[:, None]) & (c[None, :] < c0[:, None] + NA_KW)
    col_idx = np.clip(c[None, :] - c[:, None] + NA_KW - 1, 0, 2 * NA_KW - 2)
    onehot = (col_idx[None] == np.arange(2 * NA_KW - 1)[:, None, None]).astype(np.float32)
    t = jnp.einsum("lhdj,jck->lhdck", rpb.astype(F32), onehot, precision=lax.Precision.HIGHEST) * LOG2E
    t = jnp.where(col_ok, t, MASK_VALUE)
    masked = jnp.full(t.shape[:2] + (GRID_W, GRID_W), MASK_VALUE, F32)
    slabs = [t[:, :, d] for d in range(2 * NA_KH - 1)] + [masked]
    return jnp.stack([jnp.concatenate([slabs[a], slabs[b]], axis=-1) for a, b in pairs], axis=2)


def _na_kernel(q_ref, k_ref, v_ref, kc_ref, vc_ref, t2_ref, o_ref, vo_ref, vco_ref, *, starts, plan):
    ones = jnp.ones((SEQ, HEAD_DIM), BF16)
    vo_ref[:, :HEAD_DIM] = v_ref[...]
    vo_ref[:, HEAD_DIM:] = ones
    vco_ref[:, :HEAD_DIM] = vc_ref[...]
    vco_ref[:, HEAD_DIM:] = ones[:CTX_LEN]
    kc = kc_ref[...]
    vc = vco_ref[...]
    for t, (kw0, tile) in enumerate(zip(starts, plan)):
        q = q_ref[t * NA_TQ:(t + 1) * NA_TQ, :]
        kw = k_ref[kw0 * GRID_W:kw0 * GRID_W + NA_TK, :]
        vw = vo_ref[kw0 * GRID_W:kw0 * GRID_W + NA_TK, :]
        bias = jnp.concatenate(
            [jnp.concatenate([t2_ref[0, 0, idx] for idx in row], axis=1) for row in tile], axis=0)
        s = _dot_nt(q, kw) * (NA_SCALE * LOG2E) + bias
        sc = _dot_nt(q, kc) * (NA_SCALE * LOG2E)
        o_ref[t * NA_TQ:(t + 1) * NA_TQ, :] = _softmax_pv([s, sc], [vw, vc], True).astype(o_ref.dtype)


def _na_latent(p, t2, l, starts, plan):
    cq = COL_NA // HEAD_DIM
    ck = cq + NA_HEADS
    cv = ck + NA_HEADS
    n_pairs = t2.shape[2]
    return pl.pallas_call(
        functools.partial(_na_kernel, starts=starts, plan=plan),
        grid=(NA_HEADS, BATCH),
        in_specs=[
            pl.BlockSpec((SEQ, HEAD_DIM), lambda h, b: (b, cq + h)),
            pl.BlockSpec((SEQ, HEAD_DIM), lambda h, b: (b, ck + h)),
            pl.BlockSpec((SEQ, HEAD_DIM), lambda h, b: (b, cv + h)),
            pl.BlockSpec((CTX_LEN, HEAD_DIM), lambda h, b: (CTX_BLOCK0 + b, ck + h)),
            pl.BlockSpec((CTX_LEN, HEAD_DIM), lambda h, b: (CTX_BLOCK0 + b, cv + h)),
            pl.BlockSpec((1, 1, n_pairs, GRID_W, 2 * GRID_W), lambda h, b: (l, h, 0, 0, 0)),
        ],
        out_specs=pl.BlockSpec((SEQ, HEAD_DIM), lambda h, b: (b, h)),
        out_shape=jax.ShapeDtypeStruct((N_LAT, NA_HEADS * HEAD_DIM), BF16),
        scratch_shapes=[pltpu.VMEM((SEQ, 2 * HEAD_DIM), BF16), pltpu.VMEM((CTX_LEN, 2 * HEAD_DIM), BF16)],
        compiler_params=_cparams(("parallel", "parallel")),
        name="na_latent",
    )(p, p, p, p, p, t2)


def _conv_kernel(v_ref, x1_ref, x2_ref, wv_ref, w1_ref, w2_ref, bv_ref, b1_ref, b2_ref, zin_ref, x2o_ref):
    n = v_ref.shape[0]
    row = lax.broadcasted_iota(jnp.int32, (n, 1), 0)

    def short_conv(p_ref, w_ref, b_ref):
        p = p_ref[...].astype(F32)
        prev = jnp.where(row == 0, 0.0, pltpu.roll(p, 1, 0))
        nxt = jnp.where(row == n - 1, 0.0, pltpu.roll(p, n - 1, 0))
        w = w_ref[0]
        return prev * w[0:1] + p * w[1:2] + nxt * w[2:3] + b_ref[0]

    zin_ref[...] = (short_conv(x1_ref, w1_ref, b1_ref) * short_conv(v_ref, wv_ref, bv_ref)).astype(zin_ref.dtype)
    x2o_ref[...] = short_conv(x2_ref, w2_ref, b2_ref).astype(x2o_ref.dtype)


def _hy_conv(p, conv_w, conv_b, l, n, row_block0, *, tc=256):
    nc = HY_WIDTH // tc

    def seg(s):
        return (pl.BlockSpec((n, tc), lambda b, j: (row_block0 + b, s * nc + j)),
                pl.BlockSpec((1, 3, tc), lambda b, j: (l, 0, s * nc + j)),
                pl.BlockSpec((1, 1, tc), lambda b, j: (l, 0, s * nc + j)))

    (pv, wv, bv), (p1, w1, b1), (p2, w2, b2) = seg(0), seg(1), seg(2)
    out_spec = pl.BlockSpec((n, tc), lambda b, j: (0, b * nc + j))
    out = jax.ShapeDtypeStruct((n, BATCH * HY_WIDTH), BF16)
    conv_b = conv_b.reshape(DEPTH, 1, -1)
    return pl.pallas_call(
        _conv_kernel,
        grid=(BATCH, nc),
        in_specs=[pv, p1, p2, wv, w1, w2, bv, b1, b2],
        out_specs=[out_spec, out_spec],
        out_shape=[out, out],
        compiler_params=_cparams(("parallel", "parallel")),
        name="hy_conv",
    )(p, p, p, conv_w, conv_w, conv_w, conv_b, conv_b, conv_b)


def _filt_kernel(z_ref, t_ref, dl_ref, w1_ref, b1_ref, w2_ref, b2_ref, w3_ref, fr_ref, hs_ref, ha_ref, kn_ref):
    n = z_ref.shape[0]
    fr = fr_ref[...]
    h = jnp.sin(fr * (_dot_hi(z_ref[...], w1_ref[...]) + b1_ref[...]))
    h = jnp.sin(fr * (_dot_hi(h, w2_ref[...]) + b2_ref[...]))
    h = _dot_hi(h, w3_ref[...])
    decay = jnp.exp(-t_ref[...] * dl_ref[...])
    row = lax.broadcasted_iota(jnp.int32, (n, 1), 0)
    hf = h[:, :HY_WIDTH] * decay
    hb = jnp.where(row == 0, 0.0, h[:, HY_WIDTH:] * decay)
    hs = hf + hb
    hs_ref[...] = hs.astype(hs_ref.dtype)
    ha_ref[...] = (hf - hb).astype(ha_ref.dtype)
    sign = jnp.where((row & 1) == 0, 1.0, -1.0)
    kn_ref[...] = jnp.sum(hs * sign, axis=0, keepdims=True)


def _hy_filter_taps(n, f_w1, f_b1, f_w2, f_b2, f_w3, f_freq):
    pos = jnp.arange(n, dtype=F32)
    t = jnp.linspace(0.0, 1.0, n, dtype=F32)
    bands = jnp.linspace(1e-4, HY_POS_BANDS - 1, HY_POS_BANDS, dtype=F32)
    ang = (2.0 * math.pi / n) * pos[:, None] * bands[None, :]
    z = jnp.concatenate([t[:, None], jnp.cos(ang), -jnp.sin(ang)], axis=-1)
    pad = HY_FILTER_HIDDEN - z.shape[1]
    z = jnp.pad(z, ((0, 0), (0, pad)))
    w1 = jnp.pad(f_w1.astype(F32), ((0, pad), (0, 0)))
    deltas = jnp.abs(jnp.linspace(math.log(HY_DECAY_TARGET) / HY_FAST_DECAY,
                                  math.log(HY_DECAY_TARGET) / HY_SLOW_DECAY, HY_WIDTH, dtype=F32))
    hid = HY_FILTER_HIDDEN
    return pl.pallas_call(
        _filt_kernel,
        out_shape=[
            jax.ShapeDtypeStruct((n, HY_WIDTH), BF16),
            jax.ShapeDtypeStruct((n, HY_WIDTH), BF16),
            jax.ShapeDtypeStruct((1, HY_WIDTH), F32),
        ],
        compiler_params=pltpu.CompilerParams(vmem_limit_bytes=VMEM_LIMIT),
        name="hy_filter",
    )(z, t[:, None], deltas[None, :], w1, f_b1.reshape(1, hid), f_w2, f_b2.reshape(1, hid), f_w3,
      f_freq.reshape(1, hid))


def _dft_tables(n):
    lo = 16 if n < 1024 else 32
    hi = n // lo
    t = jnp.arange(n, dtype=jnp.int32)[None, :]
    big = 2 * n

    def ang(ff):
        return ((ff * t) % big).astype(F32) * (2.0 * math.pi / big)

    a = ang(lo * jnp.arange(hi, dtype=jnp.int32)[:, None])
    b = ang(jnp.arange(lo, dtype=jnp.int32)[:, None])
    ca, sa, cb, sb = jnp.cos(a), jnp.sin(a), jnp.cos(b), jnp.sin(b)
    cos = (ca[:, None, :] * cb[None, :, :] - sa[:, None, :] * sb[None, :, :]).reshape(n, n)
    base = -(sa[:, None, :] * cb[None, :, :] + ca[:, None, :] * sb[None, :, :]).reshape(n, n)
    idx = jnp.arange(n)
    alt = jnp.where(idx % 2 == 0, 1.0, -1.0).astype(F32)
    msin = jnp.where(idx[:, None] == 0, alt[None, :], base)
    msin_t = jnp.where(idx[None, :] == 0, alt[:, None], base)
    return cos.astype(BF16), msin.astype(BF16), msin_t.astype(BF16)


def _dft_filt_kernel(c_ref, s_ref, hs_ref, ha_ref, kr_ref, ki_ref):
    kr_ref[...] = _dot(c_ref[...], hs_ref[...])
    ki_ref[...] = _dot(s_ref[...], ha_ref[...])


def _hy_filter_spectrum(cos, msin, hs, ha, *, tf):
    n = cos.shape[0]
    tf = min(tf, n)
    tab = pl.BlockSpec((tf, n), lambda i: (i, 0))
    taps = pl.BlockSpec((n, HY_WIDTH), lambda i: (0, 0))
    out = pl.BlockSpec((tf, HY_WIDTH), lambda i: (i, 0))
    return pl.pallas_call(
        _dft_filt_kernel,
        grid=(n // tf,),
        in_specs=[tab, tab, taps, taps],
        out_specs=[out, out],
        out_shape=[jax.ShapeDtypeStruct((n, HY_WIDTH), F32)] * 2,
        compiler_params=_cparams(("parallel",)),
        name="hy_filter_dft",
    )(cos, msin, hs, ha)


def _batch_cols(b):
    return slice(b * HY_WIDTH, (b + 1) * HY_WIDTH)


def _dft_fwd_kernel(c_ref, s_ref, x_ref, kr_ref, ki_ref, kn_ref, yr_ref, yi_ref, *, tf, inv_n):
    x = x_ref[...]
    zr = _dot(c_ref[...], x)
    zi = _dot(s_ref[...], x)
    row = pl.program_id(0) * tf + lax.broadcasted_iota(jnp.int32, (tf, 1), 0)
    bin0 = row == 0
    wt = jnp.where(bin0, inv_n, 2.0 * inv_n)
    kr = kr_ref[...] * wt
    ki = jnp.where(bin0, 0.0, ki_ref[...] * wt)
    kr_im = jnp.where(bin0, kn_ref[...] * wt, kr)
    for b in range(BATCH):
        cols = _batch_cols(b)
        yr_ref[:, cols] = (zr[:, cols] * kr - zi[:, cols] * ki).astype(yr_ref.dtype)
        yi_ref[:, cols] = (zr[:, cols] * ki + zi[:, cols] * kr_im).astype(yi_ref.dtype)


def _hy_dft_fwd(cos, msin, zin, kr, ki, kn, *, tf):
    n = cos.shape[0]
    tf = min(tf, n)
    tab = pl.BlockSpec((tf, n), lambda i: (i, 0))
    filt = pl.BlockSpec((tf, HY_WIDTH), lambda i: (i, 0))
    out = pl.BlockSpec((tf, BATCH * HY_WIDTH), lambda i: (i, 0))
    whole = pl.BlockSpec((n, BATCH * HY_WIDTH), lambda i: (0, 0), pipeline_mode=pl.Buffered(1))
    return pl.pallas_call(
        functools.partial(_dft_fwd_kernel, tf=tf, inv_n=1.0 / (2 * n)),
        grid=(n // tf,),
        in_specs=[tab, tab, whole, filt, filt, pl.BlockSpec((1, HY_WIDTH), lambda i: (0, 0))],
        out_specs=[out, out],
        out_shape=[jax.ShapeDtypeStruct((n, BATCH * HY_WIDTH), BF16)] * 2,
        compiler_params=_cparams(("parallel",)),
        name="hy_dft_fwd",
    )(cos, msin, zin, kr, ki, kn)


def _dft_inv_kernel(c_ref, st_ref, yr_ref, yi_ref, zin_ref, x2_ref, b_ref, o_ref):
    y = _dot(c_ref[...], yr_ref[...]) + _dot(st_ref[...], yi_ref[...])
    bias = b_ref[0]
    for b in range(BATCH):
        cols = _batch_cols(b)
        yb = y[:, cols] + zin_ref[:, cols].astype(F32) * bias
        o_ref[b] = (x2_ref[:, cols].astype(F32) * yb).astype(o_ref.dtype)


def _hy_dft_inv(cos, msin_t, yr, yi, zin, x2, bias, l, *, tt):
    n = cos.shape[0]
    tt = min(tt, n)
    tab = pl.BlockSpec((tt, n), lambda i: (i, 0))
    whole = pl.BlockSpec((n, BATCH * HY_WIDTH), lambda i: (0, 0), pipeline_mode=pl.Buffered(1))
    rows = pl.BlockSpec((tt, BATCH * HY_WIDTH), lambda i: (i, 0))
    out = pl.pallas_call(
        _dft_inv_kernel,
        grid=(n // tt,),
        in_specs=[tab, tab, whole, whole, rows, rows, pl.BlockSpec((1, 1, HY_WIDTH), lambda i: (l, 0, 0))],
        out_specs=pl.BlockSpec((BATCH, tt, HY_WIDTH), lambda i: (0, i, 0)),
        out_shape=jax.ShapeDtypeStruct((BATCH, n, HY_WIDTH), BF16),
        compiler_params=_cparams(("parallel",)),
        name="hy_dft_inv",
    )(cos, msin_t, yr, yi, zin, x2, bias.reshape(DEPTH, 1, HY_WIDTH))
    return out.reshape(BATCH * n, HY_WIDTH)


def _hyena(p, l, n, row_block0, tables, conv_w, conv_b, f_w1, f_b1, f_w2, f_b2, f_w3, f_freq, bias):
    cos, msin, msin_t = tables
    zin, x2 = _hy_conv(p, conv_w, conv_b, l, n, row_block0)
    hs, ha, kn = _hy_filter_taps(n, f_w1[l], f_b1[l], f_w2[l], f_b2[l], f_w3[l], f_freq[l])
    kr, ki = _hy_filter_spectrum(cos, msin, hs, ha, tf=512)
    yr, yi = _hy_dft_fwd(cos, msin, zin, kr, ki, kn, tf=512)
    return _hy_dft_inv(cos, msin_t, yr, yi, zin, x2, bias, l, tt=512)


def _rope_tables():
    tok = jnp.arange(SEQ)
    row = (tok // GRID_W).astype(F32)
    col = (tok % GRID_W).astype(F32)
    n_freq = MLA_ROPE // 4
    inv = ROPE_THETA ** (-jnp.arange(n_freq, dtype=F32) / n_freq)
    ang = jnp.concatenate([row[:, None] * inv, col[:, None] * inv], axis=-1)
    cos, sin = jnp.cos(ang), jnp.sin(ang)
    half = MLA_ROPE // 2
    zeros = jnp.zeros((SEQ, half), F32)
    rest = HEAD_DIM - MLA_ROPE
    cos_t = jnp.concatenate([cos, cos, jnp.ones((SEQ, rest), F32)], axis=-1)
    sin_a = jnp.concatenate([-sin, zeros, jnp.zeros((SEQ, rest), F32)], axis=-1)
    sin_b = jnp.concatenate([zeros, sin, jnp.zeros((SEQ, rest), F32)], axis=-1)
    ident = jnp.ones((N_CTX, HEAD_DIM), F32)
    none = jnp.zeros((N_CTX, HEAD_DIM), F32)
    return (jnp.concatenate([cos_t, ident]), jnp.concatenate([sin_a, none]), jnp.concatenate([sin_b, none]))


def _layout_w_uq(w):
    w = w.reshape(Q_LORA, MLA_HEADS, HEAD_DIM + MLA_ROPE)
    w = jnp.pad(w, ((0, 0), (0, 0), (0, MLA_QK - HEAD_DIM - MLA_ROPE)))
    return w.reshape(Q_LORA, MLA_HEADS * MLA_QK).astype(BF16)


def kernel(x, c, ctx, c_ctx, w_ada, b_ada, g_attn_pre, g_attn_post, g_ffn_pre, g_ffn_post, w_in, hy_conv_w, hy_conv_b, hy_f_w1, hy_f_b1, hy_f_w2, hy_f_b2, hy_f_w3, hy_f_freq, hy_bias, mla_g_q, mla_w_uq, mla_g_kv, mla_w_ukv, na_rpb, w_out, w_ffn_gate, w_ffn_up, w_ffn_down):
    cc = jnp.concatenate([c, c_ctx[None, :], jnp.zeros((8 - BATCH - 1, D_MODEL), F32)], axis=0)
    mods_all = _ada(cc, w_ada, b_ada)
    mods = [mods_all[l].reshape(8, 1, 6 * D_MODEL) for l in range(DEPTH)]

    rope_tabs = _rope_tables()
    dft_lat = _dft_tables(SEQ)
    dft_ctx = _dft_tables(CTX_LEN)
    na_starts, na_plan, na_pairs = _na_plan()
    na_t2 = _na_bias_pairs(na_rpb, na_pairs)
    w_in_t = jnp.swapaxes(w_in, 1, 2)
    hy_w = (hy_conv_w, hy_conv_b, hy_f_w1, hy_f_b1, hy_f_w2, hy_f_b2, hy_f_w3, hy_f_freq, hy_bias)

    res = (x.reshape(N_LAT, D_MODEL), ctx.reshape(N_CTX, D_MODEL))
    xn = _prenorm(res[0], res[1], g_attn_pre[0], mods[0])

    for l in range(DEPTH):
        ctx_out = l < DEPTH - 1
        m_rows = N_TOK if ctx_out else N_LAT
        p = _in_proj(xn, w_in_t, l)

        q = _q_proj(p, m_rows, mla_g_q[l], _layout_w_uq(mla_w_uq[l]), rope_tabs)
        kn, v, kr = _kv_proj(p, mla_g_kv[l], mla_w_ukv, l, rope_tabs)
        mla = [_mla_latent(q, kn, v, kr), None]
        na = [_na_latent(p, na_t2, l, na_starts, na_plan), None]
        hy = [_hyena(p, l, SEQ, 0, dft_lat, *hy_w), None]

        if ctx_out:
            mla[1] = _mla_ctx(q, kn, v, kr)
            na[1] = _na_ctx(p)
            hy[1] = _hyena(p, l, CTX_LEN, CTX_BLOCK0, dft_ctx, *hy_w)
            res_l = res
        else:
            res_l = (res[0], None)

        stream, xn = _proj_post([tuple(hy), tuple(mla), tuple(na)], w_out, l, g_attn_post[l], mods[l], 2, res_l,
                                m_rows, nxt=(g_ffn_pre[l], mods[l], 3, 4), tm=POST_TM_OUT)
        h = _ffn_up(xn, w_ffn_gate, w_ffn_up, l)
        nxt = (g_attn_pre[l + 1], mods[l + 1], 0, 1) if ctx_out else None
        stream, xn = _proj_post([(h, None)], w_ffn_down, l, g_ffn_post[l], mods[l], 5, (stream, None), m_rows,
                                nxt=nxt, tm=POST_TM_DOWN)
        res = (stream, None)

    return stream.reshape(BATCH, SEQ, D_MODEL)
```

```python
import functools
import math

import jax
import jax.numpy as jnp
import numpy as np
from jax import lax
from jax.experimental import pallas as pl
from jax.experimental.pallas import tpu as pltpu

F32 = jnp.float32
BF16 = jnp.bfloat16

D_MODEL = 2048
BATCH = 4
SEQ = 2048
DEPTH = 2
GRID_W = 64
CTX_LEN = 256
HEAD_DIM = 128
HY_WIDTH = D_MODEL // 4
HY_FILTER_HIDDEN = 64
HY_POS_BANDS = 16
HY_DECAY_TARGET = 1e-2
HY_FAST_DECAY = 0.3
HY_SLOW_DECAY = 1.5
MLA_HEADS = (D_MODEL // 2) // HEAD_DIM
MLA_ROPE = 64
Q_LORA = 3 * D_MODEL // 8
KV_LORA = D_MODEL // 4
MLA_SCALE = (HEAD_DIM + MLA_ROPE) ** -0.5
NA_HEADS = (D_MODEL // 4) // HEAD_DIM
NA_KH = 8
NA_KW = 16
NA_SCALE = HEAD_DIM ** -0.5
FFN_HIDDEN = ((8 * D_MODEL + 3 * 256 - 1) // (3 * 256)) * 256
ROPE_THETA = 10000.0
RMS_EPS = 1e-6
MASK_VALUE = -1e30
LOG2E = math.log2(math.e)

N_LAT = BATCH * SEQ
N_CTX = BATCH * CTX_LEN
N_TOK = N_LAT + N_CTX
CTX_BLOCK0 = N_LAT // CTX_LEN

IN_TN = 512
COL_HY = 0
COL_CQ = 3 * HY_WIDTH
COL_CKV = COL_CQ + Q_LORA
COL_KR = COL_CKV + KV_LORA
W_IN_NA = COL_KR + MLA_ROPE
COL_NA = -(-W_IN_NA // IN_TN) * IN_TN
P_COLS = COL_NA + 3 * NA_HEADS * HEAD_DIM
MLA_QK = 256
MLA_VW = 256

NA_TR = 4
NA_WR = 12
NA_TQ = NA_TR * GRID_W
NA_TK = NA_WR * GRID_W

VMEM_LIMIT = 52 * 1024 * 1024


def _cparams(sem):
    return pltpu.CompilerParams(dimension_semantics=sem, vmem_limit_bytes=VMEM_LIMIT)


def _dot(a, b):
    return jnp.dot(a, b, preferred_element_type=F32)


def _dot_nt(a, b):
    return lax.dot_general(a, b, (((1,), (1,)), ((), ())), preferred_element_type=F32)


def _dot_hi(a, b):
    return jnp.dot(a, b, preferred_element_type=F32, precision=lax.Precision.HIGHEST)


def _rms(x, g):
    ms = jnp.mean(x * x, axis=-1, keepdims=True)
    return x * lax.rsqrt(ms + RMS_EPS) * g


def _mod_row(tm):
    n_lat, per_b = N_LAT // tm, SEQ // tm
    return lambda i: jnp.where(i < n_lat, i // per_b, BATCH)


def _rope_row(tm):
    n_lat, per_b = N_LAT // tm, SEQ // tm
    return lambda i: jnp.where(i < n_lat, i % per_b, per_b + i - n_lat)


def _ada_kernel(c_ref, w_ref, b_ref, o_ref):
    a = c_ref[...]
    a = a * jax.nn.sigmoid(a)
    o_ref[0] = _dot(a.astype(BF16), w_ref[0].astype(BF16)) + b_ref[0]


def _ada(cc, w_ada, b_ada):
    tn = 1024
    n = w_ada.shape[-1]
    return pl.pallas_call(
        _ada_kernel,
        grid=(DEPTH, n // tn),
        in_specs=[
            pl.BlockSpec((8, D_MODEL), lambda l, j: (0, 0)),
            pl.BlockSpec((1, D_MODEL, tn), lambda l, j: (l, 0, j)),
            pl.BlockSpec((1, 1, tn), lambda l, j: (l, 0, j)),
        ],
        out_specs=pl.BlockSpec((1, 8, tn), lambda l, j: (l, 0, j)),
        out_shape=jax.ShapeDtypeStruct((DEPTH, 8, n), F32),
        compiler_params=_cparams(("parallel", "parallel")),
        name="ada",
    )(cc, w_ada, b_ada.reshape(DEPTH, 1, n))


def _norm_mod_to(xn_ref, x_ref, g_ref, sh_ref, sc_ref, chunk=256):
    g = g_ref[...]
    sc = 1.0 + sc_ref[0]
    sh = sh_ref[0]

    def body(r, carry):
        rows = pl.ds(pl.multiple_of(r * chunk, chunk), chunk)
        xn_ref[rows, :] = (_rms(x_ref[rows, :], g) * sc + sh).astype(BF16)
        return carry

    lax.fori_loop(0, x_ref.shape[0] // chunk, body, 0)


def _prenorm_kernel(x_ref, c_ref, g_ref, sh_ref, sc_ref, o_ref, *, n_lat):
    i = pl.program_id(0)

    @pl.when(i < n_lat)
    def _():
        _norm_mod_to(o_ref, x_ref, g_ref, sh_ref, sc_ref)

    @pl.when(i >= n_lat)
    def _():
        _norm_mod_to(o_ref, c_ref, g_ref, sh_ref, sc_ref)


def _prenorm(x2d, ctx2d, g, mods, *, tm=1024):
    n_lat = N_LAT // tm
    row = _mod_row(tm)
    return pl.pallas_call(
        functools.partial(_prenorm_kernel, n_lat=n_lat),
        grid=(N_TOK // tm,),
        in_specs=[
            pl.BlockSpec((tm, D_MODEL), lambda i: (jnp.minimum(i, n_lat - 1), 0)),
            pl.BlockSpec((tm, D_MODEL), lambda i: (jnp.maximum(i - n_lat, 0), 0)),
            pl.BlockSpec((1, D_MODEL), lambda i: (0, 0)),
            pl.BlockSpec((1, 1, D_MODEL), lambda i: (row(i), 0, 0)),
            pl.BlockSpec((1, 1, D_MODEL), lambda i: (row(i), 0, 1)),
        ],
        out_specs=pl.BlockSpec((tm, D_MODEL), lambda i: (i, 0)),
        out_shape=jax.ShapeDtypeStruct((N_TOK, D_MODEL), BF16),
        compiler_params=_cparams(("parallel",)),
        name="prenorm",
    )(x2d, ctx2d, g.reshape(1, D_MODEL), mods, mods)


def _in_kernel(x_ref, w_ref, o_ref):
    o_ref[...] = _dot_nt(x_ref[...], w_ref[0].astype(BF16)).astype(o_ref.dtype)


def _in_proj(xn, w_in_t, l, *, tm=3072):
    n_head = COL_NA // IN_TN

    def w_row(j):
        per = IN_TN // MLA_ROPE
        return MLA_ROPE * jnp.where(j < n_head, j * per, W_IN_NA // MLA_ROPE + (j - n_head) * per)

    return pl.pallas_call(
        _in_kernel,
        grid=(N_TOK // tm, P_COLS // IN_TN),
        in_specs=[
            pl.BlockSpec((tm, D_MODEL), lambda i, j: (i, 0)),
            pl.BlockSpec((pl.Element(1), pl.Element(IN_TN), pl.Element(D_MODEL)), lambda i, j: (l, w_row(j), 0)),
        ],
        out_specs=pl.BlockSpec((tm, IN_TN), lambda i, j: (i, j)),
        out_shape=jax.ShapeDtypeStruct((N_TOK, P_COLS), BF16),
        compiler_params=_cparams(("parallel", "arbitrary")),
        name="in_proj",
    )(xn, w_in_t)


FFN_CW = 256


def _swiglu_kernel(x_ref, wg_ref, wu_ref, o_ref):
    xn = x_ref[...]
    for c in range(o_ref.shape[1] // FFN_CW):
        cols = slice(c * FFN_CW, (c + 1) * FFN_CW)
        gate = _dot(xn, wg_ref[0, :, cols].astype(BF16))
        up = _dot(xn, wu_ref[0, :, cols].astype(BF16))
        o_ref[:, cols] = (gate * jax.nn.sigmoid(gate) * up).astype(o_ref.dtype)


def _ffn_up(xn, wg, wu, l, *, tn=512):
    m = xn.shape[0]
    tm = 2048 if m % 2048 == 0 else 1536
    w_spec = pl.BlockSpec((1, D_MODEL, tn), lambda i, j: (l, 0, j))
    return pl.pallas_call(
        _swiglu_kernel,
        grid=(m // tm, FFN_HIDDEN // tn),
        in_specs=[pl.BlockSpec((tm, D_MODEL), lambda i, j: (i, 0)), w_spec, w_spec],
        out_specs=pl.BlockSpec((tm, tn), lambda i, j: (i, j)),
        out_shape=jax.ShapeDtypeStruct((m, FFN_HIDDEN), BF16),
        compiler_params=_cparams(("parallel", "arbitrary")),
        name="ffn_up",
    )(xn, wg, wu)


POST_TM_OUT = 512
POST_TM_DOWN = 256
POST_SUB = 128
POST_WCH = 256
POST_WSLOTS = 4
POST_SLOTS = 3


def _post_kernel(*refs, widths, two_src, emit_xn, n_lat, l, tm):
    refs = list(refs)
    n_act = len(widths)
    acts_lat = [refs.pop(0) for _ in range(n_act)]
    acts_ctx = [refs.pop(0) for _ in range(n_act)] if two_src else None
    w_hbm, g_ref, gate_ref, res_lat = (refs.pop(0) for _ in range(4))
    res_ctx = refs.pop(0) if two_src else None
    if emit_xn:
        g2_ref, sh_ref, sc_ref = (refs.pop(0) for _ in range(3))
    o_hbm = refs.pop(0)
    xn_ref = refs.pop(0) if emit_xn else None
    w_scr, stage, buf, sem_w, sem_res, sem_out = refs

    i = pl.program_id(0)
    n_i = pl.num_programs(0)
    n_chunks = sum(widths) // POST_WCH
    is_lat = i < n_lat
    is_ctx = jnp.logical_not(is_lat)

    def w_copy(c, slot):
        return pltpu.make_async_copy(w_hbm.at[l, pl.ds(c * POST_WCH, POST_WCH)], stage.at[slot], sem_w.at[slot])

    def res_copy(src, row0, slot):
        return pltpu.make_async_copy(src.at[pl.ds(row0, tm)], buf.at[slot], sem_res.at[slot])

    def res_start(tile, slot):
        if two_src:
            pl.when(tile < n_lat)(lambda: res_copy(res_lat, tile * tm, slot).start())
            pl.when(tile >= n_lat)(lambda: res_copy(res_ctx, (tile - n_lat) * tm, slot).start())
        else:
            res_copy(res_lat, tile * tm, slot).start()

    def out_copy(tile, slot):
        return pltpu.make_async_copy(buf.at[slot], o_hbm.at[pl.ds(tile * tm, tm)], sem_out.at[slot])

    @pl.when(i == 0)
    def _():
        res_start(0, 0)
        for c in range(POST_WSLOTS - 1):
            w_copy(c, c).start()

        def body(c, carry):
            s = c % POST_WSLOTS
            w_copy(c, s).wait()
            ahead = c + POST_WSLOTS - 1

            @pl.when(ahead < n_chunks)
            def _():
                w_copy(ahead, ahead % POST_WSLOTS).start()

            w_scr[pl.ds(pl.multiple_of(c * POST_WCH, POST_WCH), POST_WCH), :] = stage[s].astype(BF16)
            return carry

        lax.fori_loop(0, n_chunks, body, 0)

    slot = i % POST_SLOTS
    nslot = (i + 1) % POST_SLOTS

    @pl.when(i + 1 < n_i)
    def _():
        @pl.when(i >= POST_SLOTS - 1)
        def _():
            out_copy(i + 1 - POST_SLOTS, nslot).wait()

        res_start(i + 1, nslot)

    res_copy(res_lat, 0, slot).wait()

    def compute(act_refs):
        gg = gate_ref[0] * g_ref[...]
        if emit_xn:
            g2s = g2_ref[...] * (1.0 + sc_ref[0])
            sh = sh_ref[0]
        for r in range(tm // POST_SUB):
            rows = pl.ds(r * POST_SUB, POST_SUB)
            y = None
            k0 = 0
            for a_ref, kw in zip(act_refs, widths):
                part = _dot(a_ref[rows, :], w_scr[k0:k0 + kw, :])
                y = part if y is None else y + part
                k0 += kw
            x_new = buf[slot, rows, :] + _rms(y, gg)
            buf[slot, rows, :] = x_new
            if emit_xn:
                xn_ref[rows, :] = (_rms(x_new, g2s) + sh).astype(xn_ref.dtype)

    if two_src:
        pl.when(is_lat)(functools.partial(compute, acts_lat))
        pl.when(is_ctx)(functools.partial(compute, acts_ctx))
    else:
        compute(acts_lat)

    out_copy(i, slot).start()

    @pl.when(i == n_i - 1)
    def _():
        for back in range(POST_SLOTS):
            out_copy(i - back, (i - back) % POST_SLOTS).wait()


def _proj_post(acts, w, l, g, mods, gate_chunk, res, m_rows, nxt=None, *, tm):
    two_src = res[1] is not None
    n_lat = N_LAT // tm
    widths = tuple(a.shape[1] for a, _ in acts)
    k_total = sum(widths)
    emit_xn = nxt is not None
    row = _mod_row(tm)

    def lat_row(i):
        return jnp.minimum(i, n_lat - 1) if two_src else i

    def ctx_row(i):
        return jnp.maximum(i - n_lat, 0)

    def act_specs(rowf):
        return [pl.BlockSpec((tm, kw), lambda i: (rowf(i), 0)) for kw in widths]

    vec_spec = pl.BlockSpec((1, D_MODEL), lambda i: (0, 0))

    def mod_spec(chunk):
        return pl.BlockSpec((1, 1, D_MODEL), lambda i: (row(i), 0, chunk))

    hbm = pl.BlockSpec(memory_space=pl.ANY)
    in_specs = act_specs(lat_row)
    args = [a for a, _ in acts]
    if two_src:
        in_specs += act_specs(ctx_row)
        args += [c for _, c in acts]
    in_specs += [hbm, vec_spec, mod_spec(gate_chunk), hbm]
    args += [w, g.reshape(1, D_MODEL), mods, res[0]]
    if two_src:
        in_specs.append(hbm)
        args.append(res[1])
    out_specs = [hbm]
    out_shape = [jax.ShapeDtypeStruct((m_rows, D_MODEL), F32)]
    if emit_xn:
        g2, mods2, sh_chunk, sc_chunk = nxt
        in_specs += [vec_spec, mod_spec(sh_chunk), mod_spec(sc_chunk)]
        args += [g2.reshape(1, D_MODEL), mods2, mods2]
        out_specs.append(pl.BlockSpec((tm, D_MODEL), lambda i: (i, 0)))
        out_shape.append(jax.ShapeDtypeStruct((m_rows, D_MODEL), BF16))

    out = pl.pallas_call(
        functools.partial(_post_kernel, widths=widths, two_src=two_src, emit_xn=emit_xn, n_lat=n_lat, l=l, tm=tm),
        grid=(m_rows // tm,),
        in_specs=in_specs,
        out_specs=out_specs,
        out_shape=out_shape,
        scratch_shapes=[
            pltpu.VMEM((k_total, D_MODEL), BF16),
            pltpu.VMEM((POST_WSLOTS, POST_WCH, D_MODEL), F32),
            pltpu.VMEM((POST_SLOTS, tm, D_MODEL), F32),
            pltpu.SemaphoreType.DMA((POST_WSLOTS,)),
            pltpu.SemaphoreType.DMA((POST_SLOTS,)),
            pltpu.SemaphoreType.DMA((POST_SLOTS,)),
        ],
        compiler_params=_cparams(("arbitrary",)),
        name="proj_post",
    )(*args)
    return (out[0], out[1]) if emit_xn else (out[0], None)


def _rope128(r, cos_ref, sa_ref, sb_ref):
    return r * cos_ref[...] + pltpu.roll(r, 96, 1) * sa_ref[...] + pltpu.roll(r, 32, 1) * sb_ref[...]


def _q_kernel(x_ref, g_ref, w_ref, cos_ref, sa_ref, sb_ref, o_ref):
    xn = _rms(x_ref[...].astype(F32), g_ref[...] * (MLA_SCALE * LOG2E)).astype(BF16)
    for h in range(MLA_HEADS):
        acc = _dot(xn, w_ref[:, h * MLA_QK:(h + 1) * MLA_QK])
        o_ref[:, h * MLA_QK:h * MLA_QK + HEAD_DIM] = acc[:, :HEAD_DIM].astype(o_ref.dtype)
        o_ref[:, h * MLA_QK + HEAD_DIM:(h + 1) * MLA_QK] = _rope128(
            acc[:, HEAD_DIM:], cos_ref, sa_ref, sb_ref).astype(o_ref.dtype)


def _q_proj(p, m_rows, g, w, tabs, *, tm=1024):
    rope = _rope_row(tm)
    tab_spec = pl.BlockSpec((tm, HEAD_DIM), lambda i: (rope(i), 0))
    return pl.pallas_call(
        _q_kernel,
        grid=(m_rows // tm,),
        in_specs=[
            pl.BlockSpec((tm, Q_LORA), lambda i: (i, COL_CQ // Q_LORA)),
            pl.BlockSpec((1, Q_LORA), lambda i: (0, 0)),
            pl.BlockSpec((Q_LORA, MLA_HEADS * MLA_QK), lambda i: (0, 0)),
            tab_spec, tab_spec, tab_spec,
        ],
        out_specs=pl.BlockSpec((tm, MLA_HEADS * MLA_QK), lambda i: (i, 0)),
        out_shape=jax.ShapeDtypeStruct((m_rows, MLA_HEADS * MLA_QK), BF16),
        compiler_params=_cparams(("parallel",)),
        name="q_proj",
    )(p, g.reshape(1, Q_LORA), w, *tabs)


def _kv_kernel(xa_ref, xb_ref, kr_ref, g_ref, w_ref, cos_ref, sa_ref, sb_ref, k_ref, v_ref, krr_ref):
    half = KV_LORA // 2
    xa = xa_ref[...].astype(F32)
    xb = xb_ref[...].astype(F32)
    ms = (jnp.sum(xa * xa, axis=-1, keepdims=True) + jnp.sum(xb * xb, axis=-1, keepdims=True)) * (1.0 / KV_LORA)
    rs = lax.rsqrt(ms + RMS_EPS)
    g = g_ref[...]
    xna = (xa * rs * g[:, :half]).astype(BF16)
    xnb = (xb * rs * g[:, half:]).astype(BF16)
    lane = lax.broadcasted_iota(jnp.int32, (1, HEAD_DIM), 1)
    krr = _rope128(kr_ref[...].astype(F32), cos_ref, sa_ref, sb_ref)
    krr_ref[...] = jnp.where(lane < MLA_ROPE, krr, 0.0).astype(krr_ref.dtype)
    for h in range(MLA_HEADS):
        cols = slice(h * HEAD_DIM, (h + 1) * HEAD_DIM)
        w = w_ref[0, :, h * 2 * HEAD_DIM:(h + 1) * 2 * HEAD_DIM].astype(BF16)
        acc = _dot(xna, w[:half]) + _dot(xnb, w[half:])
        k_ref[:, cols] = acc[:, :HEAD_DIM].astype(k_ref.dtype)
        v_ref[:, cols] = acc[:, HEAD_DIM:].astype(v_ref.dtype)


def _kv_proj(p, g, w, l, tabs, *, tm=1024):
    m = p.shape[0]
    rope = _rope_row(tm)
    half = KV_LORA // 2
    tab_spec = pl.BlockSpec((tm, HEAD_DIM), lambda i: (rope(i), 0))
    return pl.pallas_call(
        _kv_kernel,
        grid=(m // tm,),
        in_specs=[
            pl.BlockSpec((tm, half), lambda i: (i, COL_CKV // half)),
            pl.BlockSpec((tm, half), lambda i: (i, COL_CKV // half + 1)),
            pl.BlockSpec((tm, HEAD_DIM), lambda i: (i, COL_KR // HEAD_DIM)),
            pl.BlockSpec((1, KV_LORA), lambda i: (0, 0)),
            pl.BlockSpec((1, KV_LORA, MLA_HEADS * 2 * HEAD_DIM), lambda i: (l, 0, 0)),
            tab_spec, tab_spec, tab_spec,
        ],
        out_specs=[
            pl.BlockSpec((tm, MLA_HEADS * HEAD_DIM), lambda i: (i, 0)),
            pl.BlockSpec((tm, MLA_HEADS * HEAD_DIM), lambda i: (i, 0)),
            pl.BlockSpec((tm, HEAD_DIM), lambda i: (i, 0)),
        ],
        out_shape=[
            jax.ShapeDtypeStruct((m, MLA_HEADS * HEAD_DIM), BF16),
            jax.ShapeDtypeStruct((m, MLA_HEADS * HEAD_DIM), BF16),
            jax.ShapeDtypeStruct((m, HEAD_DIM), BF16),
        ],
        compiler_params=_cparams(("parallel",)),
        name="kv_proj",
    )(p, p, p, g.reshape(1, KV_LORA), w, *tabs)


def _softmax_pv(s_list, v_list, ones_col):
    m = jnp.max(s_list[0], axis=-1, keepdims=True)
    for s in s_list[1:]:
        m = jnp.maximum(m, jnp.max(s, axis=-1, keepdims=True))
    acc = None
    den = None
    for s, v in zip(s_list, v_list):
        p = jnp.exp2(s - m)
        if not ones_col:
            d = jnp.sum(p, axis=-1, keepdims=True)
            den = d if den is None else den + d
        o = _dot(p.astype(BF16), v)
        acc = o if acc is None else acc + o
    if ones_col:
        return acc[:, :HEAD_DIM] / acc[:, HEAD_DIM:]
    return acc / den


def _mla_kernel(*refs, hs, chains, n_lat):
    if n_lat:
        q_ref, kn_ref, kr_ref, v_ref, knc_ref, krc_ref, vc_ref, o_ref, k_scr, v_scr = refs
    else:
        q_ref, knc_ref, krc_ref, vc_ref, o_ref, k_scr, v_scr = refs
    n_ctx = knc_ref.shape[0]
    ones = jnp.ones((n_lat + n_ctx, MLA_VW - HEAD_DIM), BF16)
    for j in range(hs):
        src = slice(j * HEAD_DIM, (j + 1) * HEAD_DIM)
        nope = slice(j * MLA_QK, j * MLA_QK + HEAD_DIM)
        rot = slice(j * MLA_QK + HEAD_DIM, (j + 1) * MLA_QK)
        val = slice(j * MLA_VW, j * MLA_VW + HEAD_DIM)
        if n_lat:
            k_scr[:n_lat, nope] = kn_ref[:, src]
            k_scr[:n_lat, rot] = kr_ref[...]
            v_scr[:n_lat, val] = v_ref[:, src]
        k_scr[n_lat:, nope] = knc_ref[:, src]
        k_scr[n_lat:, rot] = krc_ref[...]
        v_scr[n_lat:, val] = vc_ref[:, src]
        v_scr[:, j * MLA_VW + HEAD_DIM:(j + 1) * MLA_VW] = ones
    tq = q_ref.shape[0] // chains
    for j in range(hs):
        qk_cols = slice(j * MLA_QK, (j + 1) * MLA_QK)
        v_cols = slice(j * MLA_VW, (j + 1) * MLA_VW)
        for c in range(chains):
            rows = slice(c * tq, (c + 1) * tq)
            s = _dot_nt(q_ref[rows, qk_cols], k_scr[:, qk_cols])
            o_ref[rows, j * HEAD_DIM:(j + 1) * HEAD_DIM] = _softmax_pv([s], [v_scr[:, v_cols]], True).astype(
                o_ref.dtype)


def _mla_scratch(n_keys, hs):
    return [pltpu.VMEM((n_keys, hs * MLA_QK), BF16), pltpu.VMEM((n_keys, hs * MLA_VW), BF16)]


def _mla_latent(q, kn, v, kr, *, hs=4, chains=4):
    lat = pl.BlockSpec((SEQ, hs * HEAD_DIM), lambda b, h: (b, h))
    ctx = pl.BlockSpec((CTX_LEN, hs * HEAD_DIM), lambda b, h: (CTX_BLOCK0 + b, h))
    return pl.pallas_call(
        functools.partial(_mla_kernel, hs=hs, chains=chains, n_lat=SEQ),
        grid=(BATCH, MLA_HEADS // hs),
        in_specs=[
            pl.BlockSpec((SEQ, hs * MLA_QK), lambda b, h: (b, h)),
            lat,
            pl.BlockSpec((SEQ, HEAD_DIM), lambda b, h: (b, 0)),
            lat,
            ctx,
            pl.BlockSpec((CTX_LEN, HEAD_DIM), lambda b, h: (CTX_BLOCK0 + b, 0)),
            ctx,
        ],
        out_specs=pl.BlockSpec((SEQ, hs * HEAD_DIM), lambda b, h: (b, h)),
        out_shape=jax.ShapeDtypeStruct((N_LAT, MLA_HEADS * HEAD_DIM), BF16),
        scratch_shapes=_mla_scratch(SEQ + CTX_LEN, hs),
        compiler_params=_cparams(("parallel", "parallel")),
        name="mla_latent",
    )(q, kn, kr, v, kn, kr, v)


def _mla_ctx(q, kn, v, kr):
    wide = pl.BlockSpec((CTX_LEN, MLA_HEADS * HEAD_DIM), lambda b: (CTX_BLOCK0 + b, 0))
    return pl.pallas_call(
        functools.partial(_mla_kernel, hs=MLA_HEADS, chains=1, n_lat=0),
        grid=(BATCH,),
        in_specs=[
            pl.BlockSpec((CTX_LEN, MLA_HEADS * MLA_QK), lambda b: (CTX_BLOCK0 + b, 0)),
            wide,
            pl.BlockSpec((CTX_LEN, HEAD_DIM), lambda b: (CTX_BLOCK0 + b, 0)),
            wide,
        ],
        out_specs=pl.BlockSpec((CTX_LEN, MLA_HEADS * HEAD_DIM), lambda b: (b, 0)),
        out_shape=jax.ShapeDtypeStruct((N_CTX, MLA_HEADS * HEAD_DIM), BF16),
        scratch_shapes=_mla_scratch(CTX_LEN, MLA_HEADS),
        compiler_params=_cparams(("parallel",)),
        name="mla_ctx",
    )(q, kn, kr, v)


def _na_ctx_kernel(q_ref, k_ref, v_ref, o_ref):
    for h in range(NA_HEADS):
        cols = slice(h * HEAD_DIM, (h + 1) * HEAD_DIM)
        s = _dot_nt(q_ref[:, cols], k_ref[:, cols]) * (NA_SCALE * LOG2E)
        o_ref[:, cols] = _softmax_pv([s], [v_ref[:, cols]], False).astype(o_ref.dtype)


def _na_ctx(p):
    width = NA_HEADS * HEAD_DIM
    col0 = COL_NA // width

    def spec(c):
        return pl.BlockSpec((CTX_LEN, width), lambda b: (CTX_BLOCK0 + b, col0 + c))

    return pl.pallas_call(
        _na_ctx_kernel,
        grid=(BATCH,),
        in_specs=[spec(0), spec(1), spec(2)],
        out_specs=pl.BlockSpec((CTX_LEN, width), lambda b: (b, 0)),
        out_shape=jax.ShapeDtypeStruct((N_CTX, width), BF16),
        compiler_params=_cparams(("parallel",)),
        name="na_ctx",
    )(p, p, p)


def _na_plan():
    rows = SEQ // GRID_W
    invalid = 2 * NA_KH - 1
    pairs, plan, starts = [], [], []
    for t in range(rows // NA_TR):
        kw0 = int(np.clip(NA_TR * t - NA_KH // 2, 0, rows - NA_WR))
        starts.append(kw0)
        tile = []
        for ri in range(NA_TR):
            r = NA_TR * t + ri
            r0 = int(np.clip(r - NA_KH // 2, 0, rows - NA_KH))
            assert kw0 <= r0 and r0 + NA_KH <= kw0 + NA_WR
            row = []
            for kp in range(NA_WR // 2):
                pair = []
                for kr in (kw0 + 2 * kp, kw0 + 2 * kp + 1):
                    pair.append(kr - r + NA_KH - 1 if r0 <= kr < r0 + NA_KH else invalid)
                pair = tuple(pair)
                if pair not in pairs:
                    pairs.append(pair)
                row.append(pairs.index(pair))
            tile.append(row)
        plan.append(tile)
    return starts, plan, pairs


def _na_bias_pairs(rpb, pairs):
    c = np.arange(GRID_W)
    c0 = np.clip(c - NA_KW // 2, 0, GRID_W - NA_KW)
    col_ok = (c[None, :] >= c0[:, None]) & (c[None, :] < c0[:, None] + NA_KW)
    col_idx = np.clip(c[None, :] - c[:, None] + NA_KW - 1, 0, 2 * NA_KW - 2)
    onehot = (col_idx[None] == np.arange(2 * NA_KW - 1)[:, None, None]).astype(np.float32)
    t = jnp.einsum("lhdj,jck->lhdck", rpb.astype(F32), onehot, precision=lax.Precision.HIGHEST) * LOG2E
    t = jnp.where(col_ok, t, MASK_VALUE)
    masked = jnp.full(t.shape[:2] + (GRID_W, GRID_W), MASK_VALUE, F32)
    slabs = [t[:, :, d] for d in range(2 * NA_KH - 1)] + [masked]
    return jnp.stack([jnp.concatenate([slabs[a], slabs[b]], axis=-1) for a, b in pairs], axis=2)


def _na_kernel(q_ref, k_ref, v_ref, kc_ref, vc_ref, t2_ref, o_ref, vo_ref, vco_ref, *, starts, plan):
    ones = jnp.ones((SEQ, HEAD_DIM), BF16)
    vo_ref[:, :HEAD_DIM] = v_ref[...]
    vo_ref[:, HEAD_DIM:] = ones
    vco_ref[:, :HEAD_DIM] = vc_ref[...]
    vco_ref[:, HEAD_DIM:] = ones[:CTX_LEN]
    kc = kc_ref[...]
    vc = vco_ref[...]
    for t, (kw0, tile) in enumerate(zip(starts, plan)):
        q = q_ref[t * NA_TQ:(t + 1) * NA_TQ, :]
        kw = k_ref[kw0 * GRID_W:kw0 * GRID_W + NA_TK, :]
        vw = vo_ref[kw0 * GRID_W:kw0 * GRID_W + NA_TK, :]
        bias = jnp.concatenate(
            [jnp.concatenate([t2_ref[0, 0, idx] for idx in row], axis=1) for row in tile], axis=0)
        s = _dot_nt(q, kw) * (NA_SCALE * LOG2E) + bias
        sc = _dot_nt(q, kc) * (NA_SCALE * LOG2E)
        o_ref[t * NA_TQ:(t + 1) * NA_TQ, :] = _softmax_pv([s, sc], [vw, vc], True).astype(o_ref.dtype)


def _na_latent(p, t2, l, starts, plan):
    cq = COL_NA // HEAD_DIM
    ck = cq + NA_HEADS
    cv = ck + NA_HEADS
    n_pairs = t2.shape[2]
    return pl.pallas_call(
        functools.partial(_na_kernel, starts=starts, plan=plan),
        grid=(NA_HEADS, BATCH),
        in_specs=[
            pl.BlockSpec((SEQ, HEAD_DIM), lambda h, b: (b, cq + h)),
            pl.BlockSpec((SEQ, HEAD_DIM), lambda h, b: (b, ck + h)),
            pl.BlockSpec((SEQ, HEAD_DIM), lambda h, b: (b, cv + h)),
            pl.BlockSpec((CTX_LEN, HEAD_DIM), lambda h, b: (CTX_BLOCK0 + b, ck + h)),
            pl.BlockSpec((CTX_LEN, HEAD_DIM), lambda h, b: (CTX_BLOCK0 + b, cv + h)),
            pl.BlockSpec((1, 1, n_pairs, GRID_W, 2 * GRID_W), lambda h, b: (l, h, 0, 0, 0)),
        ],
        out_specs=pl.BlockSpec((SEQ, HEAD_DIM), lambda h, b: (b, h)),
        out_shape=jax.ShapeDtypeStruct((N_LAT, NA_HEADS * HEAD_DIM), BF16),
        scratch_shapes=[pltpu.VMEM((SEQ, 2 * HEAD_DIM), BF16), pltpu.VMEM((CTX_LEN, 2 * HEAD_DIM), BF16)],
        compiler_params=_cparams(("parallel", "parallel")),
        name="na_latent",
    )(p, p, p, p, p, t2)


def _conv_kernel(v_ref, x1_ref, x2_ref, wv_ref, w1_ref, w2_ref, bv_ref, b1_ref, b2_ref, zin_ref, x2o_ref):
    n = v_ref.shape[0]
    row = lax.broadcasted_iota(jnp.int32, (n, 1), 0)

    def short_conv(p_ref, w_ref, b_ref):
        p = p_ref[...].astype(F32)
        prev = jnp.where(row == 0, 0.0, pltpu.roll(p, 1, 0))
        nxt = jnp.where(row == n - 1, 0.0, pltpu.roll(p, n - 1, 0))
        w = w_ref[0]
        return prev * w[0:1] + p * w[1:2] + nxt * w[2:3] + b_ref[0]

    zin_ref[...] = (short_conv(x1_ref, w1_ref, b1_ref) * short_conv(v_ref, wv_ref, bv_ref)).astype(zin_ref.dtype)
    x2o_ref[...] = short_conv(x2_ref, w2_ref, b2_ref).astype(x2o_ref.dtype)


def _hy_conv(p, conv_w, conv_b, l, n, row_block0, *, tc=256):
    nc = HY_WIDTH // tc

    def seg(s):
        return (pl.BlockSpec((n, tc), lambda b, j: (row_block0 + b, s * nc + j)),
                pl.BlockSpec((1, 3, tc), lambda b, j: (l, 0, s * nc + j)),
                pl.BlockSpec((1, 1, tc), lambda b, j: (l, 0, s * nc + j)))

    (pv, wv, bv), (p1, w1, b1), (p2, w2, b2) = seg(0), seg(1), seg(2)
    out_spec = pl.BlockSpec((n, tc), lambda b, j: (0, b * nc + j))
    out = jax.ShapeDtypeStruct((n, BATCH * HY_WIDTH), BF16)
    conv_b = conv_b.reshape(DEPTH, 1, -1)
    return pl.pallas_call(
        _conv_kernel,
        grid=(BATCH, nc),
        in_specs=[pv, p1, p2, wv, w1, w2, bv, b1, b2],
        out_specs=[out_spec, out_spec],
        out_shape=[out, out],
        compiler_params=_cparams(("parallel", "parallel")),
        name="hy_conv",
    )(p, p, p, conv_w, conv_w, conv_w, conv_b, conv_b, conv_b)


def _filt_kernel(z_ref, t_ref, dl_ref, w1_ref, b1_ref, w2_ref, b2_ref, w3_ref, fr_ref, hs_ref, ha_ref, kn_ref):
    n = z_ref.shape[0]
    fr = fr_ref[...]
    h = jnp.sin(fr * (_dot_hi(z_ref[...], w1_ref[...]) + b1_ref[...]))
    h = jnp.sin(fr * (_dot_hi(h, w2_ref[...]) + b2_ref[...]))
    h = _dot_hi(h, w3_ref[...])
    decay = jnp.exp(-t_ref[...] * dl_ref[...])
    row = lax.broadcasted_iota(jnp.int32, (n, 1), 0)
    hf = h[:, :HY_WIDTH] * decay
    hb = jnp.where(row == 0, 0.0, h[:, HY_WIDTH:] * decay)
    hs = hf + hb
    hs_ref[...] = hs.astype(hs_ref.dtype)
    ha_ref[...] = (hf - hb).astype(ha_ref.dtype)
    sign = jnp.where((row & 1) == 0, 1.0, -1.0)
    kn_ref[...] = jnp.sum(hs * sign, axis=0, keepdims=True)


def _hy_filter_taps(n, f_w1, f_b1, f_w2, f_b2, f_w3, f_freq):
    pos = jnp.arange(n, dtype=F32)
    t = jnp.linspace(0.0, 1.0, n, dtype=F32)
    bands = jnp.linspace(1e-4, HY_POS_BANDS - 1, HY_POS_BANDS, dtype=F32)
    ang = (2.0 * math.pi / n) * pos[:, None] * bands[None, :]
    z = jnp.concatenate([t[:, None], jnp.cos(ang), -jnp.sin(ang)], axis=-1)
    pad = HY_FILTER_HIDDEN - z.shape[1]
    z = jnp.pad(z, ((0, 0), (0, pad)))
    w1 = jnp.pad(f_w1.astype(F32), ((0, pad), (0, 0)))
    deltas = jnp.abs(jnp.linspace(math.log(HY_DECAY_TARGET) / HY_FAST_DECAY,
                                  math.log(HY_DECAY_TARGET) / HY_SLOW_DECAY, HY_WIDTH, dtype=F32))
    hid = HY_FILTER_HIDDEN
    return pl.pallas_call(
        _filt_kernel,
        out_shape=[
            jax.ShapeDtypeStruct((n, HY_WIDTH), BF16),
            jax.ShapeDtypeStruct((n, HY_WIDTH), BF16),
            jax.ShapeDtypeStruct((1, HY_WIDTH), F32),
        ],
        compiler_params=pltpu.CompilerParams(vmem_limit_bytes=VMEM_LIMIT),
        name="hy_filter",
    )(z, t[:, None], deltas[None, :], w1, f_b1.reshape(1, hid), f_w2, f_b2.reshape(1, hid), f_w3,
      f_freq.reshape(1, hid))


def _dft_tables(n):
    lo = 16 if n < 1024 else 32
    hi = n // lo
    t = jnp.arange(n, dtype=jnp.int32)[None, :]
    big = 2 * n

    def ang(ff):
        return ((ff * t) % big).astype(F32) * (2.0 * math.pi / big)

    a = ang(lo * jnp.arange(hi, dtype=jnp.int32)[:, None])
    b = ang(jnp.arange(lo, dtype=jnp.int32)[:, None])
    ca, sa, cb, sb = jnp.cos(a), jnp.sin(a), jnp.cos(b), jnp.sin(b)
    cos = (ca[:, None, :] * cb[None, :, :] - sa[:, None, :] * sb[None, :, :]).reshape(n, n)
    base = -(sa[:, None, :] * cb[None, :, :] + ca[:, None, :] * sb[None, :, :]).reshape(n, n)
    idx = jnp.arange(n)
    alt = jnp.where(idx % 2 == 0, 1.0, -1.0).astype(F32)
    msin = jnp.where(idx[:, None] == 0, alt[None, :], base)
    msin_t = jnp.where(idx[None, :] == 0, alt[:, None], base)
    return cos.astype(BF16), msin.astype(BF16), msin_t.astype(BF16)


def _dft_filt_kernel(c_ref, s_ref, hs_ref, ha_ref, kr_ref, ki_ref):
    kr_ref[...] = _dot(c_ref[...], hs_ref[...])
    ki_ref[...] = _dot(s_ref[...], ha_ref[...])


def _hy_filter_spectrum(cos, msin, hs, ha, *, tf):
    n = cos.shape[0]
    tf = min(tf, n)
    tab = pl.BlockSpec((tf, n), lambda i: (i, 0))
    taps = pl.BlockSpec((n, HY_WIDTH), lambda i: (0, 0))
    out = pl.BlockSpec((tf, HY_WIDTH), lambda i: (i, 0))
    return pl.pallas_call(
        _dft_filt_kernel,
        grid=(n // tf,),
        in_specs=[tab, tab, taps, taps],
        out_specs=[out, out],
        out_shape=[jax.ShapeDtypeStruct((n, HY_WIDTH), F32)] * 2,
        compiler_params=_cparams(("parallel",)),
        name="hy_filter_dft",
    )(cos, msin, hs, ha)


def _batch_cols(b):
    return slice(b * HY_WIDTH, (b + 1) * HY_WIDTH)


def _dft_fwd_kernel(c_ref, s_ref, x_ref, kr_ref, ki_ref, kn_ref, yr_ref, yi_ref, *, tf, inv_n):
    x = x_ref[...]
    zr = _dot(c_ref[...], x)
    zi = _dot(s_ref[...], x)
    row = pl.program_id(0) * tf + lax.broadcasted_iota(jnp.int32, (tf, 1), 0)
    bin0 = row == 0
    wt = jnp.where(bin0, inv_n, 2.0 * inv_n)
    kr = kr_ref[...] * wt
    ki = jnp.where(bin0, 0.0, ki_ref[...] * wt)
    kr_im = jnp.where(bin0, kn_ref[...] * wt, kr)
    for b in range(BATCH):
        cols = _batch_cols(b)
        yr_ref[:, cols] = (zr[:, cols] * kr - zi[:, cols] * ki).astype(yr_ref.dtype)
        yi_ref[:, cols] = (zr[:, cols] * ki + zi[:, cols] * kr_im).astype(yi_ref.dtype)


def _hy_dft_fwd(cos, msin, zin, kr, ki, kn, *, tf):
    n = cos.shape[0]
    tf = min(tf, n)
    tab = pl.BlockSpec((tf, n), lambda i: (i, 0))
    filt = pl.BlockSpec((tf, HY_WIDTH), lambda i: (i, 0))
    out = pl.BlockSpec((tf, BATCH * HY_WIDTH), lambda i: (i, 0))
    whole = pl.BlockSpec((n, BATCH * HY_WIDTH), lambda i: (0, 0), pipeline_mode=pl.Buffered(1))
    return pl.pallas_call(
        functools.partial(_dft_fwd_kernel, tf=tf, inv_n=1.0 / (2 * n)),
        grid=(n // tf,),
        in_specs=[tab, tab, whole, filt, filt, pl.BlockSpec((1, HY_WIDTH), lambda i: (0, 0))],
        out_specs=[out, out],
        out_shape=[jax.ShapeDtypeStruct((n, BATCH * HY_WIDTH), BF16)] * 2,
        compiler_params=_cparams(("parallel",)),
        name="hy_dft_fwd",
    )(cos, msin, zin, kr, ki, kn)


def _dft_inv_kernel(c_ref, st_ref, yr_ref, yi_ref, zin_ref, x2_ref, b_ref, o_ref):
    y = _dot(c_ref[...], yr_ref[...]) + _dot(st_ref[...], yi_ref[...])
    bias = b_ref[0]
    for b in range(BATCH):
        cols = _batch_cols(b)
        yb = y[:, cols] + zin_ref[:, cols].astype(F32) * bias
        o_ref[b] = (x2_ref[:, cols].astype(F32) * yb).astype(o_ref.dtype)


def _hy_dft_inv(cos, msin_t, yr, yi, zin, x2, bias, l, *, tt):
    n = cos.shape[0]
    tt = min(tt, n)
    tab = pl.BlockSpec((tt, n), lambda i: (i, 0))
    whole = pl.BlockSpec((n, BATCH * HY_WIDTH), lambda i: (0, 0), pipeline_mode=pl.Buffered(1))
    rows = pl.BlockSpec((tt, BATCH * HY_WIDTH), lambda i: (i, 0))
    out = pl.pallas_call(
        _dft_inv_kernel,
        grid=(n // tt,),
        in_specs=[tab, tab, whole, whole, rows, rows, pl.BlockSpec((1, 1, HY_WIDTH), lambda i: (l, 0, 0))],
        out_specs=pl.BlockSpec((BATCH, tt, HY_WIDTH), lambda i: (0, i, 0)),
        out_shape=jax.ShapeDtypeStruct((BATCH, n, HY_WIDTH), BF16),
        compiler_params=_cparams(("parallel",)),
        name="hy_dft_inv",
    )(cos, msin_t, yr, yi, zin, x2, bias.reshape(DEPTH, 1, HY_WIDTH))
    return out.reshape(BATCH * n, HY_WIDTH)


def _hyena(p, l, n, row_block0, tables, conv_w, conv_b, f_w1, f_b1, f_w2, f_b2, f_w3, f_freq, bias):
    cos, msin, msin_t = tables
    zin, x2 = _hy_conv(p, conv_w, conv_b, l, n, row_block0)
    hs, ha, kn = _hy_filter_taps(n, f_w1[l], f_b1[l], f_w2[l], f_b2[l], f_w3[l], f_freq[l])
    kr, ki = _hy_filter_spectrum(cos, msin, hs, ha, tf=512)
    yr, yi = _hy_dft_fwd(cos, msin, zin, kr, ki, kn, tf=512)
    return _hy_dft_inv(cos, msin_t, yr, yi, zin, x2, bias, l, tt=512)


def _rope_tables():
    tok = jnp.arange(SEQ)
    row = (tok // GRID_W).astype(F32)
    col = (tok % GRID_W).astype(F32)
    n_freq = MLA_ROPE // 4
    inv = ROPE_THETA ** (-jnp.arange(n_freq, dtype=F32) / n_freq)
    ang = jnp.concatenate([row[:, None] * inv, col[:, None] * inv], axis=-1)
    cos, sin = jnp.cos(ang), jnp.sin(ang)
    half = MLA_ROPE // 2
    zeros = jnp.zeros((SEQ, half), F32)
    rest = HEAD_DIM - MLA_ROPE
    cos_t = jnp.concatenate([cos, cos, jnp.ones((SEQ, rest), F32)], axis=-1)
    sin_a = jnp.concatenate([-sin, zeros, jnp.zeros((SEQ, rest), F32)], axis=-1)
    sin_b = jnp.concatenate([zeros, sin, jnp.zeros((SEQ, rest), F32)], axis=-1)
    ident = jnp.ones((N_CTX, HEAD_DIM), F32)
    none = jnp.zeros((N_CTX, HEAD_DIM), F32)
    return (jnp.concatenate([cos_t, ident]), jnp.concatenate([sin_a, none]), jnp.concatenate([sin_b, none]))


def _layout_w_uq(w):
    w = w.reshape(Q_LORA, MLA_HEADS, HEAD_DIM + MLA_ROPE)
    w = jnp.pad(w, ((0, 0), (0, 0), (0, MLA_QK - HEAD_DIM - MLA_ROPE)))
    return w.reshape(Q_LORA, MLA_HEADS * MLA_QK).astype(BF16)


def kernel(x, c, ctx, c_ctx, w_ada, b_ada, g_attn_pre, g_attn_post, g_ffn_pre, g_ffn_post, w_in, hy_conv_w, hy_conv_b, hy_f_w1, hy_f_b1, hy_f_w2, hy_f_b2, hy_f_w3, hy_f_freq, hy_bias, mla_g_q, mla_w_uq, mla_g_kv, mla_w_ukv, na_rpb, w_out, w_ffn_gate, w_ffn_up, w_ffn_down):
    cc = jnp.concatenate([c, c_ctx[None, :], jnp.zeros((8 - BATCH - 1, D_MODEL), F32)], axis=0)
    mods_all = _ada(cc, w_ada, b_ada)
    mods = [mods_all[l].reshape(8, 1, 6 * D_MODEL) for l in range(DEPTH)]

    rope_tabs = _rope_tables()
    dft_lat = _dft_tables(SEQ)
    dft_ctx = _dft_tables(CTX_LEN)
    na_starts, na_plan, na_pairs = _na_plan()
    na_t2 = _na_bias_pairs(na_rpb, na_pairs)
    w_in_t = jnp.swapaxes(w_in, 1, 2)
    hy_w = (hy_conv_w, hy_conv_b, hy_f_w1, hy_f_b1, hy_f_w2, hy_f_b2, hy_f_w3, hy_f_freq, hy_bias)

    res = (x.reshape(N_LAT, D_MODEL), ctx.reshape(N_CTX, D_MODEL))
    xn = _prenorm(res[0], res[1], g_attn_pre[0], mods[0])

    for l in range(DEPTH):
        ctx_out = l < DEPTH - 1
        m_rows = N_TOK if ctx_out else N_LAT
        p = _in_proj(xn, w_in_t, l)

        q = _q_proj(p, m_rows, mla_g_q[l], _layout_w_uq(mla_w_uq[l]), rope_tabs)
        kn, v, kr = _kv_proj(p, mla_g_kv[l], mla_w_ukv, l, rope_tabs)
        mla = [_mla_latent(q, kn, v, kr), None]
        na = [_na_latent(p, na_t2, l, na_starts, na_plan), None]
        hy = [_hyena(p, l, SEQ, 0, dft_lat, *hy_w), None]

        if ctx_out:
            mla[1] = _mla_ctx(q, kn, v, kr)
            na[1] = _na_ctx(p)
            hy[1] = _hyena(p, l, CTX_LEN, CTX_BLOCK0, dft_ctx, *hy_w)
            res_l = res
        else:
            res_l = (res[0], None)

        stream, xn = _proj_post([tuple(hy), tuple(mla), tuple(na)], w_out, l, g_attn_post[l], mods[l], 2, res_l,
                                m_rows, nxt=(g_ffn_pre[l], mods[l], 3, 4), tm=POST_TM_OUT)
        h = _ffn_up(xn, w_ffn_gate, w_ffn_up, l)
        nxt = (g_attn_pre[l + 1], mods[l + 1], 0, 1) if ctx_out else None
        stream, xn = _proj_post([(h, None)], w_ffn_down, l, g_ffn_post[l], mods[l], 5, (stream, None), m_rows,
                                nxt=nxt, tm=POST_TM_DOWN)
        res = (stream, None)

    return stream.reshape(BATCH, SEQ, D_MODEL)
```

```python
import functools
import math

import jax
import jax.numpy as jnp
import numpy as np
from jax import lax
from jax.experimental import pallas as pl
from jax.experimental.pallas import tpu as pltpu

F32 = jnp.float32
BF16 = jnp.bfloat16

D_MODEL = 2048
BATCH = 4
SEQ = 2048
DEPTH = 2
GRID_W = 64
CTX_LEN = 256
HEAD_DIM = 128
HY_WIDTH = D_MODEL // 4
HY_FILTER_HIDDEN = 64
HY_POS_BANDS = 16
HY_DECAY_TARGET = 1e-2
HY_FAST_DECAY = 0.3
HY_SLOW_DECAY = 1.5
MLA_HEADS = (D_MODEL // 2) // HEAD_DIM
MLA_ROPE = 64
Q_LORA = 3 * D_MODEL // 8
KV_LORA = D_MODEL // 4
MLA_SCALE = (HEAD_DIM + MLA_ROPE) ** -0.5
NA_HEADS = (D_MODEL // 4) // HEAD_DIM
NA_KH = 8
NA_KW = 16
NA_SCALE = HEAD_DIM ** -0.5
FFN_HIDDEN = ((8 * D_MODEL + 3 * 256 - 1) // (3 * 256)) * 256
ROPE_THETA = 10000.0
RMS_EPS = 1e-6
MASK_VALUE = -1e30
LOG2E = math.log2(math.e)

N_LAT = BATCH * SEQ
N_CTX = BATCH * CTX_LEN
N_TOK = N_LAT + N_CTX
CTX_BLOCK0 = N_LAT // CTX_LEN

IN_TN = 512
COL_HY = 0
COL_CQ = 3 * HY_WIDTH
COL_CKV = COL_CQ + Q_LORA
COL_KR = COL_CKV + KV_LORA
W_IN_NA = COL_KR + MLA_ROPE
COL_NA = -(-W_IN_NA // IN_TN) * IN_TN
P_COLS = COL_NA + 3 * NA_HEADS * HEAD_DIM
MLA_QK = 256
MLA_VW = 256

NA_TR = 4
NA_WR = 12
NA_TQ = NA_TR * GRID_W
NA_TK = NA_WR * GRID_W

VMEM_LIMIT = 52 * 1024 * 1024


def _cparams(sem):
    return pltpu.CompilerParams(dimension_semantics=sem, vmem_limit_bytes=VMEM_LIMIT)


def _dot(a, b):
    return jnp.dot(a, b, preferred_element_type=F32)


def _dot_nt(a, b):
    return lax.dot_general(a, b, (((1,), (1,)), ((), ())), preferred_element_type=F32)


def _dot_hi(a, b):
    return jnp.dot(a, b, preferred_element_type=F32, precision=lax.Precision.HIGHEST)


def _rms(x, g):
    ms = jnp.mean(x * x, axis=-1, keepdims=True)
    return x * lax.rsqrt(ms + RMS_EPS) * g


def _mod_row(tm):
    n_lat, per_b = N_LAT // tm, SEQ // tm
    return lambda i: jnp.where(i < n_lat, i // per_b, BATCH)


def _rope_row(tm):
    n_lat, per_b = N_LAT // tm, SEQ // tm
    return lambda i: jnp.where(i < n_lat, i % per_b, per_b + i - n_lat)


def _ada_kernel(c_ref, w_ref, b_ref, o_ref):
    a = c_ref[...]
    a = a * jax.nn.sigmoid(a)
    o_ref[0] = _dot(a.astype(BF16), w_ref[0].astype(BF16)) + b_ref[0]


def _ada(cc, w_ada, b_ada):
    tn = 1024
    n = w_ada.shape[-1]
    return pl.pallas_call(
        _ada_kernel,
        grid=(DEPTH, n // tn),
        in_specs=[
            pl.BlockSpec((8, D_MODEL), lambda l, j: (0, 0)),
            pl.BlockSpec((1, D_MODEL, tn), lambda l, j: (l, 0, j)),
            pl.BlockSpec((1, 1, tn), lambda l, j: (l, 0, j)),
        ],
        out_specs=pl.BlockSpec((1, 8, tn), lambda l, j: (l, 0, j)),
        out_shape=jax.ShapeDtypeStruct((DEPTH, 8, n), F32),
        compiler_params=_cparams(("parallel", "parallel")),
        name="ada",
    )(cc, w_ada, b_ada.reshape(DEPTH, 1, n))


def _norm_mod_to(xn_ref, x_ref, g_ref, sh_ref, sc_ref, chunk=256):
    g = g_ref[...]
    sc = 1.0 + sc_ref[0]
    sh = sh_ref[0]

    def body(r, carry):
        rows = pl.ds(pl.multiple_of(r * chunk, chunk), chunk)
        xn_ref[rows, :] = (_rms(x_ref[rows, :], g) * sc + sh).astype(BF16)
        return carry

    lax.fori_loop(0, x_ref.shape[0] // chunk, body, 0)


def _prenorm_kernel(x_ref, c_ref, g_ref, sh_ref, sc_ref, o_ref, *, n_lat):
    i = pl.program_id(0)

    @pl.when(i < n_lat)
    def _():
        _norm_mod_to(o_ref, x_ref, g_ref, sh_ref, sc_ref)

    @pl.when(i >= n_lat)
    def _():
        _norm_mod_to(o_ref, c_ref, g_ref, sh_ref, sc_ref)


def _prenorm(x2d, ctx2d, g, mods, *, tm=1024):
    n_lat = N_LAT // tm
    row = _mod_row(tm)
    return pl.pallas_call(
        functools.partial(_prenorm_kernel, n_lat=n_lat),
        grid=(N_TOK // tm,),
        in_specs=[
            pl.BlockSpec((tm, D_MODEL), lambda i: (jnp.minimum(i, n_lat - 1), 0)),
            pl.BlockSpec((tm, D_MODEL), lambda i: (jnp.maximum(i - n_lat, 0), 0)),
            pl.BlockSpec((1, D_MODEL), lambda i: (0, 0)),
            pl.BlockSpec((1, 1, D_MODEL), lambda i: (row(i), 0, 0)),
            pl.BlockSpec((1, 1, D_MODEL), lambda i: (row(i), 0, 1)),
        ],
        out_specs=pl.BlockSpec((tm, D_MODEL), lambda i: (i, 0)),
        out_shape=jax.ShapeDtypeStruct((N_TOK, D_MODEL), BF16),
        compiler_params=_cparams(("parallel",)),
        name="prenorm",
    )(x2d, ctx2d, g.reshape(1, D_MODEL), mods, mods)


def _in_kernel(x_ref, w_ref, o_ref):
    o_ref[...] = _dot_nt(x_ref[...], w_ref[0].astype(BF16)).astype(o_ref.dtype)


def _in_proj(xn, w_in_t, l, *, tm=3072):
    n_head = COL_NA // IN_TN

    def w_row(j):
        per = IN_TN // MLA_ROPE
        return MLA_ROPE * jnp.where(j < n_head, j * per, W_IN_NA // MLA_ROPE + (j - n_head) * per)

    return pl.pallas_call(
        _in_kernel,
        grid=(N_TOK // tm, P_COLS // IN_TN),
        in_specs=[
            pl.BlockSpec((tm, D_MODEL), lambda i, j: (i, 0)),
            pl.BlockSpec((pl.Element(1), pl.Element(IN_TN), pl.Element(D_MODEL)), lambda i, j: (l, w_row(j), 0)),
        ],
        out_specs=pl.BlockSpec((tm, IN_TN), lambda i, j: (i, j)),
        out_shape=jax.ShapeDtypeStruct((N_TOK, P_COLS), BF16),
        compiler_params=_cparams(("parallel", "arbitrary")),
        name="in_proj",
    )(xn, w_in_t)


FFN_CW = 256


def _swiglu_kernel(x_ref, wg_ref, wu_ref, o_ref):
    xn = x_ref[...]
    for c in range(o_ref.shape[1] // FFN_CW):
        cols = slice(c * FFN_CW, (c + 1) * FFN_CW)
        gate = _dot(xn, wg_ref[0, :, cols].astype(BF16))
        up = _dot(xn, wu_ref[0, :, cols].astype(BF16))
        o_ref[:, cols] = (gate * jax.nn.sigmoid(gate) * up).astype(o_ref.dtype)


def _ffn_up(xn, wg, wu, l, *, tn=512):
    m = xn.shape[0]
    tm = 2048 if m % 2048 == 0 else 1536
    w_spec = pl.BlockSpec((1, D_MODEL, tn), lambda i, j: (l, 0, j))
    return pl.pallas_call(
        _swiglu_kernel,
        grid=(m // tm, FFN_HIDDEN // tn),
        in_specs=[pl.BlockSpec((tm, D_MODEL), lambda i, j: (i, 0)), w_spec, w_spec],
        out_specs=pl.BlockSpec((tm, tn), lambda i, j: (i, j)),
        out_shape=jax.ShapeDtypeStruct((m, FFN_HIDDEN), BF16),
        compiler_params=_cparams(("parallel", "arbitrary")),
        name="ffn_up",
    )(xn, wg, wu)


POST_TM_OUT = 512
POST_TM_DOWN = 256
POST_SUB = 128
POST_WCH = 256
POST_WSLOTS = 4
POST_SLOTS = 3


def _post_kernel(*refs, widths, two_src, emit_xn, n_lat, l, tm):
    refs = list(refs)
    n_act = len(widths)
    acts_lat = [refs.pop(0) for _ in range(n_act)]
    acts_ctx = [refs.pop(0) for _ in range(n_act)] if two_src else None
    w_hbm, g_ref, gate_ref, res_lat = (refs.pop(0) for _ in range(4))
    res_ctx = refs.pop(0) if two_src else None
    if emit_xn:
        g2_ref, sh_ref, sc_ref = (refs.pop(0) for _ in range(3))
    o_hbm = refs.pop(0)
    xn_ref = refs.pop(0) if emit_xn else None
    w_scr, stage, buf, sem_w, sem_res, sem_out = refs

    i = pl.program_id(0)
    n_i = pl.num_programs(0)
    n_chunks = sum(widths) // POST_WCH
    is_lat = i < n_lat
    is_ctx = jnp.logical_not(is_lat)

    def w_copy(c, slot):
        return pltpu.make_async_copy(w_hbm.at[l, pl.ds(c * POST_WCH, POST_WCH)], stage.at[slot], sem_w.at[slot])

    def res_copy(src, row0, slot):
        return pltpu.make_async_copy(src.at[pl.ds(row0, tm)], buf.at[slot], sem_res.at[slot])

    def res_start(tile, slot):
        if two_src:
            pl.when(tile < n_lat)(lambda: res_copy(res_lat, tile * tm, slot).start())
            pl.when(tile >= n_lat)(lambda: res_copy(res_ctx, (tile - n_lat) * tm, slot).start())
        else:
            res_copy(res_lat, tile * tm, slot).start()

    def out_copy(tile, slot):
        return pltpu.make_async_copy(buf.at[slot], o_hbm.at[pl.ds(tile * tm, tm)], sem_out.at[slot])

    @pl.when(i == 0)
    def _():
        res_start(0, 0)
        for c in range(POST_WSLOTS - 1):
            w_copy(c, c).start()

        def body(c, carry):
            s = c % POST_WSLOTS
            w_copy(c, s).wait()
            ahead = c + POST_WSLOTS - 1

            @pl.when(ahead < n_chunks)
            def _():
                w_copy(ahead, ahead % POST_WSLOTS).start()

            w_scr[pl.ds(pl.multiple_of(c * POST_WCH, POST_WCH), POST_WCH), :] = stage[s].astype(BF16)
            return carry

        lax.fori_loop(0, n_chunks, body, 0)

    slot = i % POST_SLOTS
    nslot = (i + 1) % POST_SLOTS

    @pl.when(i + 1 < n_i)
    def _():
        @pl.when(i >= POST_SLOTS - 1)
        def _():
            out_copy(i + 1 - POST_SLOTS, nslot).wait()

        res_start(i + 1, nslot)

    res_copy(res_lat, 0, slot).wait()

    def compute(act_refs):
        gg = gate_ref[0] * g_ref[...]
        if emit_xn:
            g2s = g2_ref[...] * (1.0 + sc_ref[0])
            sh = sh_ref[0]
        for r in range(tm // POST_SUB):
            rows = pl.ds(r * POST_SUB, POST_SUB)
            y = None
            k0 = 0
            for a_ref, kw in zip(act_refs, widths):
                part = _dot(a_ref[rows, :], w_scr[k0:k0 + kw, :])
                y = part if y is None else y + part
                k0 += kw
            x_new = buf[slot, rows, :] + _rms(y, gg)
            buf[slot, rows, :] = x_new
            if emit_xn:
                xn_ref[rows, :] = (_rms(x_new, g2s) + sh).astype(xn_ref.dtype)

    if two_src:
        pl.when(is_lat)(functools.partial(compute, acts_lat))
        pl.when(is_ctx)(functools.partial(compute, acts_ctx))
    else:
        compute(acts_lat)

    out_copy(i, slot).start()

    @pl.when(i == n_i - 1)
    def _():
        for back in range(POST_SLOTS):
            out_copy(i - back, (i - back) % POST_SLOTS).wait()


def _proj_post(acts, w, l, g, mods, gate_chunk, res, m_rows, nxt=None, *, tm):
    two_src = res[1] is not None
    n_lat = N_LAT // tm
    widths = tuple(a.shape[1] for a, _ in acts)
    k_total = sum(widths)
    emit_xn = nxt is not None
    row = _mod_row(tm)

    def lat_row(i):
        return jnp.minimum(i, n_lat - 1) if two_src else i

    def ctx_row(i):
        return jnp.maximum(i - n_lat, 0)

    def act_specs(rowf):
        return [pl.BlockSpec((tm, kw), lambda i: (rowf(i), 0)) for kw in widths]

    vec_spec = pl.BlockSpec((1, D_MODEL), lambda i: (0, 0))

    def mod_spec(chunk):
        return pl.BlockSpec((1, 1, D_MODEL), lambda i: (row(i), 0, chunk))

    hbm = pl.BlockSpec(memory_space=pl.ANY)
    in_specs = act_specs(lat_row)
    args = [a for a, _ in acts]
    if two_src:
        in_specs += act_specs(ctx_row)
        args += [c for _, c in acts]
    in_specs += [hbm, vec_spec, mod_spec(gate_chunk), hbm]
    args += [w, g.reshape(1, D_MODEL), mods, res[0]]
    if two_src:
        in_specs.append(hbm)
        args.append(res[1])
    out_specs = [hbm]
    out_shape = [jax.ShapeDtypeStruct((m_rows, D_MODEL), F32)]
    if emit_xn:
        g2, mods2, sh_chunk, sc_chunk = nxt
        in_specs += [vec_spec, mod_spec(sh_chunk), mod_spec(sc_chunk)]
        args += [g2.reshape(1, D_MODEL), mods2, mods2]
        out_specs.append(pl.BlockSpec((tm, D_MODEL), lambda i: (i, 0)))
        out_shape.append(jax.ShapeDtypeStruct((m_rows, D_MODEL), BF16))

    out = pl.pallas_call(
        functools.partial(_post_kernel, widths=widths, two_src=two_src, emit_xn=emit_xn, n_lat=n_lat, l=l, tm=tm),
        grid=(m_rows // tm,),
        in_specs=in_specs,
        out_specs=out_specs,
        out_shape=out_shape,
        scratch_shapes=[
            pltpu.VMEM((k_total, D_MODEL), BF16),
            pltpu.VMEM((POST_WSLOTS, POST_WCH, D_MODEL), F32),
            pltpu.VMEM((POST_SLOTS, tm, D_MODEL), F32),
            pltpu.SemaphoreType.DMA((POST_WSLOTS,)),
            pltpu.SemaphoreType.DMA((POST_SLOTS,)),
            pltpu.SemaphoreType.DMA((POST_SLOTS,)),
        ],
        compiler_params=_cparams(("arbitrary",)),
        name="proj_post",
    )(*args)
    return (out[0], out[1]) if emit_xn else (out[0], None)


def _rope128(r, cos_ref, sa_ref, sb_ref):
    return r * cos_ref[...] + pltpu.roll(r, 96, 1) * sa_ref[...] + pltpu.roll(r, 32, 1) * sb_ref[...]


def _q_kernel(x_ref, g_ref, w_ref, cos_ref, sa_ref, sb_ref, o_ref):
    xn = _rms(x_ref[...].astype(F32), g_ref[...] * (MLA_SCALE * LOG2E)).astype(BF16)
    for h in range(MLA_HEADS):
        acc = _dot(xn, w_ref[:, h * MLA_QK:(h + 1) * MLA_QK])
        o_ref[:, h * MLA_QK:h * MLA_QK + HEAD_DIM] = acc[:, :HEAD_DIM].astype(o_ref.dtype)
        o_ref[:, h * MLA_QK + HEAD_DIM:(h + 1) * MLA_QK] = _rope128(
            acc[:, HEAD_DIM:], cos_ref, sa_ref, sb_ref).astype(o_ref.dtype)


def _q_proj(p, m_rows, g, w, tabs, *, tm=1024):
    rope = _rope_row(tm)
    tab_spec = pl.BlockSpec((tm, HEAD_DIM), lambda i: (rope(i), 0))
    return pl.pallas_call(
        _q_kernel,
        grid=(m_rows // tm,),
        in_specs=[
            pl.BlockSpec((tm, Q_LORA), lambda i: (i, COL_CQ // Q_LORA)),
            pl.BlockSpec((1, Q_LORA), lambda i: (0, 0)),
            pl.BlockSpec((Q_LORA, MLA_HEADS * MLA_QK), lambda i: (0, 0)),
            tab_spec, tab_spec, tab_spec,
        ],
        out_specs=pl.BlockSpec((tm, MLA_HEADS * MLA_QK), lambda i: (i, 0)),
        out_shape=jax.ShapeDtypeStruct((m_rows, MLA_HEADS * MLA_QK), BF16),
        compiler_params=_cparams(("parallel",)),
        name="q_proj",
    )(p, g.reshape(1, Q_LORA), w, *tabs)


def _kv_kernel(xa_ref, xb_ref, kr_ref, g_ref, w_ref, cos_ref, sa_ref, sb_ref, k_ref, v_ref, krr_ref):
    half = KV_LORA // 2
    xa = xa_ref[...].astype(F32)
    xb = xb_ref[...].astype(F32)
    ms = (jnp.sum(xa * xa, axis=-1, keepdims=True) + jnp.sum(xb * xb, axis=-1, keepdims=True)) * (1.0 / KV_LORA)
    rs = lax.rsqrt(ms + RMS_EPS)
    g = g_ref[...]
    xna = (xa * rs * g[:, :half]).astype(BF16)
    xnb = (xb * rs * g[:, half:]).astype(BF16)
    lane = lax.broadcasted_iota(jnp.int32, (1, HEAD_DIM), 1)
    krr = _rope128(kr_ref[...].astype(F32), cos_ref, sa_ref, sb_ref)
    krr_ref[...] = jnp.where(lane < MLA_ROPE, krr, 0.0).astype(krr_ref.dtype)
    for h in range(MLA_HEADS):
        cols = slice(h * HEAD_DIM, (h + 1) * HEAD_DIM)
        w = w_ref[0, :, h * 2 * HEAD_DIM:(h + 1) * 2 * HEAD_DIM].astype(BF16)
        acc = _dot(xna, w[:half]) + _dot(xnb, w[half:])
        k_ref[:, cols] = acc[:, :HEAD_DIM].astype(k_ref.dtype)
        v_ref[:, cols] = acc[:, HEAD_DIM:].astype(v_ref.dtype)


def _kv_proj(p, g, w, l, tabs, *, tm=1024):
    m = p.shape[0]
    rope = _rope_row(tm)
    half = KV_LORA // 2
    tab_spec = pl.BlockSpec((tm, HEAD_DIM), lambda i: (rope(i), 0))
    return pl.pallas_call(
        _kv_kernel,
        grid=(m // tm,),
        in_specs=[
            pl.BlockSpec((tm, half), lambda i: (i, COL_CKV // half)),
            pl.BlockSpec((tm, half), lambda i: (i, COL_CKV // half + 1)),
            pl.BlockSpec((tm, HEAD_DIM), lambda i: (i, COL_KR // HEAD_DIM)),
            pl.BlockSpec((1, KV_LORA), lambda i: (0, 0)),
            pl.BlockSpec((1, KV_LORA, MLA_HEADS * 2 * HEAD_DIM), lambda i: (l, 0, 0)),
            tab_spec, tab_spec, tab_spec,
        ],
        out_specs=[
            pl.BlockSpec((tm, MLA_HEADS * HEAD_DIM), lambda i: (i, 0)),
            pl.BlockSpec((tm, MLA_HEADS * HEAD_DIM), lambda i: (i, 0)),
            pl.BlockSpec((tm, HEAD_DIM), lambda i: (i, 0)),
        ],
        out_shape=[
            jax.ShapeDtypeStruct((m, MLA_HEADS * HEAD_DIM), BF16),
            jax.ShapeDtypeStruct((m, MLA_HEADS * HEAD_DIM), BF16),
            jax.ShapeDtypeStruct((m, HEAD_DIM), BF16),
        ],
        compiler_params=_cparams(("parallel",)),
        name="kv_proj",
    )(p, p, p, g.reshape(1, KV_LORA), w, *tabs)


def _softmax_pv(s_list, v_list, ones_col):
    m = jnp.max(s_list[0], axis=-1, keepdims=True)
    for s in s_list[1:]:
        m = jnp.maximum(m, jnp.max(s, axis=-1, keepdims=True))
    acc = None
    den = None
    for s, v in zip(s_list, v_list):
        p = jnp.exp2(s - m)
        if not ones_col:
            d = jnp.sum(p, axis=-1, keepdims=True)
            den = d if den is None else den + d
        o = _dot(p.astype(BF16), v)
        acc = o if acc is None else acc + o
    if ones_col:
        return acc[:, :HEAD_DIM] / acc[:, HEAD_DIM:]
    return acc / den


def _mla_kernel(*refs, hs, chains, n_lat):
    if n_lat:
        q_ref, kn_ref, kr_ref, v_ref, knc_ref, krc_ref, vc_ref, o_ref, k_scr, v_scr = refs
    else:
        q_ref, knc_ref, krc_ref, vc_ref, o_ref, k_scr, v_scr = refs
    n_ctx = knc_ref.shape[0]
    ones = jnp.ones((n_lat + n_ctx, MLA_VW - HEAD_DIM), BF16)
    for j in range(hs):
        src = slice(j * HEAD_DIM, (j + 1) * HEAD_DIM)
        nope = slice(j * MLA_QK, j * MLA_QK + HEAD_DIM)
        rot = slice(j * MLA_QK + HEAD_DIM, (j + 1) * MLA_QK)
        val = slice(j * MLA_VW, j * MLA_VW + HEAD_DIM)
        if n_lat:
            k_scr[:n_lat, nope] = kn_ref[:, src]
            k_scr[:n_lat, rot] = kr_ref[...]
            v_scr[:n_lat, val] = v_ref[:, src]
        k_scr[n_lat:, nope] = knc_ref[:, src]
        k_scr[n_lat:, rot] = krc_ref[...]
        v_scr[n_lat:, val] = vc_ref[:, src]
        v_scr[:, j * MLA_VW + HEAD_DIM:(j + 1) * MLA_VW] = ones
    tq = q_ref.shape[0] // chains
    for j in range(hs):
        qk_cols = slice(j * MLA_QK, (j + 1) * MLA_QK)
        v_cols = slice(j * MLA_VW, (j + 1) * MLA_VW)
        for c in range(chains):
            rows = slice(c * tq, (c + 1) * tq)
            s = _dot_nt(q_ref[rows, qk_cols], k_scr[:, qk_cols])
            o_ref[rows, j * HEAD_DIM:(j + 1) * HEAD_DIM] = _softmax_pv([s], [v_scr[:, v_cols]], True).astype(
                o_ref.dtype)


def _mla_scratch(n_keys, hs):
    return [pltpu.VMEM((n_keys, hs * MLA_QK), BF16), pltpu.VMEM((n_keys, hs * MLA_VW), BF16)]


def _mla_latent(q, kn, v, kr, *, hs=4, chains=4):
    lat = pl.BlockSpec((SEQ, hs * HEAD_DIM), lambda b, h: (b, h))
    ctx = pl.BlockSpec((CTX_LEN, hs * HEAD_DIM), lambda b, h: (CTX_BLOCK0 + b, h))
    return pl.pallas_call(
        functools.partial(_mla_kernel, hs=hs, chains=chains, n_lat=SEQ),
        grid=(BATCH, MLA_HEADS // hs),
        in_specs=[
            pl.BlockSpec((SEQ, hs * MLA_QK), lambda b, h: (b, h)),
            lat,
            pl.BlockSpec((SEQ, HEAD_DIM), lambda b, h: (b, 0)),
            lat,
            ctx,
            pl.BlockSpec((CTX_LEN, HEAD_DIM), lambda b, h: (CTX_BLOCK0 + b, 0)),
            ctx,
        ],
        out_specs=pl.BlockSpec((SEQ, hs * HEAD_DIM), lambda b, h: (b, h)),
        out_shape=jax.ShapeDtypeStruct((N_LAT, MLA_HEADS * HEAD_DIM), BF16),
        scratch_shapes=_mla_scratch(SEQ + CTX_LEN, hs),
        compiler_params=_cparams(("parallel", "parallel")),
        name="mla_latent",
    )(q, kn, kr, v, kn, kr, v)


def _mla_ctx(q, kn, v, kr):
    wide = pl.BlockSpec((CTX_LEN, MLA_HEADS * HEAD_DIM), lambda b: (CTX_BLOCK0 + b, 0))
    return pl.pallas_call(
        functools.partial(_mla_kernel, hs=MLA_HEADS, chains=1, n_lat=0),
        grid=(BATCH,),
        in_specs=[
            pl.BlockSpec((CTX_LEN, MLA_HEADS * MLA_QK), lambda b: (CTX_BLOCK0 + b, 0)),
            wide,
            pl.BlockSpec((CTX_LEN, HEAD_DIM), lambda b: (CTX_BLOCK0 + b, 0)),
            wide,
        ],
        out_specs=pl.BlockSpec((CTX_LEN, MLA_HEADS * HEAD_DIM), lambda b: (b, 0)),
        out_shape=jax.ShapeDtypeStruct((N_CTX, MLA_HEADS * HEAD_DIM), BF16),
        scratch_shapes=_mla_scratch(CTX_LEN, MLA_HEADS),
        compiler_params=_cparams(("parallel",)),
        name="mla_ctx",
    )(q, kn, kr, v)


def _na_ctx_kernel(q_ref, k_ref, v_ref, o_ref):
    for h in range(NA_HEADS):
        cols = slice(h * HEAD_DIM, (h + 1) * HEAD_DIM)
        s = _dot_nt(q_ref[:, cols], k_ref[:, cols]) * (NA_SCALE * LOG2E)
        o_ref[:, cols] = _softmax_pv([s], [v_ref[:, cols]], False).astype(o_ref.dtype)


def _na_ctx(p):
    width = NA_HEADS * HEAD_DIM
    col0 = COL_NA // width

    def spec(c):
        return pl.BlockSpec((CTX_LEN, width), lambda b: (CTX_BLOCK0 + b, col0 + c))

    return pl.pallas_call(
        _na_ctx_kernel,
        grid=(BATCH,),
        in_specs=[spec(0), spec(1), spec(2)],
        out_specs=pl.BlockSpec((CTX_LEN, width), lambda b: (b, 0)),
        out_shape=jax.ShapeDtypeStruct((N_CTX, width), BF16),
        compiler_params=_cparams(("parallel",)),
        name="na_ctx",
    )(p, p, p)


def _na_plan():
    rows = SEQ // GRID_W
    invalid = 2 * NA_KH - 1
    pairs, plan, starts = [], [], []
    for t in range(rows // NA_TR):
        kw0 = int(np.clip(NA_TR * t - NA_KH // 2, 0, rows - NA_WR))
        starts.append(kw0)
        tile = []
        for ri in range(NA_TR):
            r = NA_TR * t + ri
            r0 = int(np.clip(r - NA_KH // 2, 0, rows - NA_KH))
            assert kw0 <= r0 and r0 + NA_KH <= kw0 + NA_WR
            row = []
            for kp in range(NA_WR // 2):
                pair = []
                for kr in (kw0 + 2 * kp, kw0 + 2 * kp + 1):
                    pair.append(kr - r + NA_KH - 1 if r0 <= kr < r0 + NA_KH else invalid)
                pair = tuple(pair)
                if pair not in pairs:
                    pairs.append(pair)
                row.append(pairs.index(pair))
            tile.append(row)
        plan.append(tile)
    return starts, plan, pairs


def _na_bias_pairs(rpb, pairs):
    c = np.arange(GRID_W)
    c0 = np.clip(c - NA_KW // 2, 0, GRID_W - NA_KW)
    col_ok = (c[None, :] >= c0[:, None]) & (c[None, :] < c0[:, None] + NA_KW)
    col_idx = np.clip(c[None, :] - c[:, None] + NA_KW - 1, 0, 2 * NA_KW - 2)
    onehot = (col_idx[None] == np.arange(2 * NA_KW - 1)[:, None, None]).astype(np.float32)
    t = jnp.einsum("lhdj,jck->lhdck", rpb.astype(F32), onehot, precision=lax.Precision.HIGHEST) * LOG2E
    t = jnp.where(col_ok, t, MASK_VALUE)
    masked = jnp.full(t.shape[:2] + (GRID_W, GRID_W), MASK_VALUE, F32)
    slabs = [t[:, :, d] for d in range(2 * NA_KH - 1)] + [masked]
    return jnp.stack([jnp.concatenate([slabs[a], slabs[b]], axis=-1) for a, b in pairs], axis=2)


NA_HS = 2


def _na_kernel(q_ref, k_ref, v_ref, kc_ref, vc_ref, t2_ref, o_ref, vo_ref, vco_ref, *, starts, plan):
    ones = jnp.ones((SEQ, HEAD_DIM), BF16)
    for j in range(NA_HS):
        cols = slice(j * HEAD_DIM, (j + 1) * HEAD_DIM)
        wide = slice(2 * j * HEAD_DIM, 2 * (j + 1) * HEAD_DIM)
        vo_ref[:, 2 * j * HEAD_DIM:(2 * j + 1) * HEAD_DIM] = v_ref[:, cols]
        vo_ref[:, (2 * j + 1) * HEAD_DIM:2 * (j + 1) * HEAD_DIM] = ones
        vco_ref[:, 2 * j * HEAD_DIM:(2 * j + 1) * HEAD_DIM] = vc_ref[:, cols]
        vco_ref[:, (2 * j + 1) * HEAD_DIM:2 * (j + 1) * HEAD_DIM] = ones[:CTX_LEN]
        kc = kc_ref[:, cols]
        vc = vco_ref[:, wide]
        for t, (kw0, tile) in enumerate(zip(starts, plan)):
            rows = slice(t * NA_TQ, (t + 1) * NA_TQ)
            win = slice(kw0 * GRID_W, kw0 * GRID_W + NA_TK)
            q = q_ref[rows, cols]
            bias = jnp.concatenate(
                [jnp.concatenate([t2_ref[0, j, idx] for idx in row], axis=1) for row in tile], axis=0)
            s = _dot_nt(q, k_ref[win, cols]) * (NA_SCALE * LOG2E) + bias
            sc = _dot_nt(q, kc) * (NA_SCALE * LOG2E)
            o_ref[rows, cols] = _softmax_pv([s, sc], [vo_ref[win, wide], vc], True).astype(o_ref.dtype)


def _na_latent(p, t2, l, starts, plan):
    width = NA_HS * HEAD_DIM
    cq = COL_NA // width
    ck = cq + NA_HEADS // NA_HS
    cv = ck + NA_HEADS // NA_HS
    n_pairs = t2.shape[2]
    return pl.pallas_call(
        functools.partial(_na_kernel, starts=starts, plan=plan),
        grid=(NA_HEADS // NA_HS, BATCH),
        in_specs=[
            pl.BlockSpec((SEQ, width), lambda h, b: (b, cq + h)),
            pl.BlockSpec((SEQ, width), lambda h, b: (b, ck + h)),
            pl.BlockSpec((SEQ, width), lambda h, b: (b, cv + h)),
            pl.BlockSpec((CTX_LEN, width), lambda h, b: (CTX_BLOCK0 + b, ck + h)),
            pl.BlockSpec((CTX_LEN, width), lambda h, b: (CTX_BLOCK0 + b, cv + h)),
            pl.BlockSpec((1, NA_HS, n_pairs, GRID_W, 2 * GRID_W), lambda h, b: (l, h, 0, 0, 0)),
        ],
        out_specs=pl.BlockSpec((SEQ, width), lambda h, b: (b, h)),
        out_shape=jax.ShapeDtypeStruct((N_LAT, NA_HEADS * HEAD_DIM), BF16),
        scratch_shapes=[pltpu.VMEM((SEQ, 2 * width), BF16), pltpu.VMEM((CTX_LEN, 2 * width), BF16)],
        compiler_params=_cparams(("parallel", "parallel")),
        name="na_latent",
    )(p, p, p, p, p, t2)


def _conv_kernel(v_ref, x1_ref, x2_ref, wv_ref, w1_ref, w2_ref, bv_ref, b1_ref, b2_ref, zin_ref, x2o_ref):
    n = v_ref.shape[0]
    row = lax.broadcasted_iota(jnp.int32, (n, 1), 0)

    def short_conv(p_ref, w_ref, b_ref):
        p = p_ref[...].astype(F32)
        prev = jnp.where(row == 0, 0.0, pltpu.roll(p, 1, 0))
        nxt = jnp.where(row == n - 1, 0.0, pltpu.roll(p, n - 1, 0))
        w = w_ref[0]
        return prev * w[0:1] + p * w[1:2] + nxt * w[2:3] + b_ref[0]

    zin_ref[...] = (short_conv(x1_ref, w1_ref, b1_ref) * short_conv(v_ref, wv_ref, bv_ref)).astype(zin_ref.dtype)
    x2o_ref[...] = short_conv(x2_ref, w2_ref, b2_ref).astype(x2o_ref.dtype)


def _hy_conv(p, conv_w, conv_b, l, n, row_block0, *, tc=256):
    nc = HY_WIDTH // tc

    def seg(s):
        return (pl.BlockSpec((n, tc), lambda b, j: (row_block0 + b, s * nc + j)),
                pl.BlockSpec((1, 3, tc), lambda b, j: (l, 0, s * nc + j)),
                pl.BlockSpec((1, 1, tc), lambda b, j: (l, 0, s * nc + j)))

    (pv, wv, bv), (p1, w1, b1), (p2, w2, b2) = seg(0), seg(1), seg(2)
    out_spec = pl.BlockSpec((n, tc), lambda b, j: (0, b * nc + j))
    out = jax.ShapeDtypeStruct((n, BATCH * HY_WIDTH), BF16)
    conv_b = conv_b.reshape(DEPTH, 1, -1)
    return pl.pallas_call(
        _conv_kernel,
        grid=(BATCH, nc),
        in_specs=[pv, p1, p2, wv, w1, w2, bv, b1, b2],
        out_specs=[out_spec, out_spec],
        out_shape=[out, out],
        compiler_params=_cparams(("parallel", "parallel")),
        name="hy_conv",
    )(p, p, p, conv_w, conv_w, conv_w, conv_b, conv_b, conv_b)


def _filt_kernel(z_ref, t_ref, dl_ref, w1_ref, b1_ref, w2_ref, b2_ref, w3_ref, fr_ref, hs_ref, ha_ref, kn_ref):
    n = z_ref.shape[0]
    fr = fr_ref[...]
    h = jnp.sin(fr * (_dot_hi(z_ref[...], w1_ref[...]) + b1_ref[...]))
    h = jnp.sin(fr * (_dot_hi(h, w2_ref[...]) + b2_ref[...]))
    h = _dot_hi(h, w3_ref[...])
    decay = jnp.exp(-t_ref[...] * dl_ref[...])
    row = lax.broadcasted_iota(jnp.int32, (n, 1), 0)
    hf = h[:, :HY_WIDTH] * decay
    hb = jnp.where(row == 0, 0.0, h[:, HY_WIDTH:] * decay)
    hs = hf + hb
    hs_ref[...] = hs.astype(hs_ref.dtype)
    ha_ref[...] = (hf - hb).astype(ha_ref.dtype)
    sign = jnp.where((row & 1) == 0, 1.0, -1.0)
    kn_ref[...] = jnp.sum(hs * sign, axis=0, keepdims=True)


def _hy_filter_taps(n, f_w1, f_b1, f_w2, f_b2, f_w3, f_freq):
    pos = jnp.arange(n, dtype=F32)
    t = jnp.linspace(0.0, 1.0, n, dtype=F32)
    bands = jnp.linspace(1e-4, HY_POS_BANDS - 1, HY_POS_BANDS, dtype=F32)
    ang = (2.0 * math.pi / n) * pos[:, None] * bands[None, :]
    z = jnp.concatenate([t[:, None], jnp.cos(ang), -jnp.sin(ang)], axis=-1)
    pad = HY_FILTER_HIDDEN - z.shape[1]
    z = jnp.pad(z, ((0, 0), (0, pad)))
    w1 = jnp.pad(f_w1.astype(F32), ((0, pad), (0, 0)))
    deltas = jnp.abs(jnp.linspace(math.log(HY_DECAY_TARGET) / HY_FAST_DECAY,
                                  math.log(HY_DECAY_TARGET) / HY_SLOW_DECAY, HY_WIDTH, dtype=F32))
    hid = HY_FILTER_HIDDEN
    return pl.pallas_call(
        _filt_kernel,
        out_shape=[
            jax.ShapeDtypeStruct((n, HY_WIDTH), BF16),
            jax.ShapeDtypeStruct((n, HY_WIDTH), BF16),
            jax.ShapeDtypeStruct((1, HY_WIDTH), F32),
        ],
        compiler_params=pltpu.CompilerParams(vmem_limit_bytes=VMEM_LIMIT),
        name="hy_filter",
    )(z, t[:, None], deltas[None, :], w1, f_b1.reshape(1, hid), f_w2, f_b2.reshape(1, hid), f_w3,
      f_freq.reshape(1, hid))


def _dft_tables(n):
    lo = 16 if n < 1024 else 32
    hi = n // lo
    t = jnp.arange(n, dtype=jnp.int32)[None, :]
    big = 2 * n

    def ang(ff):
        return ((ff * t) % big).astype(F32) * (2.0 * math.pi / big)

    a = ang(lo * jnp.arange(hi, dtype=jnp.int32)[:, None])
    b = ang(jnp.arange(lo, dtype=jnp.int32)[:, None])
    ca, sa, cb, sb = jnp.cos(a), jnp.sin(a), jnp.cos(b), jnp.sin(b)
    cos = (ca[:, None, :] * cb[None, :, :] - sa[:, None, :] * sb[None, :, :]).reshape(n, n)
    base = -(sa[:, None, :] * cb[None, :, :] + ca[:, None, :] * sb[None, :, :]).reshape(n, n)
    idx = jnp.arange(n)
    alt = jnp.where(idx % 2 == 0, 1.0, -1.0).astype(F32)
    msin = jnp.where(idx[:, None] == 0, alt[None, :], base)
    msin_t = jnp.where(idx[None, :] == 0, alt[:, None], base)
    return cos.astype(BF16), msin.astype(BF16), msin_t.astype(BF16)


def _dft_filt_kernel(c_ref, s_ref, hs_ref, ha_ref, kr_ref, ki_ref):
    kr_ref[...] = _dot(c_ref[...], hs_ref[...])
    ki_ref[...] = _dot(s_ref[...], ha_ref[...])


def _hy_filter_spectrum(cos, msin, hs, ha, *, tf):
    n = cos.shape[0]
    tf = min(tf, n)
    tab = pl.BlockSpec((tf, n), lambda i: (i, 0))
    taps = pl.BlockSpec((n, HY_WIDTH), lambda i: (0, 0))
    out = pl.BlockSpec((tf, HY_WIDTH), lambda i: (i, 0))
    return pl.pallas_call(
        _dft_filt_kernel,
        grid=(n // tf,),
        in_specs=[tab, tab, taps, taps],
        out_specs=[out, out],
        out_shape=[jax.ShapeDtypeStruct((n, HY_WIDTH), F32)] * 2,
        compiler_params=_cparams(("parallel",)),
        name="hy_filter_dft",
    )(cos, msin, hs, ha)


def _batch_cols(b):
    return slice(b * HY_WIDTH, (b + 1) * HY_WIDTH)


def _dft_fwd_kernel(c_ref, s_ref, x_ref, kr_ref, ki_ref, kn_ref, yr_ref, yi_ref, *, tf, inv_n):
    x = x_ref[...]
    zr = _dot(c_ref[...], x)
    zi = _dot(s_ref[...], x)
    row = pl.program_id(0) * tf + lax.broadcasted_iota(jnp.int32, (tf, 1), 0)
    bin0 = row == 0
    wt = jnp.where(bin0, inv_n, 2.0 * inv_n)
    kr = kr_ref[...] * wt
    ki = jnp.where(bin0, 0.0, ki_ref[...] * wt)
    kr_im = jnp.where(bin0, kn_ref[...] * wt, kr)
    for b in range(BATCH):
        cols = _batch_cols(b)
        yr_ref[:, cols] = (zr[:, cols] * kr - zi[:, cols] * ki).astype(yr_ref.dtype)
        yi_ref[:, cols] = (zr[:, cols] * ki + zi[:, cols] * kr_im).astype(yi_ref.dtype)


def _hy_dft_fwd(cos, msin, zin, kr, ki, kn, *, tf):
    n = cos.shape[0]
    tf = min(tf, n)
    tab = pl.BlockSpec((tf, n), lambda i: (i, 0))
    filt = pl.BlockSpec((tf, HY_WIDTH), lambda i: (i, 0))
    out = pl.BlockSpec((tf, BATCH * HY_WIDTH), lambda i: (i, 0))
    whole = pl.BlockSpec((n, BATCH * HY_WIDTH), lambda i: (0, 0), pipeline_mode=pl.Buffered(1))
    return pl.pallas_call(
        functools.partial(_dft_fwd_kernel, tf=tf, inv_n=1.0 / (2 * n)),
        grid=(n // tf,),
        in_specs=[tab, tab, whole, filt, filt, pl.BlockSpec((1, HY_WIDTH), lambda i: (0, 0))],
        out_specs=[out, out],
        out_shape=[jax.ShapeDtypeStruct((n, BATCH * HY_WIDTH), BF16)] * 2,
        compiler_params=_cparams(("parallel",)),
        name="hy_dft_fwd",
    )(cos, msin, zin, kr, ki, kn)


def _dft_inv_kernel(c_ref, st_ref, yr_ref, yi_ref, zin_ref, x2_ref, b_ref, o_ref):
    y = _dot(c_ref[...], yr_ref[...]) + _dot(st_ref[...], yi_ref[...])
    bias = b_ref[0]
    for b in range(BATCH):
        cols = _batch_cols(b)
        yb = y[:, cols] + zin_ref[:, cols].astype(F32) * bias
        o_ref[b] = (x2_ref[:, cols].astype(F32) * yb).astype(o_ref.dtype)


def _hy_dft_inv(cos, msin_t, yr, yi, zin, x2, bias, l, *, tt):
    n = cos.shape[0]
    tt = min(tt, n)
    tab = pl.BlockSpec((tt, n), lambda i: (i, 0))
    whole = pl.BlockSpec((n, BATCH * HY_WIDTH), lambda i: (0, 0), pipeline_mode=pl.Buffered(1))
    rows = pl.BlockSpec((tt, BATCH * HY_WIDTH), lambda i: (i, 0))
    out = pl.pallas_call(
        _dft_inv_kernel,
        grid=(n // tt,),
        in_specs=[tab, tab, whole, whole, rows, rows, pl.BlockSpec((1, 1, HY_WIDTH), lambda i: (l, 0, 0))],
        out_specs=pl.BlockSpec((BATCH, tt, HY_WIDTH), lambda i: (0, i, 0)),
        out_shape=jax.ShapeDtypeStruct((BATCH, n, HY_WIDTH), BF16),
        compiler_params=_cparams(("parallel",)),
        name="hy_dft_inv",
    )(cos, msin_t, yr, yi, zin, x2, bias.reshape(DEPTH, 1, HY_WIDTH))
    return out.reshape(BATCH * n, HY_WIDTH)


def _hyena(p, l, n, row_block0, tables, conv_w, conv_b, f_w1, f_b1, f_w2, f_b2, f_w3, f_freq, bias):
    cos, msin, msin_t = tables
    zin, x2 = _hy_conv(p, conv_w, conv_b, l, n, row_block0)
    hs, ha, kn = _hy_filter_taps(n, f_w1[l], f_b1[l], f_w2[l], f_b2[l], f_w3[l], f_freq[l])
    kr, ki = _hy_filter_spectrum(cos, msin, hs, ha, tf=512)
    yr, yi = _hy_dft_fwd(cos, msin, zin, kr, ki, kn, tf=512)
    return _hy_dft_inv(cos, msin_t, yr, yi, zin, x2, bias, l, tt=512)


def _rope_tables():
    tok = jnp.arange(SEQ)
    row = (tok // GRID_W).astype(F32)
    col = (tok % GRID_W).astype(F32)
    n_freq = MLA_ROPE // 4
    inv = ROPE_THETA ** (-jnp.arange(n_freq, dtype=F32) / n_freq)
    ang = jnp.concatenate([row[:, None] * inv, col[:, None] * inv], axis=-1)
    cos, sin = jnp.cos(ang), jnp.sin(ang)
    half = MLA_ROPE // 2
    zeros = jnp.zeros((SEQ, half), F32)
    rest = HEAD_DIM - MLA_ROPE
    cos_t = jnp.concatenate([cos, cos, jnp.ones((SEQ, rest), F32)], axis=-1)
    sin_a = jnp.concatenate([-sin, zeros, jnp.zeros((SEQ, rest), F32)], axis=-1)
    sin_b = jnp.concatenate([zeros, sin, jnp.zeros((SEQ, rest), F32)], axis=-1)
    ident = jnp.ones((N_CTX, HEAD_DIM), F32)
    none = jnp.zeros((N_CTX, HEAD_DIM), F32)
    return (jnp.concatenate([cos_t, ident]), jnp.concatenate([sin_a, none]), jnp.concatenate([sin_b, none]))


def _layout_w_uq(w):
    w = w.reshape(Q_LORA, MLA_HEADS, HEAD_DIM + MLA_ROPE)
    w = jnp.pad(w, ((0, 0), (0, 0), (0, MLA_QK - HEAD_DIM - MLA_ROPE)))
    return w.reshape(Q_LORA, MLA_HEADS * MLA_QK).astype(BF16)


def kernel(x, c, ctx, c_ctx, w_ada, b_ada, g_attn_pre, g_attn_post, g_ffn_pre, g_ffn_post, w_in, hy_conv_w, hy_conv_b, hy_f_w1, hy_f_b1, hy_f_w2, hy_f_b2, hy_f_w3, hy_f_freq, hy_bias, mla_g_q, mla_w_uq, mla_g_kv, mla_w_ukv, na_rpb, w_out, w_ffn_gate, w_ffn_up, w_ffn_down):
    cc = jnp.concatenate([c, c_ctx[None, :], jnp.zeros((8 - BATCH - 1, D_MODEL), F32)], axis=0)
    mods_all = _ada(cc, w_ada, b_ada)
    mods = [mods_all[l].reshape(8, 1, 6 * D_MODEL) for l in range(DEPTH)]

    rope_tabs = _rope_tables()
    dft_lat = _dft_tables(SEQ)
    dft_ctx = _dft_tables(CTX_LEN)
    na_starts, na_plan, na_pairs = _na_plan()
    na_t2 = _na_bias_pairs(na_rpb, na_pairs)
    w_in_t = jnp.swapaxes(w_in, 1, 2)
    hy_w = (hy_conv_w, hy_conv_b, hy_f_w1, hy_f_b1, hy_f_w2, hy_f_b2, hy_f_w3, hy_f_freq, hy_bias)

    res = (x.reshape(N_LAT, D_MODEL), ctx.reshape(N_CTX, D_MODEL))
    xn = _prenorm(res[0], res[1], g_attn_pre[0], mods[0])

    for l in range(DEPTH):
        ctx_out = l < DEPTH - 1
        m_rows = N_TOK if ctx_out else N_LAT
        p = _in_proj(xn, w_in_t, l)

        q = _q_proj(p, m_rows, mla_g_q[l], _layout_w_uq(mla_w_uq[l]), rope_tabs)
        kn, v, kr = _kv_proj(p, mla_g_kv[l], mla_w_ukv, l, rope_tabs)
        mla = [_mla_latent(q, kn, v, kr), None]
        na = [_na_latent(p, na_t2, l, na_starts, na_plan), None]
        hy = [_hyena(p, l, SEQ, 0, dft_lat, *hy_w), None]

        if ctx_out:
            mla[1] = _mla_ctx(q, kn, v, kr)
            na[1] = _na_ctx(p)
            hy[1] = _hyena(p, l, CTX_LEN, CTX_BLOCK0, dft_ctx, *hy_w)
            res_l = res
        else:
            res_l = (res[0], None)

        stream, xn = _proj_post([tuple(hy), tuple(mla), tuple(na)], w_out, l, g_attn_post[l], mods[l], 2, res_l,
                                m_rows, nxt=(g_ffn_pre[l], mods[l], 3, 4), tm=POST_TM_OUT)
        h = _ffn_up(xn, w_ffn_gate, w_ffn_up, l)
        nxt = (g_attn_pre[l + 1], mods[l + 1], 0, 1) if ctx_out else None
        stream, xn = _proj_post([(h, None)], w_ffn_down, l, g_ffn_post[l], mods[l], 5, (stream, None), m_rows,
                                nxt=nxt, tm=POST_TM_DOWN)
        res = (stream, None)

    return stream.reshape(BATCH, SEQ, D_MODEL)
```

```python
import functools
import math

import jax
import jax.numpy as jnp
import numpy as np
from jax import lax
from jax.experimental import pallas as pl
from jax.experimental.pallas import tpu as pltpu

F32 = jnp.float32
BF16 = jnp.bfloat16

D_MODEL = 2048
BATCH = 4
SEQ = 2048
DEPTH = 2
GRID_W = 64
CTX_LEN = 256
HEAD_DIM = 128
HY_WIDTH = D_MODEL // 4
HY_FILTER_HIDDEN = 64
HY_POS_BANDS = 16
HY_DECAY_TARGET = 1e-2
HY_FAST_DECAY = 0.3
HY_SLOW_DECAY = 1.5
MLA_HEADS = (D_MODEL // 2) // HEAD_DIM
MLA_ROPE = 64
Q_LORA = 3 * D_MODEL // 8
KV_LORA = D_MODEL // 4
MLA_SCALE = (HEAD_DIM + MLA_ROPE) ** -0.5
NA_HEADS = (D_MODEL // 4) // HEAD_DIM
NA_KH = 8
NA_KW = 16
NA_SCALE = HEAD_DIM ** -0.5
FFN_HIDDEN = ((8 * D_MODEL + 3 * 256 - 1) // (3 * 256)) * 256
ROPE_THETA = 10000.0
RMS_EPS = 1e-6
MASK_VALUE = -1e30
LOG2E = math.log2(math.e)

N_LAT = BATCH * SEQ
N_CTX = BATCH * CTX_LEN
N_TOK = N_LAT + N_CTX
CTX_BLOCK0 = N_LAT // CTX_LEN

IN_TN = 512
COL_HY = 0
COL_CQ = 3 * HY_WIDTH
COL_CKV = COL_CQ + Q_LORA
COL_KR = COL_CKV + KV_LORA
W_IN_NA = COL_KR + MLA_ROPE
COL_NA = -(-W_IN_NA // IN_TN) * IN_TN
P_COLS = COL_NA + 3 * NA_HEADS * HEAD_DIM
MLA_QK = 256
MLA_VW = 256

NA_TR = 4
NA_WR = 12
NA_TQ = NA_TR * GRID_W
NA_TK = NA_WR * GRID_W

VMEM_LIMIT = 52 * 1024 * 1024


def _cparams(sem):
    return pltpu.CompilerParams(dimension_semantics=sem, vmem_limit_bytes=VMEM_LIMIT)


def _dot(a, b):
    return jnp.dot(a, b, preferred_element_type=F32)


def _dot_nt(a, b):
    return lax.dot_general(a, b, (((1,), (1,)), ((), ())), preferred_element_type=F32)


def _dot_hi(a, b):
    return jnp.dot(a, b, preferred_element_type=F32, precision=lax.Precision.HIGHEST)


def _rms(x, g):
    ms = jnp.mean(x * x, axis=-1, keepdims=True)
    return x * lax.rsqrt(ms + RMS_EPS) * g


def _mod_row(tm):
    n_lat, per_b = N_LAT // tm, SEQ // tm
    return lambda i: jnp.where(i < n_lat, i // per_b, BATCH)


def _rope_row(tm):
    n_lat, per_b = N_LAT // tm, SEQ // tm
    return lambda i: jnp.where(i < n_lat, i % per_b, per_b + i - n_lat)


def _ada_kernel(c_ref, w_ref, b_ref, o_ref):
    a = c_ref[...]
    a = a * jax.nn.sigmoid(a)
    o_ref[0] = _dot(a.astype(BF16), w_ref[0].astype(BF16)) + b_ref[0]


def _ada(cc, w_ada, b_ada):
    tn = 1024
    n = w_ada.shape[-1]
    return pl.pallas_call(
        _ada_kernel,
        grid=(DEPTH, n // tn),
        in_specs=[
            pl.BlockSpec((8, D_MODEL), lambda l, j: (0, 0)),
            pl.BlockSpec((1, D_MODEL, tn), lambda l, j: (l, 0, j)),
            pl.BlockSpec((1, 1, tn), lambda l, j: (l, 0, j)),
        ],
        out_specs=pl.BlockSpec((1, 8, tn), lambda l, j: (l, 0, j)),
        out_shape=jax.ShapeDtypeStruct((DEPTH, 8, n), F32),
        compiler_params=_cparams(("parallel", "parallel")),
        name="ada",
    )(cc, w_ada, b_ada.reshape(DEPTH, 1, n))


def _norm_mod_to(xn_ref, x_ref, g_ref, sh_ref, sc_ref, chunk=256):
    g = g_ref[...]
    sc = 1.0 + sc_ref[0]
    sh = sh_ref[0]

    def body(r, carry):
        rows = pl.ds(pl.multiple_of(r * chunk, chunk), chunk)
        xn_ref[rows, :] = (_rms(x_ref[rows, :], g) * sc + sh).astype(BF16)
        return carry

    lax.fori_loop(0, x_ref.shape[0] // chunk, body, 0)


def _prenorm_kernel(x_ref, c_ref, g_ref, sh_ref, sc_ref, o_ref, *, n_lat):
    i = pl.program_id(0)

    @pl.when(i < n_lat)
    def _():
        _norm_mod_to(o_ref, x_ref, g_ref, sh_ref, sc_ref)

    @pl.when(i >= n_lat)
    def _():
        _norm_mod_to(o_ref, c_ref, g_ref, sh_ref, sc_ref)


def _prenorm(x2d, ctx2d, g, mods, *, tm=1024):
    n_lat = N_LAT // tm
    row = _mod_row(tm)
    return pl.pallas_call(
        functools.partial(_prenorm_kernel, n_lat=n_lat),
        grid=(N_TOK // tm,),
        in_specs=[
            pl.BlockSpec((tm, D_MODEL), lambda i: (jnp.minimum(i, n_lat - 1), 0)),
            pl.BlockSpec((tm, D_MODEL), lambda i: (jnp.maximum(i - n_lat, 0), 0)),
            pl.BlockSpec((1, D_MODEL), lambda i: (0, 0)),
            pl.BlockSpec((1, 1, D_MODEL), lambda i: (row(i), 0, 0)),
            pl.BlockSpec((1, 1, D_MODEL), lambda i: (row(i), 0, 1)),
        ],
        out_specs=pl.BlockSpec((tm, D_MODEL), lambda i: (i, 0)),
        out_shape=jax.ShapeDtypeStruct((N_TOK, D_MODEL), BF16),
        compiler_params=_cparams(("parallel",)),
        name="prenorm",
    )(x2d, ctx2d, g.reshape(1, D_MODEL), mods, mods)


def _in_kernel(x_ref, w_ref, o_ref):
    o_ref[...] = _dot_nt(x_ref[...], w_ref[0].astype(BF16)).astype(o_ref.dtype)


def _in_proj(xn, w_in_t, l, *, tm=3072):
    n_head = COL_NA // IN_TN

    def w_row(j):
        per = IN_TN // MLA_ROPE
        return MLA_ROPE * jnp.where(j < n_head, j * per, W_IN_NA // MLA_ROPE + (j - n_head) * per)

    return pl.pallas_call(
        _in_kernel,
        grid=(N_TOK // tm, P_COLS // IN_TN),
        in_specs=[
            pl.BlockSpec((tm, D_MODEL), lambda i, j: (i, 0)),
            pl.BlockSpec((pl.Element(1), pl.Element(IN_TN), pl.Element(D_MODEL)), lambda i, j: (l, w_row(j), 0)),
        ],
        out_specs=pl.BlockSpec((tm, IN_TN), lambda i, j: (i, j)),
        out_shape=jax.ShapeDtypeStruct((N_TOK, P_COLS), BF16),
        compiler_params=_cparams(("parallel", "arbitrary")),
        name="in_proj",
    )(xn, w_in_t)


FFN_CW = 256


def _swiglu_kernel(x_ref, wg_ref, wu_ref, o_ref):
    xn = x_ref[...]
    for c in range(o_ref.shape[1] // FFN_CW):
        cols = slice(c * FFN_CW, (c + 1) * FFN_CW)
        gate = _dot(xn, wg_ref[0, :, cols].astype(BF16))
        up = _dot(xn, wu_ref[0, :, cols].astype(BF16))
        o_ref[:, cols] = (gate * jax.nn.sigmoid(gate) * up).astype(o_ref.dtype)


def _ffn_up(xn, wg, wu, l, *, tn=512):
    m = xn.shape[0]
    tm = 2048 if m % 2048 == 0 else 2304
    w_spec = pl.BlockSpec((1, D_MODEL, tn), lambda i, j: (l, 0, j))
    return pl.pallas_call(
        _swiglu_kernel,
        grid=(m // tm, FFN_HIDDEN // tn),
        in_specs=[pl.BlockSpec((tm, D_MODEL), lambda i, j: (i, 0)), w_spec, w_spec],
        out_specs=pl.BlockSpec((tm, tn), lambda i, j: (i, j)),
        out_shape=jax.ShapeDtypeStruct((m, FFN_HIDDEN), BF16),
        compiler_params=_cparams(("parallel", "arbitrary")),
        name="ffn_up",
    )(xn, wg, wu)


POST_TM_OUT = 512
POST_TM_DOWN = 256
POST_SUB = 128
POST_WCH = 256
POST_WSLOTS = 4
POST_SLOTS = 3


def _post_kernel(*refs, widths, two_src, emit_xn, n_lat, l, tm):
    refs = list(refs)
    n_act = len(widths)
    acts_lat = [refs.pop(0) for _ in range(n_act)]
    acts_ctx = [refs.pop(0) for _ in range(n_act)] if two_src else None
    w_hbm, g_ref, gate_ref, res_lat = (refs.pop(0) for _ in range(4))
    res_ctx = refs.pop(0) if two_src else None
    if emit_xn:
        g2_ref, sh_ref, sc_ref = (refs.pop(0) for _ in range(3))
    o_hbm = refs.pop(0)
    xn_ref = refs.pop(0) if emit_xn else None
    w_scr, stage, buf, sem_w, sem_res, sem_out = refs

    i = pl.program_id(0)
    n_i = pl.num_programs(0)
    n_chunks = sum(widths) // POST_WCH
    is_lat = i < n_lat
    is_ctx = jnp.logical_not(is_lat)

    def w_copy(c, slot):
        return pltpu.make_async_copy(w_hbm.at[l, pl.ds(c * POST_WCH, POST_WCH)], stage.at[slot], sem_w.at[slot])

    def res_copy(src, row0, slot):
        return pltpu.make_async_copy(src.at[pl.ds(row0, tm)], buf.at[slot], sem_res.at[slot])

    def res_start(tile, slot):
        if two_src:
            pl.when(tile < n_lat)(lambda: res_copy(res_lat, tile * tm, slot).start())
            pl.when(tile >= n_lat)(lambda: res_copy(res_ctx, (tile - n_lat) * tm, slot).start())
        else:
            res_copy(res_lat, tile * tm, slot).start()

    def out_copy(tile, slot):
        return pltpu.make_async_copy(buf.at[slot], o_hbm.at[pl.ds(tile * tm, tm)], sem_out.at[slot])

    @pl.when(i == 0)
    def _():
        res_start(0, 0)
        for c in range(POST_WSLOTS - 1):
            w_copy(c, c).start()

        def body(c, carry):
            s = c % POST_WSLOTS
            w_copy(c, s).wait()
            ahead = c + POST_WSLOTS - 1

            @pl.when(ahead < n_chunks)
            def _():
                w_copy(ahead, ahead % POST_WSLOTS).start()

            w_scr[pl.ds(pl.multiple_of(c * POST_WCH, POST_WCH), POST_WCH), :] = stage[s].astype(BF16)
            return carry

        lax.fori_loop(0, n_chunks, body, 0)

    slot = i % POST_SLOTS
    nslot = (i + 1) % POST_SLOTS

    @pl.when(i + 1 < n_i)
    def _():
        @pl.when(i >= POST_SLOTS - 1)
        def _():
            out_copy(i + 1 - POST_SLOTS, nslot).wait()

        res_start(i + 1, nslot)

    res_copy(res_lat, 0, slot).wait()

    def compute(act_refs):
        gg = gate_ref[0] * g_ref[...]
        if emit_xn:
            g2s = g2_ref[...] * (1.0 + sc_ref[0])
            sh = sh_ref[0]
        for r in range(tm // POST_SUB):
            rows = pl.ds(r * POST_SUB, POST_SUB)
            y = None
            k0 = 0
            for a_ref, kw in zip(act_refs, widths):
                part = _dot(a_ref[rows, :], w_scr[k0:k0 + kw, :])
                y = part if y is None else y + part
                k0 += kw
            x_new = buf[slot, rows, :] + _rms(y, gg)
            buf[slot, rows, :] = x_new
            if emit_xn:
                xn_ref[rows, :] = (_rms(x_new, g2s) + sh).astype(xn_ref.dtype)

    if two_src:
        pl.when(is_lat)(functools.partial(compute, acts_lat))
        pl.when(is_ctx)(functools.partial(compute, acts_ctx))
    else:
        compute(acts_lat)

    out_copy(i, slot).start()

    @pl.when(i == n_i - 1)
    def _():
        for back in range(POST_SLOTS):
            out_copy(i - back, (i - back) % POST_SLOTS).wait()


def _proj_post(acts, w, l, g, mods, gate_chunk, res, m_rows, nxt=None, *, tm):
    two_src = res[1] is not None
    n_lat = N_LAT // tm
    widths = tuple(a.shape[1] for a, _ in acts)
    k_total = sum(widths)
    emit_xn = nxt is not None
    row = _mod_row(tm)

    def lat_row(i):
        return jnp.minimum(i, n_lat - 1) if two_src else i

    def ctx_row(i):
        return jnp.maximum(i - n_lat, 0)

    def act_specs(rowf):
        return [pl.BlockSpec((tm, kw), lambda i: (rowf(i), 0)) for kw in widths]

    vec_spec = pl.BlockSpec((1, D_MODEL), lambda i: (0, 0))

    def mod_spec(chunk):
        return pl.BlockSpec((1, 1, D_MODEL), lambda i: (row(i), 0, chunk))

    hbm = pl.BlockSpec(memory_space=pl.ANY)
    in_specs = act_specs(lat_row)
    args = [a for a, _ in acts]
    if two_src:
        in_specs += act_specs(ctx_row)
        args += [c for _, c in acts]
    in_specs += [hbm, vec_spec, mod_spec(gate_chunk), hbm]
    args += [w, g.reshape(1, D_MODEL), mods, res[0]]
    if two_src:
        in_specs.append(hbm)
        args.append(res[1])
    out_specs = [hbm]
    out_shape = [jax.ShapeDtypeStruct((m_rows, D_MODEL), F32)]
    if emit_xn:
        g2, mods2, sh_chunk, sc_chunk = nxt
        in_specs += [vec_spec, mod_spec(sh_chunk), mod_spec(sc_chunk)]
        args += [g2.reshape(1, D_MODEL), mods2, mods2]
        out_specs.append(pl.BlockSpec((tm, D_MODEL), lambda i: (i, 0)))
        out_shape.append(jax.ShapeDtypeStruct((m_rows, D_MODEL), BF16))

    out = pl.pallas_call(
        functools.partial(_post_kernel, widths=widths, two_src=two_src, emit_xn=emit_xn, n_lat=n_lat, l=l, tm=tm),
        grid=(m_rows // tm,),
        in_specs=in_specs,
        out_specs=out_specs,
        out_shape=out_shape,
        scratch_shapes=[
            pltpu.VMEM((k_total, D_MODEL), BF16),
            pltpu.VMEM((POST_WSLOTS, POST_WCH, D_MODEL), F32),
            pltpu.VMEM((POST_SLOTS, tm, D_MODEL), F32),
            pltpu.SemaphoreType.DMA((POST_WSLOTS,)),
            pltpu.SemaphoreType.DMA((POST_SLOTS,)),
            pltpu.SemaphoreType.DMA((POST_SLOTS,)),
        ],
        compiler_params=_cparams(("arbitrary",)),
        name="proj_post",
    )(*args)
    return (out[0], out[1]) if emit_xn else (out[0], None)


def _rope128(r, cos_ref, sa_ref, sb_ref):
    return r * cos_ref[...] + pltpu.roll(r, 96, 1) * sa_ref[...] + pltpu.roll(r, 32, 1) * sb_ref[...]


def _q_kernel(x_ref, g_ref, w_ref, cos_ref, sa_ref, sb_ref, o_ref):
    xn = _rms(x_ref[...].astype(F32), g_ref[...] * (MLA_SCALE * LOG2E)).astype(BF16)
    for h in range(MLA_HEADS):
        acc = _dot(xn, w_ref[:, h * MLA_QK:(h + 1) * MLA_QK])
        o_ref[:, h * MLA_QK:h * MLA_QK + HEAD_DIM] = acc[:, :HEAD_DIM].astype(o_ref.dtype)
        o_ref[:, h * MLA_QK + HEAD_DIM:(h + 1) * MLA_QK] = _rope128(
            acc[:, HEAD_DIM:], cos_ref, sa_ref, sb_ref).astype(o_ref.dtype)


def _q_proj(p, m_rows, g, w, tabs, *, tm=1024):
    rope = _rope_row(tm)
    tab_spec = pl.BlockSpec((tm, HEAD_DIM), lambda i: (rope(i), 0))
    return pl.pallas_call(
        _q_kernel,
        grid=(m_rows // tm,),
        in_specs=[
            pl.BlockSpec((tm, Q_LORA), lambda i: (i, COL_CQ // Q_LORA)),
            pl.BlockSpec((1, Q_LORA), lambda i: (0, 0)),
            pl.BlockSpec((Q_LORA, MLA_HEADS * MLA_QK), lambda i: (0, 0)),
            tab_spec, tab_spec, tab_spec,
        ],
        out_specs=pl.BlockSpec((tm, MLA_HEADS * MLA_QK), lambda i: (i, 0)),
        out_shape=jax.ShapeDtypeStruct((m_rows, MLA_HEADS * MLA_QK), BF16),
        compiler_params=_cparams(("parallel",)),
        name="q_proj",
    )(p, g.reshape(1, Q_LORA), w, *tabs)


def _kv_kernel(xa_ref, xb_ref, kr_ref, g_ref, w_ref, cos_ref, sa_ref, sb_ref, k_ref, v_ref, krr_ref):
    half = KV_LORA // 2
    xa = xa_ref[...].astype(F32)
    xb = xb_ref[...].astype(F32)
    ms = (jnp.sum(xa * xa, axis=-1, keepdims=True) + jnp.sum(xb * xb, axis=-1, keepdims=True)) * (1.0 / KV_LORA)
    rs = lax.rsqrt(ms + RMS_EPS)
    g = g_ref[...]
    xna = (xa * rs * g[:, :half]).astype(BF16)
    xnb = (xb * rs * g[:, half:]).astype(BF16)
    lane = lax.broadcasted_iota(jnp.int32, (1, HEAD_DIM), 1)
    krr = _rope128(kr_ref[...].astype(F32), cos_ref, sa_ref, sb_ref)
    krr_ref[...] = jnp.where(lane < MLA_ROPE, krr, 0.0).astype(krr_ref.dtype)
    for h in range(MLA_HEADS):
        cols = slice(h * HEAD_DIM, (h + 1) * HEAD_DIM)
        w = w_ref[0, :, h * 2 * HEAD_DIM:(h + 1) * 2 * HEAD_DIM].astype(BF16)
        acc = _dot(xna, w[:half]) + _dot(xnb, w[half:])
        k_ref[:, cols] = acc[:, :HEAD_DIM].astype(k_ref.dtype)
        v_ref[:, cols] = acc[:, HEAD_DIM:].astype(v_ref.dtype)


def _kv_proj(p, g, w, l, tabs, *, tm=1024):
    m = p.shape[0]
    rope = _rope_row(tm)
    half = KV_LORA // 2
    tab_spec = pl.BlockSpec((tm, HEAD_DIM), lambda i: (rope(i), 0))
    return pl.pallas_call(
        _kv_kernel,
        grid=(m // tm,),
        in_specs=[
            pl.BlockSpec((tm, half), lambda i: (i, COL_CKV // half)),
            pl.BlockSpec((tm, half), lambda i: (i, COL_CKV // half + 1)),
            pl.BlockSpec((tm, HEAD_DIM), lambda i: (i, COL_KR // HEAD_DIM)),
            pl.BlockSpec((1, KV_LORA), lambda i: (0, 0)),
            pl.BlockSpec((1, KV_LORA, MLA_HEADS * 2 * HEAD_DIM), lambda i: (l, 0, 0)),
            tab_spec, tab_spec, tab_spec,
        ],
        out_specs=[
            pl.BlockSpec((tm, MLA_HEADS * HEAD_DIM), lambda i: (i, 0)),
            pl.BlockSpec((tm, MLA_HEADS * HEAD_DIM), lambda i: (i, 0)),
            pl.BlockSpec((tm, HEAD_DIM), lambda i: (i, 0)),
        ],
        out_shape=[
            jax.ShapeDtypeStruct((m, MLA_HEADS * HEAD_DIM), BF16),
            jax.ShapeDtypeStruct((m, MLA_HEADS * HEAD_DIM), BF16),
            jax.ShapeDtypeStruct((m, HEAD_DIM), BF16),
        ],
        compiler_params=_cparams(("parallel",)),
        name="kv_proj",
    )(p, p, p, g.reshape(1, KV_LORA), w, *tabs)


def _softmax_pv(s_list, v_list, ones_col):
    m = jnp.max(s_list[0], axis=-1, keepdims=True)
    for s in s_list[1:]:
        m = jnp.maximum(m, jnp.max(s, axis=-1, keepdims=True))
    acc = None
    den = None
    for s, v in zip(s_list, v_list):
        p = jnp.exp2(s - m)
        if not ones_col:
            d = jnp.sum(p, axis=-1, keepdims=True)
            den = d if den is None else den + d
        o = _dot(p.astype(BF16), v)
        acc = o if acc is None else acc + o
    if ones_col:
        return acc[:, :HEAD_DIM] / acc[:, HEAD_DIM:]
    return acc / den


def _mla_kernel(*refs, hs, chains, n_lat):
    if n_lat:
        q_ref, kn_ref, kr_ref, v_ref, knc_ref, krc_ref, vc_ref, o_ref, k_scr, v_scr = refs
    else:
        q_ref, knc_ref, krc_ref, vc_ref, o_ref, k_scr, v_scr = refs
    n_ctx = knc_ref.shape[0]
    ones = jnp.ones((n_lat + n_ctx, MLA_VW - HEAD_DIM), BF16)
    for j in range(hs):
        src = slice(j * HEAD_DIM, (j + 1) * HEAD_DIM)
        nope = slice(j * MLA_QK, j * MLA_QK + HEAD_DIM)
        rot = slice(j * MLA_QK + HEAD_DIM, (j + 1) * MLA_QK)
        val = slice(j * MLA_VW, j * MLA_VW + HEAD_DIM)
        if n_lat:
            k_scr[:n_lat, nope] = kn_ref[:, src]
            k_scr[:n_lat, rot] = kr_ref[...]
            v_scr[:n_lat, val] = v_ref[:, src]
        k_scr[n_lat:, nope] = knc_ref[:, src]
        k_scr[n_lat:, rot] = krc_ref[...]
        v_scr[n_lat:, val] = vc_ref[:, src]
        v_scr[:, j * MLA_VW + HEAD_DIM:(j + 1) * MLA_VW] = ones
    tq = q_ref.shape[0] // chains
    for j in range(hs):
        qk_cols = slice(j * MLA_QK, (j + 1) * MLA_QK)
        v_cols = slice(j * MLA_VW, (j + 1) * MLA_VW)
        for c in range(chains):
            rows = slice(c * tq, (c + 1) * tq)
            s = _dot_nt(q_ref[rows, qk_cols], k_scr[:, qk_cols])
            o_ref[rows, j * HEAD_DIM:(j + 1) * HEAD_DIM] = _softmax_pv([s], [v_scr[:, v_cols]], True).astype(
                o_ref.dtype)


def _mla_scratch(n_keys, hs):
    return [pltpu.VMEM((n_keys, hs * MLA_QK), BF16), pltpu.VMEM((n_keys, hs * MLA_VW), BF16)]


def _mla_latent(q, kn, v, kr, *, hs=4, chains=4):
    lat = pl.BlockSpec((SEQ, hs * HEAD_DIM), lambda b, h: (b, h))
    ctx = pl.BlockSpec((CTX_LEN, hs * HEAD_DIM), lambda b, h: (CTX_BLOCK0 + b, h))
    return pl.pallas_call(
        functools.partial(_mla_kernel, hs=hs, chains=chains, n_lat=SEQ),
        grid=(BATCH, MLA_HEADS // hs),
        in_specs=[
            pl.BlockSpec((SEQ, hs * MLA_QK), lambda b, h: (b, h)),
            lat,
            pl.BlockSpec((SEQ, HEAD_DIM), lambda b, h: (b, 0)),
            lat,
            ctx,
            pl.BlockSpec((CTX_LEN, HEAD_DIM), lambda b, h: (CTX_BLOCK0 + b, 0)),
            ctx,
        ],
        out_specs=pl.BlockSpec((SEQ, hs * HEAD_DIM), lambda b, h: (b, h)),
        out_shape=jax.ShapeDtypeStruct((N_LAT, MLA_HEADS * HEAD_DIM), BF16),
        scratch_shapes=_mla_scratch(SEQ + CTX_LEN, hs),
        compiler_params=_cparams(("parallel", "parallel")),
        name="mla_latent",
    )(q, kn, kr, v, kn, kr, v)


def _mla_ctx(q, kn, v, kr):
    wide = pl.BlockSpec((CTX_LEN, MLA_HEADS * HEAD_DIM), lambda b: (CTX_BLOCK0 + b, 0))
    return pl.pallas_call(
        functools.partial(_mla_kernel, hs=MLA_HEADS, chains=1, n_lat=0),
        grid=(BATCH,),
        in_specs=[
            pl.BlockSpec((CTX_LEN, MLA_HEADS * MLA_QK), lambda b: (CTX_BLOCK0 + b, 0)),
            wide,
            pl.BlockSpec((CTX_LEN, HEAD_DIM), lambda b: (CTX_BLOCK0 + b, 0)),
            wide,
        ],
        out_specs=pl.BlockSpec((CTX_LEN, MLA_HEADS * HEAD_DIM), lambda b: (b, 0)),
        out_shape=jax.ShapeDtypeStruct((N_CTX, MLA_HEADS * HEAD_DIM), BF16),
        scratch_shapes=_mla_scratch(CTX_LEN, MLA_HEADS),
        compiler_params=_cparams(("parallel",)),
        name="mla_ctx",
    )(q, kn, kr, v)


def _na_ctx_kernel(q_ref, k_ref, v_ref, o_ref):
    for h in range(NA_HEADS):
        cols = slice(h * HEAD_DIM, (h + 1) * HEAD_DIM)
        s = _dot_nt(q_ref[:, cols], k_ref[:, cols]) * (NA_SCALE * LOG2E)
        o_ref[:, cols] = _softmax_pv([s], [v_ref[:, cols]], False).astype(o_ref.dtype)


def _na_ctx(p):
    width = NA_HEADS * HEAD_DIM
    col0 = COL_NA // width

    def spec(c):
        return pl.BlockSpec((CTX_LEN, width), lambda b: (CTX_BLOCK0 + b, col0 + c))

    return pl.pallas_call(
        _na_ctx_kernel,
        grid=(BATCH,),
        in_specs=[spec(0), spec(1), spec(2)],
        out_specs=pl.BlockSpec((CTX_LEN, width), lambda b: (b, 0)),
        out_shape=jax.ShapeDtypeStruct((N_CTX, width), BF16),
        compiler_params=_cparams(("parallel",)),
        name="na_ctx",
    )(p, p, p)


def _na_plan():
    rows = SEQ // GRID_W
    invalid = 2 * NA_KH - 1
    pairs, plan, starts = [], [], []
    for t in range(rows // NA_TR):
        kw0 = int(np.clip(NA_TR * t - NA_KH // 2, 0, rows - NA_WR))
        starts.append(kw0)
        tile = []
        for ri in range(NA_TR):
            r = NA_TR * t + ri
            r0 = int(np.clip(r - NA_KH // 2, 0, rows - NA_KH))
            assert kw0 <= r0 and r0 + NA_KH <= kw0 + NA_WR
            row = []
            for kp in range(NA_WR // 2):
                pair = []
                for kr in (kw0 + 2 * kp, kw0 + 2 * kp + 1):
                    pair.append(kr - r + NA_KH - 1 if r0 <= kr < r0 + NA_KH else invalid)
                pair = tuple(pair)
                if pair not in pairs:
                    pairs.append(pair)
                row.append(pairs.index(pair))
            tile.append(row)
        plan.append(tile)
    return starts, plan, pairs


def _na_bias_pairs(rpb, pairs):
    c = np.arange(GRID_W)
    c0 = np.clip(c - NA_KW // 2, 0, GRID_W - NA_KW)
    col_ok = (c[None, :] >= c0[:, None]) & (c[None, :] < c0[:, None] + NA_KW)
    col_idx = np.clip(c[None, :] - c[:, None] + NA_KW - 1, 0, 2 * NA_KW - 2)
    onehot = (col_idx[None] == np.arange(2 * NA_KW - 1)[:, None, None]).astype(np.float32)
    t = jnp.einsum("lhdj,jck->lhdck", rpb.astype(F32), onehot, precision=lax.Precision.HIGHEST) * LOG2E
    t = jnp.where(col_ok, t, MASK_VALUE)
    masked = jnp.full(t.shape[:2] + (GRID_W, GRID_W), MASK_VALUE, F32)
    slabs = [t[:, :, d] for d in range(2 * NA_KH - 1)] + [masked]
    return jnp.stack([jnp.concatenate([slabs[a], slabs[b]], axis=-1) for a, b in pairs], axis=2)


NA_HS = 2


def _na_kernel(q_ref, k_ref, v_ref, kc_ref, vc_ref, t2_ref, o_ref, vo_ref, vco_ref, *, starts, plan):
    ones = jnp.ones((SEQ, HEAD_DIM), BF16)
    for j in range(NA_HS):
        cols = slice(j * HEAD_DIM, (j + 1) * HEAD_DIM)
        wide = slice(2 * j * HEAD_DIM, 2 * (j + 1) * HEAD_DIM)
        vo_ref[:, 2 * j * HEAD_DIM:(2 * j + 1) * HEAD_DIM] = v_ref[:, cols]
        vo_ref[:, (2 * j + 1) * HEAD_DIM:2 * (j + 1) * HEAD_DIM] = ones
        vco_ref[:, 2 * j * HEAD_DIM:(2 * j + 1) * HEAD_DIM] = vc_ref[:, cols]
        vco_ref[:, (2 * j + 1) * HEAD_DIM:2 * (j + 1) * HEAD_DIM] = ones[:CTX_LEN]
        kc = kc_ref[:, cols]
        vc = vco_ref[:, wide]
        for t, (kw0, tile) in enumerate(zip(starts, plan)):
            rows = slice(t * NA_TQ, (t + 1) * NA_TQ)
            win = slice(kw0 * GRID_W, kw0 * GRID_W + NA_TK)
            q = q_ref[rows, cols]
            bias = jnp.concatenate(
                [jnp.concatenate([t2_ref[0, j, idx] for idx in row], axis=1) for row in tile], axis=0)
            s = _dot_nt(q, k_ref[win, cols]) * (NA_SCALE * LOG2E) + bias
            sc = _dot_nt(q, kc) * (NA_SCALE * LOG2E)
            o_ref[rows, cols] = _softmax_pv([s, sc], [vo_ref[win, wide], vc], True).astype(o_ref.dtype)


def _na_latent(p, t2, l, starts, plan):
    width = NA_HS * HEAD_DIM
    cq = COL_NA // width
    ck = cq + NA_HEADS // NA_HS
    cv = ck + NA_HEADS // NA_HS
    n_pairs = t2.shape[2]
    return pl.pallas_call(
        functools.partial(_na_kernel, starts=starts, plan=plan),
        grid=(NA_HEADS // NA_HS, BATCH),
        in_specs=[
            pl.BlockSpec((SEQ, width), lambda h, b: (b, cq + h)),
            pl.BlockSpec((SEQ, width), lambda h, b: (b, ck + h)),
            pl.BlockSpec((SEQ, width), lambda h, b: (b, cv + h)),
            pl.BlockSpec((CTX_LEN, width), lambda h, b: (CTX_BLOCK0 + b, ck + h)),
            pl.BlockSpec((CTX_LEN, width), lambda h, b: (CTX_BLOCK0 + b, cv + h)),
            pl.BlockSpec((1, NA_HS, n_pairs, GRID_W, 2 * GRID_W), lambda h, b: (l, h, 0, 0, 0)),
        ],
        out_specs=pl.BlockSpec((SEQ, width), lambda h, b: (b, h)),
        out_shape=jax.ShapeDtypeStruct((N_LAT, NA_HEADS * HEAD_DIM), BF16),
        scratch_shapes=[pltpu.VMEM((SEQ, 2 * width), BF16), pltpu.VMEM((CTX_LEN, 2 * width), BF16)],
        compiler_params=_cparams(("parallel", "parallel")),
        name="na_latent",
    )(p, p, p, p, p, t2)


def _conv_kernel(v_ref, x1_ref, x2_ref, wv_ref, w1_ref, w2_ref, bv_ref, b1_ref, b2_ref, zin_ref, x2o_ref):
    n = v_ref.shape[0]
    row = lax.broadcasted_iota(jnp.int32, (n, 1), 0)

    def short_conv(p_ref, w_ref, b_ref):
        p = p_ref[...].astype(F32)
        prev = jnp.where(row == 0, 0.0, pltpu.roll(p, 1, 0))
        nxt = jnp.where(row == n - 1, 0.0, pltpu.roll(p, n - 1, 0))
        w = w_ref[0]
        return prev * w[0:1] + p * w[1:2] + nxt * w[2:3] + b_ref[0]

    zin_ref[...] = (short_conv(x1_ref, w1_ref, b1_ref) * short_conv(v_ref, wv_ref, bv_ref)).astype(zin_ref.dtype)
    x2o_ref[...] = short_conv(x2_ref, w2_ref, b2_ref).astype(x2o_ref.dtype)


def _hy_conv(p, conv_w, conv_b, l, n, row_block0, *, tc=256):
    nc = HY_WIDTH // tc

    def seg(s):
        return (pl.BlockSpec((n, tc), lambda b, j: (row_block0 + b, s * nc + j)),
                pl.BlockSpec((1, 3, tc), lambda b, j: (l, 0, s * nc + j)),
                pl.BlockSpec((1, 1, tc), lambda b, j: (l, 0, s * nc + j)))

    (pv, wv, bv), (p1, w1, b1), (p2, w2, b2) = seg(0), seg(1), seg(2)
    out_spec = pl.BlockSpec((n, tc), lambda b, j: (0, b * nc + j))
    out = jax.ShapeDtypeStruct((n, BATCH * HY_WIDTH), BF16)
    conv_b = conv_b.reshape(DEPTH, 1, -1)
    return pl.pallas_call(
        _conv_kernel,
        grid=(BATCH, nc),
        in_specs=[pv, p1, p2, wv, w1, w2, bv, b1, b2],
        out_specs=[out_spec, out_spec],
        out_shape=[out, out],
        compiler_params=_cparams(("parallel", "parallel")),
        name="hy_conv",
    )(p, p, p, conv_w, conv_w, conv_w, conv_b, conv_b, conv_b)


def _filt_kernel(z_ref, t_ref, dl_ref, w1_ref, b1_ref, w2_ref, b2_ref, w3_ref, fr_ref, hs_ref, ha_ref, kn_ref):
    n = z_ref.shape[0]
    fr = fr_ref[...]
    h = jnp.sin(fr * (_dot_hi(z_ref[...], w1_ref[...]) + b1_ref[...]))
    h = jnp.sin(fr * (_dot_hi(h, w2_ref[...]) + b2_ref[...]))
    h = _dot_hi(h, w3_ref[...])
    decay = jnp.exp(-t_ref[...] * dl_ref[...])
    row = lax.broadcasted_iota(jnp.int32, (n, 1), 0)
    hf = h[:, :HY_WIDTH] * decay
    hb = jnp.where(row == 0, 0.0, h[:, HY_WIDTH:] * decay)
    hs = hf + hb
    hs_ref[...] = hs.astype(hs_ref.dtype)
    ha_ref[...] = (hf - hb).astype(ha_ref.dtype)
    sign = jnp.where((row & 1) == 0, 1.0, -1.0)
    kn_ref[...] = jnp.sum(hs * sign, axis=0, keepdims=True)


def _hy_filter_taps(n, f_w1, f_b1, f_w2, f_b2, f_w3, f_freq):
    pos = jnp.arange(n, dtype=F32)
    t = jnp.linspace(0.0, 1.0, n, dtype=F32)
    bands = jnp.linspace(1e-4, HY_POS_BANDS - 1, HY_POS_BANDS, dtype=F32)
    ang = (2.0 * math.pi / n) * pos[:, None] * bands[None, :]
    z = jnp.concatenate([t[:, None], jnp.cos(ang), -jnp.sin(ang)], axis=-1)
    pad = HY_FILTER_HIDDEN - z.shape[1]
    z = jnp.pad(z, ((0, 0), (0, pad)))
    w1 = jnp.pad(f_w1.astype(F32), ((0, pad), (0, 0)))
    deltas = jnp.abs(jnp.linspace(math.log(HY_DECAY_TARGET) / HY_FAST_DECAY,
                                  math.log(HY_DECAY_TARGET) / HY_SLOW_DECAY, HY_WIDTH, dtype=F32))
    hid = HY_FILTER_HIDDEN
    return pl.pallas_call(
        _filt_kernel,
        out_shape=[
            jax.ShapeDtypeStruct((n, HY_WIDTH), BF16),
            jax.ShapeDtypeStruct((n, HY_WIDTH), BF16),
            jax.ShapeDtypeStruct((1, HY_WIDTH), F32),
        ],
        compiler_params=pltpu.CompilerParams(vmem_limit_bytes=VMEM_LIMIT),
        name="hy_filter",
    )(z, t[:, None], deltas[None, :], w1, f_b1.reshape(1, hid), f_w2, f_b2.reshape(1, hid), f_w3,
      f_freq.reshape(1, hid))


def _dft_tables(n):
    lo = 16 if n < 1024 else 32
    hi = n // lo
    t = jnp.arange(n, dtype=jnp.int32)[None, :]
    big = 2 * n

    def ang(ff):
        return ((ff * t) % big).astype(F32) * (2.0 * math.pi / big)

    a = ang(lo * jnp.arange(hi, dtype=jnp.int32)[:, None])
    b = ang(jnp.arange(lo, dtype=jnp.int32)[:, None])
    ca, sa, cb, sb = jnp.cos(a), jnp.sin(a), jnp.cos(b), jnp.sin(b)
    cos = (ca[:, None, :] * cb[None, :, :] - sa[:, None, :] * sb[None, :, :]).reshape(n, n)
    base = -(sa[:, None, :] * cb[None, :, :] + ca[:, None, :] * sb[None, :, :]).reshape(n, n)
    idx = jnp.arange(n)
    alt = jnp.where(idx % 2 == 0, 1.0, -1.0).astype(F32)
    msin = jnp.where(idx[:, None] == 0, alt[None, :], base)
    msin_t = jnp.where(idx[None, :] == 0, alt[:, None], base)
    return cos.astype(BF16), msin.astype(BF16), msin_t.astype(BF16)


def _dft_filt_kernel(c_ref, s_ref, hs_ref, ha_ref, kr_ref, ki_ref):
    kr_ref[...] = _dot(c_ref[...], hs_ref[...])
    ki_ref[...] = _dot(s_ref[...], ha_ref[...])


def _hy_filter_spectrum(cos, msin, hs, ha, *, tf):
    n = cos.shape[0]
    tf = min(tf, n)
    tab = pl.BlockSpec((tf, n), lambda i: (i, 0))
    taps = pl.BlockSpec((n, HY_WIDTH), lambda i: (0, 0))
    out = pl.BlockSpec((tf, HY_WIDTH), lambda i: (i, 0))
    return pl.pallas_call(
        _dft_filt_kernel,
        grid=(n // tf,),
        in_specs=[tab, tab, taps, taps],
        out_specs=[out, out],
        out_shape=[jax.ShapeDtypeStruct((n, HY_WIDTH), F32)] * 2,
        compiler_params=_cparams(("parallel",)),
        name="hy_filter_dft",
    )(cos, msin, hs, ha)


def _batch_cols(b):
    return slice(b * HY_WIDTH, (b + 1) * HY_WIDTH)


def _dft_fwd_kernel(c_ref, s_ref, x_ref, kr_ref, ki_ref, kn_ref, yr_ref, yi_ref, *, tf, inv_n):
    x = x_ref[...]
    zr = _dot(c_ref[...], x)
    zi = _dot(s_ref[...], x)
    row = pl.program_id(0) * tf + lax.broadcasted_iota(jnp.int32, (tf, 1), 0)
    bin0 = row == 0
    wt = jnp.where(bin0, inv_n, 2.0 * inv_n)
    kr = kr_ref[...] * wt
    ki = jnp.where(bin0, 0.0, ki_ref[...] * wt)
    kr_im = jnp.where(bin0, kn_ref[...] * wt, kr)
    for b in range(BATCH):
        cols = _batch_cols(b)
        yr_ref[:, cols] = (zr[:, cols] * kr - zi[:, cols] * ki).astype(yr_ref.dtype)
        yi_ref[:, cols] = (zr[:, cols] * ki + zi[:, cols] * kr_im).astype(yi_ref.dtype)


def _hy_dft_fwd(cos, msin, zin, kr, ki, kn, *, tf):
    n = cos.shape[0]
    tf = min(tf, n)
    tab = pl.BlockSpec((tf, n), lambda i: (i, 0))
    filt = pl.BlockSpec((tf, HY_WIDTH), lambda i: (i, 0))
    out = pl.BlockSpec((tf, BATCH * HY_WIDTH), lambda i: (i, 0))
    whole = pl.BlockSpec((n, BATCH * HY_WIDTH), lambda i: (0, 0), pipeline_mode=pl.Buffered(1))
    return pl.pallas_call(
        functools.partial(_dft_fwd_kernel, tf=tf, inv_n=1.0 / (2 * n)),
        grid=(n // tf,),
        in_specs=[tab, tab, whole, filt, filt, pl.BlockSpec((1, HY_WIDTH), lambda i: (0, 0))],
        out_specs=[out, out],
        out_shape=[jax.ShapeDtypeStruct((n, BATCH * HY_WIDTH), BF16)] * 2,
        compiler_params=_cparams(("parallel",)),
        name="hy_dft_fwd",
    )(cos, msin, zin, kr, ki, kn)


def _dft_inv_kernel(c_ref, st_ref, yr_ref, yi_ref, zin_ref, x2_ref, b_ref, o_ref):
    y = _dot(c_ref[...], yr_ref[...]) + _dot(st_ref[...], yi_ref[...])
    bias = b_ref[0]
    for b in range(BATCH):
        cols = _batch_cols(b)
        yb = y[:, cols] + zin_ref[:, cols].astype(F32) * bias
        o_ref[b] = (x2_ref[:, cols].astype(F32) * yb).astype(o_ref.dtype)


def _hy_dft_inv(cos, msin_t, yr, yi, zin, x2, bias, l, *, tt):
    n = cos.shape[0]
    tt = min(tt, n)
    tab = pl.BlockSpec((tt, n), lambda i: (i, 0))
    whole = pl.BlockSpec((n, BATCH * HY_WIDTH), lambda i: (0, 0), pipeline_mode=pl.Buffered(1))
    rows = pl.BlockSpec((tt, BATCH * HY_WIDTH), lambda i: (i, 0))
    out = pl.pallas_call(
        _dft_inv_kernel,
        grid=(n // tt,),
        in_specs=[tab, tab, whole, whole, rows, rows, pl.BlockSpec((1, 1, HY_WIDTH), lambda i: (l, 0, 0))],
        out_specs=pl.BlockSpec((BATCH, tt, HY_WIDTH), lambda i: (0, i, 0)),
        out_shape=jax.ShapeDtypeStruct((BATCH, n, HY_WIDTH), BF16),
        compiler_params=_cparams(("parallel",)),
        name="hy_dft_inv",
    )(cos, msin_t, yr, yi, zin, x2, bias.reshape(DEPTH, 1, HY_WIDTH))
    return out.reshape(BATCH * n, HY_WIDTH)


def _hyena(p, l, n, row_block0, tables, conv_w, conv_b, f_w1, f_b1, f_w2, f_b2, f_w3, f_freq, bias):
    cos, msin, msin_t = tables
    zin, x2 = _hy_conv(p, conv_w, conv_b, l, n, row_block0)
    hs, ha, kn = _hy_filter_taps(n, f_w1[l], f_b1[l], f_w2[l], f_b2[l], f_w3[l], f_freq[l])
    kr, ki = _hy_filter_spectrum(cos, msin, hs, ha, tf=512)
    yr, yi = _hy_dft_fwd(cos, msin, zin, kr, ki, kn, tf=512)
    return _hy_dft_inv(cos, msin_t, yr, yi, zin, x2, bias, l, tt=512)


def _rope_tables():
    tok = jnp.arange(SEQ)
    row = (tok // GRID_W).astype(F32)
    col = (tok % GRID_W).astype(F32)
    n_freq = MLA_ROPE // 4
    inv = ROPE_THETA ** (-jnp.arange(n_freq, dtype=F32) / n_freq)
    ang = jnp.concatenate([row[:, None] * inv, col[:, None] * inv], axis=-1)
    cos, sin = jnp.cos(ang), jnp.sin(ang)
    half = MLA_ROPE // 2
    zeros = jnp.zeros((SEQ, half), F32)
    rest = HEAD_DIM - MLA_ROPE
    cos_t = jnp.concatenate([cos, cos, jnp.ones((SEQ, rest), F32)], axis=-1)
    sin_a = jnp.concatenate([-sin, zeros, jnp.zeros((SEQ, rest), F32)], axis=-1)
    sin_b = jnp.concatenate([zeros, sin, jnp.zeros((SEQ, rest), F32)], axis=-1)
    ident = jnp.ones((N_CTX, HEAD_DIM), F32)
    none = jnp.zeros((N_CTX, HEAD_DIM), F32)
    return (jnp.concatenate([cos_t, ident]), jnp.concatenate([sin_a, none]), jnp.concatenate([sin_b, none]))


def _layout_w_uq(w):
    w = w.reshape(Q_LORA, MLA_HEADS, HEAD_DIM + MLA_ROPE)
    w = jnp.pad(w, ((0, 0), (0, 0), (0, MLA_QK - HEAD_DIM - MLA_ROPE)))
    return w.reshape(Q_LORA, MLA_HEADS * MLA_QK).astype(BF16)


def kernel(x, c, ctx, c_ctx, w_ada, b_ada, g_attn_pre, g_attn_post, g_ffn_pre, g_ffn_post, w_in, hy_conv_w, hy_conv_b, hy_f_w1, hy_f_b1, hy_f_w2, hy_f_b2, hy_f_w3, hy_f_freq, hy_bias, mla_g_q, mla_w_uq, mla_g_kv, mla_w_ukv, na_rpb, w_out, w_ffn_gate, w_ffn_up, w_ffn_down):
    cc = jnp.concatenate([c, c_ctx[None, :], jnp.zeros((8 - BATCH - 1, D_MODEL), F32)], axis=0)
    mods_all = _ada(cc, w_ada, b_ada)
    mods = [mods_all[l].reshape(8, 1, 6 * D_MODEL) for l in range(DEPTH)]

    rope_tabs = _rope_tables()
    dft_lat = _dft_tables(SEQ)
    dft_ctx = _dft_tables(CTX_LEN)
    na_starts, na_plan, na_pairs = _na_plan()
    na_t2 = _na_bias_pairs(na_rpb, na_pairs)
    w_in_t = jnp.swapaxes(w_in, 1, 2)
    hy_w = (hy_conv_w, hy_conv_b, hy_f_w1, hy_f_b1, hy_f_w2, hy_f_b2, hy_f_w3, hy_f_freq, hy_bias)

    res = (x.reshape(N_LAT, D_MODEL), ctx.reshape(N_CTX, D_MODEL))
    xn = _prenorm(res[0], res[1], g_attn_pre[0], mods[0])

    for l in range(DEPTH):
        ctx_out = l < DEPTH - 1
        m_rows = N_TOK if ctx_out else N_LAT
        p = _in_proj(xn, w_in_t, l)

        q = _q_proj(p, m_rows, mla_g_q[l], _layout_w_uq(mla_w_uq[l]), rope_tabs)
        kn, v, kr = _kv_proj(p, mla_g_kv[l], mla_w_ukv, l, rope_tabs)
        mla = [_mla_latent(q, kn, v, kr), None]
        na = [_na_latent(p, na_t2, l, na_starts, na_plan), None]
        hy = [_hyena(p, l, SEQ, 0, dft_lat, *hy_w), None]

        if ctx_out:
            mla[1] = _mla_ctx(q, kn, v, kr)
            na[1] = _na_ctx(p)
            hy[1] = _hyena(p, l, CTX_LEN, CTX_BLOCK0, dft_ctx, *hy_w)
            res_l = res
        else:
            res_l = (res[0], None)

        stream, xn = _proj_post([tuple(hy), tuple(mla), tuple(na)], w_out, l, g_attn_post[l], mods[l], 2, res_l,
                                m_rows, nxt=(g_ffn_pre[l], mods[l], 3, 4), tm=POST_TM_OUT)
        h = _ffn_up(xn, w_ffn_gate, w_ffn_up, l)
        nxt = (g_attn_pre[l + 1], mods[l + 1], 0, 1) if ctx_out else None
        stream, xn = _proj_post([(h, None)], w_ffn_down, l, g_ffn_post[l], mods[l], 5, (stream, None), m_rows,
                                nxt=nxt, tm=POST_TM_DOWN)
        res = (stream, None)

    return stream.reshape(BATCH, SEQ, D_MODEL)
```

```python
import functools
import math

import jax
import jax.numpy as jnp
import numpy as np
from jax import lax
from jax.experimental import pallas as pl
from jax.experimental.pallas import tpu as pltpu

F32 = jnp.float32
BF16 = jnp.bfloat16

D_MODEL = 2048
BATCH = 4
SEQ = 2048
DEPTH = 2
GRID_W = 64
CTX_LEN = 256
HEAD_DIM = 128
HY_WIDTH = D_MODEL // 4
HY_FILTER_HIDDEN = 64
HY_POS_BANDS = 16
HY_DECAY_TARGET = 1e-2
HY_FAST_DECAY = 0.3
HY_SLOW_DECAY = 1.5
MLA_HEADS = (D_MODEL // 2) // HEAD_DIM
MLA_ROPE = 64
Q_LORA = 3 * D_MODEL // 8
KV_LORA = D_MODEL // 4
MLA_SCALE = (HEAD_DIM + MLA_ROPE) ** -0.5
NA_HEADS = (D_MODEL // 4) // HEAD_DIM
NA_KH = 8
NA_KW = 16
NA_SCALE = HEAD_DIM ** -0.5
FFN_HIDDEN = ((8 * D_MODEL + 3 * 256 - 1) // (3 * 256)) * 256
ROPE_THETA = 10000.0
RMS_EPS = 1e-6
MASK_VALUE = -1e30
LOG2E = math.log2(math.e)

N_LAT = BATCH * SEQ
N_CTX = BATCH * CTX_LEN
N_TOK = N_LAT + N_CTX
CTX_BLOCK0 = N_LAT // CTX_LEN

IN_TN = 512
COL_HY = 0
COL_CQ = 3 * HY_WIDTH
COL_CKV = COL_CQ + Q_LORA
COL_KR = COL_CKV + KV_LORA
W_IN_NA = COL_KR + MLA_ROPE
COL_NA = -(-W_IN_NA // IN_TN) * IN_TN
P_COLS = COL_NA + 3 * NA_HEADS * HEAD_DIM
MLA_QK = 256
MLA_VW = 256

NA_TR = 4
NA_WR = 12
NA_TQ = NA_TR * GRID_W
NA_TK = NA_WR * GRID_W

VMEM_LIMIT = 52 * 1024 * 1024


def _cparams(sem):
    return pltpu.CompilerParams(dimension_semantics=sem, vmem_limit_bytes=VMEM_LIMIT)


def _dot(a, b):
    return jnp.dot(a, b, preferred_element_type=F32)


def _dot_nt(a, b):
    return lax.dot_general(a, b, (((1,), (1,)), ((), ())), preferred_element_type=F32)


def _dot_hi(a, b):
    return jnp.dot(a, b, preferred_element_type=F32, precision=lax.Precision.HIGHEST)


def _rms(x, g):
    ms = jnp.mean(x * x, axis=-1, keepdims=True)
    return x * lax.rsqrt(ms + RMS_EPS) * g


def _mod_row(tm):
    n_lat, per_b = N_LAT // tm, SEQ // tm
    return lambda i: jnp.where(i < n_lat, i // per_b, BATCH)


def _rope_row(tm):
    n_lat, per_b = N_LAT // tm, SEQ // tm
    return lambda i: jnp.where(i < n_lat, i % per_b, per_b + i - n_lat)


def _ada_kernel(c_ref, w_ref, b_ref, o_ref):
    a = c_ref[...]
    a = a * jax.nn.sigmoid(a)
    o_ref[0] = _dot(a.astype(BF16), w_ref[0].astype(BF16)) + b_ref[0]


def _ada(cc, w_ada, b_ada):
    tn = 1024
    n = w_ada.shape[-1]
    return pl.pallas_call(
        _ada_kernel,
        grid=(DEPTH, n // tn),
        in_specs=[
            pl.BlockSpec((8, D_MODEL), lambda l, j: (0, 0)),
            pl.BlockSpec((1, D_MODEL, tn), lambda l, j: (l, 0, j)),
            pl.BlockSpec((1, 1, tn), lambda l, j: (l, 0, j)),
        ],
        out_specs=pl.BlockSpec((1, 8, tn), lambda l, j: (l, 0, j)),
        out_shape=jax.ShapeDtypeStruct((DEPTH, 8, n), F32),
        compiler_params=_cparams(("parallel", "parallel")),
        name="ada",
    )(cc, w_ada, b_ada.reshape(DEPTH, 1, n))


def _norm_mod_to(xn_ref, x_ref, g_ref, sh_ref, sc_ref, chunk=256):
    g = g_ref[...]
    sc = 1.0 + sc_ref[0]
    sh = sh_ref[0]

    def body(r, carry):
        rows = pl.ds(pl.multiple_of(r * chunk, chunk), chunk)
        xn_ref[rows, :] = (_rms(x_ref[rows, :], g) * sc + sh).astype(BF16)
        return carry

    lax.fori_loop(0, x_ref.shape[0] // chunk, body, 0)


def _prenorm_kernel(x_ref, c_ref, g_ref, sh_ref, sc_ref, o_ref, *, n_lat):
    i = pl.program_id(0)

    @pl.when(i < n_lat)
    def _():
        _norm_mod_to(o_ref, x_ref, g_ref, sh_ref, sc_ref)

    @pl.when(i >= n_lat)
    def _():
        _norm_mod_to(o_ref, c_ref, g_ref, sh_ref, sc_ref)


def _prenorm(x2d, ctx2d, g, mods, *, tm=1024):
    n_lat = N_LAT // tm
    row = _mod_row(tm)
    return pl.pallas_call(
        functools.partial(_prenorm_kernel, n_lat=n_lat),
        grid=(N_TOK // tm,),
        in_specs=[
            pl.BlockSpec((tm, D_MODEL), lambda i: (jnp.minimum(i, n_lat - 1), 0)),
            pl.BlockSpec((tm, D_MODEL), lambda i: (jnp.maximum(i - n_lat, 0), 0)),
            pl.BlockSpec((1, D_MODEL), lambda i: (0, 0)),
            pl.BlockSpec((1, 1, D_MODEL), lambda i: (row(i), 0, 0)),
            pl.BlockSpec((1, 1, D_MODEL), lambda i: (row(i), 0, 1)),
        ],
        out_specs=pl.BlockSpec((tm, D_MODEL), lambda i: (i, 0)),
        out_shape=jax.ShapeDtypeStruct((N_TOK, D_MODEL), BF16),
        compiler_params=_cparams(("parallel",)),
        name="prenorm",
    )(x2d, ctx2d, g.reshape(1, D_MODEL), mods, mods)


def _in_kernel(x_ref, w_ref, o_ref):
    o_ref[...] = _dot_nt(x_ref[...], w_ref[0].astype(BF16)).astype(o_ref.dtype)


def _in_proj(xn, w_in_t, l, *, tm=3072):
    n_head = COL_NA // IN_TN

    def w_row(j):
        per = IN_TN // MLA_ROPE
        return MLA_ROPE * jnp.where(j < n_head, j * per, W_IN_NA // MLA_ROPE + (j - n_head) * per)

    return pl.pallas_call(
        _in_kernel,
        grid=(N_TOK // tm, P_COLS // IN_TN),
        in_specs=[
            pl.BlockSpec((tm, D_MODEL), lambda i, j: (i, 0)),
            pl.BlockSpec((pl.Element(1), pl.Element(IN_TN), pl.Element(D_MODEL)), lambda i, j: (l, w_row(j), 0)),
        ],
        out_specs=pl.BlockSpec((tm, IN_TN), lambda i, j: (i, j)),
        out_shape=jax.ShapeDtypeStruct((N_TOK, P_COLS), BF16),
        compiler_params=_cparams(("parallel", "arbitrary")),
        name="in_proj",
    )(xn, w_in_t)


FFN_CW = 256


def _swiglu_kernel(x_ref, wg_ref, wu_ref, o_ref):
    xn = x_ref[...]
    for c in range(o_ref.shape[1] // FFN_CW):
        cols = slice(c * FFN_CW, (c + 1) * FFN_CW)
        gate = _dot(xn, wg_ref[0, :, cols].astype(BF16))
        up = _dot(xn, wu_ref[0, :, cols].astype(BF16))
        o_ref[:, cols] = (gate * jax.nn.sigmoid(gate) * up).astype(o_ref.dtype)


def _ffn_up(xn, wg, wu, l, *, tn=512):
    m = xn.shape[0]
    tm = 2048 if m % 2048 == 0 else 2304
    w_spec = pl.BlockSpec((1, D_MODEL, tn), lambda i, j: (l, 0, j))
    return pl.pallas_call(
        _swiglu_kernel,
        grid=(m // tm, FFN_HIDDEN // tn),
        in_specs=[pl.BlockSpec((tm, D_MODEL), lambda i, j: (i, 0)), w_spec, w_spec],
        out_specs=pl.BlockSpec((tm, tn), lambda i, j: (i, j)),
        out_shape=jax.ShapeDtypeStruct((m, FFN_HIDDEN), BF16),
        compiler_params=_cparams(("parallel", "arbitrary")),
        name="ffn_up",
    )(xn, wg, wu)


POST_TM_OUT = 512
POST_TM_DOWN = 256
POST_SUB = 128
POST_WCH = 256
POST_WSLOTS = 4
POST_SLOTS = 3


def _post_kernel(*refs, widths, two_src, emit_xn, n_lat, l, tm):
    refs = list(refs)
    n_act = len(widths)
    acts_lat = [refs.pop(0) for _ in range(n_act)]
    acts_ctx = [refs.pop(0) for _ in range(n_act)] if two_src else None
    w_hbm, g_ref, gate_ref, res_lat = (refs.pop(0) for _ in range(4))
    res_ctx = refs.pop(0) if two_src else None
    if emit_xn:
        g2_ref, sh_ref, sc_ref = (refs.pop(0) for _ in range(3))
    o_hbm = refs.pop(0)
    xn_ref = refs.pop(0) if emit_xn else None
    w_scr, stage, buf, sem_w, sem_res, sem_out = refs

    i = pl.program_id(0)
    n_i = pl.num_programs(0)
    n_chunks = sum(widths) // POST_WCH
    is_lat = i < n_lat
    is_ctx = jnp.logical_not(is_lat)

    def w_copy(c, slot):
        return pltpu.make_async_copy(w_hbm.at[l, pl.ds(c * POST_WCH, POST_WCH)], stage.at[slot], sem_w.at[slot])

    def res_copy(src, row0, slot):
        return pltpu.make_async_copy(src.at[pl.ds(row0, tm)], buf.at[slot], sem_res.at[slot])

    def res_start(tile, slot):
        if two_src:
            pl.when(tile < n_lat)(lambda: res_copy(res_lat, tile * tm, slot).start())
            pl.when(tile >= n_lat)(lambda: res_copy(res_ctx, (tile - n_lat) * tm, slot).start())
        else:
            res_copy(res_lat, tile * tm, slot).start()

    def out_copy(tile, slot):
        return pltpu.make_async_copy(buf.at[slot], o_hbm.at[pl.ds(tile * tm, tm)], sem_out.at[slot])

    @pl.when(i == 0)
    def _():
        res_start(0, 0)
        for c in range(POST_WSLOTS - 1):
            w_copy(c, c).start()

        def body(c, carry):
            s = c % POST_WSLOTS
            w_copy(c, s).wait()
            ahead = c + POST_WSLOTS - 1

            @pl.when(ahead < n_chunks)
            def _():
                w_copy(ahead, ahead % POST_WSLOTS).start()

            w_scr[pl.ds(pl.multiple_of(c * POST_WCH, POST_WCH), POST_WCH), :] = stage[s].astype(BF16)
            return carry

        lax.fori_loop(0, n_chunks, body, 0)

    slot = i % POST_SLOTS
    nslot = (i + 1) % POST_SLOTS

    @pl.when(i + 1 < n_i)
    def _():
        @pl.when(i >= POST_SLOTS - 1)
        def _():
            out_copy(i + 1 - POST_SLOTS, nslot).wait()

        res_start(i + 1, nslot)

    res_copy(res_lat, 0, slot).wait()

    def compute(act_refs):
        gg = gate_ref[0] * g_ref[...]
        if emit_xn:
            g2s = g2_ref[...] * (1.0 + sc_ref[0])
            sh = sh_ref[0]
        for r in range(tm // POST_SUB):
            rows = pl.ds(r * POST_SUB, POST_SUB)
            y = None
            k0 = 0
            for a_ref, kw in zip(act_refs, widths):
                part = _dot(a_ref[rows, :], w_scr[k0:k0 + kw, :])
                y = part if y is None else y + part
                k0 += kw
            x_new = buf[slot, rows, :] + _rms(y, gg)
            buf[slot, rows, :] = x_new
            if emit_xn:
                xn_ref[rows, :] = (_rms(x_new, g2s) + sh).astype(xn_ref.dtype)

    if two_src:
        pl.when(is_lat)(functools.partial(compute, acts_lat))
        pl.when(is_ctx)(functools.partial(compute, acts_ctx))
    else:
        compute(acts_lat)

    out_copy(i, slot).start()

    @pl.when(i == n_i - 1)
    def _():
        for back in range(POST_SLOTS):
            out_copy(i - back, (i - back) % POST_SLOTS).wait()


def _proj_post(acts, w, l, g, mods, gate_chunk, res, m_rows, nxt=None, *, tm):
    two_src = res[1] is not None
    n_lat = N_LAT // tm
    widths = tuple(a.shape[1] for a, _ in acts)
    k_total = sum(widths)
    emit_xn = nxt is not None
    row = _mod_row(tm)

    def lat_row(i):
        return jnp.minimum(i, n_lat - 1) if two_src else i

    def ctx_row(i):
        return jnp.maximum(i - n_lat, 0)

    def act_specs(rowf):
        return [pl.BlockSpec((tm, kw), lambda i: (rowf(i), 0)) for kw in widths]

    vec_spec = pl.BlockSpec((1, D_MODEL), lambda i: (0, 0))

    def mod_spec(chunk):
        return pl.BlockSpec((1, 1, D_MODEL), lambda i: (row(i), 0, chunk))

    hbm = pl.BlockSpec(memory_space=pl.ANY)
    in_specs = act_specs(lat_row)
    args = [a for a, _ in acts]
    if two_src:
        in_specs += act_specs(ctx_row)
        args += [c for _, c in acts]
    in_specs += [hbm, vec_spec, mod_spec(gate_chunk), hbm]
    args += [w, g.reshape(1, D_MODEL), mods, res[0]]
    if two_src:
        in_specs.append(hbm)
        args.append(res[1])
    out_specs = [hbm]
    out_shape = [jax.ShapeDtypeStruct((m_rows, D_MODEL), F32)]
    if emit_xn:
        g2, mods2, sh_chunk, sc_chunk = nxt
        in_specs += [vec_spec, mod_spec(sh_chunk), mod_spec(sc_chunk)]
        args += [g2.reshape(1, D_MODEL), mods2, mods2]
        out_specs.append(pl.BlockSpec((tm, D_MODEL), lambda i: (i, 0)))
        out_shape.append(jax.ShapeDtypeStruct((m_rows, D_MODEL), BF16))

    out = pl.pallas_call(
        functools.partial(_post_kernel, widths=widths, two_src=two_src, emit_xn=emit_xn, n_lat=n_lat, l=l, tm=tm),
        grid=(m_rows // tm,),
        in_specs=in_specs,
        out_specs=out_specs,
        out_shape=out_shape,
        scratch_shapes=[
            pltpu.VMEM((k_total, D_MODEL), BF16),
            pltpu.VMEM((POST_WSLOTS, POST_WCH, D_MODEL), F32),
            pltpu.VMEM((POST_SLOTS, tm, D_MODEL), F32),
            pltpu.SemaphoreType.DMA((POST_WSLOTS,)),
            pltpu.SemaphoreType.DMA((POST_SLOTS,)),
            pltpu.SemaphoreType.DMA((POST_SLOTS,)),
        ],
        compiler_params=_cparams(("arbitrary",)),
        name="proj_post",
    )(*args)
    return (out[0], out[1]) if emit_xn else (out[0], None)


def _rope128(r, cos_ref, sa_ref, sb_ref):
    return r * cos_ref[...] + pltpu.roll(r, 96, 1) * sa_ref[...] + pltpu.roll(r, 32, 1) * sb_ref[...]


def _q_kernel(x_ref, g_ref, w_ref, cos_ref, sa_ref, sb_ref, o_ref):
    xn = _rms(x_ref[...].astype(F32), g_ref[...] * (MLA_SCALE * LOG2E)).astype(BF16)
    for h in range(MLA_HEADS):
        acc = _dot(xn, w_ref[:, h * MLA_QK:(h + 1) * MLA_QK])
        o_ref[:, h * MLA_QK:h * MLA_QK + HEAD_DIM] = acc[:, :HEAD_DIM].astype(o_ref.dtype)
        o_ref[:, h * MLA_QK + HEAD_DIM:(h + 1) * MLA_QK] = _rope128(
            acc[:, HEAD_DIM:], cos_ref, sa_ref, sb_ref).astype(o_ref.dtype)


def _q_proj(p, m_rows, g, w, tabs, *, tm=1024):
    rope = _rope_row(tm)
    tab_spec = pl.BlockSpec((tm, HEAD_DIM), lambda i: (rope(i), 0))
    return pl.pallas_call(
        _q_kernel,
        grid=(m_rows // tm,),
        in_specs=[
            pl.BlockSpec((tm, Q_LORA), lambda i: (i, COL_CQ // Q_LORA)),
            pl.BlockSpec((1, Q_LORA), lambda i: (0, 0)),
            pl.BlockSpec((Q_LORA, MLA_HEADS * MLA_QK), lambda i: (0, 0)),
            tab_spec, tab_spec, tab_spec,
        ],
        out_specs=pl.BlockSpec((tm, MLA_HEADS * MLA_QK), lambda i: (i, 0)),
        out_shape=jax.ShapeDtypeStruct((m_rows, MLA_HEADS * MLA_QK), BF16),
        compiler_params=_cparams(("parallel",)),
        name="q_proj",
    )(p, g.reshape(1, Q_LORA), w, *tabs)


def _kv_kernel(xa_ref, xb_ref, kr_ref, g_ref, w_ref, cos_ref, sa_ref, sb_ref, k_ref, v_ref, krr_ref):
    half = KV_LORA // 2
    xa = xa_ref[...].astype(F32)
    xb = xb_ref[...].astype(F32)
    ms = (jnp.sum(xa * xa, axis=-1, keepdims=True) + jnp.sum(xb * xb, axis=-1, keepdims=True)) * (1.0 / KV_LORA)
    rs = lax.rsqrt(ms + RMS_EPS)
    g = g_ref[...]
    xna = (xa * rs * g[:, :half]).astype(BF16)
    xnb = (xb * rs * g[:, half:]).astype(BF16)
    lane = lax.broadcasted_iota(jnp.int32, (1, HEAD_DIM), 1)
    krr = _rope128(kr_ref[...].astype(F32), cos_ref, sa_ref, sb_ref)
    krr_ref[...] = jnp.where(lane < MLA_ROPE, krr, 0.0).astype(krr_ref.dtype)
    for h in range(MLA_HEADS):
        cols = slice(h * HEAD_DIM, (h + 1) * HEAD_DIM)
        w = w_ref[0, :, h * 2 * HEAD_DIM:(h + 1) * 2 * HEAD_DIM].astype(BF16)
        acc = _dot(xna, w[:half]) + _dot(xnb, w[half:])
        k_ref[:, cols] = acc[:, :HEAD_DIM].astype(k_ref.dtype)
        v_ref[:, cols] = acc[:, HEAD_DIM:].astype(v_ref.dtype)


def _kv_proj(p, g, w, l, tabs, *, tm=1024):
    m = p.shape[0]
    rope = _rope_row(tm)
    half = KV_LORA // 2
    tab_spec = pl.BlockSpec((tm, HEAD_DIM), lambda i: (rope(i), 0))
    return pl.pallas_call(
        _kv_kernel,
        grid=(m // tm,),
        in_specs=[
            pl.BlockSpec((tm, half), lambda i: (i, COL_CKV // half)),
            pl.BlockSpec((tm, half), lambda i: (i, COL_CKV // half + 1)),
            pl.BlockSpec((tm, HEAD_DIM), lambda i: (i, COL_KR // HEAD_DIM)),
            pl.BlockSpec((1, KV_LORA), lambda i: (0, 0)),
            pl.BlockSpec((1, KV_LORA, MLA_HEADS * 2 * HEAD_DIM), lambda i: (l, 0, 0)),
            tab_spec, tab_spec, tab_spec,
        ],
        out_specs=[
            pl.BlockSpec((tm, MLA_HEADS * HEAD_DIM), lambda i: (i, 0)),
            pl.BlockSpec((tm, MLA_HEADS * HEAD_DIM), lambda i: (i, 0)),
            pl.BlockSpec((tm, HEAD_DIM), lambda i: (i, 0)),
        ],
        out_shape=[
            jax.ShapeDtypeStruct((m, MLA_HEADS * HEAD_DIM), BF16),
            jax.ShapeDtypeStruct((m, MLA_HEADS * HEAD_DIM), BF16),
            jax.ShapeDtypeStruct((m, HEAD_DIM), BF16),
        ],
        compiler_params=_cparams(("parallel",)),
        name="kv_proj",
    )(p, p, p, g.reshape(1, KV_LORA), w, *tabs)


def _softmax_pv(s_list, v_list, ones_col):
    m = jnp.max(s_list[0], axis=-1, keepdims=True)
    for s in s_list[1:]:
        m = jnp.maximum(m, jnp.max(s, axis=-1, keepdims=True))
    acc = None
    den = None
    for s, v in zip(s_list, v_list):
        p = jnp.exp2(s - m)
        if not ones_col:
            d = jnp.sum(p, axis=-1, keepdims=True)
            den = d if den is None else den + d
        o = _dot(p.astype(BF16), v)
        acc = o if acc is None else acc + o
    if ones_col:
        return acc[:, :HEAD_DIM] / acc[:, HEAD_DIM:]
    return acc / den


def _mla_kernel(*refs, hs, chains, n_lat):
    if n_lat:
        q_ref, kn_ref, kr_ref, v_ref, knc_ref, krc_ref, vc_ref, o_ref, k_scr, v_scr = refs
    else:
        q_ref, knc_ref, krc_ref, vc_ref, o_ref, k_scr, v_scr = refs
    n_ctx = knc_ref.shape[0]
    ones = jnp.ones((n_lat + n_ctx, MLA_VW - HEAD_DIM), BF16)
    for j in range(hs):
        src = slice(j * HEAD_DIM, (j + 1) * HEAD_DIM)
        nope = slice(j * MLA_QK, j * MLA_QK + HEAD_DIM)
        rot = slice(j * MLA_QK + HEAD_DIM, (j + 1) * MLA_QK)
        val = slice(j * MLA_VW, j * MLA_VW + HEAD_DIM)
        if n_lat:
            k_scr[:n_lat, nope] = kn_ref[:, src]
            k_scr[:n_lat, rot] = kr_ref[...]
            v_scr[:n_lat, val] = v_ref[:, src]
        k_scr[n_lat:, nope] = knc_ref[:, src]
        k_scr[n_lat:, rot] = krc_ref[...]
        v_scr[n_lat:, val] = vc_ref[:, src]
        v_scr[:, j * MLA_VW + HEAD_DIM:(j + 1) * MLA_VW] = ones
    tq = q_ref.shape[0] // chains
    for j in range(hs):
        qk_cols = slice(j * MLA_QK, (j + 1) * MLA_QK)
        v_cols = slice(j * MLA_VW, (j + 1) * MLA_VW)
        for c in range(chains):
            rows = slice(c * tq, (c + 1) * tq)
            s = _dot_nt(q_ref[rows, qk_cols], k_scr[:, qk_cols])
            o_ref[rows, j * HEAD_DIM:(j + 1) * HEAD_DIM] = _softmax_pv([s], [v_scr[:, v_cols]], True).astype(
                o_ref.dtype)


def _mla_scratch(n_keys, hs):
    return [pltpu.VMEM((n_keys, hs * MLA_QK), BF16), pltpu.VMEM((n_keys, hs * MLA_VW), BF16)]


def _mla_latent(q, kn, v, kr, *, hs=4, chains=4):
    lat = pl.BlockSpec((SEQ, hs * HEAD_DIM), lambda b, h: (b, h))
    ctx = pl.BlockSpec((CTX_LEN, hs * HEAD_DIM), lambda b, h: (CTX_BLOCK0 + b, h))
    return pl.pallas_call(
        functools.partial(_mla_kernel, hs=hs, chains=chains, n_lat=SEQ),
        grid=(BATCH, MLA_HEADS // hs),
        in_specs=[
            pl.BlockSpec((SEQ, hs * MLA_QK), lambda b, h: (b, h)),
            lat,
            pl.BlockSpec((SEQ, HEAD_DIM), lambda b, h: (b, 0)),
            lat,
            ctx,
            pl.BlockSpec((CTX_LEN, HEAD_DIM), lambda b, h: (CTX_BLOCK0 + b, 0)),
            ctx,
        ],
        out_specs=pl.BlockSpec((SEQ, hs * HEAD_DIM), lambda b, h: (b, h)),
        out_shape=jax.ShapeDtypeStruct((N_LAT, MLA_HEADS * HEAD_DIM), BF16),
        scratch_shapes=_mla_scratch(SEQ + CTX_LEN, hs),
        compiler_params=_cparams(("parallel", "parallel")),
        name="mla_latent",
    )(q, kn, kr, v, kn, kr, v)


def _mla_ctx(q, kn, v, kr):
    wide = pl.BlockSpec((CTX_LEN, MLA_HEADS * HEAD_DIM), lambda b: (CTX_BLOCK0 + b, 0))
    return pl.pallas_call(
        functools.partial(_mla_kernel, hs=MLA_HEADS, chains=1, n_lat=0),
        grid=(BATCH,),
        in_specs=[
            pl.BlockSpec((CTX_LEN, MLA_HEADS * MLA_QK), lambda b: (CTX_BLOCK0 + b, 0)),
            wide,
            pl.BlockSpec((CTX_LEN, HEAD_DIM), lambda b: (CTX_BLOCK0 + b, 0)),
            wide,
        ],
        out_specs=pl.BlockSpec((CTX_LEN, MLA_HEADS * HEAD_DIM), lambda b: (b, 0)),
        out_shape=jax.ShapeDtypeStruct((N_CTX, MLA_HEADS * HEAD_DIM), BF16),
        scratch_shapes=_mla_scratch(CTX_LEN, MLA_HEADS),
        compiler_params=_cparams(("parallel",)),
        name="mla_ctx",
    )(q, kn, kr, v)


def _na_ctx_kernel(q_ref, k_ref, v_ref, o_ref):
    for h in range(NA_HEADS):
        cols = slice(h * HEAD_DIM, (h + 1) * HEAD_DIM)
        s = _dot_nt(q_ref[:, cols], k_ref[:, cols]) * (NA_SCALE * LOG2E)
        o_ref[:, cols] = _softmax_pv([s], [v_ref[:, cols]], False).astype(o_ref.dtype)


def _na_ctx(p):
    width = NA_HEADS * HEAD_DIM
    col0 = COL_NA // width

    def spec(c):
        return pl.BlockSpec((CTX_LEN, width), lambda b: (CTX_BLOCK0 + b, col0 + c))

    return pl.pallas_call(
        _na_ctx_kernel,
        grid=(BATCH,),
        in_specs=[spec(0), spec(1), spec(2)],
        out_specs=pl.BlockSpec((CTX_LEN, width), lambda b: (b, 0)),
        out_shape=jax.ShapeDtypeStruct((N_CTX, width), BF16),
        compiler_params=_cparams(("parallel",)),
        name="na_ctx",
    )(p, p, p)


def _na_plan():
    rows = SEQ // GRID_W
    invalid = 2 * NA_KH - 1
    pairs, plan, starts = [], [], []
    for t in range(rows // NA_TR):
        kw0 = int(np.clip(NA_TR * t - NA_KH // 2, 0, rows - NA_WR))
        starts.append(kw0)
        tile = []
        for ri in range(NA_TR):
            r = NA_TR * t + ri
            r0 = int(np.clip(r - NA_KH // 2, 0, rows - NA_KH))
            assert kw0 <= r0 and r0 + NA_KH <= kw0 + NA_WR
            row = []
            for kp in range(NA_WR // 2):
                pair = []
                for kr in (kw0 + 2 * kp, kw0 + 2 * kp + 1):
                    pair.append(kr - r + NA_KH - 1 if r0 <= kr < r0 + NA_KH else invalid)
                pair = tuple(pair)
                if pair not in pairs:
                    pairs.append(pair)
                row.append(pairs.index(pair))
            tile.append(row)
        plan.append(tile)
    return starts, plan, pairs


def _na_bias_pairs(rpb, pairs):
    c = np.arange(GRID_W)
    c0 = np.clip(c - NA_KW // 2, 0, GRID_W - NA_KW)
    col_ok = (c[None, :] >= c0[:, None]) & (c[None, :] < c0[:, None] + NA_KW)
    col_idx = np.clip(c[None, :] - c[:, None] + NA_KW - 1, 0, 2 * NA_KW - 2)
    onehot = (col_idx[None] == np.arange(2 * NA_KW - 1)[:, None, None]).astype(np.float32)
    t = jnp.einsum("lhdj,jck->lhdck", rpb.astype(F32), onehot, precision=lax.Precision.HIGHEST) * LOG2E
    t = jnp.where(col_ok, t, MASK_VALUE)
    masked = jnp.full(t.shape[:2] + (GRID_W, GRID_W), MASK_VALUE, F32)
    slabs = [t[:, :, d] for d in range(2 * NA_KH - 1)] + [masked]
    return jnp.stack([jnp.concatenate([slabs[a], slabs[b]], axis=-1) for a, b in pairs], axis=2)


NA_HS = 2


def _na_kernel(q_ref, k_ref, v_ref, kc_ref, vc_ref, t2_ref, o_ref, vo_ref, vco_ref, *, starts, plan):
    ones = jnp.ones((SEQ, HEAD_DIM), BF16)
    for j in range(NA_HS):
        cols = slice(j * HEAD_DIM, (j + 1) * HEAD_DIM)
        wide = slice(2 * j * HEAD_DIM, 2 * (j + 1) * HEAD_DIM)
        vo_ref[:, 2 * j * HEAD_DIM:(2 * j + 1) * HEAD_DIM] = v_ref[:, cols]
        vo_ref[:, (2 * j + 1) * HEAD_DIM:2 * (j + 1) * HEAD_DIM] = ones
        vco_ref[:, 2 * j * HEAD_DIM:(2 * j + 1) * HEAD_DIM] = vc_ref[:, cols]
        vco_ref[:, (2 * j + 1) * HEAD_DIM:2 * (j + 1) * HEAD_DIM] = ones[:CTX_LEN]
        kc = kc_ref[:, cols]
        vc = vco_ref[:, wide]
        for t, (kw0, tile) in enumerate(zip(starts, plan)):
            rows = slice(t * NA_TQ, (t + 1) * NA_TQ)
            win = slice(kw0 * GRID_W, kw0 * GRID_W + NA_TK)
            q = q_ref[rows, cols]
            bias = jnp.concatenate(
                [jnp.concatenate([t2_ref[0, j, idx] for idx in row], axis=1) for row in tile], axis=0)
            s = _dot_nt(q, k_ref[win, cols]) * (NA_SCALE * LOG2E) + bias
            sc = _dot_nt(q, kc) * (NA_SCALE * LOG2E)
            o_ref[rows, cols] = _softmax_pv([s, sc], [vo_ref[win, wide], vc], True).astype(o_ref.dtype)


def _na_latent(p, t2, l, starts, plan):
    width = NA_HS * HEAD_DIM
    cq = COL_NA // width
    ck = cq + NA_HEADS // NA_HS
    cv = ck + NA_HEADS // NA_HS
    n_pairs = t2.shape[2]
    return pl.pallas_call(
        functools.partial(_na_kernel, starts=starts, plan=plan),
        grid=(NA_HEADS // NA_HS, BATCH),
        in_specs=[
            pl.BlockSpec((SEQ, width), lambda h, b: (b, cq + h)),
            pl.BlockSpec((SEQ, width), lambda h, b: (b, ck + h)),
            pl.BlockSpec((SEQ, width), lambda h, b: (b, cv + h)),
            pl.BlockSpec((CTX_LEN, width), lambda h, b: (CTX_BLOCK0 + b, ck + h)),
            pl.BlockSpec((CTX_LEN, width), lambda h, b: (CTX_BLOCK0 + b, cv + h)),
            pl.BlockSpec((1, NA_HS, n_pairs, GRID_W, 2 * GRID_W), lambda h, b: (l, h, 0, 0, 0)),
        ],
        out_specs=pl.BlockSpec((SEQ, width), lambda h, b: (b, h)),
        out_shape=jax.ShapeDtypeStruct((N_LAT, NA_HEADS * HEAD_DIM), BF16),
        scratch_shapes=[pltpu.VMEM((SEQ, 2 * width), BF16), pltpu.VMEM((CTX_LEN, 2 * width), BF16)],
        compiler_params=_cparams(("parallel", "parallel")),
        name="na_latent",
    )(p, p, p, p, p, t2)


def _conv_kernel(v_ref, x1_ref, x2_ref, wv_ref, w1_ref, w2_ref, bv_ref, b1_ref, b2_ref, zin_ref, x2o_ref):
    n = v_ref.shape[0]
    row = lax.broadcasted_iota(jnp.int32, (n, 1), 0)

    def short_conv(p_ref, w_ref, b_ref):
        p = p_ref[...].astype(F32)
        prev = jnp.where(row == 0, 0.0, pltpu.roll(p, 1, 0))
        nxt = jnp.where(row == n - 1, 0.0, pltpu.roll(p, n - 1, 0))
        w = w_ref[0]
        return prev * w[0:1] + p * w[1:2] + nxt * w[2:3] + b_ref[0]

    zin_ref[...] = (short_conv(x1_ref, w1_ref, b1_ref) * short_conv(v_ref, wv_ref, bv_ref)).astype(zin_ref.dtype)
    x2o_ref[...] = short_conv(x2_ref, w2_ref, b2_ref).astype(x2o_ref.dtype)


def _hy_conv(p, conv_w, conv_b, l, n, row_block0, *, tc=256):
    nc = HY_WIDTH // tc

    def seg(s):
        return (pl.BlockSpec((n, tc), lambda b, j: (row_block0 + b, s * nc + j)),
                pl.BlockSpec((1, 3, tc), lambda b, j: (l, 0, s * nc + j)),
                pl.BlockSpec((1, 1, tc), lambda b, j: (l, 0, s * nc + j)))

    (pv, wv, bv), (p1, w1, b1), (p2, w2, b2) = seg(0), seg(1), seg(2)
    out_spec = pl.BlockSpec((n, tc), lambda b, j: (0, b * nc + j))
    out = jax.ShapeDtypeStruct((n, BATCH * HY_WIDTH), BF16)
    conv_b = conv_b.reshape(DEPTH, 1, -1)
    return pl.pallas_call(
        _conv_kernel,
        grid=(BATCH, nc),
        in_specs=[pv, p1, p2, wv, w1, w2, bv, b1, b2],
        out_specs=[out_spec, out_spec],
        out_shape=[out, out],
        compiler_params=_cparams(("parallel", "parallel")),
        name="hy_conv",
    )(p, p, p, conv_w, conv_w, conv_w, conv_b, conv_b, conv_b)


def _filt_kernel(z_ref, t_ref, dl_ref, w1_ref, b1_ref, w2_ref, b2_ref, w3_ref, fr_ref, hs_ref, ha_ref, kn_ref):
    n = z_ref.shape[0]
    fr = fr_ref[...]
    h = jnp.sin(fr * (_dot_hi(z_ref[...], w1_ref[...]) + b1_ref[...]))
    h = jnp.sin(fr * (_dot_hi(h, w2_ref[...]) + b2_ref[...]))
    h = _dot_hi(h, w3_ref[...])
    decay = jnp.exp(-t_ref[...] * dl_ref[...])
    row = lax.broadcasted_iota(jnp.int32, (n, 1), 0)
    hf = h[:, :HY_WIDTH] * decay
    hb = jnp.where(row == 0, 0.0, h[:, HY_WIDTH:] * decay)
    hs = hf + hb
    hs_ref[...] = hs.astype(hs_ref.dtype)
    ha_ref[...] = (hf - hb).astype(ha_ref.dtype)
    sign = jnp.where((row & 1) == 0, 1.0, -1.0)
    kn_ref[...] = jnp.sum(hs * sign, axis=0, keepdims=True)


def _hy_filter_taps(n, f_w1, f_b1, f_w2, f_b2, f_w3, f_freq):
    pos = jnp.arange(n, dtype=F32)
    t = jnp.linspace(0.0, 1.0, n, dtype=F32)
    bands = jnp.linspace(1e-4, HY_POS_BANDS - 1, HY_POS_BANDS, dtype=F32)
    ang = (2.0 * math.pi / n) * pos[:, None] * bands[None, :]
    z = jnp.concatenate([t[:, None], jnp.cos(ang), -jnp.sin(ang)], axis=-1)
    pad = HY_FILTER_HIDDEN - z.shape[1]
    z = jnp.pad(z, ((0, 0), (0, pad)))
    w1 = jnp.pad(f_w1.astype(F32), ((0, pad), (0, 0)))
    deltas = jnp.abs(jnp.linspace(math.log(HY_DECAY_TARGET) / HY_FAST_DECAY,
                                  math.log(HY_DECAY_TARGET) / HY_SLOW_DECAY, HY_WIDTH, dtype=F32))
    hid = HY_FILTER_HIDDEN
    return pl.pallas_call(
        _filt_kernel,
        out_shape=[
            jax.ShapeDtypeStruct((n, HY_WIDTH), BF16),
            jax.ShapeDtypeStruct((n, HY_WIDTH), BF16),
            jax.ShapeDtypeStruct((1, HY_WIDTH), F32),
        ],
        compiler_params=pltpu.CompilerParams(vmem_limit_bytes=VMEM_LIMIT),
        name="hy_filter",
    )(z, t[:, None], deltas[None, :], w1, f_b1.reshape(1, hid), f_w2, f_b2.reshape(1, hid), f_w3,
      f_freq.reshape(1, hid))


def _dft_tables(n):
    lo = 16 if n < 1024 else 32
    hi = n // lo
    t = jnp.arange(n, dtype=jnp.int32)[None, :]
    big = 2 * n

    def ang(ff):
        return ((ff * t) % big).astype(F32) * (2.0 * math.pi / big)

    a = ang(lo * jnp.arange(hi, dtype=jnp.int32)[:, None])
    b = ang(jnp.arange(lo, dtype=jnp.int32)[:, None])
    ca, sa, cb, sb = jnp.cos(a), jnp.sin(a), jnp.cos(b), jnp.sin(b)
    cos = (ca[:, None, :] * cb[None, :, :] - sa[:, None, :] * sb[None, :, :]).reshape(n, n)
    base = -(sa[:, None, :] * cb[None, :, :] + ca[:, None, :] * sb[None, :, :]).reshape(n, n)
    idx = jnp.arange(n)
    alt = jnp.where(idx % 2 == 0, 1.0, -1.0).astype(F32)
    msin = jnp.where(idx[:, None] == 0, alt[None, :], base)
    msin_t = jnp.where(idx[None, :] == 0, alt[:, None], base)
    return cos.astype(BF16), msin.astype(BF16), msin_t.astype(BF16)


def _batch_cols(b):
    return slice(b * HY_WIDTH, (b + 1) * HY_WIDTH)


def _dft_fwd_kernel(c_ref, s_ref, x_ref, hs_ref, ha_ref, kn_ref, yr_ref, yi_ref, *, tf, inv_n):
    x = x_ref[...]
    c = c_ref[...]
    s = s_ref[...]
    zr = _dot(c, x)
    zi = _dot(s, x)
    row = pl.program_id(0) * tf + lax.broadcasted_iota(jnp.int32, (tf, 1), 0)
    bin0 = row == 0
    wt = jnp.where(bin0, inv_n, 2.0 * inv_n)
    kr = _dot(c, hs_ref[...]) * wt
    ki = jnp.where(bin0, 0.0, _dot(s, ha_ref[...]) * wt)
    kr_im = jnp.where(bin0, kn_ref[...] * wt, kr)
    for b in range(BATCH):
        cols = _batch_cols(b)
        yr_ref[:, cols] = (zr[:, cols] * kr - zi[:, cols] * ki).astype(yr_ref.dtype)
        yi_ref[:, cols] = (zr[:, cols] * ki + zi[:, cols] * kr_im).astype(yi_ref.dtype)


def _hy_dft_fwd(cos, msin, zin, hs, ha, kn, *, tf):
    n = cos.shape[0]
    tf = min(tf, n)
    tab = pl.BlockSpec((tf, n), lambda i: (i, 0))
    filt = pl.BlockSpec((n, HY_WIDTH), lambda i: (0, 0))
    out = pl.BlockSpec((tf, BATCH * HY_WIDTH), lambda i: (i, 0))
    whole = pl.BlockSpec((n, BATCH * HY_WIDTH), lambda i: (0, 0), pipeline_mode=pl.Buffered(1))
    return pl.pallas_call(
        functools.partial(_dft_fwd_kernel, tf=tf, inv_n=1.0 / (2 * n)),
        grid=(n // tf,),
        in_specs=[tab, tab, whole, filt, filt, pl.BlockSpec((1, HY_WIDTH), lambda i: (0, 0))],
        out_specs=[out, out],
        out_shape=[jax.ShapeDtypeStruct((n, BATCH * HY_WIDTH), BF16)] * 2,
        compiler_params=_cparams(("parallel",)),
        name="hy_dft_fwd",
    )(cos, msin, zin, hs, ha, kn)


def _dft_inv_kernel(c_ref, st_ref, yr_ref, yi_ref, zin_ref, x2_ref, b_ref, o_ref):
    y = _dot(c_ref[...], yr_ref[...]) + _dot(st_ref[...], yi_ref[...])
    bias = b_ref[0]
    for b in range(BATCH):
        cols = _batch_cols(b)
        yb = y[:, cols] + zin_ref[:, cols].astype(F32) * bias
        o_ref[b] = (x2_ref[:, cols].astype(F32) * yb).astype(o_ref.dtype)


def _hy_dft_inv(cos, msin_t, yr, yi, zin, x2, bias, l, *, tt):
    n = cos.shape[0]
    tt = min(tt, n)
    tab = pl.BlockSpec((tt, n), lambda i: (i, 0))
    whole = pl.BlockSpec((n, BATCH * HY_WIDTH), lambda i: (0, 0), pipeline_mode=pl.Buffered(1))
    rows = pl.BlockSpec((tt, BATCH * HY_WIDTH), lambda i: (i, 0))
    out = pl.pallas_call(
        _dft_inv_kernel,
        grid=(n // tt,),
        in_specs=[tab, tab, whole, whole, rows, rows, pl.BlockSpec((1, 1, HY_WIDTH), lambda i: (l, 0, 0))],
        out_specs=pl.BlockSpec((BATCH, tt, HY_WIDTH), lambda i: (0, i, 0)),
        out_shape=jax.ShapeDtypeStruct((BATCH, n, HY_WIDTH), BF16),
        compiler_params=_cparams(("parallel",)),
        name="hy_dft_inv",
    )(cos, msin_t, yr, yi, zin, x2, bias.reshape(DEPTH, 1, HY_WIDTH))
    return out.reshape(BATCH * n, HY_WIDTH)


def _hyena(p, l, n, row_block0, tables, conv_w, conv_b, f_w1, f_b1, f_w2, f_b2, f_w3, f_freq, bias):
    cos, msin, msin_t = tables
    zin, x2 = _hy_conv(p, conv_w, conv_b, l, n, row_block0)
    hs, ha, kn = _hy_filter_taps(n, f_w1[l], f_b1[l], f_w2[l], f_b2[l], f_w3[l], f_freq[l])
    yr, yi = _hy_dft_fwd(cos, msin, zin, hs, ha, kn, tf=512)
    return _hy_dft_inv(cos, msin_t, yr, yi, zin, x2, bias, l, tt=512)


def _rope_tables():
    tok = jnp.arange(SEQ)
    row = (tok // GRID_W).astype(F32)
    col = (tok % GRID_W).astype(F32)
    n_freq = MLA_ROPE // 4
    inv = ROPE_THETA ** (-jnp.arange(n_freq, dtype=F32) / n_freq)
    ang = jnp.concatenate([row[:, None] * inv, col[:, None] * inv], axis=-1)
    cos, sin = jnp.cos(ang), jnp.sin(ang)
    half = MLA_ROPE // 2
    zeros = jnp.zeros((SEQ, half), F32)
    rest = HEAD_DIM - MLA_ROPE
    cos_t = jnp.concatenate([cos, cos, jnp.ones((SEQ, rest), F32)], axis=-1)
    sin_a = jnp.concatenate([-sin, zeros, jnp.zeros((SEQ, rest), F32)], axis=-1)
    sin_b = jnp.concatenate([zeros, sin, jnp.zeros((SEQ, rest), F32)], axis=-1)
    ident = jnp.ones((N_CTX, HEAD_DIM), F32)
    none = jnp.zeros((N_CTX, HEAD_DIM), F32)
    return (jnp.concatenate([cos_t, ident]), jnp.concatenate([sin_a, none]), jnp.concatenate([sin_b, none]))


def _layout_w_uq(w):
    w = w.reshape(Q_LORA, MLA_HEADS, HEAD_DIM + MLA_ROPE)
    w = jnp.pad(w, ((0, 0), (0, 0), (0, MLA_QK - HEAD_DIM - MLA_ROPE)))
    return w.reshape(Q_LORA, MLA_HEADS * MLA_QK).astype(BF16)


def kernel(x, c, ctx, c_ctx, w_ada, b_ada, g_attn_pre, g_attn_post, g_ffn_pre, g_ffn_post, w_in, hy_conv_w, hy_conv_b, hy_f_w1, hy_f_b1, hy_f_w2, hy_f_b2, hy_f_w3, hy_f_freq, hy_bias, mla_g_q, mla_w_uq, mla_g_kv, mla_w_ukv, na_rpb, w_out, w_ffn_gate, w_ffn_up, w_ffn_down):
    cc = jnp.concatenate([c, c_ctx[None, :], jnp.zeros((8 - BATCH - 1, D_MODEL), F32)], axis=0)
    mods_all = _ada(cc, w_ada, b_ada)
    mods = [mods_all[l].reshape(8, 1, 6 * D_MODEL) for l in range(DEPTH)]

    rope_tabs = _rope_tables()
    dft_lat = _dft_tables(SEQ)
    dft_ctx = _dft_tables(CTX_LEN)
    na_starts, na_plan, na_pairs = _na_plan()
    na_t2 = _na_bias_pairs(na_rpb, na_pairs)
    w_in_t = jnp.swapaxes(w_in, 1, 2)
    hy_w = (hy_conv_w, hy_conv_b, hy_f_w1, hy_f_b1, hy_f_w2, hy_f_b2, hy_f_w3, hy_f_freq, hy_bias)

    res = (x.reshape(N_LAT, D_MODEL), ctx.reshape(N_CTX, D_MODEL))
    xn = _prenorm(res[0], res[1], g_attn_pre[0], mods[0])

    for l in range(DEPTH):
        ctx_out = l < DEPTH - 1
        m_rows = N_TOK if ctx_out else N_LAT
        p = _in_proj(xn, w_in_t, l)

        q = _q_proj(p, m_rows, mla_g_q[l], _layout_w_uq(mla_w_uq[l]), rope_tabs)
        kn, v, kr = _kv_proj(p, mla_g_kv[l], mla_w_ukv, l, rope_tabs)
        mla = [_mla_latent(q, kn, v, kr), None]
        na = [_na_latent(p, na_t2, l, na_starts, na_plan), None]
        hy = [_hyena(p, l, SEQ, 0, dft_lat, *hy_w), None]

        if ctx_out:
            mla[1] = _mla_ctx(q, kn, v, kr)
            na[1] = _na_ctx(p)
            hy[1] = _hyena(p, l, CTX_LEN, CTX_BLOCK0, dft_ctx, *hy_w)
            res_l = res
        else:
            res_l = (res[0], None)

        stream, xn = _proj_post([tuple(hy), tuple(mla), tuple(na)], w_out, l, g_attn_post[l], mods[l], 2, res_l,
                                m_rows, nxt=(g_ffn_pre[l], mods[l], 3, 4), tm=POST_TM_OUT)
        h = _ffn_up(xn, w_ffn_gate, w_ffn_up, l)
        nxt = (g_attn_pre[l + 1], mods[l + 1], 0, 1) if ctx_out else None
        stream, xn = _proj_post([(h, None)], w_ffn_down, l, g_ffn_post[l], mods[l], 5, (stream, None), m_rows,
                                nxt=nxt, tm=POST_TM_DOWN)
        res = (stream, None)

    return stream.reshape(BATCH, SEQ, D_MODEL)
```

```python
import functools
import math

import jax
import jax.numpy as jnp
import numpy as np
from jax import lax
from jax.experimental import pallas as pl
from jax.experimental.pallas import tpu as pltpu

F32 = jnp.float32
BF16 = jnp.bfloat16

D_MODEL = 2048
BATCH = 4
SEQ = 2048
DEPTH = 2
GRID_W = 64
CTX_LEN = 256
HEAD_DIM = 128
HY_WIDTH = D_MODEL // 4
HY_FILTER_HIDDEN = 64
HY_POS_BANDS = 16
HY_DECAY_TARGET = 1e-2
HY_FAST_DECAY = 0.3
HY_SLOW_DECAY = 1.5
MLA_HEADS = (D_MODEL // 2) // HEAD_DIM
MLA_ROPE = 64
Q_LORA = 3 * D_MODEL // 8
KV_LORA = D_MODEL // 4
MLA_SCALE = (HEAD_DIM + MLA_ROPE) ** -0.5
NA_HEADS = (D_MODEL // 4) // HEAD_DIM
NA_KH = 8
NA_KW = 16
NA_SCALE = HEAD_DIM ** -0.5
FFN_HIDDEN = ((8 * D_MODEL + 3 * 256 - 1) // (3 * 256)) * 256
ROPE_THETA = 10000.0
RMS_EPS = 1e-6
MASK_VALUE = -1e30
LOG2E = math.log2(math.e)

N_LAT = BATCH * SEQ
N_CTX = BATCH * CTX_LEN
N_TOK = N_LAT + N_CTX
CTX_BLOCK0 = N_LAT // CTX_LEN

IN_TN = 512
COL_HY = 0
COL_CQ = 3 * HY_WIDTH
COL_CKV = COL_CQ + Q_LORA
COL_KR = COL_CKV + KV_LORA
W_IN_NA = COL_KR + MLA_ROPE
COL_NA = -(-W_IN_NA // IN_TN) * IN_TN
P_COLS = COL_NA + 3 * NA_HEADS * HEAD_DIM
MLA_QK = 256
MLA_VW = 256

NA_TR = 4
NA_WR = 12
NA_TQ = NA_TR * GRID_W
NA_TK = NA_WR * GRID_W

VMEM_LIMIT = 52 * 1024 * 1024


def _cparams(sem):
    return pltpu.CompilerParams(dimension_semantics=sem, vmem_limit_bytes=VMEM_LIMIT)


def _dot(a, b):
    return jnp.dot(a, b, preferred_element_type=F32)


def _dot_nt(a, b):
    return lax.dot_general(a, b, (((1,), (1,)), ((), ())), preferred_element_type=F32)


def _dot_hi(a, b):
    return jnp.dot(a, b, preferred_element_type=F32, precision=lax.Precision.HIGHEST)


def _rms(x, g):
    ms = jnp.mean(x * x, axis=-1, keepdims=True)
    return x * lax.rsqrt(ms + RMS_EPS) * g


def _mod_row(tm):
    n_lat, per_b = N_LAT // tm, SEQ // tm
    return lambda i: jnp.where(i < n_lat, i // per_b, BATCH)


def _rope_row(tm):
    n_lat, per_b = N_LAT // tm, SEQ // tm
    return lambda i: jnp.where(i < n_lat, i % per_b, per_b + i - n_lat)


def _ada_kernel(c_ref, w_ref, b_ref, o_ref):
    a = c_ref[...]
    a = a * jax.nn.sigmoid(a)
    o_ref[0] = _dot(a.astype(BF16), w_ref[0].astype(BF16)) + b_ref[0]


def _ada(cc, w_ada, b_ada):
    tn = 2048
    n = w_ada.shape[-1]
    return pl.pallas_call(
        _ada_kernel,
        grid=(DEPTH, n // tn),
        in_specs=[
            pl.BlockSpec((8, D_MODEL), lambda l, j: (0, 0)),
            pl.BlockSpec((1, D_MODEL, tn), lambda l, j: (l, 0, j)),
            pl.BlockSpec((1, 1, tn), lambda l, j: (l, 0, j)),
        ],
        out_specs=pl.BlockSpec((1, 8, tn), lambda l, j: (l, 0, j)),
        out_shape=jax.ShapeDtypeStruct((DEPTH, 8, n), F32),
        compiler_params=_cparams(("parallel", "parallel")),
        name="ada",
    )(cc, w_ada, b_ada.reshape(DEPTH, 1, n))


def _norm_mod_to(xn_ref, x_ref, g_ref, sh_ref, sc_ref, chunk=256):
    g = g_ref[...]
    sc = 1.0 + sc_ref[0]
    sh = sh_ref[0]

    def body(r, carry):
        rows = pl.ds(pl.multiple_of(r * chunk, chunk), chunk)
        xn_ref[rows, :] = (_rms(x_ref[rows, :], g) * sc + sh).astype(BF16)
        return carry

    lax.fori_loop(0, x_ref.shape[0] // chunk, body, 0)


def _prenorm_kernel(x_ref, c_ref, g_ref, sh_ref, sc_ref, o_ref, *, n_lat):
    i = pl.program_id(0)

    @pl.when(i < n_lat)
    def _():
        _norm_mod_to(o_ref, x_ref, g_ref, sh_ref, sc_ref)

    @pl.when(i >= n_lat)
    def _():
        _norm_mod_to(o_ref, c_ref, g_ref, sh_ref, sc_ref)


def _prenorm(x2d, ctx2d, g, mods, *, tm=1024):
    n_lat = N_LAT // tm
    row = _mod_row(tm)
    return pl.pallas_call(
        functools.partial(_prenorm_kernel, n_lat=n_lat),
        grid=(N_TOK // tm,),
        in_specs=[
            pl.BlockSpec((tm, D_MODEL), lambda i: (jnp.minimum(i, n_lat - 1), 0)),
            pl.BlockSpec((tm, D_MODEL), lambda i: (jnp.maximum(i - n_lat, 0), 0)),
            pl.BlockSpec((1, D_MODEL), lambda i: (0, 0)),
            pl.BlockSpec((1, 1, D_MODEL), lambda i: (row(i), 0, 0)),
            pl.BlockSpec((1, 1, D_MODEL), lambda i: (row(i), 0, 1)),
        ],
        out_specs=pl.BlockSpec((tm, D_MODEL), lambda i: (i, 0)),
        out_shape=jax.ShapeDtypeStruct((N_TOK, D_MODEL), BF16),
        compiler_params=_cparams(("parallel",)),
        name="prenorm",
    )(x2d, ctx2d, g.reshape(1, D_MODEL), mods, mods)


def _in_kernel(x_ref, w_ref, o_ref):
    o_ref[...] = _dot_nt(x_ref[...], w_ref[0].astype(BF16)).astype(o_ref.dtype)


def _in_proj(xn, w_in_t, l, *, tm=3072):
    n_head = COL_NA // IN_TN

    def w_row(j):
        per = IN_TN // MLA_ROPE
        return MLA_ROPE * jnp.where(j < n_head, j * per, W_IN_NA // MLA_ROPE + (j - n_head) * per)

    return pl.pallas_call(
        _in_kernel,
        grid=(N_TOK // tm, P_COLS // IN_TN),
        in_specs=[
            pl.BlockSpec((tm, D_MODEL), lambda i, j: (i, 0)),
            pl.BlockSpec((pl.Element(1), pl.Element(IN_TN), pl.Element(D_MODEL)), lambda i, j: (l, w_row(j), 0)),
        ],
        out_specs=pl.BlockSpec((tm, IN_TN), lambda i, j: (i, j)),
        out_shape=jax.ShapeDtypeStruct((N_TOK, P_COLS), BF16),
        compiler_params=_cparams(("parallel", "arbitrary")),
        name="in_proj",
    )(xn, w_in_t)


FFN_CW = 256


def _swiglu_kernel(x_ref, wg_ref, wu_ref, o_ref):
    xn = x_ref[...]
    for c in range(o_ref.shape[1] // FFN_CW):
        cols = slice(c * FFN_CW, (c + 1) * FFN_CW)
        gate = _dot(xn, wg_ref[0, :, cols].astype(BF16))
        up = _dot(xn, wu_ref[0, :, cols].astype(BF16))
        o_ref[:, cols] = (gate * jax.nn.sigmoid(gate) * up).astype(o_ref.dtype)


def _ffn_up(xn, wg, wu, l, *, tn=512):
    m = xn.shape[0]
    tm = 2048 if m % 2048 == 0 else 2304
    w_spec = pl.BlockSpec((1, D_MODEL, tn), lambda i, j: (l, 0, j))
    return pl.pallas_call(
        _swiglu_kernel,
        grid=(m // tm, FFN_HIDDEN // tn),
        in_specs=[pl.BlockSpec((tm, D_MODEL), lambda i, j: (i, 0)), w_spec, w_spec],
        out_specs=pl.BlockSpec((tm, tn), lambda i, j: (i, j)),
        out_shape=jax.ShapeDtypeStruct((m, FFN_HIDDEN), BF16),
        compiler_params=_cparams(("parallel", "arbitrary")),
        name="ffn_up",
    )(xn, wg, wu)


POST_TM_OUT = 512
POST_TM_DOWN = 256
POST_SUB = 128
POST_WCH = 256
POST_WSLOTS = 4
POST_SLOTS = 3


def _post_kernel(*refs, widths, two_src, emit_xn, n_lat, l, tm):
    refs = list(refs)
    n_act = len(widths)
    acts_lat = [refs.pop(0) for _ in range(n_act)]
    acts_ctx = [refs.pop(0) for _ in range(n_act)] if two_src else None
    w_hbm, g_ref, gate_ref, res_lat = (refs.pop(0) for _ in range(4))
    res_ctx = refs.pop(0) if two_src else None
    if emit_xn:
        g2_ref, sh_ref, sc_ref = (refs.pop(0) for _ in range(3))
    o_hbm = refs.pop(0)
    xn_ref = refs.pop(0) if emit_xn else None
    w_scr, stage, buf, sem_w, sem_res, sem_out = refs

    i = pl.program_id(0)
    n_i = pl.num_programs(0)
    n_chunks = sum(widths) // POST_WCH
    is_lat = i < n_lat
    is_ctx = jnp.logical_not(is_lat)

    def w_copy(c, slot):
        return pltpu.make_async_copy(w_hbm.at[l, pl.ds(c * POST_WCH, POST_WCH)], stage.at[slot], sem_w.at[slot])

    def res_copy(src, row0, slot):
        return pltpu.make_async_copy(src.at[pl.ds(row0, tm)], buf.at[slot], sem_res.at[slot])

    def res_start(tile, slot):
        if two_src:
            pl.when(tile < n_lat)(lambda: res_copy(res_lat, tile * tm, slot).start())
            pl.when(tile >= n_lat)(lambda: res_copy(res_ctx, (tile - n_lat) * tm, slot).start())
        else:
            res_copy(res_lat, tile * tm, slot).start()

    def out_copy(tile, slot):
        return pltpu.make_async_copy(buf.at[slot], o_hbm.at[pl.ds(tile * tm, tm)], sem_out.at[slot])

    @pl.when(i == 0)
    def _():
        res_start(0, 0)
        for c in range(POST_WSLOTS - 1):
            w_copy(c, c).start()

        def body(c, carry):
            s = c % POST_WSLOTS
            w_copy(c, s).wait()
            ahead = c + POST_WSLOTS - 1

            @pl.when(ahead < n_chunks)
            def _():
                w_copy(ahead, ahead % POST_WSLOTS).start()

            w_scr[pl.ds(pl.multiple_of(c * POST_WCH, POST_WCH), POST_WCH), :] = stage[s].astype(BF16)
            return carry

        lax.fori_loop(0, n_chunks, body, 0)

    slot = i % POST_SLOTS
    nslot = (i + 1) % POST_SLOTS

    @pl.when(i + 1 < n_i)
    def _():
        @pl.when(i >= POST_SLOTS - 1)
        def _():
            out_copy(i + 1 - POST_SLOTS, nslot).wait()

        res_start(i + 1, nslot)

    res_copy(res_lat, 0, slot).wait()

    def compute(act_refs):
        gg = gate_ref[0] * g_ref[...]
        if emit_xn:
            g2s = g2_ref[...] * (1.0 + sc_ref[0])
            sh = sh_ref[0]
        for r in range(tm // POST_SUB):
            rows = pl.ds(r * POST_SUB, POST_SUB)
            y = None
            k0 = 0
            for a_ref, kw in zip(act_refs, widths):
                part = _dot(a_ref[rows, :], w_scr[k0:k0 + kw, :])
                y = part if y is None else y + part
                k0 += kw
            x_new = buf[slot, rows, :] + _rms(y, gg)
            buf[slot, rows, :] = x_new
            if emit_xn:
                xn_ref[rows, :] = (_rms(x_new, g2s) + sh).astype(xn_ref.dtype)

    if two_src:
        pl.when(is_lat)(functools.partial(compute, acts_lat))
        pl.when(is_ctx)(functools.partial(compute, acts_ctx))
    else:
        compute(acts_lat)

    out_copy(i, slot).start()

    @pl.when(i == n_i - 1)
    def _():
        for back in range(POST_SLOTS):
            out_copy(i - back, (i - back) % POST_SLOTS).wait()


def _proj_post(acts, w, l, g, mods, gate_chunk, res, m_rows, nxt=None, *, tm):
    two_src = res[1] is not None
    n_lat = N_LAT // tm
    widths = tuple(a.shape[1] for a, _ in acts)
    k_total = sum(widths)
    emit_xn = nxt is not None
    row = _mod_row(tm)

    def lat_row(i):
        return jnp.minimum(i, n_lat - 1) if two_src else i

    def ctx_row(i):
        return jnp.maximum(i - n_lat, 0)

    def act_specs(rowf):
        return [pl.BlockSpec((tm, kw), lambda i: (rowf(i), 0)) for kw in widths]

    vec_spec = pl.BlockSpec((1, D_MODEL), lambda i: (0, 0))

    def mod_spec(chunk):
        return pl.BlockSpec((1, 1, D_MODEL), lambda i: (row(i), 0, chunk))

    hbm = pl.BlockSpec(memory_space=pl.ANY)
    in_specs = act_specs(lat_row)
    args = [a for a, _ in acts]
    if two_src:
        in_specs += act_specs(ctx_row)
        args += [c for _, c in acts]
    in_specs += [hbm, vec_spec, mod_spec(gate_chunk), hbm]
    args += [w, g.reshape(1, D_MODEL), mods, res[0]]
    if two_src:
        in_specs.append(hbm)
        args.append(res[1])
    out_specs = [hbm]
    out_shape = [jax.ShapeDtypeStruct((m_rows, D_MODEL), F32)]
    if emit_xn:
        g2, mods2, sh_chunk, sc_chunk = nxt
        in_specs += [vec_spec, mod_spec(sh_chunk), mod_spec(sc_chunk)]
        args += [g2.reshape(1, D_MODEL), mods2, mods2]
        out_specs.append(pl.BlockSpec((tm, D_MODEL), lambda i: (i, 0)))
        out_shape.append(jax.ShapeDtypeStruct((m_rows, D_MODEL), BF16))

    out = pl.pallas_call(
        functools.partial(_post_kernel, widths=widths, two_src=two_src, emit_xn=emit_xn, n_lat=n_lat, l=l, tm=tm),
        grid=(m_rows // tm,),
        in_specs=in_specs,
        out_specs=out_specs,
        out_shape=out_shape,
        scratch_shapes=[
            pltpu.VMEM((k_total, D_MODEL), BF16),
            pltpu.VMEM((POST_WSLOTS, POST_WCH, D_MODEL), F32),
            pltpu.VMEM((POST_SLOTS, tm, D_MODEL), F32),
            pltpu.SemaphoreType.DMA((POST_WSLOTS,)),
            pltpu.SemaphoreType.DMA((POST_SLOTS,)),
            pltpu.SemaphoreType.DMA((POST_SLOTS,)),
        ],
        compiler_params=_cparams(("arbitrary",)),
        name="proj_post",
    )(*args)
    return (out[0], out[1]) if emit_xn else (out[0], None)


def _rope128(r, cos_ref, sa_ref, sb_ref):
    return r * cos_ref[...] + pltpu.roll(r, 96, 1) * sa_ref[...] + pltpu.roll(r, 32, 1) * sb_ref[...]


def _q_kernel(x_ref, g_ref, w_ref, cos_ref, sa_ref, sb_ref, o_ref):
    xn = _rms(x_ref[...].astype(F32), g_ref[...] * (MLA_SCALE * LOG2E)).astype(BF16)
    for h in range(MLA_HEADS):
        acc = _dot(xn, w_ref[:, h * MLA_QK:(h + 1) * MLA_QK])
        o_ref[:, h * MLA_QK:h * MLA_QK + HEAD_DIM] = acc[:, :HEAD_DIM].astype(o_ref.dtype)
        o_ref[:, h * MLA_QK + HEAD_DIM:(h + 1) * MLA_QK] = _rope128(
            acc[:, HEAD_DIM:], cos_ref, sa_ref, sb_ref).astype(o_ref.dtype)


def _q_proj(p, m_rows, g, w, tabs, *, tm=1024):
    rope = _rope_row(tm)
    tab_spec = pl.BlockSpec((tm, HEAD_DIM), lambda i: (rope(i), 0))
    return pl.pallas_call(
        _q_kernel,
        grid=(m_rows // tm,),
        in_specs=[
            pl.BlockSpec((tm, Q_LORA), lambda i: (i, COL_CQ // Q_LORA)),
            pl.BlockSpec((1, Q_LORA), lambda i: (0, 0)),
            pl.BlockSpec((Q_LORA, MLA_HEADS * MLA_QK), lambda i: (0, 0)),
            tab_spec, tab_spec, tab_spec,
        ],
        out_specs=pl.BlockSpec((tm, MLA_HEADS * MLA_QK), lambda i: (i, 0)),
        out_shape=jax.ShapeDtypeStruct((m_rows, MLA_HEADS * MLA_QK), BF16),
        compiler_params=_cparams(("parallel",)),
        name="q_proj",
    )(p, g.reshape(1, Q_LORA), w, *tabs)


def _kv_kernel(xa_ref, xb_ref, kr_ref, g_ref, w_ref, cos_ref, sa_ref, sb_ref, k_ref, v_ref, krr_ref):
    half = KV_LORA // 2
    xa = xa_ref[...].astype(F32)
    xb = xb_ref[...].astype(F32)
    ms = (jnp.sum(xa * xa, axis=-1, keepdims=True) + jnp.sum(xb * xb, axis=-1, keepdims=True)) * (1.0 / KV_LORA)
    rs = lax.rsqrt(ms + RMS_EPS)
    g = g_ref[...]
    xna = (xa * rs * g[:, :half]).astype(BF16)
    xnb = (xb * rs * g[:, half:]).astype(BF16)
    lane = lax.broadcasted_iota(jnp.int32, (1, HEAD_DIM), 1)
    krr = _rope128(kr_ref[...].astype(F32), cos_ref, sa_ref, sb_ref)
    krr_ref[...] = jnp.where(lane < MLA_ROPE, krr, 0.0).astype(krr_ref.dtype)
    for h in range(MLA_HEADS):
        cols = slice(h * HEAD_DIM, (h + 1) * HEAD_DIM)
        w = w_ref[0, :, h * 2 * HEAD_DIM:(h + 1) * 2 * HEAD_DIM].astype(BF16)
        acc = _dot(xna, w[:half]) + _dot(xnb, w[half:])
        k_ref[:, cols] = acc[:, :HEAD_DIM].astype(k_ref.dtype)
        v_ref[:, cols] = acc[:, HEAD_DIM:].astype(v_ref.dtype)


def _kv_proj(p, g, w, l, tabs, *, tm=1024):
    m = p.shape[0]
    rope = _rope_row(tm)
    half = KV_LORA // 2
    tab_spec = pl.BlockSpec((tm, HEAD_DIM), lambda i: (rope(i), 0))
    return pl.pallas_call(
        _kv_kernel,
        grid=(m // tm,),
        in_specs=[
            pl.BlockSpec((tm, half), lambda i: (i, COL_CKV // half)),
            pl.BlockSpec((tm, half), lambda i: (i, COL_CKV // half + 1)),
            pl.BlockSpec((tm, HEAD_DIM), lambda i: (i, COL_KR // HEAD_DIM)),
            pl.BlockSpec((1, KV_LORA), lambda i: (0, 0)),
            pl.BlockSpec((1, KV_LORA, MLA_HEADS * 2 * HEAD_DIM), lambda i: (l, 0, 0)),
            tab_spec, tab_spec, tab_spec,
        ],
        out_specs=[
            pl.BlockSpec((tm, MLA_HEADS * HEAD_DIM), lambda i: (i, 0)),
            pl.BlockSpec((tm, MLA_HEADS * HEAD_DIM), lambda i: (i, 0)),
            pl.BlockSpec((tm, HEAD_DIM), lambda i: (i, 0)),
        ],
        out_shape=[
            jax.ShapeDtypeStruct((m, MLA_HEADS * HEAD_DIM), BF16),
            jax.ShapeDtypeStruct((m, MLA_HEADS * HEAD_DIM), BF16),
            jax.ShapeDtypeStruct((m, HEAD_DIM), BF16),
        ],
        compiler_params=_cparams(("parallel",)),
        name="kv_proj",
    )(p, p, p, g.reshape(1, KV_LORA), w, *tabs)


def _softmax_pv(s_list, v_list, ones_col):
    m = jnp.max(s_list[0], axis=-1, keepdims=True)
    for s in s_list[1:]:
        m = jnp.maximum(m, jnp.max(s, axis=-1, keepdims=True))
    acc = None
    den = None
    for s, v in zip(s_list, v_list):
        p = jnp.exp2(s - m)
        if not ones_col:
            d = jnp.sum(p, axis=-1, keepdims=True)
            den = d if den is None else den + d
        o = _dot(p.astype(BF16), v)
        acc = o if acc is None else acc + o
    if ones_col:
        return acc[:, :HEAD_DIM] / acc[:, HEAD_DIM:]
    return acc / den


def _mla_kernel(*refs, hs, chains, n_lat):
    if n_lat:
        q_ref, kn_ref, kr_ref, v_ref, knc_ref, krc_ref, vc_ref, o_ref, k_scr, v_scr = refs
    else:
        q_ref, knc_ref, krc_ref, vc_ref, o_ref, k_scr, v_scr = refs
    n_ctx = knc_ref.shape[0]
    ones = jnp.ones((n_lat + n_ctx, MLA_VW - HEAD_DIM), BF16)
    for j in range(hs):
        src = slice(j * HEAD_DIM, (j + 1) * HEAD_DIM)
        nope = slice(j * MLA_QK, j * MLA_QK + HEAD_DIM)
        rot = slice(j * MLA_QK + HEAD_DIM, (j + 1) * MLA_QK)
        val = slice(j * MLA_VW, j * MLA_VW + HEAD_DIM)
        if n_lat:
            k_scr[:n_lat, nope] = kn_ref[:, src]
            k_scr[:n_lat, rot] = kr_ref[...]
            v_scr[:n_lat, val] = v_ref[:, src]
        k_scr[n_lat:, nope] = knc_ref[:, src]
        k_scr[n_lat:, rot] = krc_ref[...]
        v_scr[n_lat:, val] = vc_ref[:, src]
        v_scr[:, j * MLA_VW + HEAD_DIM:(j + 1) * MLA_VW] = ones
    tq = q_ref.shape[0] // chains
    for j in range(hs):
        qk_cols = slice(j * MLA_QK, (j + 1) * MLA_QK)
        v_cols = slice(j * MLA_VW, (j + 1) * MLA_VW)
        for c in range(chains):
            rows = slice(c * tq, (c + 1) * tq)
            s = _dot_nt(q_ref[rows, qk_cols], k_scr[:, qk_cols])
            o_ref[rows, j * HEAD_DIM:(j + 1) * HEAD_DIM] = _softmax_pv([s], [v_scr[:, v_cols]], True).astype(
                o_ref.dtype)


def _mla_scratch(n_keys, hs):
    return [pltpu.VMEM((n_keys, hs * MLA_QK), BF16), pltpu.VMEM((n_keys, hs * MLA_VW), BF16)]


def _mla_latent(q, kn, v, kr, *, hs=4, chains=4):
    lat = pl.BlockSpec((SEQ, hs * HEAD_DIM), lambda b, h: (b, h))
    ctx = pl.BlockSpec((CTX_LEN, hs * HEAD_DIM), lambda b, h: (CTX_BLOCK0 + b, h))
    return pl.pallas_call(
        functools.partial(_mla_kernel, hs=hs, chains=chains, n_lat=SEQ),
        grid=(BATCH, MLA_HEADS // hs),
        in_specs=[
            pl.BlockSpec((SEQ, hs * MLA_QK), lambda b, h: (b, h)),
            lat,
            pl.BlockSpec((SEQ, HEAD_DIM), lambda b, h: (b, 0)),
            lat,
            ctx,
            pl.BlockSpec((CTX_LEN, HEAD_DIM), lambda b, h: (CTX_BLOCK0 + b, 0)),
            ctx,
        ],
        out_specs=pl.BlockSpec((SEQ, hs * HEAD_DIM), lambda b, h: (b, h)),
        out_shape=jax.ShapeDtypeStruct((N_LAT, MLA_HEADS * HEAD_DIM), BF16),
        scratch_shapes=_mla_scratch(SEQ + CTX_LEN, hs),
        compiler_params=_cparams(("parallel", "parallel")),
        name="mla_latent",
    )(q, kn, kr, v, kn, kr, v)


def _mla_ctx(q, kn, v, kr):
    wide = pl.BlockSpec((CTX_LEN, MLA_HEADS * HEAD_DIM), lambda b: (CTX_BLOCK0 + b, 0))
    return pl.pallas_call(
        functools.partial(_mla_kernel, hs=MLA_HEADS, chains=1, n_lat=0),
        grid=(BATCH,),
        in_specs=[
            pl.BlockSpec((CTX_LEN, MLA_HEADS * MLA_QK), lambda b: (CTX_BLOCK0 + b, 0)),
            wide,
            pl.BlockSpec((CTX_LEN, HEAD_DIM), lambda b: (CTX_BLOCK0 + b, 0)),
            wide,
        ],
        out_specs=pl.BlockSpec((CTX_LEN, MLA_HEADS * HEAD_DIM), lambda b: (b, 0)),
        out_shape=jax.ShapeDtypeStruct((N_CTX, MLA_HEADS * HEAD_DIM), BF16),
        scratch_shapes=_mla_scratch(CTX_LEN, MLA_HEADS),
        compiler_params=_cparams(("parallel",)),
        name="mla_ctx",
    )(q, kn, kr, v)


def _na_ctx_kernel(q_ref, k_ref, v_ref, o_ref):
    for h in range(NA_HEADS):
        cols = slice(h * HEAD_DIM, (h + 1) * HEAD_DIM)
        s = _dot_nt(q_ref[:, cols], k_ref[:, cols]) * (NA_SCALE * LOG2E)
        o_ref[:, cols] = _softmax_pv([s], [v_ref[:, cols]], False).astype(o_ref.dtype)


def _na_ctx(p):
    width = NA_HEADS * HEAD_DIM
    col0 = COL_NA // width

    def spec(c):
        return pl.BlockSpec((CTX_LEN, width), lambda b: (CTX_BLOCK0 + b, col0 + c))

    return pl.pallas_call(
        _na_ctx_kernel,
        grid=(BATCH,),
        in_specs=[spec(0), spec(1), spec(2)],
        out_specs=pl.BlockSpec((CTX_LEN, width), lambda b: (b, 0)),
        out_shape=jax.ShapeDtypeStruct((N_CTX, width), BF16),
        compiler_params=_cparams(("parallel",)),
        name="na_ctx",
    )(p, p, p)


def _na_plan():
    rows = SEQ // GRID_W
    invalid = 2 * NA_KH - 1
    pairs, plan, starts = [], [], []
    for t in range(rows // NA_TR):
        kw0 = int(np.clip(NA_TR * t - NA_KH // 2, 0, rows - NA_WR))
        starts.append(kw0)
        tile = []
        for ri in range(NA_TR):
            r = NA_TR * t + ri
            r0 = int(np.clip(r - NA_KH // 2, 0, rows - NA_KH))
            assert kw0 <= r0 and r0 + NA_KH <= kw0 + NA_WR
            row = []
            for kp in range(NA_WR // 2):
                pair = []
                for kr in (kw0 + 2 * kp, kw0 + 2 * kp + 1):
                    pair.append(kr - r + NA_KH - 1 if r0 <= kr < r0 + NA_KH else invalid)
                pair = tuple(pair)
                if pair not in pairs:
                    pairs.append(pair)
                row.append(pairs.index(pair))
            tile.append(row)
        plan.append(tile)
    return starts, plan, pairs


def _na_bias_pairs(rpb, pairs):
    c = np.arange(GRID_W)
    c0 = np.clip(c - NA_KW // 2, 0, GRID_W - NA_KW)
    col_ok = (c[None, :] >= c0[:, None]) & (c[None, :] < c0[:, None] + NA_KW)
    col_idx = np.clip(c[None, :] - c[:, None] + NA_KW - 1, 0, 2 * NA_KW - 2)
    onehot = (col_idx[None] == np.arange(2 * NA_KW - 1)[:, None, None]).astype(np.float32)
    t = jnp.einsum("lhdj,jck->lhdck", rpb.astype(F32), onehot, precision=lax.Precision.HIGHEST) * LOG2E
    t = jnp.where(col_ok, t, MASK_VALUE)
    masked = jnp.full(t.shape[:2] + (GRID_W, GRID_W), MASK_VALUE, F32)
    slabs = [t[:, :, d] for d in range(2 * NA_KH - 1)] + [masked]
    return jnp.stack([jnp.concatenate([slabs[a], slabs[b]], axis=-1) for a, b in pairs], axis=2)


NA_HS = 2


def _na_kernel(q_ref, k_ref, v_ref, kc_ref, vc_ref, t2_ref, o_ref, vo_ref, vco_ref, *, starts, plan):
    ones = jnp.ones((SEQ, HEAD_DIM), BF16)
    for j in range(NA_HS):
        cols = slice(j * HEAD_DIM, (j + 1) * HEAD_DIM)
        wide = slice(2 * j * HEAD_DIM, 2 * (j + 1) * HEAD_DIM)
        vo_ref[:, 2 * j * HEAD_DIM:(2 * j + 1) * HEAD_DIM] = v_ref[:, cols]
        vo_ref[:, (2 * j + 1) * HEAD_DIM:2 * (j + 1) * HEAD_DIM] = ones
        vco_ref[:, 2 * j * HEAD_DIM:(2 * j + 1) * HEAD_DIM] = vc_ref[:, cols]
        vco_ref[:, (2 * j + 1) * HEAD_DIM:2 * (j + 1) * HEAD_DIM] = ones[:CTX_LEN]
        kc = kc_ref[:, cols]
        vc = vco_ref[:, wide]
        for t, (kw0, tile) in enumerate(zip(starts, plan)):
            rows = slice(t * NA_TQ, (t + 1) * NA_TQ)
            win = slice(kw0 * GRID_W, kw0 * GRID_W + NA_TK)
            q = q_ref[rows, cols]
            bias = jnp.concatenate(
                [jnp.concatenate([t2_ref[0, j, idx] for idx in row], axis=1) for row in tile], axis=0)
            s = _dot_nt(q, k_ref[win, cols]) * (NA_SCALE * LOG2E) + bias
            sc = _dot_nt(q, kc) * (NA_SCALE * LOG2E)
            o_ref[rows, cols] = _softmax_pv([s, sc], [vo_ref[win, wide], vc], True).astype(o_ref.dtype)


def _na_latent(p, t2, l, starts, plan):
    width = NA_HS * HEAD_DIM
    cq = COL_NA // width
    ck = cq + NA_HEADS // NA_HS
    cv = ck + NA_HEADS // NA_HS
    n_pairs = t2.shape[2]
    return pl.pallas_call(
        functools.partial(_na_kernel, starts=starts, plan=plan),
        grid=(NA_HEADS // NA_HS, BATCH),
        in_specs=[
            pl.BlockSpec((SEQ, width), lambda h, b: (b, cq + h)),
            pl.BlockSpec((SEQ, width), lambda h, b: (b, ck + h)),
            pl.BlockSpec((SEQ, width), lambda h, b: (b, cv + h)),
            pl.BlockSpec((CTX_LEN, width), lambda h, b: (CTX_BLOCK0 + b, ck + h)),
            pl.BlockSpec((CTX_LEN, width), lambda h, b: (CTX_BLOCK0 + b, cv + h)),
            pl.BlockSpec((1, NA_HS, n_pairs, GRID_W, 2 * GRID_W), lambda h, b: (l, h, 0, 0, 0)),
        ],
        out_specs=pl.BlockSpec((SEQ, width), lambda h, b: (b, h)),
        out_shape=jax.ShapeDtypeStruct((N_LAT, NA_HEADS * HEAD_DIM), BF16),
        scratch_shapes=[pltpu.VMEM((SEQ, 2 * width), BF16), pltpu.VMEM((CTX_LEN, 2 * width), BF16)],
        compiler_params=_cparams(("parallel", "parallel")),
        name="na_latent",
    )(p, p, p, p, p, t2)


def _conv_kernel(v_ref, x1_ref, x2_ref, wv_ref, w1_ref, w2_ref, bv_ref, b1_ref, b2_ref, zin_ref, x2o_ref):
    n = v_ref.shape[0]
    row = lax.broadcasted_iota(jnp.int32, (n, 1), 0)

    def short_conv(p_ref, w_ref, b_ref):
        p = p_ref[...].astype(F32)
        prev = jnp.where(row == 0, 0.0, pltpu.roll(p, 1, 0))
        nxt = jnp.where(row == n - 1, 0.0, pltpu.roll(p, n - 1, 0))
        w = w_ref[0]
        return prev * w[0:1] + p * w[1:2] + nxt * w[2:3] + b_ref[0]

    zin_ref[...] = (short_conv(x1_ref, w1_ref, b1_ref) * short_conv(v_ref, wv_ref, bv_ref)).astype(zin_ref.dtype)
    x2o_ref[...] = short_conv(x2_ref, w2_ref, b2_ref).astype(x2o_ref.dtype)


def _hy_conv(p, conv_w, conv_b, l, n, row_block0, *, tc=256):
    nc = HY_WIDTH // tc

    def seg(s):
        return (pl.BlockSpec((n, tc), lambda b, j: (row_block0 + b, s * nc + j)),
                pl.BlockSpec((1, 3, tc), lambda b, j: (l, 0, s * nc + j)),
                pl.BlockSpec((1, 1, tc), lambda b, j: (l, 0, s * nc + j)))

    (pv, wv, bv), (p1, w1, b1), (p2, w2, b2) = seg(0), seg(1), seg(2)
    out_spec = pl.BlockSpec((n, tc), lambda b, j: (0, b * nc + j))
    out = jax.ShapeDtypeStruct((n, BATCH * HY_WIDTH), BF16)
    conv_b = conv_b.reshape(DEPTH, 1, -1)
    return pl.pallas_call(
        _conv_kernel,
        grid=(BATCH, nc),
        in_specs=[pv, p1, p2, wv, w1, w2, bv, b1, b2],
        out_specs=[out_spec, out_spec],
        out_shape=[out, out],
        compiler_params=_cparams(("parallel", "parallel")),
        name="hy_conv",
    )(p, p, p, conv_w, conv_w, conv_w, conv_b, conv_b, conv_b)


def _filt_kernel(z_ref, t_ref, dl_ref, w1_ref, b1_ref, w2_ref, b2_ref, w3_ref, fr_ref, hs_ref, ha_ref, kn_ref):
    n = z_ref.shape[0]
    fr = fr_ref[...]
    h = jnp.sin(fr * (_dot_hi(z_ref[...], w1_ref[...]) + b1_ref[...]))
    h = jnp.sin(fr * (_dot_hi(h, w2_ref[...]) + b2_ref[...]))
    h = _dot_hi(h, w3_ref[...])
    decay = jnp.exp(-t_ref[...] * dl_ref[...])
    row = lax.broadcasted_iota(jnp.int32, (n, 1), 0)
    hf = h[:, :HY_WIDTH] * decay
    hb = jnp.where(row == 0, 0.0, h[:, HY_WIDTH:] * decay)
    hs = hf + hb
    hs_ref[...] = hs.astype(hs_ref.dtype)
    ha_ref[...] = (hf - hb).astype(ha_ref.dtype)
    sign = jnp.where((row & 1) == 0, 1.0, -1.0)
    kn_ref[...] = jnp.sum(hs * sign, axis=0, keepdims=True)


def _hy_filter_taps(n, f_w1, f_b1, f_w2, f_b2, f_w3, f_freq):
    pos = jnp.arange(n, dtype=F32)
    t = jnp.linspace(0.0, 1.0, n, dtype=F32)
    bands = jnp.linspace(1e-4, HY_POS_BANDS - 1, HY_POS_BANDS, dtype=F32)
    ang = (2.0 * math.pi / n) * pos[:, None] * bands[None, :]
    z = jnp.concatenate([t[:, None], jnp.cos(ang), -jnp.sin(ang)], axis=-1)
    pad = HY_FILTER_HIDDEN - z.shape[1]
    z = jnp.pad(z, ((0, 0), (0, pad)))
    w1 = jnp.pad(f_w1.astype(F32), ((0, pad), (0, 0)))
    deltas = jnp.abs(jnp.linspace(math.log(HY_DECAY_TARGET) / HY_FAST_DECAY,
                                  math.log(HY_DECAY_TARGET) / HY_SLOW_DECAY, HY_WIDTH, dtype=F32))
    hid = HY_FILTER_HIDDEN
    return pl.pallas_call(
        _filt_kernel,
        out_shape=[
            jax.ShapeDtypeStruct((n, HY_WIDTH), BF16),
            jax.ShapeDtypeStruct((n, HY_WIDTH), BF16),
            jax.ShapeDtypeStruct((1, HY_WIDTH), F32),
        ],
        compiler_params=pltpu.CompilerParams(vmem_limit_bytes=VMEM_LIMIT),
        name="hy_filter",
    )(z, t[:, None], deltas[None, :], w1, f_b1.reshape(1, hid), f_w2, f_b2.reshape(1, hid), f_w3,
      f_freq.reshape(1, hid))


def _dft_tables(n):
    lo = 16 if n < 1024 else 32
    hi = n // lo
    t = jnp.arange(n, dtype=jnp.int32)[None, :]
    big = 2 * n

    def ang(ff):
        return ((ff * t) % big).astype(F32) * (2.0 * math.pi / big)

    a = ang(lo * jnp.arange(hi, dtype=jnp.int32)[:, None])
    b = ang(jnp.arange(lo, dtype=jnp.int32)[:, None])
    ca, sa, cb, sb = jnp.cos(a), jnp.sin(a), jnp.cos(b), jnp.sin(b)
    cos = (ca[:, None, :] * cb[None, :, :] - sa[:, None, :] * sb[None, :, :]).reshape(n, n)
    base = -(sa[:, None, :] * cb[None, :, :] + ca[:, None, :] * sb[None, :, :]).reshape(n, n)
    idx = jnp.arange(n)
    alt = jnp.where(idx % 2 == 0, 1.0, -1.0).astype(F32)
    msin = jnp.where(idx[:, None] == 0, alt[None, :], base)
    msin_t = jnp.where(idx[None, :] == 0, alt[:, None], base)
    return cos.astype(BF16), msin.astype(BF16), msin_t.astype(BF16)


def _batch_cols(b):
    return slice(b * HY_WIDTH, (b + 1) * HY_WIDTH)


def _dft_fwd_kernel(c_ref, s_ref, x_ref, hs_ref, ha_ref, kn_ref, yr_ref, yi_ref, *, tf, inv_n):
    x = x_ref[...]
    c = c_ref[...]
    s = s_ref[...]
    zr = _dot(c, x)
    zi = _dot(s, x)
    row = pl.program_id(0) * tf + lax.broadcasted_iota(jnp.int32, (tf, 1), 0)
    bin0 = row == 0
    wt = jnp.where(bin0, inv_n, 2.0 * inv_n)
    kr = _dot(c, hs_ref[...]) * wt
    ki = jnp.where(bin0, 0.0, _dot(s, ha_ref[...]) * wt)
    kr_im = jnp.where(bin0, kn_ref[...] * wt, kr)
    for b in range(BATCH):
        cols = _batch_cols(b)
        yr_ref[:, cols] = (zr[:, cols] * kr - zi[:, cols] * ki).astype(yr_ref.dtype)
        yi_ref[:, cols] = (zr[:, cols] * ki + zi[:, cols] * kr_im).astype(yi_ref.dtype)


def _hy_dft_fwd(cos, msin, zin, hs, ha, kn, *, tf):
    n = cos.shape[0]
    tf = min(tf, n)
    tab = pl.BlockSpec((tf, n), lambda i: (i, 0))
    filt = pl.BlockSpec((n, HY_WIDTH), lambda i: (0, 0))
    out = pl.BlockSpec((tf, BATCH * HY_WIDTH), lambda i: (i, 0))
    whole = pl.BlockSpec((n, BATCH * HY_WIDTH), lambda i: (0, 0), pipeline_mode=pl.Buffered(1))
    return pl.pallas_call(
        functools.partial(_dft_fwd_kernel, tf=tf, inv_n=1.0 / (2 * n)),
        grid=(n // tf,),
        in_specs=[tab, tab, whole, filt, filt, pl.BlockSpec((1, HY_WIDTH), lambda i: (0, 0))],
        out_specs=[out, out],
        out_shape=[jax.ShapeDtypeStruct((n, BATCH * HY_WIDTH), BF16)] * 2,
        compiler_params=_cparams(("parallel",)),
        name="hy_dft_fwd",
    )(cos, msin, zin, hs, ha, kn)


def _dft_inv_kernel(c_ref, st_ref, yr_ref, yi_ref, zin_ref, x2_ref, b_ref, o_ref):
    y = _dot(c_ref[...], yr_ref[...]) + _dot(st_ref[...], yi_ref[...])
    bias = b_ref[0]
    for b in range(BATCH):
        cols = _batch_cols(b)
        yb = y[:, cols] + zin_ref[:, cols].astype(F32) * bias
        o_ref[b] = (x2_ref[:, cols].astype(F32) * yb).astype(o_ref.dtype)


def _hy_dft_inv(cos, msin_t, yr, yi, zin, x2, bias, l, *, tt):
    n = cos.shape[0]
    tt = min(tt, n)
    tab = pl.BlockSpec((tt, n), lambda i: (i, 0))
    whole = pl.BlockSpec((n, BATCH * HY_WIDTH), lambda i: (0, 0), pipeline_mode=pl.Buffered(1))
    rows = pl.BlockSpec((tt, BATCH * HY_WIDTH), lambda i: (i, 0))
    out = pl.pallas_call(
        _dft_inv_kernel,
        grid=(n // tt,),
        in_specs=[tab, tab, whole, whole, rows, rows, pl.BlockSpec((1, 1, HY_WIDTH), lambda i: (l, 0, 0))],
        out_specs=pl.BlockSpec((BATCH, tt, HY_WIDTH), lambda i: (0, i, 0)),
        out_shape=jax.ShapeDtypeStruct((BATCH, n, HY_WIDTH), BF16),
        compiler_params=_cparams(("parallel",)),
        name="hy_dft_inv",
    )(cos, msin_t, yr, yi, zin, x2, bias.reshape(DEPTH, 1, HY_WIDTH))
    return out.reshape(BATCH * n, HY_WIDTH)


def _hyena(p, l, n, row_block0, tables, conv_w, conv_b, f_w1, f_b1, f_w2, f_b2, f_w3, f_freq, bias):
    cos, msin, msin_t = tables
    zin, x2 = _hy_conv(p, conv_w, conv_b, l, n, row_block0)
    hs, ha, kn = _hy_filter_taps(n, f_w1[l], f_b1[l], f_w2[l], f_b2[l], f_w3[l], f_freq[l])
    yr, yi = _hy_dft_fwd(cos, msin, zin, hs, ha, kn, tf=512)
    return _hy_dft_inv(cos, msin_t, yr, yi, zin, x2, bias, l, tt=512)


def _rope_tables():
    tok = jnp.arange(SEQ)
    row = (tok // GRID_W).astype(F32)
    col = (tok % GRID_W).astype(F32)
    n_freq = MLA_ROPE // 4
    inv = ROPE_THETA ** (-jnp.arange(n_freq, dtype=F32) / n_freq)
    ang = jnp.concatenate([row[:, None] * inv, col[:, None] * inv], axis=-1)
    cos, sin = jnp.cos(ang), jnp.sin(ang)
    half = MLA_ROPE // 2
    zeros = jnp.zeros((SEQ, half), F32)
    rest = HEAD_DIM - MLA_ROPE
    cos_t = jnp.concatenate([cos, cos, jnp.ones((SEQ, rest), F32)], axis=-1)
    sin_a = jnp.concatenate([-sin, zeros, jnp.zeros((SEQ, rest), F32)], axis=-1)
    sin_b = jnp.concatenate([zeros, sin, jnp.zeros((SEQ, rest), F32)], axis=-1)
    ident = jnp.ones((N_CTX, HEAD_DIM), F32)
    none = jnp.zeros((N_CTX, HEAD_DIM), F32)
    return (jnp.concatenate([cos_t, ident]), jnp.concatenate([sin_a, none]), jnp.concatenate([sin_b, none]))


def _layout_w_uq(w):
    w = w.reshape(Q_LORA, MLA_HEADS, HEAD_DIM + MLA_ROPE)
    w = jnp.pad(w, ((0, 0), (0, 0), (0, MLA_QK - HEAD_DIM - MLA_ROPE)))
    return w.reshape(Q_LORA, MLA_HEADS * MLA_QK).astype(BF16)


def kernel(x, c, ctx, c_ctx, w_ada, b_ada, g_attn_pre, g_attn_post, g_ffn_pre, g_ffn_post, w_in, hy_conv_w, hy_conv_b, hy_f_w1, hy_f_b1, hy_f_w2, hy_f_b2, hy_f_w3, hy_f_freq, hy_bias, mla_g_q, mla_w_uq, mla_g_kv, mla_w_ukv, na_rpb, w_out, w_ffn_gate, w_ffn_up, w_ffn_down):
    cc = jnp.concatenate([c, c_ctx[None, :], jnp.zeros((8 - BATCH - 1, D_MODEL), F32)], axis=0)
    mods_all = _ada(cc, w_ada, b_ada)
    mods = [mods_all[l].reshape(8, 1, 6 * D_MODEL) for l in range(DEPTH)]

    rope_tabs = _rope_tables()
    dft_lat = _dft_tables(SEQ)
    dft_ctx = _dft_tables(CTX_LEN)
    na_starts, na_plan, na_pairs = _na_plan()
    na_t2 = _na_bias_pairs(na_rpb, na_pairs)
    w_in_t = jnp.swapaxes(w_in, 1, 2)
    hy_w = (hy_conv_w, hy_conv_b, hy_f_w1, hy_f_b1, hy_f_w2, hy_f_b2, hy_f_w3, hy_f_freq, hy_bias)

    res = (x.reshape(N_LAT, D_MODEL), ctx.reshape(N_CTX, D_MODEL))
    xn = _prenorm(res[0], res[1], g_attn_pre[0], mods[0])

    for l in range(DEPTH):
        ctx_out = l < DEPTH - 1
        m_rows = N_TOK if ctx_out else N_LAT
        p = _in_proj(xn, w_in_t, l)

        q = _q_proj(p, m_rows, mla_g_q[l], _layout_w_uq(mla_w_uq[l]), rope_tabs)
        kn, v, kr = _kv_proj(p, mla_g_kv[l], mla_w_ukv, l, rope_tabs)
        mla = [_mla_latent(q, kn, v, kr), None]
        na = [_na_latent(p, na_t2, l, na_starts, na_plan), None]
        hy = [_hyena(p, l, SEQ, 0, dft_lat, *hy_w), None]

        if ctx_out:
            mla[1] = _mla_ctx(q, kn, v, kr)
            na[1] = _na_ctx(p)
            hy[1] = _hyena(p, l, CTX_LEN, CTX_BLOCK0, dft_ctx, *hy_w)
            res_l = res
        else:
            res_l = (res[0], None)

        stream, xn = _proj_post([tuple(hy), tuple(mla), tuple(na)], w_out, l, g_attn_post[l], mods[l], 2, res_l,
                                m_rows, nxt=(g_ffn_pre[l], mods[l], 3, 4), tm=POST_TM_OUT)
        h = _ffn_up(xn, w_ffn_gate, w_ffn_up, l)
        nxt = (g_attn_pre[l + 1], mods[l + 1], 0, 1) if ctx_out else None
        stream, xn = _proj_post([(h, None)], w_ffn_down, l, g_ffn_post[l], mods[l], 5, (stream, None), m_rows,
                                nxt=nxt, tm=POST_TM_DOWN)
        res = (stream, None)

    return stream.reshape(BATCH, SEQ, D_MODEL)
```
